```python
import jax, jax.numpy as jnp
from jax import lax
import numpy as np

D_MODEL = 2048
BATCH = 8
SEQ = 8192
DEPTH = 4

CTX_LEN = 256
GRID_W = 64
MIX_W = 2 * D_MODEL
SSD_W = MIX_W // 2
SSD_HEADS = 32
SSD_HEAD_DIM = SSD_W // SSD_HEADS
SSD_GROUPS = 8
SSD_STATE = 128
SSD_CHUNK = 128
CONV_W = 5
MLP_W = MIX_W - SSD_W
MLP_GROUPS = 16
MLP_GROUP_DIM = MLP_W // MLP_GROUPS
MLP_CHUNK = 128
GN = SSD_GROUPS * SSD_STATE
XBC_W = SSD_W + 2 * GN
DT_W = 2 * SSD_HEADS
IN_W = XBC_W + DT_W + SSD_W + 3 * MLP_W
EPS = 1e-6

kernel_name = "hybrid_ssd_chunkmlp_prefix_dit"


def _rmsnorm(x, g):
    xf = x.astype(jnp.float32)
    r = lax.rsqrt(jnp.mean(xf * xf, axis=-1, keepdims=True) + EPS)
    return (xf * r).astype(x.dtype) * g


def _dwconv_rows(x, w, b, n_rows, row_len):
    bsz, L, C = x.shape
    pad = CONV_W // 2
    xp = jnp.pad(x.reshape(bsz, n_rows, row_len, C), ((0, 0), (0, 0), (pad, pad), (0, 0)))
    y = b
    for k in range(CONV_W):
        y = y + xp[:, :, k:k + row_len] * w[k]
    return y.reshape(bsz, L, C)


def _ssd(xh, dt, A, Bm, Cm, h0, with_output):
    f32 = jnp.float32
    bsz, L, H, P = xh.shape
    G, N, Q = SSD_GROUPS, SSD_STATE, SSD_CHUNK
    R = H // G
    nc = L // Q
    x = xh.astype(f32).reshape(bsz, nc, Q, G, R, P)
    dtc = dt.reshape(bsz, nc, Q, G, R)
    a = dtc * A.reshape(G, R)
    xdt = x * dtc[..., None]
    Bc = Bm.astype(f32).reshape(bsz, nc, Q, G, N)
    Cc = Cm.astype(f32).reshape(bsz, nc, Q, G, N)
    acs = jnp.cumsum(a, axis=2)
    a_tot = acs[:, :, -1]
    decay_to_end = jnp.exp(a_tot[:, :, None] - acs)
    states = jnp.einsum('bckgn,bckgr,bckgrp->bcgrpn', Bc, decay_to_end, xdt)

    def step(h, inp):
        s, at = inp
        return h * jnp.exp(at)[..., None, None] + s, h

    hT, h_starts = lax.scan(step, h0.reshape(bsz, G, R, P, N),
                            (jnp.moveaxis(states, 1, 0), jnp.moveaxis(a_tot, 1, 0)))
    final = hT.reshape(bsz, H, P, N)
    if not with_output:
        return None, final
    h_starts = jnp.moveaxis(h_starts, 0, 1)
    CB = jnp.einsum('bcqgn,bckgn->bcgqk', Cc, Bc)
    acs_t = jnp.moveaxis(acs, 2, -1)
    diff = acs_t[..., :, None] - acs_t[..., None, :]
    mask = jnp.tril(jnp.ones((Q, Q), dtype=bool))
    Lmat = jnp.exp(jnp.where(mask, diff, -jnp.inf))
    y_diag = jnp.einsum('bcgqk,bcgrqk,bckgrp->bcqgrp', CB, Lmat, xdt)
    y_off = jnp.einsum('bcqgn,bcgrpn,bcqgr->bcqgrp', Cc, h_starts, jnp.exp(acs))
    return (y_diag + y_off).reshape(bsz, L, H, P), final


def _ssd_prep(z, conv_w, conv_b, dt_bias, a_log, n_rows, row_len):
    bsz, L, _ = z.shape
    xbc = jax.nn.silu(_dwconv_rows(z[..., :XBC_W], conv_w, conv_b, n_rows, row_len))
    xh = xbc[..., :SSD_W].reshape(bsz, L, SSD_HEADS, SSD_HEAD_DIM)
    Bm = xbc[..., SSD_W:SSD_W + GN].reshape(bsz, L, SSD_GROUPS, SSD_STATE)
    Cm = xbc[..., SSD_W + GN:XBC_W].reshape(bsz, L, SSD_GROUPS, SSD_STATE)
    dt_raw = z[..., XBC_W:XBC_W + DT_W].astype(jnp.float32).reshape(bsz, L, 2, SSD_HEADS)
    dt = jax.nn.softplus(dt_raw + dt_bias.astype(jnp.float32))
    A = -jnp.exp(a_log.astype(jnp.float32))
    return xh, Bm, Cm, dt, A


def _ssd_bidir(xh, Bm, Cm, dt, A, h0f, h0b, with_output):
    flip = lambda t: jnp.flip(t, axis=1)
    y_f, h_f = _ssd(xh, dt[:, :, 0], A[0], Bm, Cm, h0f, with_output)
    y_b, h_b = _ssd(flip(xh), flip(dt[:, :, 1]), A[1], flip(Bm), flip(Cm), h0b, with_output)
    if not with_output:
        return None, h_f, h_b
    return y_f + flip(y_b), h_f, h_b


def _mix_out(z, xh, y_ssd, d_skip, g_ssd, g_v, w_s, b_s, g_mlp, w_out):
    bsz, L, _ = z.shape
    o = XBC_W + DT_W
    z_ssd = z[..., o:o + SSD_W]
    u = z[..., o + SSD_W:o + SSD_W + MLP_W]
    v = z[..., o + SSD_W + MLP_W:o + SSD_W + 2 * MLP_W]
    z_mlp = z[..., o + SSD_W + 2 * MLP_W:]
    y = (y_ssd + d_skip.astype(jnp.float32)[:, None] * xh.astype(jnp.float32))
    y = y.reshape(bsz, L, SSD_W).astype(z.dtype)
    y_a = _rmsnorm(y * jax.nn.silu(z_ssd), g_ssd)
    vn = _rmsnorm(v, g_v).reshape(bsz, L // MLP_CHUNK, MLP_CHUNK, MLP_GROUPS, MLP_GROUP_DIM)
    sg = jnp.einsum('gqk,bckgd->bcqgd', w_s, vn) + jnp.swapaxes(b_s, 0, 1)[:, :, None]
    y_b = _rmsnorm(u * sg.reshape(bsz, L, MLP_W) * jax.nn.silu(z_mlp), g_mlp)
    return jnp.concatenate([y_a, y_b], axis=-1) @ w_out


def _fwd_setup_inputs(seed: int = 0) -> dict:
    key = jax.random.key(seed)
    ks = jax.random.split(key, 20)
    D = D_MODEL
    nrm = jax.random.normal
    x = nrm(ks[0], (BATCH, SEQ, D), jnp.float32)
    c = nrm(ks[1], (BATCH, D), jnp.float32)
    ctx = nrm(ks[2], (BATCH, CTX_LEN, D), jnp.float32)
    c_ctx = nrm(ks[3], (D,), jnp.float32)
    w_ada = nrm(ks[4], (DEPTH, D, 3 * D), jnp.float32) * (0.5 * D ** -0.5)
    b_ada = 0.01 * nrm(ks[5], (DEPTH, 3 * D), jnp.float32)
    g_pre = 1.0 + 0.05 * nrm(ks[6], (DEPTH, D), jnp.float32)
    g_post = 1.0 + 0.05 * nrm(ks[7], (DEPTH, D), jnp.float32)
    w_in = nrm(ks[8], (DEPTH, D, IN_W), jnp.float32) * D ** -0.5
    conv_w = nrm(ks[9], (DEPTH, CONV_W, XBC_W), jnp.float32) * CONV_W ** -0.5
    conv_b = 0.01 * nrm(ks[10], (DEPTH, XBC_W), jnp.float32)
    u_dt = jax.random.uniform(ks[11], (DEPTH, 2, SSD_HEADS), jnp.float32)
    dt0 = jnp.exp(u_dt * (np.log(0.1) - np.log(0.001)) + np.log(0.001))
    dt_bias = dt0 + jnp.log(-jnp.expm1(-dt0))
    a_log = jnp.log(jax.random.uniform(ks[12], (DEPTH, 2, SSD_HEADS), jnp.float32, 1.0, 16.0))
    d_skip = 1.0 + 0.1 * nrm(ks[13], (DEPTH, SSD_HEADS), jnp.float32)
    g_ssd = 1.0 + 0.05 * nrm(ks[14], (DEPTH, SSD_W), jnp.float32)
    g_v = 1.0 + 0.05 * nrm(ks[15], (DEPTH, MLP_W), jnp.float32)
    w_s = nrm(ks[16], (DEPTH, MLP_GROUPS, MLP_CHUNK, MLP_CHUNK), jnp.float32) * (0.5 * MLP_CHUNK ** -0.5)
    b_s = 1.0 + 0.05 * nrm(ks[17], (DEPTH, MLP_GROUPS, MLP_CHUNK), jnp.float32)
    g_mlp = 1.0 + 0.05 * nrm(ks[18], (DEPTH, MLP_W), jnp.float32)
    w_out = nrm(ks[19], (DEPTH, MIX_W, D), jnp.float32) * MIX_W ** -0.5
    return {"x": x, "c": c, "ctx": ctx, "c_ctx": c_ctx, "w_ada": w_ada, "b_ada": b_ada,
            "g_pre": g_pre, "g_post": g_post, "w_in": w_in, "conv_w": conv_w, "conv_b": conv_b,
            "dt_bias": dt_bias, "a_log": a_log, "d_skip": d_skip, "g_ssd": g_ssd, "g_v": g_v,
            "w_s": w_s, "b_s": b_s, "g_mlp": g_mlp, "w_out": w_out}


def _fwd_reference(x, c, ctx, c_ctx, w_ada, b_ada, g_pre, g_post, w_in, conv_w, conv_b,
              dt_bias, a_log, d_skip, g_ssd, g_v, w_s, b_s, g_mlp, w_out):
    bsz, L, _ = x.shape
    ROWS = L // GRID_W
    sc = jax.nn.silu(c)
    scc = jax.nn.silu(c_ctx)
    h0 = jnp.zeros((bsz, SSD_HEADS, SSD_HEAD_DIM, SSD_STATE), jnp.float32)
    for l in range(DEPTH):
        last = l == DEPTH - 1
        shift, scale, gate = jnp.split(sc @ w_ada[l] + b_ada[l], 3, axis=-1)
        shift_c, scale_c, gate_c = jnp.split(scc @ w_ada[l] + b_ada[l], 3, axis=-1)
        hc = _rmsnorm(ctx, g_pre[l]) * (1.0 + scale_c) + shift_c
        zc = hc @ (w_in[l][:, :XBC_W + DT_W] if last else w_in[l])
        xh_c, B_c, C_c, dt_c, A = _ssd_prep(zc, conv_w[l], conv_b[l], dt_bias[l], a_log[l], 1, CTX_LEN)
        y_c, h_f, h_b = _ssd_bidir(xh_c, B_c, C_c, dt_c, A, h0, h0, not last)
        hx = _rmsnorm(x, g_pre[l]) * (1.0 + scale[:, None]) + shift[:, None]
        zx = hx @ w_in[l]
        xh, Bm, Cm, dt, A = _ssd_prep(zx, conv_w[l], conv_b[l], dt_bias[l], a_log[l], ROWS, GRID_W)
        y_x, _, _ = _ssd_bidir(xh, Bm, Cm, dt, A, h_f, h_b, True)
        out = _rmsnorm(_mix_out(zx, xh, y_x, d_skip[l], g_ssd[l], g_v[l], w_s[l], b_s[l],
                                g_mlp[l], w_out[l]), g_post[l])
        x = x + gate[:, None] * out
        if not last:
            out_c = _rmsnorm(_mix_out(zc, xh_c, y_c, d_skip[l], g_ssd[l], g_v[l], w_s[l], b_s[l],
                                      g_mlp[l], w_out[l]), g_post[l])
            ctx = ctx + gate_c * out_c
    return x


import jax as _jax
import jax.numpy as _jnp

TWIN_FORMAT = 'train_step'
FWD_PARAMS = ['x', 'c', 'ctx', 'c_ctx', 'w_ada', 'b_ada', 'g_pre', 'g_post', 'w_in', 'conv_w', 'conv_b', 'dt_bias', 'a_log', 'd_skip', 'g_ssd', 'g_v', 'w_s', 'b_s', 'g_mlp', 'w_out']
TWIN_WEIGHTS = ['c_ctx', 'w_ada', 'b_ada', 'g_pre', 'g_post', 'w_in', 'conv_w', 'conv_b', 'dt_bias', 'a_log', 'd_skip', 'g_ssd', 'g_v', 'w_s', 'b_s', 'g_mlp', 'w_out']
TWIN_DIFF_INPUT = 'x'
TWIN_INPUTS = ['x', 'c', 'ctx', 'c_ctx', 'w_ada', 'b_ada', 'g_pre', 'g_post', 'w_in', 'conv_w', 'conv_b', 'dt_bias', 'a_log', 'd_skip', 'g_ssd', 'g_v', 'w_s', 'b_s', 'g_mlp', 'w_out', 'loss_target', 'm_c_ctx', 'm_w_ada', 'm_b_ada', 'm_g_pre', 'm_g_post', 'm_w_in', 'm_conv_w', 'm_conv_b', 'm_dt_bias', 'm_a_log', 'm_d_skip', 'm_g_ssd', 'm_g_v', 'm_w_s', 'm_b_s', 'm_g_mlp', 'm_w_out', 'v_c_ctx', 'v_w_ada', 'v_b_ada', 'v_g_pre', 'v_g_post', 'v_w_in', 'v_conv_w', 'v_conv_b', 'v_dt_bias', 'v_a_log', 'v_d_skip', 'v_g_ssd', 'v_g_v', 'v_w_s', 'v_b_s', 'v_g_mlp', 'v_w_out']
TWIN_OUTPUTS = ['loss', 'grad_x', 'grad_c_ctx', 'grad_w_ada', 'grad_b_ada', 'grad_g_pre', 'grad_g_post', 'grad_w_in', 'grad_conv_w', 'grad_conv_b', 'grad_dt_bias', 'grad_a_log', 'grad_d_skip', 'grad_g_ssd', 'grad_g_v', 'grad_w_s', 'grad_b_s', 'grad_g_mlp', 'grad_w_out', 'delta_c_ctx', 'delta_w_ada', 'delta_b_ada', 'delta_g_pre', 'delta_g_post', 'delta_w_in', 'delta_conv_w', 'delta_conv_b', 'delta_dt_bias', 'delta_a_log', 'delta_d_skip', 'delta_g_ssd', 'delta_g_v', 'delta_w_s', 'delta_b_s', 'delta_g_mlp', 'delta_w_out', 'new_m_c_ctx', 'new_m_w_ada', 'new_m_b_ada', 'new_m_g_pre', 'new_m_g_post', 'new_m_w_in', 'new_m_conv_w', 'new_m_conv_b', 'new_m_dt_bias', 'new_m_a_log', 'new_m_d_skip', 'new_m_g_ssd', 'new_m_g_v', 'new_m_w_s', 'new_m_b_s', 'new_m_g_mlp', 'new_m_w_out', 'new_v_c_ctx', 'new_v_w_ada', 'new_v_b_ada', 'new_v_g_pre', 'new_v_g_post', 'new_v_w_in', 'new_v_conv_w', 'new_v_conv_b', 'new_v_dt_bias', 'new_v_a_log', 'new_v_d_skip', 'new_v_g_ssd', 'new_v_g_v', 'new_v_w_s', 'new_v_b_s', 'new_v_g_mlp', 'new_v_w_out']
TWIN_LEAF_KINDS = {'loss': 'loss', 'grad_x': 'grad_x', 'grad_c_ctx': 'grad_w', 'grad_w_ada': 'grad_w', 'grad_b_ada': 'grad_w', 'grad_g_pre': 'grad_w', 'grad_g_post': 'grad_w', 'grad_w_in': 'grad_w', 'grad_conv_w': 'grad_w', 'grad_conv_b': 'grad_w', 'grad_dt_bias': 'grad_w', 'grad_a_log': 'grad_w', 'grad_d_skip': 'grad_w', 'grad_g_ssd': 'grad_w', 'grad_g_v': 'grad_w', 'grad_w_s': 'grad_w', 'grad_b_s': 'grad_w', 'grad_g_mlp': 'grad_w', 'grad_w_out': 'grad_w', 'delta_c_ctx': 'delta_w', 'delta_w_ada': 'delta_w', 'delta_b_ada': 'delta_w', 'delta_g_pre': 'delta_w', 'delta_g_post': 'delta_w', 'delta_w_in': 'delta_w', 'delta_conv_w': 'delta_w', 'delta_conv_b': 'delta_w', 'delta_dt_bias': 'delta_w', 'delta_a_log': 'delta_w', 'delta_d_skip': 'delta_w', 'delta_g_ssd': 'delta_w', 'delta_g_v': 'delta_w', 'delta_w_s': 'delta_w', 'delta_b_s': 'delta_w', 'delta_g_mlp': 'delta_w', 'delta_w_out': 'delta_w', 'new_m_c_ctx': 'new_m', 'new_m_w_ada': 'new_m', 'new_m_b_ada': 'new_m', 'new_m_g_pre': 'new_m', 'new_m_g_post': 'new_m', 'new_m_w_in': 'new_m', 'new_m_conv_w': 'new_m', 'new_m_conv_b': 'new_m', 'new_m_dt_bias': 'new_m', 'new_m_a_log': 'new_m', 'new_m_d_skip': 'new_m', 'new_m_g_ssd': 'new_m', 'new_m_g_v': 'new_m', 'new_m_w_s': 'new_m', 'new_m_b_s': 'new_m', 'new_m_g_mlp': 'new_m', 'new_m_w_out': 'new_m', 'new_v_c_ctx': 'new_v', 'new_v_w_ada': 'new_v', 'new_v_b_ada': 'new_v', 'new_v_g_pre': 'new_v', 'new_v_g_post': 'new_v', 'new_v_w_in': 'new_v', 'new_v_conv_w': 'new_v', 'new_v_conv_b': 'new_v', 'new_v_dt_bias': 'new_v', 'new_v_a_log': 'new_v', 'new_v_d_skip': 'new_v', 'new_v_g_ssd': 'new_v', 'new_v_g_v': 'new_v', 'new_v_w_s': 'new_v', 'new_v_b_s': 'new_v', 'new_v_g_mlp': 'new_v', 'new_v_w_out': 'new_v'}


def _forward(args):
    return _fwd_reference(*[args[k] for k in FWD_PARAMS])


def _output_shape():
    def fwd():
        inp = _fwd_setup_inputs(0)
        return _fwd_reference(*[inp[k] for k in FWD_PARAMS])
    out = _jax.eval_shape(fwd)
    return out.shape, out.dtype

N_MICROBATCH = 1
ADAM_LR = 0.001
ADAM_B1 = 0.9
ADAM_B2 = 0.999
ADAM_EPS = 1e-08
ADAM_WD = 0.01
ADAM_STEP = 10
PER_EXAMPLE_BATCH_AXIS = {'x': 0, 'c': 0, 'ctx': 0, 'loss_target': 0}
SHARED_INPUTS = []
_WEIGHT_DTYPES = {'c_ctx': _jnp.float32, 'w_ada': _jnp.float32, 'b_ada': _jnp.float32, 'g_pre': _jnp.float32, 'g_post': _jnp.float32, 'w_in': _jnp.float32, 'conv_w': _jnp.float32, 'conv_b': _jnp.float32, 'dt_bias': _jnp.float32, 'a_log': _jnp.float32, 'd_skip': _jnp.float32, 'g_ssd': _jnp.float32, 'g_v': _jnp.float32, 'w_s': _jnp.float32, 'b_s': _jnp.float32, 'g_mlp': _jnp.float32, 'w_out': _jnp.float32}
MOMENT_SCALE = {'c_ctx': 7.655628e-03, 'w_ada': 1.173778e+00, 'b_ada': 2.533483e+00, 'g_pre': 9.708292e-02, 'g_post': 3.257035e+00, 'w_in': 4.819247e-02, 'conv_w': 5.404735e-02, 'conv_b': 1.091828e-01, 'dt_bias': 1.331406e-01, 'a_log': 1.656499e-01, 'd_skip': 2.231590e-01, 'g_ssd': 9.383191e-02, 'g_v': 2.189584e-02, 'w_s': 4.178626e-02, 'b_s': 4.094183e-02, 'g_mlp': 5.606758e-02, 'w_out': 1.029721e-01}


def _to_microbatches(a, axis):
    t = _jnp.moveaxis(a, axis, 0)
    t = t.reshape((N_MICROBATCH, t.shape[0] // N_MICROBATCH) + t.shape[1:])
    return _jnp.moveaxis(t, 1, axis + 1)


def setup_inputs(seed: int = 0) -> dict:
    inp = _fwd_setup_inputs(seed)
    key = _jax.random.fold_in(_jax.random.key(seed), 7919)
    shape, _ = _output_shape()
    out = dict(inp)
    out["loss_target"] = _jax.random.normal(_jax.random.fold_in(key, 0), shape, _jnp.float32)
    for i, name in enumerate(TWIN_WEIGHTS):
        w = inp[name].astype(_jnp.float32)
        if MOMENT_SCALE is None:
            s = _jnp.sqrt(_jnp.mean(_jnp.square(w)) + 1e-30)
        else:
            s = MOMENT_SCALE[name]
        km, kv = _jax.random.split(_jax.random.fold_in(key, i + 1))
        out[name] = w
        out["m_" + name] = s * _jax.random.normal(km, w.shape, _jnp.float32)
        out["v_" + name] = (s * s) * _jax.random.uniform(kv, w.shape, _jnp.float32, 0.5, 1.5)
    if N_MICROBATCH > 1:
        for name, axis in PER_EXAMPLE_BATCH_AXIS.items():
            out[name] = _to_microbatches(out[name], axis)
    return {'x': out['x'], 'c': out['c'], 'ctx': out['ctx'], 'c_ctx': out['c_ctx'], 'w_ada': out['w_ada'], 'b_ada': out['b_ada'], 'g_pre': out['g_pre'], 'g_post': out['g_post'], 'w_in': out['w_in'], 'conv_w': out['conv_w'], 'conv_b': out['conv_b'], 'dt_bias': out['dt_bias'], 'a_log': out['a_log'], 'd_skip': out['d_skip'], 'g_ssd': out['g_ssd'], 'g_v': out['g_v'], 'w_s': out['w_s'], 'b_s': out['b_s'], 'g_mlp': out['g_mlp'], 'w_out': out['w_out'], 'loss_target': out['loss_target'], 'm_c_ctx': out['m_c_ctx'], 'm_w_ada': out['m_w_ada'], 'm_b_ada': out['m_b_ada'], 'm_g_pre': out['m_g_pre'], 'm_g_post': out['m_g_post'], 'm_w_in': out['m_w_in'], 'm_conv_w': out['m_conv_w'], 'm_conv_b': out['m_conv_b'], 'm_dt_bias': out['m_dt_bias'], 'm_a_log': out['m_a_log'], 'm_d_skip': out['m_d_skip'], 'm_g_ssd': out['m_g_ssd'], 'm_g_v': out['m_g_v'], 'm_w_s': out['m_w_s'], 'm_b_s': out['m_b_s'], 'm_g_mlp': out['m_g_mlp'], 'm_w_out': out['m_w_out'], 'v_c_ctx': out['v_c_ctx'], 'v_w_ada': out['v_w_ada'], 'v_b_ada': out['v_b_ada'], 'v_g_pre': out['v_g_pre'], 'v_g_post': out['v_g_post'], 'v_w_in': out['v_w_in'], 'v_conv_w': out['v_conv_w'], 'v_conv_b': out['v_conv_b'], 'v_dt_bias': out['v_dt_bias'], 'v_a_log': out['v_a_log'], 'v_d_skip': out['v_d_skip'], 'v_g_ssd': out['v_g_ssd'], 'v_g_v': out['v_g_v'], 'v_w_s': out['v_w_s'], 'v_b_s': out['v_b_s'], 'v_g_mlp': out['v_g_mlp'], 'v_w_out': out['v_w_out']}


def _loss(weights, diff, rest, loss_target):
    with _jax.named_scope("forward"):
        args = {**rest, TWIN_DIFF_INPUT: diff, **{k: w.astype(_WEIGHT_DTYPES[k]) for k, w in weights.items()}}
        y = _forward(args)
    with _jax.named_scope("loss_head"):
        err = _jnp.square(y.astype(_jnp.float32) - loss_target)
        return 0.5 * _jnp.sum(_jnp.mean(err, axis=-1)) if err.ndim else 0.5 * err


def _adamw(w, g, m, v):
    m = ADAM_B1 * m + (1.0 - ADAM_B1) * g
    v = ADAM_B2 * v + (1.0 - ADAM_B2) * _jnp.square(g)
    m_hat = m / (1.0 - ADAM_B1 ** ADAM_STEP)
    v_hat = v / (1.0 - ADAM_B2 ** ADAM_STEP)
    delta = -ADAM_LR * (m_hat / (_jnp.sqrt(v_hat) + ADAM_EPS) + ADAM_WD * w)
    return delta, m, v


def reference(x, c, ctx, c_ctx, w_ada, b_ada, g_pre, g_post, w_in, conv_w, conv_b, dt_bias, a_log, d_skip, g_ssd, g_v, w_s, b_s, g_mlp, w_out, loss_target, m_c_ctx, m_w_ada, m_b_ada, m_g_pre, m_g_post, m_w_in, m_conv_w, m_conv_b, m_dt_bias, m_a_log, m_d_skip, m_g_ssd, m_g_v, m_w_s, m_b_s, m_g_mlp, m_w_out, v_c_ctx, v_w_ada, v_b_ada, v_g_pre, v_g_post, v_w_in, v_conv_w, v_conv_b, v_dt_bias, v_a_log, v_d_skip, v_g_ssd, v_g_v, v_w_s, v_b_s, v_g_mlp, v_w_out):
    given = dict(x=x, c=c, ctx=ctx, c_ctx=c_ctx, w_ada=w_ada, b_ada=b_ada, g_pre=g_pre, g_post=g_post, w_in=w_in, conv_w=conv_w, conv_b=conv_b, dt_bias=dt_bias, a_log=a_log, d_skip=d_skip, g_ssd=g_ssd, g_v=g_v, w_s=w_s, b_s=b_s, g_mlp=g_mlp, w_out=w_out, loss_target=loss_target, m_c_ctx=m_c_ctx, m_w_ada=m_w_ada, m_b_ada=m_b_ada, m_g_pre=m_g_pre, m_g_post=m_g_post, m_w_in=m_w_in, m_conv_w=m_conv_w, m_conv_b=m_conv_b, m_dt_bias=m_dt_bias, m_a_log=m_a_log, m_d_skip=m_d_skip, m_g_ssd=m_g_ssd, m_g_v=m_g_v, m_w_s=m_w_s, m_b_s=m_b_s, m_g_mlp=m_g_mlp, m_w_out=m_w_out, v_c_ctx=v_c_ctx, v_w_ada=v_w_ada, v_b_ada=v_b_ada, v_g_pre=v_g_pre, v_g_post=v_g_post, v_w_in=v_w_in, v_conv_w=v_conv_w, v_conv_b=v_conv_b, v_dt_bias=v_dt_bias, v_a_log=v_a_log, v_d_skip=v_d_skip, v_g_ssd=v_g_ssd, v_g_v=v_g_v, v_w_s=v_w_s, v_b_s=v_b_s, v_g_mlp=v_g_mlp, v_w_out=v_w_out)
    weights = {n: given[n] for n in TWIN_WEIGHTS}
    shared = {n: given[n] for n in SHARED_INPUTS}
    per_example = {n: given[n] for n in ['x', 'c', 'ctx']}
    grad_fn = _jax.value_and_grad(_loss, argnums=(0, 1))

    def one_microbatch(ex, loss_target):
        ex = dict(ex)
        diff = ex.pop(TWIN_DIFF_INPUT)
        return grad_fn(weights, diff, {**shared, **ex}, loss_target)

    if N_MICROBATCH == 1:
        loss, (grad_w, grad_x) = one_microbatch(per_example, given["loss_target"])
    else:
        def body(carry, xs):
            loss_sum, grad_sum = carry
            l_k, (gw_k, gx_k) = one_microbatch(xs[0], xs[1])
            with _jax.named_scope("update"):
                return (loss_sum + l_k, _jax.tree.map(_jnp.add, grad_sum, gw_k)), gx_k

        init = (_jnp.zeros((), _jnp.float32), _jax.tree.map(_jnp.zeros_like, weights))
        (loss, grad_w), grad_x = _jax.lax.scan(body, init, (per_example, given["loss_target"]))
    with _jax.named_scope("update"):
        delta_w, new_m, new_v = {}, {}, {}
        for n in TWIN_WEIGHTS:
            delta_w[n], new_m[n], new_v[n] = _adamw(weights[n], grad_w[n], given["m_" + n], given["v_" + n])
    return (loss, grad_x, *[grad_w[n] for n in TWIN_WEIGHTS], *[delta_w[n] for n in TWIN_WEIGHTS],
            *[new_m[n] for n in TWIN_WEIGHTS], *[new_v[n] for n in TWIN_WEIGHTS])
```

```python
import numpy as np
import jax
import jax.numpy as jnp
from jax import lax
from jax.experimental import pallas as pl
from jax.experimental.pallas import tpu as pltpu

F32 = jnp.float32
BF16 = jnp.bfloat16

D_MODEL = 2048
GRID_W = 64
SSD_W = 2048
SSD_HEADS = 32
SSD_HEAD_DIM = 64
SSD_GROUPS = 8
HEADS_PER_GROUP = 4
SSD_STATE = 128
CHUNK = 128
CONV_W = 5
MLP_W = 2048
MLP_GROUPS = 16
XBC_W = 4096
DT_W = 64
IN_W = 12352
Z_MAIN = IN_W - DT_W
GROUP_COLS = HEADS_PER_GROUP * SSD_HEAD_DIM
EPS = 1e-6
N_DEV = 8

ADAM_LR = 0.001
ADAM_B1 = 0.9
ADAM_B2 = 0.999
ADAM_EPS = 1e-08
ADAM_WD = 0.01
ADAM_STEP = 10

LANES = 128
NEG_BIG = -1e30

NN = (((1,), (0,)), ((), ()))
NT = (((1,), (1,)), ((), ()))
TN = (((0,), (0,)), ((), ()))
HI = lax.Precision.HIGHEST


def _dot(a, b, dims):
    return lax.dot_general(a.astype(BF16), b.astype(BF16), dims, preferred_element_type=F32)


def _dot_hi(a, b, dims=NN):
    return lax.dot_general(a, b, dims, preferred_element_type=F32, precision=HI)


def _sigmoid(x):
    return 1.0 / (1.0 + jnp.exp(-x))


def _silu(x):
    return x * _sigmoid(x)


def _softplus(x):
    return jnp.maximum(x, 0.0) + jnp.log(1.0 + jnp.exp(-jnp.abs(x)))


def _call(body, *, name, out_shape, grid=None, in_specs=None, out_specs=None, scratch=(),
          sem=None, vmem_mb=None, aliases=None):
    params = {}
    if sem is not None:
        params["dimension_semantics"] = sem
    if vmem_mb is not None:
        params["vmem_limit_bytes"] = vmem_mb << 20
    kw = {}
    if grid is not None:
        kw["grid"] = grid
    if in_specs is not None:
        kw["in_specs"] = in_specs
    if out_specs is not None:
        kw["out_specs"] = out_specs
    return pl.pallas_call(body, name=name, out_shape=out_shape, scratch_shapes=list(scratch),
                          input_output_aliases=aliases or {},
                          compiler_params=pltpu.CompilerParams(**params), **kw)


def _sds(shape, dtype):
    return jax.ShapeDtypeStruct(tuple(shape), dtype)


ANY = pl.BlockSpec(memory_space=pl.ANY)


def _my_pos():
    return lax.axis_index("x"), lax.axis_index("y"), lax.axis_index("c")


def _flip(v, bit):
    return 1 - v if bit else v


def _peer(pos, k):
    mx, my, mc = pos
    return (_flip(mx, (k >> 2) & 1), _flip(my, (k >> 1) & 1), _flip(mc, k & 1))


def _lin(pos):
    return 4 * pos[0] + 2 * pos[1] + pos[2]


def _all_gather_small(x, name):
    R, C = x.shape

    def body(x_ref, o_ref, send_sems, recv_sems):
        me = _my_pos()
        o_ref[_lin(me)] = x_ref[...]
        sends = []
        for k in range(1, N_DEV):
            peer = _peer(me, k)
            cp = pltpu.make_async_remote_copy(
                src_ref=x_ref, dst_ref=o_ref.at[_lin(me)], send_sem=send_sems.at[k - 1],
                recv_sem=recv_sems.at[k - 1], device_id=peer, device_id_type=pl.DeviceIdType.MESH)
            cp.start()
            sends.append(cp)
        for k in range(1, N_DEV):
            peer = _peer(me, k)
            pltpu.make_async_remote_copy(
                src_ref=x_ref, dst_ref=o_ref.at[_lin(peer)], send_sem=send_sems.at[k - 1],
                recv_sem=recv_sems.at[k - 1], device_id=peer,
                device_id_type=pl.DeviceIdType.MESH).wait_recv()
        for cp in sends:
            cp.wait_send()

    return _call(body, name=name, out_shape=_sds((N_DEV, R, C), x.dtype),
                 in_specs=[pl.BlockSpec(memory_space=pltpu.VMEM)],
                 out_specs=pl.BlockSpec(memory_space=pltpu.VMEM),
                 scratch=[pltpu.SemaphoreType.DMA((N_DEV - 1,)), pltpu.SemaphoreType.DMA((N_DEV - 1,))],
                 vmem_mb=40)(x)


def _all_gather_big(x, name):
    def body(x_ref, o_ref, send_sems, recv_sems, local_sem):
        mx, my, mc = _my_pos()
        me, sibling = (mx, my, mc), (mx, my, 1 - mc)
        chips = [(1 - mx, my), (mx, 1 - my), (1 - mx, 1 - my)]

        def slot(pos):
            return o_ref.at[_lin(pos)]

        def copy(k, block, to, src=None):
            return pltpu.make_async_remote_copy(
                src_ref=slot(block) if src is None else src, dst_ref=slot(block),
                send_sem=send_sems.at[k], recv_sem=recv_sems.at[k], device_id=to,
                device_id_type=pl.DeviceIdType.MESH)

        mine = pltpu.make_async_copy(x_ref, slot(me), local_sem)
        mine.start()
        first = [copy(0, me, sibling, src=x_ref)]
        first += [copy(1 + j, me, (*chip, mc), src=x_ref) for j, chip in enumerate(chips)]
        for cp in first:
            cp.start()
        passed = [copy(4 + j, (*chip, mc), sibling) for j, chip in enumerate(chips)]
        for j, chip in enumerate(chips):
            copy(1 + j, (*chip, mc), me).wait_recv()
            passed[j].start()
        copy(0, sibling, me).wait_recv()
        for j, chip in enumerate(chips):
            copy(4 + j, (*chip, 1 - mc), me).wait_recv()
        for cp in first + passed:
            cp.wait_send()
        mine.wait()

    return _call(body, name=name, out_shape=_sds((N_DEV,) + x.shape, x.dtype),
                 in_specs=[ANY], out_specs=ANY,
                 scratch=[pltpu.SemaphoreType.DMA((7,)), pltpu.SemaphoreType.DMA((7,)),
                          pltpu.SemaphoreType.DMA])(x)


def _slab_exchange(s, name):
    def body(s_ref, r_ref, send_sems, recv_sems, local_sem):
        me = _my_pos()
        mine = pltpu.make_async_copy(s_ref.at[_lin(me)], r_ref.at[_lin(me)], local_sem)
        mine.start()
        sends = []
        for k in range(1, N_DEV):
            peer = _peer(me, k)
            cp = pltpu.make_async_remote_copy(
                src_ref=s_ref.at[_lin(peer)], dst_ref=r_ref.at[_lin(me)], send_sem=send_sems.at[k - 1],
                recv_sem=recv_sems.at[k - 1], device_id=peer, device_id_type=pl.DeviceIdType.MESH)
            cp.start()
            sends.append(cp)
        for k in range(1, N_DEV):
            peer = _peer(me, k)
            pltpu.make_async_remote_copy(
                src_ref=s_ref.at[_lin(peer)], dst_ref=r_ref.at[_lin(peer)], send_sem=send_sems.at[k - 1],
                recv_sem=recv_sems.at[k - 1], device_id=peer,
                device_id_type=pl.DeviceIdType.MESH).wait_recv()
        for cp in sends:
            cp.wait_send()
        mine.wait()

    return _call(body, name=name, out_shape=_sds(s.shape, s.dtype), in_specs=[ANY], out_specs=ANY,
                 scratch=[pltpu.SemaphoreType.DMA((7,)), pltpu.SemaphoreType.DMA((7,)),
                          pltpu.SemaphoreType.DMA])(s)


def _mm(a, b, mode, out_dtype, name, tm, tn, tk):
    if mode == "nn":
        (M, K), (K2, N) = a.shape, b.shape
    elif mode == "nt":
        (M, K), (N, K2) = a.shape, b.shape
    else:
        (K, M), (K2, N) = a.shape, b.shape
    assert K == K2
    tm, tn, tk = min(tm, M), min(tn, N), min(tk, K)
    assert M % tm == 0 and N % tn == 0 and K % tk == 0, (name, M, N, K, tm, tn, tk)
    nk = K // tk
    dims = {"nn": NN, "nt": NT, "tn": TN}[mode]

    def body(a_ref, b_ref, o_ref, *acc):
        p = _dot(a_ref[...], b_ref[...], dims)
        if nk == 1:
            o_ref[...] = p.astype(out_dtype)
        else:
            acc_ref = acc[0]
            k = pl.program_id(2)

            @pl.when(k == 0)
            def _():
                acc_ref[...] = p

            @pl.when(k > 0)
            def _():
                acc_ref[...] += p

            @pl.when(k == nk - 1)
            def _():
                o_ref[...] = acc_ref[...].astype(out_dtype)

    if mode == "tn":
        a_spec = pl.BlockSpec((tk, tm), lambda i, j, k: (k, i))
    else:
        a_spec = pl.BlockSpec((tm, tk), lambda i, j, k: (i, k))
    if mode == "nt":
        b_spec = pl.BlockSpec((tn, tk), lambda i, j, k: (j, k))
    else:
        b_spec = pl.BlockSpec((tk, tn), lambda i, j, k: (k, j))
    return _call(body, name=name, out_shape=_sds((M, N), out_dtype), grid=(M // tm, N // tn, nk),
                 in_specs=[a_spec, b_spec], out_specs=pl.BlockSpec((tm, tn), lambda i, j, k: (i, j)),
                 scratch=[] if nk == 1 else [pltpu.VMEM((tm, tn), F32)],
                 sem=("parallel", "parallel", "arbitrary"), vmem_mb=48)(a, b)


def _rms(x):
    return lax.rsqrt(jnp.mean(x * x, axis=-1, keepdims=True) + EPS)


def _prenorm_f(x, g, sc, sh):
    return (x * _rms(x) * g) * (1.0 + sc) + sh


def _pick(is_ctx, ref):
    return jnp.where(is_ctx, ref[0:1, :], ref[1:2, :])


def _prenorm_fwd(X, g, sc2, sh2, n_ctx, tl=256):
    T, Dm = X.shape
    nct = n_ctx // tl

    def body(x_ref, g_ref, sc_ref, sh_ref, o_ref):
        is_ctx = pl.program_id(0) < nct
        o_ref[...] = _prenorm_f(x_ref[...], g_ref[...], _pick(is_ctx, sc_ref),
                                _pick(is_ctx, sh_ref)).astype(BF16)

    row = pl.BlockSpec((tl, Dm), lambda i: (i, 0))
    return _call(body, name="prenorm_fwd", out_shape=_sds((T, Dm), BF16), grid=(T // tl,),
                 in_specs=[row, pl.BlockSpec((1, Dm), lambda i: (0, 0)),
                           pl.BlockSpec((2, Dm), lambda i: (0, 0)), pl.BlockSpec((2, Dm), lambda i: (0, 0))],
                 out_specs=row, sem=("parallel",), vmem_mb=40)(X, g, sc2, sh2)


def _prenorm_bwd(X, g, sc2, sh2, d1, d2, dres, n_ctx, tl=256):
    T, Dm = X.shape
    nct = n_ctx // tl

    def body(x_ref, g_ref, sc_ref, sh_ref, d1_ref, d2_ref, dres_ref, dx_ref, acc_ref):
        i = pl.program_id(0)
        is_ctx = i < nct

        @pl.when(i == 0)
        def _():
            acc_ref[...] = jnp.zeros_like(acc_ref)

        _, vjp = jax.vjp(_prenorm_f, x_ref[...], g_ref[...], _pick(is_ctx, sc_ref), _pick(is_ctx, sh_ref))
        dx, dg, dsc, dsh = vjp(d1_ref[...] + d2_ref[...])
        dx_ref[...] = dres_ref[...] + dx
        zero = jnp.zeros_like(dsc)
        acc_ref[0:1, :] += dg
        acc_ref[1:2, :] += jnp.where(is_ctx, dsc, zero)
        acc_ref[2:3, :] += jnp.where(is_ctx, zero, dsc)
        acc_ref[3:4, :] += jnp.where(is_ctx, dsh, zero)
        acc_ref[4:5, :] += jnp.where(is_ctx, zero, dsh)

    row = pl.BlockSpec((tl, Dm), lambda i: (i, 0))
    return _call(body, name="prenorm_bwd", out_shape=(_sds((T, Dm), F32), _sds((8, Dm), F32)), grid=(T // tl,),
                 in_specs=[row, pl.BlockSpec((1, Dm), lambda i: (0, 0)),
                           pl.BlockSpec((2, Dm), lambda i: (0, 0)), pl.BlockSpec((2, Dm), lambda i: (0, 0)),
                           row, row, row],
                 out_specs=(row, pl.BlockSpec((8, Dm), lambda i: (0, 0))), sem=("arbitrary",),
                 vmem_mb=48)(X, g, sc2, sh2, d1, d2, dres)


def _post_f(o, g, gate):
    return gate * ((o * _rms(o)) * g)


def _post_fwd(X, o, g, gate2, n_ctx, tl=256):
    T, Dm = X.shape
    nct = n_ctx // tl

    def body(x_ref, o_ref, g_ref, gate_ref, y_ref):
        is_ctx = pl.program_id(0) < nct
        y_ref[...] = x_ref[...] + _post_f(o_ref[...], g_ref[...], _pick(is_ctx, gate_ref))

    row = pl.BlockSpec((tl, Dm), lambda i: (i, 0))
    return _call(body, name="post_fwd", out_shape=_sds((T, Dm), F32), grid=(T // tl,),
                 in_specs=[row, row, pl.BlockSpec((1, Dm), lambda i: (0, 0)), pl.BlockSpec((2, Dm), lambda i: (0, 0))],
                 out_specs=row, sem=("parallel",), vmem_mb=40)(X, o, g, gate2)


def _post_bwd(o, g, gate2, dX, n_ctx, tl=256):
    T, Dm = o.shape
    nct = n_ctx // tl

    def body(o_ref, g_ref, gate_ref, dx_ref, do_ref, acc_ref):
        i = pl.program_id(0)
        is_ctx = i < nct

        @pl.when(i == 0)
        def _():
            acc_ref[...] = jnp.zeros_like(acc_ref)

        _, vjp = jax.vjp(_post_f, o_ref[...], g_ref[...], _pick(is_ctx, gate_ref))
        do, dg, dgate = vjp(dx_ref[...])
        do_ref[...] = do.astype(BF16)
        zero = jnp.zeros_like(dgate)
        acc_ref[0:1, :] += dg
        acc_ref[1:2, :] += jnp.where(is_ctx, dgate, zero)
        acc_ref[2:3, :] += jnp.where(is_ctx, zero, dgate)

    row = pl.BlockSpec((tl, Dm), lambda i: (i, 0))
    return _call(body, name="post_bwd", out_shape=(_sds((T, Dm), BF16), _sds((8, Dm), F32)), grid=(T // tl,),
                 in_specs=[row, pl.BlockSpec((1, Dm), lambda i: (0, 0)), pl.BlockSpec((2, Dm), lambda i: (0, 0)), row],
                 out_specs=(row, pl.BlockSpec((8, Dm), lambda i: (0, 0))), sem=("arbitrary",),
                 vmem_mb=48)(o, g, gate2, dX)


def _loss_kernel(X, target, n_ctx, tl=256):
    T, Dm = X.shape
    nct = n_ctx // tl

    def body(x_ref, t_ref, dx_ref, acc_ref):
        i = pl.program_id(0)

        @pl.when(i == 0)
        def _():
            acc_ref[...] = jnp.zeros_like(acc_ref)

        @pl.when(i < nct)
        def _():
            dx_ref[...] = jnp.zeros_like(dx_ref)

        @pl.when(i >= nct)
        def _():
            d = x_ref[...] - t_ref[...]
            dx_ref[...] = d * (1.0 / Dm)
            acc_ref[...] += jnp.sum(d * d, axis=0, keepdims=True)

    row = pl.BlockSpec((tl, Dm), lambda i: (i, 0))
    trow = pl.BlockSpec((tl, Dm), lambda i: (jnp.maximum(i - nct, 0), 0))
    return _call(body, name="loss", out_shape=(_sds((T, Dm), F32), _sds((1, Dm), F32)), grid=(T // tl,),
                 in_specs=[row, trow], out_specs=(row, pl.BlockSpec((1, Dm), lambda i: (0, 0))),
                 sem=("arbitrary",), vmem_mb=40)(X, target)


CONV_TL = 256
CONV_CB = 512


def _conv_taps(x, pos, row_len, reverse):
    tl = x.shape[0]
    taps = []
    for k in range(CONV_W):
        off = (2 - k) if reverse else (k - 2)
        xs = x if off == 0 else pltpu.roll(x, (-off) % tl, 0)
        valid = jnp.logical_and(pos + off >= 0, pos + off < row_len)
        taps.append(jnp.where(valid, xs, 0.0))
    return taps


def _conv_pos(i, n_ctx, tl):
    row_len = jnp.where(i < n_ctx // tl, n_ctx, GRID_W)
    pos = jnp.bitwise_and(lax.broadcasted_iota(jnp.int32, (tl, 1), 0), row_len - 1)
    return pos, row_len


def _conv_fwd(z, cw, cb, n_ctx):
    T = z.shape[0]
    tl, cbw = CONV_TL, CONV_CB
    assert n_ctx == tl

    def body(z_ref, w_ref, b_ref, o_ref):
        pos, row_len = _conv_pos(pl.program_id(1), n_ctx, tl)
        taps = _conv_taps(z_ref[...].astype(F32), pos, row_len, False)
        pre = b_ref[...] + taps[0] * w_ref[0:1, :]
        for k in range(1, CONV_W):
            pre = pre + taps[k] * w_ref[k:k + 1, :]
        o_ref[...] = _silu(pre).astype(BF16)

    blk = pl.BlockSpec((tl, cbw), lambda j, i: (i, j))
    return _call(body, name="conv_fwd", out_shape=_sds((T, XBC_W), BF16), grid=(XBC_W // cbw, T // tl),
                 in_specs=[blk, pl.BlockSpec((8, cbw), lambda j, i: (0, j)), pl.BlockSpec((1, cbw), lambda j, i: (0, j))],
                 out_specs=blk, sem=("parallel", "parallel"))(z, cw, cb)


def _conv_bwd(z, cw, cb, dxbc, dz_in, n_ctx):
    T = z.shape[0]
    tl, cbw = CONV_TL, CONV_CB

    def body(z_ref, w_ref, b_ref, d_ref, dzin_ref, dz_ref, dw_ref, db_ref):
        i = pl.program_id(1)

        @pl.when(i == 0)
        def _():
            dw_ref[...] = jnp.zeros_like(dw_ref)
            db_ref[...] = jnp.zeros_like(db_ref)

        pos, row_len = _conv_pos(i, n_ctx, tl)
        taps = _conv_taps(z_ref[...].astype(F32), pos, row_len, False)
        pre = b_ref[...] + taps[0] * w_ref[0:1, :]
        for k in range(1, CONV_W):
            pre = pre + taps[k] * w_ref[k:k + 1, :]
        s = _sigmoid(pre)
        dpre = d_ref[...].astype(F32) * (s + pre * s * (1.0 - s))
        dtaps = _conv_taps(dpre, pos, row_len, True)
        dx = dtaps[0] * w_ref[0:1, :]
        for k in range(1, CONV_W):
            dx = dx + dtaps[k] * w_ref[k:k + 1, :]
        dz_ref[...] = dx.astype(BF16)
        for k in range(CONV_W):
            dw_ref[k:k + 1, :] += jnp.sum(dpre * taps[k], axis=0, keepdims=True)
        db_ref[0:1, :] += jnp.sum(dpre, axis=0, keepdims=True)

    blk = pl.BlockSpec((tl, cbw), lambda j, i: (i, j))
    par = pl.BlockSpec((8, cbw), lambda j, i: (0, j))
    return _call(body, name="conv_bwd",
                 out_shape=(_sds(dz_in.shape, BF16), _sds((8, XBC_W), F32), _sds((8, XBC_W), F32)),
                 grid=(XBC_W // cbw, T // tl),
                 in_specs=[blk, par, pl.BlockSpec((1, cbw), lambda j, i: (0, j)), blk, ANY],
                 out_specs=(blk, par, par), sem=("parallel", "arbitrary"), aliases={4: 0})(z, cw, cb, dxbc, dz_in)


MIX_TL = 128


def _gate_f(yf, yb, zs, g):
    t = (yf + yb) * _silu(zs)
    return t * _rms(t) * g


def _gate_fwd(yf, yb, z, g):
    T = yf.shape[0]
    tl = MIX_TL

    def body(yf_ref, yb_ref, zs_ref, g_ref, o_ref):
        o_ref[...] = _gate_f(yf_ref[...].astype(F32), yb_ref[...].astype(F32), zs_ref[...].astype(F32),
                             g_ref[...]).astype(BF16)

    row = pl.BlockSpec((tl, SSD_W), lambda i: (i, 0))
    return _call(body, name="gate_fwd", out_shape=_sds((T, SSD_W + MLP_W), BF16), grid=(T // tl,),
                 in_specs=[row, row, pl.BlockSpec((tl, SSD_W), lambda i: (i, XBC_W // SSD_W)),
                           pl.BlockSpec((1, SSD_W), lambda i: (0, 0))],
                 out_specs=row, sem=("parallel",))(yf, yb, z, g)


def _gate_bwd(yf, yb, z, g, dycat, dz_in):
    T = yf.shape[0]
    tl = MIX_TL

    def body(yf_ref, yb_ref, zs_ref, g_ref, d_ref, dzin_ref, dy_ref, dz_ref, acc_ref):
        @pl.when(pl.program_id(0) == 0)
        def _():
            acc_ref[...] = jnp.zeros_like(acc_ref)

        _, vjp = jax.vjp(_gate_f, yf_ref[...].astype(F32), yb_ref[...].astype(F32),
                         zs_ref[...].astype(F32), g_ref[...])
        dyf, _, dzs, dg = vjp(d_ref[...].astype(F32))
        dy_ref[...] = dyf.astype(BF16)
        dz_ref[...] = dzs.astype(BF16)
        acc_ref[0:1, :] += dg

    row = pl.BlockSpec((tl, SSD_W), lambda i: (i, 0))
    zs_spec = pl.BlockSpec((tl, SSD_W), lambda i: (i, XBC_W // SSD_W))
    return _call(body, name="gate_bwd",
                 out_shape=(_sds((T, SSD_W), BF16), _sds(dz_in.shape, BF16), _sds((8, SSD_W), F32)),
                 grid=(T // tl,),
                 in_specs=[row, row, zs_spec, pl.BlockSpec((1, SSD_W), lambda i: (0, 0)), row, ANY],
                 out_specs=(row, zs_spec, pl.BlockSpec((8, SSD_W), lambda i: (0, 0))),
                 sem=("arbitrary",), aliases={5: 1}, vmem_mb=40)(yf, yb, z, g, dycat, dz_in)


def _vnorm_f(v, gv):
    return v * _rms(v) * gv


def _mlp_out_f(u, sg, zm, gm):
    t = u * sg * _silu(zm)
    return t * _rms(t) * gm


U_BLK = (XBC_W + SSD_W) // MLP_W


def _mlp_mix(ws_ref, bst_ref, vn_s, sg_s):
    for gi in range(MLP_GROUPS):
        cols = pl.ds(gi * LANES, LANES)
        sg_s[:, cols] = _dot(ws_ref[gi], vn_s[:, cols], NN) + bst_ref[:, gi:gi + 1]


def _mlp_fwd(z, gv, ws, bst, gm, ycat_in):
    T = z.shape[0]
    tl = CHUNK

    def body(u_ref, v_ref, zm_ref, gv_ref, ws_ref, bst_ref, gm_ref, yin_ref, o_ref, vn_s, sg_s):
        vn_s[...] = _vnorm_f(v_ref[...].astype(F32), gv_ref[...]).astype(BF16)
        _mlp_mix(ws_ref, bst_ref, vn_s, sg_s)
        o_ref[...] = _mlp_out_f(u_ref[...].astype(F32), sg_s[...], zm_ref[...].astype(F32),
                                gm_ref[...]).astype(BF16)

    def zblk(b):
        return pl.BlockSpec((tl, MLP_W), lambda i: (i, b))

    vec = pl.BlockSpec((1, MLP_W), lambda i: (0, 0))
    return _call(body, name="mlp_fwd", out_shape=_sds(ycat_in.shape, BF16), grid=(T // tl,),
                 in_specs=[zblk(U_BLK), zblk(U_BLK + 1), zblk(U_BLK + 2), vec,
                           pl.BlockSpec((MLP_GROUPS, CHUNK, CHUNK), lambda i: (0, 0, 0)),
                           pl.BlockSpec((CHUNK, LANES), lambda i: (0, 0)), vec, ANY],
                 out_specs=pl.BlockSpec((tl, MLP_W), lambda i: (i, 1)),
                 scratch=[pltpu.VMEM((tl, MLP_W), BF16), pltpu.VMEM((tl, MLP_W), F32)],
                 sem=("parallel",), aliases={7: 0})(z, z, z, gv, ws, bst, gm, ycat_in)


def _mlp_bwd(z, gv, ws, bst, gm, dycat, dz_in):
    T = z.shape[0]
    tl = CHUNK

    def body(u_ref, v_ref, zm_ref, gv_ref, ws_ref, bst_ref, gm_ref, d_ref, dzin_ref,
             dz_ref, acc_ref, dws_ref, dbst_ref, vn_s, sg_s, dvn_s):
        @pl.when(pl.program_id(0) == 0)
        def _():
            acc_ref[...] = jnp.zeros_like(acc_ref)
            dws_ref[...] = jnp.zeros_like(dws_ref)
            dbst_ref[...] = jnp.zeros_like(dbst_ref)

        v = v_ref[...].astype(F32)
        vn, vjp_v = jax.vjp(_vnorm_f, v, gv_ref[...])
        vn_s[...] = vn.astype(BF16)
        _mlp_mix(ws_ref, bst_ref, vn_s, sg_s)
        _, vjp_o = jax.vjp(_mlp_out_f, u_ref[...].astype(F32), sg_s[...], zm_ref[...].astype(F32), gm_ref[...])
        du, dsg, dzm, dgm = vjp_o(d_ref[...].astype(F32))
        sg_s[...] = dsg
        for gi in range(MLP_GROUPS):
            cols = pl.ds(gi * LANES, LANES)
            dsg_g = sg_s[:, cols]
            dvn_s[:, cols] = _dot(ws_ref[gi], dsg_g, TN)
            dws_ref[gi] += _dot(dsg_g, vn_s[:, cols], NT)
            dbst_ref[:, gi:gi + 1] += jnp.sum(dsg_g, axis=1, keepdims=True)
        dv, dgv = vjp_v(dvn_s[...])
        dz_ref[:, 0:MLP_W] = du.astype(BF16)
        dz_ref[:, MLP_W:2 * MLP_W] = dv.astype(BF16)
        dz_ref[:, 2 * MLP_W:3 * MLP_W] = dzm.astype(BF16)
        acc_ref[0:1, :] += dgv
        acc_ref[1:2, :] += dgm

    def zblk(b):
        return pl.BlockSpec((tl, MLP_W), lambda i: (i, b))

    vec = pl.BlockSpec((1, MLP_W), lambda i: (0, 0))
    ws_spec = pl.BlockSpec((MLP_GROUPS, CHUNK, CHUNK), lambda i: (0, 0, 0))
    bst_spec = pl.BlockSpec((CHUNK, LANES), lambda i: (0, 0))
    return _call(body, name="mlp_bwd",
                 out_shape=(_sds(dz_in.shape, BF16), _sds((8, MLP_W), F32),
                            _sds((MLP_GROUPS, CHUNK, CHUNK), F32), _sds((CHUNK, LANES), F32)),
                 grid=(T // tl,),
                 in_specs=[zblk(U_BLK), zblk(U_BLK + 1), zblk(U_BLK + 2), vec, ws_spec, bst_spec, vec,
                           pl.BlockSpec((tl, MLP_W), lambda i: (i, 1)), ANY],
                 out_specs=(pl.BlockSpec((tl, 3 * MLP_W), lambda i: (i, 1)),
                            pl.BlockSpec((8, MLP_W), lambda i: (0, 0)), ws_spec, bst_spec),
                 scratch=[pltpu.VMEM((tl, MLP_W), BF16), pltpu.VMEM((tl, MLP_W), F32), pltpu.VMEM((tl, MLP_W), F32)],
                 sem=("arbitrary",), aliases={8: 0}, vmem_mb=48)(z, z, z, gv, ws, bst, gm, dycat, dz_in)


def _ssd_constants(direction):
    q = CHUNK
    tri = np.tril(np.ones((q, q), np.float32))
    if direction == 1:
        tri = tri.T
    eye = np.eye(q, dtype=np.float32)
    ex = np.zeros((SSD_GROUPS, LANES, GROUP_COLS), np.float32)
    for g in range(SSD_GROUPS):
        for r in range(HEADS_PER_GROUP):
            ex[g, direction * SSD_HEADS + HEADS_PER_GROUP * g + r, r * SSD_HEAD_DIM:(r + 1) * SSD_HEAD_DIM] = 1.0
    hm = np.zeros((8, GROUP_COLS), np.float32)
    for r in range(HEADS_PER_GROUP):
        hm[r, r * SSD_HEAD_DIM:(r + 1) * SSD_HEAD_DIM] = 1.0
    return (jnp.asarray(tri), jnp.asarray(np.ascontiguousarray(tri.T)), jnp.asarray(eye), jnp.asarray(ex),
            jnp.asarray(np.ascontiguousarray(ex.transpose(0, 2, 1))), jnp.asarray(hm))


def _chunk_order(direction, nc, nctx):
    if direction == 0:
        return lambda c: c
    return lambda c: jnp.where(c < nctx, nctx - 1 - c, nc - 1 + nctx - c)


def _bcast8(row):
    return jnp.broadcast_to(row, (8, row.shape[1]))


def _ssd_common(x_ref, b_ref, c_ref, raw_ref, prm_ref, tri_ref, ex_ref, ext_ref, direction):
    g = pl.program_id(1)
    base = direction * SSD_HEADS + HEADS_PER_GROUP * g
    x = x_ref[...].astype(F32)
    pre = raw_ref[...] + prm_ref[0:1, :]
    dt = _softplus(pre)
    A = -jnp.exp(prm_ref[1:2, :])
    a = dt * A
    s = _dot_hi(tri_ref[...], a)
    tot = jnp.sum(a, axis=0, keepdims=True)
    ex = ex_ref[0]
    ext = ext_ref[0]
    dt_x = _dot_hi(dt, ex)
    s_x = _dot_hi(s, ex)
    tot_x = _dot_hi(_bcast8(tot), ex)[0:1, :]
    tot_c = jnp.sum(ext * tot, axis=1, keepdims=True)
    lane = lax.broadcasted_iota(jnp.int32, (1, LANES), 1)
    return dict(base=base, x=x, pre=pre, dt=dt, A=A, s=s, ex=ex, ext=ext, dt_x=dt_x, lane=lane,
                X=x * dt_x, E_x=jnp.exp(s_x), D_x=jnp.exp(tot_x - s_x), Etot_c=jnp.exp(tot_c),
                Bm=b_ref[...], Cm=c_ref[...])


def _head_decay(q, r, tri_ref, eye_ref):
    sel = q["lane"] == q["base"] + r
    col = jnp.sum(jnp.where(sel, q["s"], 0.0), axis=1, keepdims=True)
    row = jnp.sum(col * eye_ref[...], axis=0, keepdims=True)
    return sel, jnp.exp(jnp.where(tri_ref[...] > 0.5, col - row, NEG_BIG))


def _ssd_fwd(xbc, raw, prm, direction, n_ctx):
    T = xbc.shape[0]
    nc, nctx = T // CHUNK, n_ctx // CHUNK
    order = _chunk_order(direction, nc, nctx)
    tri, tri_t, eye, ex, ext, hm = _ssd_constants(direction)
    Q, GC, N, G = CHUNK, GROUP_COLS, SSD_STATE, SSD_GROUPS

    def body(x_ref, b_ref, c_ref, raw_ref, prm_ref, tri_ref, eye_ref, ex_ref, ext_ref, hm_ref,
             y_ref, sst_ref, S):
        c, g = pl.program_id(0), pl.program_id(1)

        @pl.when(c == 0)
        def _():
            S[g] = jnp.zeros((GC, N), F32)

        S0 = S[g]
        sst_ref[0, 0] = S0
        q = _ssd_common(x_ref, b_ref, c_ref, raw_ref, prm_ref, tri_ref, ex_ref, ext_ref, direction)
        X, Bm, Cm = q["X"], q["Bm"], q["Cm"]
        y = q["E_x"] * _dot(Cm, S0, NT)
        Gm = _dot(Cm, Bm, NT)
        for r in range(HEADS_PER_GROUP):
            _, Lm = _head_decay(q, r, tri_ref, eye_ref)
            y = y + _dot(Gm * Lm, X * hm_ref[r:r + 1, :], NN)
        if direction == 0:
            y = y + _dot_hi(_bcast8(prm_ref[2:3, :]), q["ex"])[0:1, :] * q["x"]
        y_ref[...] = y.astype(BF16)
        S[g] = q["Etot_c"] * S0 + _dot(X * q["D_x"], Bm, TN)

    cst2 = lambda shape: pl.BlockSpec(shape, lambda c, g: (0, 0))
    in_specs = [
        pl.BlockSpec((Q, GC), lambda c, g: (order(c), g)),
        pl.BlockSpec((Q, N), lambda c, g: (order(c), SSD_W // N + g)),
        pl.BlockSpec((Q, N), lambda c, g: (order(c), (SSD_W + G * N) // N + g)),
        pl.BlockSpec((Q, LANES), lambda c, g: (order(c), 0)),
        cst2((8, LANES)), cst2((Q, Q)), cst2((Q, Q)),
        pl.BlockSpec((1, LANES, GC), lambda c, g: (g, 0, 0)),
        pl.BlockSpec((1, GC, LANES), lambda c, g: (g, 0, 0)),
        cst2((8, GC)),
    ]
    out_specs = (pl.BlockSpec((Q, GC), lambda c, g: (order(c), g)),
                 pl.BlockSpec((1, 1, GC, N), lambda c, g: (order(c), g, 0, 0)))
    return _call(body, name=f"ssd_fwd_{direction}",
                 out_shape=(_sds((T, SSD_W), BF16), _sds((nc, G, GC, N), F32)),
                 grid=(nc, G), in_specs=in_specs, out_specs=out_specs,
                 scratch=[pltpu.VMEM((G, GC, N), F32)], sem=("arbitrary", "arbitrary"),
                 )(xbc, xbc, xbc, raw, prm, tri, eye, ex, ext, hm)


def _ssd_bwd(xbc, raw, prm, sst, dy, direction, n_ctx):
    T = xbc.shape[0]
    nc, nctx = T // CHUNK, n_ctx // CHUNK
    fwd_order = _chunk_order(direction, nc, nctx)
    order = lambda c: fwd_order(nc - 1 - c)
    tri, tri_t, eye, ex, ext, hm = _ssd_constants(direction)
    Q, GC, N, G = CHUNK, GROUP_COLS, SSD_STATE, SSD_GROUPS

    def body(x_ref, b_ref, c_ref, raw_ref, prm_ref, sst_ref, dy_ref, tri_ref, trit_ref, eye_ref, ex_ref,
             ext_ref, hm_ref, dx_ref, db_ref, dc_ref, draw_ref, dprm_ref, dS):
        c, g = pl.program_id(0), pl.program_id(1)

        @pl.when(c == 0)
        def _():
            dS[g] = jnp.zeros((GC, N), F32)

        @pl.when(jnp.logical_and(c == 0, g == 0))
        def _():
            dprm_ref[...] = jnp.zeros_like(dprm_ref)

        @pl.when(g == 0)
        def _():
            draw_ref[...] = jnp.zeros_like(draw_ref)

        dS1 = dS[g]
        S0 = sst_ref[0, 0]
        q = _ssd_common(x_ref, b_ref, c_ref, raw_ref, prm_ref, tri_ref, ex_ref, ext_ref, direction)
        x, X, Bm, Cm, ext = q["x"], q["X"], q["Bm"], q["Cm"], q["ext"]
        E_x, D_x, Etot_c = q["E_x"], q["D_x"], q["Etot_c"]
        dY = dy_ref[...].astype(F32)

        CS = _dot(Cm, S0, NT)
        dCS = dY * E_x
        dC = _dot(dCS, S0, NN)
        dS0 = Etot_c * dS1 + _dot(dCS, Cm, TN)
        ds_x = dY * (E_x * CS)
        dtot_c = jnp.sum(dS1 * S0, axis=1, keepdims=True) * Etot_c
        dtot = jnp.sum(dtot_c * ext, axis=0, keepdims=True)
        XD = X * D_x
        dXD = _dot(Bm, dS1, NT)
        dB = _dot(XD, dS1, NN)
        dX = dXD * D_x
        t = dXD * XD
        ds_x = ds_x - t
        dtot_x = jnp.sum(t, axis=0, keepdims=True)
        Gm = _dot(Cm, Bm, NT)
        dG = jnp.zeros((Q, Q), F32)
        ds = jnp.zeros((Q, LANES), F32)
        for r in range(HEADS_PER_GROUP):
            hmr = hm_ref[r:r + 1, :]
            sel, Lm = _head_decay(q, r, tri_ref, eye_ref)
            W = Gm * Lm
            dW = _dot(dY * hmr, X, NT)
            dX = dX + _dot(W, dY, TN) * hmr
            dG = dG + dW * Lm
            Z = dW * W
            rs = jnp.sum(Z, axis=1, keepdims=True)
            cs = jnp.sum(Z, axis=0, keepdims=True)
            cs_col = jnp.sum(cs * eye_ref[...], axis=1, keepdims=True)
            ds = ds + jnp.where(sel, rs - cs_col, 0.0)
        dC = dC + _dot(dG, Bm, NN)
        dB = dB + _dot(dG, Cm, TN)
        ds = ds + _dot_hi(ds_x, ext)
        dtot = dtot + _dot_hi(_bcast8(dtot_x), ext)[0:1, :]
        da = _dot_hi(trit_ref[...], ds) + dtot
        ddt = da * q["A"] + _dot_hi(dX * x, ext)
        dA = jnp.sum(da * q["dt"], axis=0, keepdims=True)
        dx = dX * q["dt_x"]
        if direction == 0:
            dsk_x = _dot_hi(_bcast8(prm_ref[2:3, :]), q["ex"])[0:1, :]
            dx = dx + dY * dsk_x
            dprm_ref[2:3, :] += jnp.sum(_dot_hi(dY * x, ext), axis=0, keepdims=True)
        draw = ddt * _sigmoid(q["pre"])
        dx_ref[...] = dx.astype(BF16)
        db_ref[...] = dB.astype(BF16)
        dc_ref[...] = dC.astype(BF16)
        draw_ref[...] += draw
        dprm_ref[0:1, :] += jnp.sum(draw, axis=0, keepdims=True)
        dprm_ref[1:2, :] += dA * q["A"]
        dS[g] = dS0

    cst2 = lambda shape: pl.BlockSpec(shape, lambda c, g: (0, 0))
    in_specs = [
        pl.BlockSpec((Q, GC), lambda c, g: (order(c), g)),
        pl.BlockSpec((Q, N), lambda c, g: (order(c), SSD_W // N + g)),
        pl.BlockSpec((Q, N), lambda c, g: (order(c), (SSD_W + G * N) // N + g)),
        pl.BlockSpec((Q, LANES), lambda c, g: (order(c), 0)),
        cst2((8, LANES)),
        pl.BlockSpec((1, 1, GC, N), lambda c, g: (order(c), g, 0, 0)),
        pl.BlockSpec((Q, GC), lambda c, g: (order(c), g)),
        cst2((Q, Q)), cst2((Q, Q)), cst2((Q, Q)),
        pl.BlockSpec((1, LANES, GC), lambda c, g: (g, 0, 0)),
        pl.BlockSpec((1, GC, LANES), lambda c, g: (g, 0, 0)),
        cst2((8, GC)),
    ]
    out_specs = (pl.BlockSpec((Q, GC), lambda c, g: (order(c), g)),
                 pl.BlockSpec((Q, N), lambda c, g: (order(c), g)),
                 pl.BlockSpec((Q, N), lambda c, g: (order(c), g)),
                 pl.BlockSpec((Q, LANES), lambda c, g: (order(c), 0)),
                 cst2((8, LANES)))
    return _call(body, name=f"ssd_bwd_{direction}",
                 out_shape=(_sds((T, SSD_W), BF16), _sds((T, G * N), BF16), _sds((T, G * N), BF16),
                            _sds((T, LANES), F32), _sds((8, LANES), F32)),
                 grid=(nc, G), in_specs=in_specs, out_specs=out_specs,
                 scratch=[pltpu.VMEM((G, GC, N), F32)], sem=("arbitrary", "arbitrary"),
                 )(xbc, xbc, xbc, raw, prm, sst, dy, tri, tri_t, eye, ex, ext, hm)


ADA_ROWS = 16


def _ada_fwd(c16, w_ada, b_loc):
    depth, Dm, W = w_ada.shape

    def body(c_ref, w_ref, b_ref, o_ref):
        o_ref[0] = _dot_hi(_silu(c_ref[...]), w_ref[0]) + b_ref[0]

    return _call(body, name="ada_fwd", out_shape=_sds((depth, ADA_ROWS, W), F32), grid=(depth,),
                 in_specs=[pl.BlockSpec((ADA_ROWS, Dm), lambda l: (0, 0)),
                           pl.BlockSpec((1, Dm, W), lambda l: (l, 0, 0)),
                           pl.BlockSpec((1, 1, W), lambda l: (l, 0, 0))],
                 out_specs=pl.BlockSpec((1, ADA_ROWS, W), lambda l: (l, 0, 0)),
                 sem=("parallel",), vmem_mb=40)(c16, w_ada, b_loc)


def _ada_bwd(c16, w_ada, dmod):
    depth, Dm, W = w_ada.shape

    def body(c_ref, w_ref, d_ref, gw_ref, dc_ref):
        l = pl.program_id(0)

        @pl.when(l == 0)
        def _():
            dc_ref[...] = jnp.zeros_like(dc_ref)

        cc = c_ref[...]
        sg = _sigmoid(cc)
        gw_ref[0] = _dot_hi(cc * sg, d_ref[0], TN)
        dsc = _dot_hi(d_ref[0], w_ref[0], NT)
        dc_ref[...] += dsc[8:16, :] * (sg + cc * sg * (1.0 - sg))[8:16, :]

    return _call(body, name="ada_bwd", out_shape=(_sds((depth, Dm, W), F32), _sds((8, Dm), F32)), grid=(depth,),
                 in_specs=[pl.BlockSpec((ADA_ROWS, Dm), lambda l: (0, 0)),
                           pl.BlockSpec((1, Dm, W), lambda l: (l, 0, 0)),
                           pl.BlockSpec((1, ADA_ROWS, W), lambda l: (l, 0, 0))],
                 out_specs=(pl.BlockSpec((1, Dm, W), lambda l: (l, 0, 0)), pl.BlockSpec((8, Dm), lambda l: (0, 0))),
                 sem=("arbitrary",), vmem_mb=48)(c16, w_ada, dmod)


def _sum_slabs(g, name, tr=512):
    _, R, C = g.shape
    tr = min(tr, R)

    def body(g_ref, o_ref):
        acc = g_ref[0]
        for k in range(1, N_DEV):
            acc = acc + g_ref[k]
        o_ref[...] = acc

    return _call(body, name=name, out_shape=_sds((R, C), F32), grid=(R // tr,),
                 in_specs=[pl.BlockSpec((N_DEV, tr, C), lambda i: (0, i, 0))],
                 out_specs=pl.BlockSpec((tr, C), lambda i: (i, 0)), sem=("parallel",), vmem_mb=40)(g)


def _adamw_math(w, g, m, v):
    m = ADAM_B1 * m + (1.0 - ADAM_B1) * g
    v = ADAM_B2 * v + (1.0 - ADAM_B2) * (g * g)
    m_hat = m / (1.0 - ADAM_B1 ** ADAM_STEP)
    v_hat = v / (1.0 - ADAM_B2 ** ADAM_STEP)
    delta = -ADAM_LR * (m_hat / (jnp.sqrt(v_hat) + ADAM_EPS) + ADAM_WD * w)
    return delta, m, v


def _adamw(w, g, m, v, name, tr):
    R, C = w.shape
    tr = min(tr, R)
    assert R % tr == 0

    def body(w_ref, g_ref, m_ref, v_ref, d_ref, nm_ref, nv_ref):
        d, nm, nv = _adamw_math(w_ref[...], g_ref[...], m_ref[...], v_ref[...])
        d_ref[...] = d
        nm_ref[...] = nm
        nv_ref[...] = nv

    blk = pl.BlockSpec((tr, C), lambda i: (i, 0))
    out = _sds((R, C), F32)
    return _call(body, name=name, out_shape=(out, out, out), grid=(R // tr,), in_specs=[blk] * 4,
                 out_specs=(blk, blk, blk), sem=("parallel",), vmem_mb=40)(w, g, m, v)


def _adamw_slabs(w, slabs, m, v, name, tr):
    R, C = w.shape
    assert R % tr == 0

    def body(w_ref, s_ref, m_ref, v_ref, g_ref, d_ref, nm_ref, nv_ref):
        g = s_ref[0].astype(F32)
        for k in range(1, N_DEV):
            g = g + s_ref[k].astype(F32)
        d, nm, nv = _adamw_math(w_ref[...], g, m_ref[...], v_ref[...])
        g_ref[...] = g
        d_ref[...] = d
        nm_ref[...] = nm
        nv_ref[...] = nv

    blk = pl.BlockSpec((tr, C), lambda i: (i, 0))
    out = _sds((R, C), F32)
    return _call(body, name=name, out_shape=(out, out, out, out), grid=(R // tr,),
                 in_specs=[blk, pl.BlockSpec((N_DEV, tr, C), lambda i: (0, i, 0)), blk, blk],
                 out_specs=(blk, blk, blk, blk), sem=("parallel",), vmem_mb=48)(w, slabs, m, v)


PACK_QUANTUM = 512 * LANES


def _pack(arrays):
    flat = jnp.concatenate([a.reshape(-1).astype(F32) for a in arrays])
    pad = (-flat.shape[0]) % PACK_QUANTUM
    return jnp.pad(flat, (0, pad)).reshape(-1, LANES)


def _unpack(bundle, shapes):
    flat = bundle.reshape(-1)
    out, off = [], 0
    for shp in shapes:
        n = int(np.prod(shp))
        out.append(flat[off:off + n].reshape(shp))
        off += n
    return out


def _row(v):
    return v.reshape(1, -1)


def _pad_rows(a, rows):
    return jnp.pad(a, ((0, rows - a.shape[0]), (0, 0)))


def kernel(x, c, ctx, c_ctx, w_ada, b_ada, g_pre, g_post, w_in, conv_w, conv_b, dt_bias, a_log, d_skip, g_ssd, g_v, w_s, b_s, g_mlp, w_out, loss_target, m_c_ctx, m_w_ada, m_b_ada, m_g_pre, m_g_post, m_w_in, m_conv_w, m_conv_b, m_dt_bias, m_a_log, m_d_skip, m_g_ssd, m_g_v, m_w_s, m_b_s, m_g_mlp, m_w_out, v_c_ctx, v_w_ada, v_b_ada, v_g_pre, v_g_post, v_w_in, v_conv_w, v_conv_b, v_dt_bias, v_a_log, v_d_skip, v_g_ssd, v_g_v, v_w_s, v_b_s, v_g_mlp, v_w_out):
    depth = w_in.shape[0]
    L = x.shape[1]
    n_ctx = ctx.shape[1]
    T = n_ctx + L
    Dm = D_MODEL
    me = _lin(_my_pos())
    ada_w = w_ada.shape[2]
    in_w = w_in.shape[2]
    out_r = w_out.shape[1]
    conv_c = conv_w.shape[2]
    TM = 768

    c_all = _all_gather_small(_pad_rows(c, 8), "gather_c")[:, 0, :]
    c16 = _pad_rows(jnp.concatenate([c_all, _row(c_ctx)], axis=0), ADA_ROWS)
    b_loc = lax.dynamic_slice_in_dim(b_ada, me * ada_w, ada_w, axis=1)[:, None, :]
    mod_loc = _ada_fwd(c16, w_ada, b_loc)
    mod_all = _all_gather_small(mod_loc.reshape(depth * ADA_ROWS, ada_w), "gather_mod")
    mod_all = mod_all.reshape(N_DEV, depth, ADA_ROWS, ada_w)
    mod_me = lax.dynamic_index_in_dim(mod_all, me, axis=2, keepdims=False)
    mod_me = jnp.transpose(mod_me, (1, 0, 2)).reshape(depth, N_DEV * ada_w)
    mod_cx = jnp.transpose(mod_all[:, :, 8, :], (1, 0, 2)).reshape(depth, N_DEV * ada_w)
    shift2 = jnp.stack([mod_cx[:, 0:Dm], mod_me[:, 0:Dm]], axis=1)
    scale2 = jnp.stack([mod_cx[:, Dm:2 * Dm], mod_me[:, Dm:2 * Dm]], axis=1)
    gate2 = jnp.stack([mod_cx[:, 2 * Dm:], mod_me[:, 2 * Dm:]], axis=1)

    w_in_all = _all_gather_big(w_in.astype(BF16), "gather_w_in")
    w_out_all = _all_gather_big(w_out.astype(BF16), "gather_w_out")
    conv_all = _all_gather_small(_pad_rows(conv_w.reshape(depth * CONV_W, conv_c), 24).reshape(24, conv_c),
                                 "gather_conv_w")
    conv_full = jnp.transpose(conv_all[:, :depth * CONV_W, :], (1, 0, 2)).reshape(depth, CONV_W, XBC_W)

    w_full = jnp.transpose(w_in_all, (1, 2, 0, 3)).reshape(depth, Dm, IN_W)
    w_main = jnp.concatenate([w_full[:, :, :XBC_W], w_full[:, :, XBC_W + DT_W:]], axis=2)
    w_dt = jnp.pad(w_full[:, :, XBC_W:XBC_W + DT_W], ((0, 0), (0, 0), (0, LANES - DT_W)))
    w_o = jnp.transpose(w_out_all, (1, 0, 2, 3)).reshape(depth, N_DEV * out_r, Dm)

    def ssd_prm(l):
        rows = jnp.stack([jnp.pad(dt_bias[l].reshape(-1), (0, LANES - DT_W)),
                          jnp.pad(a_log[l].reshape(-1), (0, LANES - DT_W)),
                          jnp.pad(d_skip[l], (0, LANES - SSD_HEADS))])
        return _pad_rows(rows, 8)

    ws_bf = w_s.astype(BF16)
    bst = jnp.pad(jnp.transpose(b_s, (0, 2, 1)), ((0, 0), (0, 0), (0, LANES - MLP_GROUPS)))

    X = jnp.concatenate([ctx[0], x[0]], axis=0)
    saved = []
    for l in range(depth):
        hx = _prenorm_fwd(X, _row(g_pre[l]), scale2[l], shift2[l], n_ctx)
        z = _mm(hx, w_main[l], "nn", BF16, "in_proj", TM, 512, 2048)
        raw = _mm(hx, w_dt[l], "nn", F32, "dt_proj", TM, LANES, 2048)
        cw = _pad_rows(conv_full[l], 8)
        cb = _row(conv_b[l])
        xbc = _conv_fwd(z, cw, cb, n_ctx)
        prm = ssd_prm(l)
        y_f, sst_f = _ssd_fwd(xbc, raw, prm, 0, n_ctx)
        y_b, sst_b = _ssd_fwd(xbc, raw, prm, 1, n_ctx)
        ycat = _gate_fwd(y_f, y_b, z, _row(g_ssd[l]))
        ycat = _mlp_fwd(z, _row(g_v[l]), ws_bf[l], bst[l], _row(g_mlp[l]), ycat)
        o = _mm(ycat, w_o[l], "nn", F32, "out_proj", TM, 512, 2048)
        saved.append((X, hx, z, raw, cw, cb, xbc, prm, y_f, sst_f, y_b, sst_b, ycat, o))
        X = _post_fwd(X, o, _row(g_post[l]), gate2[l], n_ctx)

    dX, sq = _loss_kernel(X, loss_target[0], n_ctx)
    loss = lax.psum(0.5 * jnp.sum(sq) / Dm, ("x", "y", "c"))

    g_small = {k: [None] * depth for k in
               ("b_ada", "g_pre", "g_post", "conv_b", "dt_bias", "a_log", "d_skip", "g_ssd", "g_v", "w_s", "b_s",
                "g_mlp", "conv_w", "dmod_c", "dmod")}
    gw_main, gw_dt, gw_out = [None] * depth, [None] * depth, [None] * depth
    for l in reversed(range(depth)):
        Xl, hx, z, raw, cw, cb, xbc, prm, y_f, sst_f, y_b, sst_b, ycat, o = saved[l]
        d_o, acc_post = _post_bwd(o, _row(g_post[l]), gate2[l], dX, n_ctx)
        dycat = _mm(d_o, w_o[l], "nt", BF16, "out_proj_dx", TM, 512, 2048)
        gw_out[l] = _mm(ycat, d_o, "tn", F32, "out_proj_dw", 1024, 1024, 768)
        dz = jnp.zeros((T, Z_MAIN), BF16)
        dy, dz, acc_gate = _gate_bwd(y_f, y_b, z, _row(g_ssd[l]), dycat, dz)
        dz, acc_mlp, dws, dbst = _mlp_bwd(z, _row(g_v[l]), ws_bf[l], bst[l], _row(g_mlp[l]), dycat, dz)
        dx0, db0, dc0, draw0, dprm0 = _ssd_bwd(xbc, raw, prm, sst_f, dy, 0, n_ctx)
        dx1, db1, dc1, draw1, dprm1 = _ssd_bwd(xbc, raw, prm, sst_b, dy, 1, n_ctx)
        dxbc = jnp.concatenate([(dx0.astype(F32) + dx1.astype(F32)), (db0.astype(F32) + db1.astype(F32)),
                                (dc0.astype(F32) + dc1.astype(F32))], axis=1).astype(BF16)
        draw = draw0 + draw1
        dprm = dprm0 + dprm1
        dz, dcw, dcb = _conv_bwd(z, cw, cb, dxbc, dz, n_ctx)
        dhx1 = _mm(dz, w_main[l], "nt", F32, "in_proj_dx", TM, 512, 2048)
        dhx2 = _mm(draw, w_dt[l], "nt", F32, "dt_proj_dx", TM, 512, LANES)
        gw_main[l] = _mm(hx, dz, "tn", F32, "in_proj_dw", 1024, 1024, 768)
        gw_dt[l] = _mm(hx, draw, "tn", F32, "dt_proj_dw", 1024, LANES, 768)
        dX, acc_pre = _prenorm_bwd(Xl, _row(g_pre[l]), scale2[l], shift2[l], dhx1, dhx2, dX, n_ctx)

        dmod_c = jnp.concatenate([acc_pre[3], acc_pre[1], acc_post[1]])
        dmod_x = jnp.concatenate([acc_pre[4], acc_pre[2], acc_post[2]])
        g_small["dmod_c"][l] = dmod_c
        g_small["dmod"][l] = dmod_x
        g_small["b_ada"][l] = dmod_c + dmod_x
        g_small["g_pre"][l] = acc_pre[0]
        g_small["g_post"][l] = acc_post[0]
        g_small["conv_b"][l] = dcb[0]
        g_small["conv_w"][l] = dcw[:CONV_W]
        g_small["dt_bias"][l] = dprm[0, :DT_W].reshape(2, SSD_HEADS)
        g_small["a_log"][l] = dprm[1, :DT_W].reshape(2, SSD_HEADS)
        g_small["d_skip"][l] = dprm[2, :SSD_HEADS]
        g_small["g_ssd"][l] = acc_gate[0]
        g_small["g_v"][l] = acc_mlp[0]
        g_small["g_mlp"][l] = acc_mlp[1]
        g_small["w_s"][l] = dws
        g_small["b_s"][l] = jnp.transpose(dbst[:, :MLP_GROUPS])

    grad_x = dX[n_ctx:][None]

    summed_names = ["b_ada", "g_pre", "g_post", "conv_b", "dt_bias", "a_log", "d_skip", "g_ssd", "g_v", "w_s",
                    "b_s", "g_mlp", "conv_w", "dmod_c"]
    parts = [jnp.stack(g_small[k]) for k in summed_names] + [jnp.stack(g_small["dmod"])]
    shapes = [p.shape for p in parts]
    bundle = _pack(parts)
    gathered = _all_gather_big(bundle, "gather_small_grads")
    reduced = _unpack(_sum_slabs(gathered, "sum_small_grads"), shapes)
    gs = dict(zip(summed_names, reduced[:-1]))
    n_el = int(np.prod(shapes[-1]))
    off = sum(int(np.prod(s)) for s in shapes[:-1])
    dmod_all = gathered.reshape(N_DEV, -1)[:, off:off + n_el].reshape(N_DEV, depth, 3 * Dm)

    dmod_rows = jnp.concatenate([jnp.transpose(dmod_all, (1, 0, 2)), gs["dmod_c"][:, None, :]], axis=1)
    dmod_rows = lax.dynamic_slice_in_dim(dmod_rows, me * ada_w, ada_w, axis=2)
    dmod_rows = jnp.pad(dmod_rows, ((0, 0), (0, ADA_ROWS - 9), (0, 0)))
    g_w_ada, dc_part = _ada_bwd(c16, w_ada, dmod_rows)
    dc_all = _all_gather_small(dc_part, "gather_dc")
    g_c_ctx = _sum_slabs(dc_all, "sum_dc")[0]

    grads_small = {"c_ctx": g_c_ctx, "b_ada": gs["b_ada"], "g_pre": gs["g_pre"], "g_post": gs["g_post"],
                   "conv_b": gs["conv_b"], "dt_bias": gs["dt_bias"], "a_log": gs["a_log"], "d_skip": gs["d_skip"],
                   "g_ssd": gs["g_ssd"], "g_v": gs["g_v"], "w_s": gs["w_s"], "b_s": gs["b_s"], "g_mlp": gs["g_mlp"],
                   "conv_w": lax.dynamic_slice_in_dim(gs["conv_w"], me * conv_c, conv_c, axis=2)}

    gw_full = jnp.concatenate([jnp.stack(gw_main)[:, :, :XBC_W], jnp.stack(gw_dt)[:, :, :DT_W],
                               jnp.stack(gw_main)[:, :, XBC_W:]], axis=2)
    in_slabs = jnp.transpose(gw_full.reshape(depth, Dm, N_DEV, in_w), (2, 0, 1, 3)).astype(BF16)
    in_recv = _slab_exchange(in_slabs, "scatter_w_in").reshape(N_DEV, depth * Dm, in_w)
    out_slabs = jnp.transpose(jnp.stack(gw_out).reshape(depth, N_DEV, out_r, Dm), (1, 0, 2, 3)).astype(BF16)
    out_recv = _slab_exchange(out_slabs, "scatter_w_out").reshape(N_DEV, depth * out_r, Dm)

    g_in, d_in, nm_in, nv_in = _adamw_slabs(w_in.reshape(depth * Dm, in_w), in_recv,
                                            m_w_in.reshape(depth * Dm, in_w), v_w_in.reshape(depth * Dm, in_w),
                                            "adamw_w_in", 128)
    g_out, d_out, nm_out, nv_out = _adamw_slabs(w_out.reshape(depth * out_r, Dm), out_recv,
                                                m_w_out.reshape(depth * out_r, Dm),
                                                v_w_out.reshape(depth * out_r, Dm), "adamw_w_out", 128)
    d_ada, nm_ada, nv_ada = _adamw(w_ada.reshape(depth * Dm, ada_w), g_w_ada.reshape(depth * Dm, ada_w),
                                   m_w_ada.reshape(depth * Dm, ada_w), v_w_ada.reshape(depth * Dm, ada_w),
                                   "adamw_w_ada", 256)

    small_names = ["c_ctx", "b_ada", "g_pre", "g_post", "conv_w", "conv_b", "dt_bias", "a_log", "d_skip", "g_ssd",
                   "g_v", "w_s", "b_s", "g_mlp"]
    small_w = dict(c_ctx=c_ctx, b_ada=b_ada, g_pre=g_pre, g_post=g_post, conv_w=conv_w, conv_b=conv_b,
                   dt_bias=dt_bias, a_log=a_log, d_skip=d_skip, g_ssd=g_ssd, g_v=g_v, w_s=w_s, b_s=b_s, g_mlp=g_mlp)
    small_m = dict(c_ctx=m_c_ctx, b_ada=m_b_ada, g_pre=m_g_pre, g_post=m_g_post, conv_w=m_conv_w, conv_b=m_conv_b,
                   dt_bias=m_dt_bias, a_log=m_a_log, d_skip=m_d_skip, g_ssd=m_g_ssd, g_v=m_g_v, w_s=m_w_s,
                   b_s=m_b_s, g_mlp=m_g_mlp)
    small_v = dict(c_ctx=v_c_ctx, b_ada=v_b_ada, g_pre=v_g_pre, g_post=v_g_post, conv_w=v_conv_w, conv_b=v_conv_b,
                   dt_bias=v_dt_bias, a_log=v_a_log, d_skip=v_d_skip, g_ssd=v_g_ssd, g_v=v_g_v, w_s=v_w_s,
                   b_s=v_b_s, g_mlp=v_g_mlp)
    s_shapes = [small_w[k].shape for k in small_names]
    d_s, nm_s, nv_s = _adamw(_pack([small_w[k] for k in small_names]),
                             _pack([grads_small[k].reshape(small_w[k].shape) for k in small_names]),
                             _pack([small_m[k] for k in small_names]), _pack([small_v[k] for k in small_names]),
                             "adamw_small", 512)
    d_s = dict(zip(small_names, _unpack(d_s, s_shapes)))
    nm_s = dict(zip(small_names, _unpack(nm_s, s_shapes)))
    nv_s = dict(zip(small_names, _unpack(nv_s, s_shapes)))

    big = {"w_ada": (g_w_ada, d_ada.reshape(w_ada.shape), nm_ada.reshape(w_ada.shape), nv_ada.reshape(w_ada.shape)),
           "w_in": tuple(t.reshape(w_in.shape) for t in (g_in, d_in, nm_in, nv_in)),
           "w_out": tuple(t.reshape(w_out.shape) for t in (g_out, d_out, nm_out, nv_out))}
    order = ["c_ctx", "w_ada", "b_ada", "g_pre", "g_post", "w_in", "conv_w", "conv_b", "dt_bias", "a_log", "d_skip",
             "g_ssd", "g_v", "w_s", "b_s", "g_mlp", "w_out"]

    def pick(k, idx):
        if k in big:
            return big[k][idx]
        return (grads_small[k].reshape(small_w[k].shape), d_s[k], nm_s[k], nv_s[k])[idx]

    return (loss, grad_x, *[pick(k, 0) for k in order], *[pick(k, 1) for k in order],
            *[pick(k, 2) for k in order], *[pick(k, 3) for k in order])
```

```python
import numpy as np
import jax
import jax.numpy as jnp
from jax import lax
from jax.experimental import pallas as pl
from jax.experimental.pallas import tpu as pltpu

F32 = jnp.float32
BF16 = jnp.bfloat16

D_MODEL = 2048
GRID_W = 64
SSD_W = 2048
SSD_HEADS = 32
SSD_HEAD_DIM = 64
SSD_GROUPS = 8
HEADS_PER_GROUP = 4
SSD_STATE = 128
CHUNK = 128
CONV_W = 5
MLP_W = 2048
MLP_GROUPS = 16
XBC_W = 4096
DT_W = 64
IN_W = 12352
Z_MAIN = IN_W - DT_W
GROUP_COLS = HEADS_PER_GROUP * SSD_HEAD_DIM
EPS = 1e-6
N_DEV = 8

ADAM_LR = 0.001
ADAM_B1 = 0.9
ADAM_B2 = 0.999
ADAM_EPS = 1e-08
ADAM_WD = 0.01
ADAM_STEP = 10

LANES = 128
NEG_BIG = -1e30

NN = (((1,), (0,)), ((), ()))
NT = (((1,), (1,)), ((), ()))
TN = (((0,), (0,)), ((), ()))
HI = lax.Precision.HIGHEST


def _dot(a, b, dims):
    return lax.dot_general(a.astype(BF16), b.astype(BF16), dims, preferred_element_type=F32)


def _dot_hi(a, b, dims=NN):
    return lax.dot_general(a, b, dims, preferred_element_type=F32, precision=HI)


def _sigmoid(x):
    return 1.0 / (1.0 + jnp.exp(-x))


def _silu(x):
    return x * _sigmoid(x)


def _softplus(x):
    return jnp.maximum(x, 0.0) + jnp.log(1.0 + jnp.exp(-jnp.abs(x)))


def _call(body, *, name, out_shape, grid=None, in_specs=None, out_specs=None, scratch=(),
          sem=None, vmem_mb=None, aliases=None):
    params = {}
    if sem is not None:
        params["dimension_semantics"] = sem
    if vmem_mb is not None:
        params["vmem_limit_bytes"] = vmem_mb << 20
    kw = {}
    if grid is not None:
        kw["grid"] = grid
    if in_specs is not None:
        kw["in_specs"] = in_specs
    if out_specs is not None:
        kw["out_specs"] = out_specs
    return pl.pallas_call(body, name=name, out_shape=out_shape, scratch_shapes=list(scratch),
                          input_output_aliases=aliases or {},
                          compiler_params=pltpu.CompilerParams(**params), **kw)


def _sds(shape, dtype):
    return jax.ShapeDtypeStruct(tuple(shape), dtype)


ANY = pl.BlockSpec(memory_space=pl.ANY)


def _my_pos():
    return lax.axis_index("x"), lax.axis_index("y"), lax.axis_index("c")


def _flip(v, bit):
    return 1 - v if bit else v


def _peer(pos, k):
    mx, my, mc = pos
    return (_flip(mx, (k >> 2) & 1), _flip(my, (k >> 1) & 1), _flip(mc, k & 1))


def _lin(pos):
    return 4 * pos[0] + 2 * pos[1] + pos[2]


def _all_gather_small(x, name):
    R, C = x.shape

    def body(x_ref, o_ref, send_sems, recv_sems):
        me = _my_pos()
        o_ref[_lin(me)] = x_ref[...]
        sends = []
        for k in range(1, N_DEV):
            peer = _peer(me, k)
            cp = pltpu.make_async_remote_copy(
                src_ref=x_ref, dst_ref=o_ref.at[_lin(me)], send_sem=send_sems.at[k - 1],
                recv_sem=recv_sems.at[k - 1], device_id=peer, device_id_type=pl.DeviceIdType.MESH)
            cp.start()
            sends.append(cp)
        for k in range(1, N_DEV):
            peer = _peer(me, k)
            pltpu.make_async_remote_copy(
                src_ref=x_ref, dst_ref=o_ref.at[_lin(peer)], send_sem=send_sems.at[k - 1],
                recv_sem=recv_sems.at[k - 1], device_id=peer,
                device_id_type=pl.DeviceIdType.MESH).wait_recv()
        for cp in sends:
            cp.wait_send()

    return _call(body, name=name, out_shape=_sds((N_DEV, R, C), x.dtype),
                 in_specs=[pl.BlockSpec(memory_space=pltpu.VMEM)],
                 out_specs=pl.BlockSpec(memory_space=pltpu.VMEM),
                 scratch=[pltpu.SemaphoreType.DMA((N_DEV - 1,)), pltpu.SemaphoreType.DMA((N_DEV - 1,))],
                 vmem_mb=40)(x)


def _all_gather_big(x, name):
    def body(x_ref, o_ref, send_sems, recv_sems, local_sem):
        mx, my, mc = _my_pos()
        me, sibling = (mx, my, mc), (mx, my, 1 - mc)
        chips = [(1 - mx, my), (mx, 1 - my), (1 - mx, 1 - my)]

        def slot(pos):
            return o_ref.at[_lin(pos)]

        def copy(k, block, to, src=None):
            return pltpu.make_async_remote_copy(
                src_ref=slot(block) if src is None else src, dst_ref=slot(block),
                send_sem=send_sems.at[k], recv_sem=recv_sems.at[k], device_id=to,
                device_id_type=pl.DeviceIdType.MESH)

        mine = pltpu.make_async_copy(x_ref, slot(me), local_sem)
        mine.start()
        first = [copy(0, me, sibling, src=x_ref)]
        first += [copy(1 + j, me, (*chip, mc), src=x_ref) for j, chip in enumerate(chips)]
        for cp in first:
            cp.start()
        passed = [copy(4 + j, (*chip, mc), sibling) for j, chip in enumerate(chips)]
        for j, chip in enumerate(chips):
            copy(1 + j, (*chip, mc), me).wait_recv()
            passed[j].start()
        copy(0, sibling, me).wait_recv()
        for j, chip in enumerate(chips):
            copy(4 + j, (*chip, 1 - mc), me).wait_recv()
        for cp in first + passed:
            cp.wait_send()
        mine.wait()

    return _call(body, name=name, out_shape=_sds((N_DEV,) + x.shape, x.dtype),
                 in_specs=[ANY], out_specs=ANY,
                 scratch=[pltpu.SemaphoreType.DMA((7,)), pltpu.SemaphoreType.DMA((7,)),
                          pltpu.SemaphoreType.DMA])(x)


EXCHANGE_SEMS = [pltpu.SemaphoreType.DMA((N_DEV - 1,)), pltpu.SemaphoreType.DMA((N_DEV - 1,)),
                 pltpu.SemaphoreType.DMA]


def _exchange_copies(kind, src_ref, dst_ref, send_sems, recv_sems, local_sem, with_arrivals=True):
    me = _my_pos()
    mine_src = src_ref if kind == "gather" else src_ref.at[_lin(me)]
    local = pltpu.make_async_copy(mine_src, dst_ref.at[_lin(me)], local_sem)
    sends, recvs = [], []
    for k in range(1, N_DEV):
        peer = _peer(me, k)
        out_src = src_ref if kind == "gather" else src_ref.at[_lin(peer)]
        sends.append(pltpu.make_async_remote_copy(
            src_ref=out_src, dst_ref=dst_ref.at[_lin(me)], send_sem=send_sems.at[k - 1],
            recv_sem=recv_sems.at[k - 1], device_id=peer, device_id_type=pl.DeviceIdType.MESH))
        if with_arrivals:
            recvs.append(pltpu.make_async_remote_copy(
                src_ref=out_src, dst_ref=dst_ref.at[_lin(peer)], send_sem=send_sems.at[k - 1],
                recv_sem=recv_sems.at[k - 1], device_id=peer, device_id_type=pl.DeviceIdType.MESH))
    return sends, recvs, local


def _exchange_start(kind, src_ref, dst_ref, send_sems, recv_sems, local_sem):
    sends, _, local = _exchange_copies(kind, src_ref, dst_ref, send_sems, recv_sems, local_sem, with_arrivals=False)
    local.start()
    for cp in sends:
        cp.start()


def _exchange_wait(kind, src_ref, dst_ref, send_sems, recv_sems, local_sem):
    sends, recvs, local = _exchange_copies(kind, src_ref, dst_ref, send_sems, recv_sems, local_sem)
    for cp in recvs:
        cp.wait_recv()
    for cp in sends:
        cp.wait_send()
    local.wait()


def _exchange_shape(kind, src):
    return _sds((N_DEV,) + src.shape if kind == "gather" else src.shape, src.dtype)


def _slab_exchange(s, name):
    def body(s_ref, r_ref, *sems):
        _exchange_start("scatter", s_ref, r_ref, *sems)
        _exchange_wait("scatter", s_ref, r_ref, *sems)

    return _call(body, name=name, out_shape=_exchange_shape("scatter", s), in_specs=[ANY], out_specs=ANY,
                 scratch=EXCHANGE_SEMS)(s)


def _mm(a, b, mode, out_dtype, name, tm, tn, tk, exchange=None):
    if mode == "nn":
        (M, K), (K2, N) = a.shape, b.shape
    elif mode == "nt":
        (M, K), (N, K2) = a.shape, b.shape
    else:
        (K, M), (K2, N) = a.shape, b.shape
    assert K == K2
    tm, tn, tk = min(tm, M), min(tn, N), min(tk, K)
    assert M % tm == 0 and N % tn == 0 and K % tk == 0, (name, M, N, K, tm, tn, tk)
    ni, nj, nk = M // tm, N // tn, K // tk
    dims = {"nn": NN, "nt": NT, "tn": TN}[mode]

    def product(a_ref, b_ref, o_ref, acc):
        p = _dot(a_ref[...], b_ref[...], dims)
        if nk == 1:
            o_ref[...] = p.astype(out_dtype)
        else:
            acc_ref = acc[0]
            k = pl.program_id(2)

            @pl.when(k == 0)
            def _():
                acc_ref[...] = p

            @pl.when(k > 0)
            def _():
                acc_ref[...] += p

            @pl.when(k == nk - 1)
            def _():
                o_ref[...] = acc_ref[...].astype(out_dtype)

    if mode == "tn":
        a_spec = pl.BlockSpec((tk, tm), lambda i, j, k: (k, i))
    else:
        a_spec = pl.BlockSpec((tm, tk), lambda i, j, k: (i, k))
    if mode == "nt":
        b_spec = pl.BlockSpec((tn, tk), lambda i, j, k: (j, k))
    else:
        b_spec = pl.BlockSpec((tk, tn), lambda i, j, k: (k, j))
    o_spec = pl.BlockSpec((tm, tn), lambda i, j, k: (i, j))
    acc_scratch = [] if nk == 1 else [pltpu.VMEM((tm, tn), F32)]
    if exchange is None:
        def body(a_ref, b_ref, o_ref, *acc):
            product(a_ref, b_ref, o_ref, acc)

        return _call(body, name=name, out_shape=_sds((M, N), out_dtype), grid=(ni, nj, nk),
                     in_specs=[a_spec, b_spec], out_specs=o_spec, scratch=acc_scratch,
                     sem=("parallel", "parallel", "arbitrary"), vmem_mb=48)(a, b)

    kind, src = exchange

    def body(a_ref, b_ref, src_ref, o_ref, dst_ref, send_sems, recv_sems, local_sem, *acc):
        i, j, k = pl.program_id(0), pl.program_id(1), pl.program_id(2)

        @pl.when(jnp.logical_and(jnp.logical_and(i == 0, j == 0), k == 0))
        def _():
            _exchange_start(kind, src_ref, dst_ref, send_sems, recv_sems, local_sem)

        product(a_ref, b_ref, o_ref, acc)

        @pl.when(jnp.logical_and(jnp.logical_and(i == ni - 1, j == nj - 1), k == nk - 1))
        def _():
            _exchange_wait(kind, src_ref, dst_ref, send_sems, recv_sems, local_sem)

    return _call(body, name=name, out_shape=(_sds((M, N), out_dtype), _exchange_shape(kind, src)), grid=(ni, nj, nk),
                 in_specs=[a_spec, b_spec, ANY], out_specs=(o_spec, ANY), scratch=EXCHANGE_SEMS + acc_scratch,
                 sem=("arbitrary", "arbitrary", "arbitrary"), vmem_mb=48)(a, b, src)


def _rms(x):
    return lax.rsqrt(jnp.mean(x * x, axis=-1, keepdims=True) + EPS)


def _prenorm_f(x, g, sc, sh):
    return (x * _rms(x) * g) * (1.0 + sc) + sh


def _pick(is_ctx, ref):
    return jnp.where(is_ctx, ref[0:1, :], ref[1:2, :])


def _prenorm_fwd(X, g, sc2, sh2, n_ctx, tl=256):
    T, Dm = X.shape
    nct = n_ctx // tl

    def body(x_ref, g_ref, sc_ref, sh_ref, o_ref):
        is_ctx = pl.program_id(0) < nct
        o_ref[...] = _prenorm_f(x_ref[...], g_ref[...], _pick(is_ctx, sc_ref),
                                _pick(is_ctx, sh_ref)).astype(BF16)

    row = pl.BlockSpec((tl, Dm), lambda i: (i, 0))
    return _call(body, name="prenorm_fwd", out_shape=_sds((T, Dm), BF16), grid=(T // tl,),
                 in_specs=[row, pl.BlockSpec((1, Dm), lambda i: (0, 0)),
                           pl.BlockSpec((2, Dm), lambda i: (0, 0)), pl.BlockSpec((2, Dm), lambda i: (0, 0))],
                 out_specs=row, sem=("parallel",), vmem_mb=40)(X, g, sc2, sh2)


def _prenorm_bwd(X, g, sc2, sh2, d1, d2, dres, n_ctx, tl=256):
    T, Dm = X.shape
    nct = n_ctx // tl

    def body(x_ref, g_ref, sc_ref, sh_ref, d1_ref, d2_ref, dres_ref, dx_ref, acc_ref):
        i = pl.program_id(0)
        is_ctx = i < nct

        @pl.when(i == 0)
        def _():
            acc_ref[...] = jnp.zeros_like(acc_ref)

        _, vjp = jax.vjp(_prenorm_f, x_ref[...], g_ref[...], _pick(is_ctx, sc_ref), _pick(is_ctx, sh_ref))
        dx, dg, dsc, dsh = vjp(d1_ref[...] + d2_ref[...])
        dx_ref[...] = dres_ref[...] + dx
        zero = jnp.zeros_like(dsc)
        acc_ref[0:1, :] += dg
        acc_ref[1:2, :] += jnp.where(is_ctx, dsc, zero)
        acc_ref[2:3, :] += jnp.where(is_ctx, zero, dsc)
        acc_ref[3:4, :] += jnp.where(is_ctx, dsh, zero)
        acc_ref[4:5, :] += jnp.where(is_ctx, zero, dsh)

    row = pl.BlockSpec((tl, Dm), lambda i: (i, 0))
    return _call(body, name="prenorm_bwd", out_shape=(_sds((T, Dm), F32), _sds((8, Dm), F32)), grid=(T // tl,),
                 in_specs=[row, pl.BlockSpec((1, Dm), lambda i: (0, 0)),
                           pl.BlockSpec((2, Dm), lambda i: (0, 0)), pl.BlockSpec((2, Dm), lambda i: (0, 0)),
                           row, row, row],
                 out_specs=(row, pl.BlockSpec((8, Dm), lambda i: (0, 0))), sem=("arbitrary",),
                 vmem_mb=48)(X, g, sc2, sh2, d1, d2, dres)


def _post_f(o, g, gate):
    return gate * ((o * _rms(o)) * g)


def _post_fwd(X, o, g, gate2, n_ctx, tl=256):
    T, Dm = X.shape
    nct = n_ctx // tl

    def body(x_ref, o_ref, g_ref, gate_ref, y_ref):
        is_ctx = pl.program_id(0) < nct
        y_ref[...] = x_ref[...] + _post_f(o_ref[...], g_ref[...], _pick(is_ctx, gate_ref))

    row = pl.BlockSpec((tl, Dm), lambda i: (i, 0))
    return _call(body, name="post_fwd", out_shape=_sds((T, Dm), F32), grid=(T // tl,),
                 in_specs=[row, row, pl.BlockSpec((1, Dm), lambda i: (0, 0)), pl.BlockSpec((2, Dm), lambda i: (0, 0))],
                 out_specs=row, sem=("parallel",), vmem_mb=40)(X, o, g, gate2)


def _post_bwd(o, g, gate2, dX, n_ctx, tl=256):
    T, Dm = o.shape
    nct = n_ctx // tl

    def body(o_ref, g_ref, gate_ref, dx_ref, do_ref, acc_ref):
        i = pl.program_id(0)
        is_ctx = i < nct

        @pl.when(i == 0)
        def _():
            acc_ref[...] = jnp.zeros_like(acc_ref)

        _, vjp = jax.vjp(_post_f, o_ref[...], g_ref[...], _pick(is_ctx, gate_ref))
        do, dg, dgate = vjp(dx_ref[...])
        do_ref[...] = do.astype(BF16)
        zero = jnp.zeros_like(dgate)
        acc_ref[0:1, :] += dg
        acc_ref[1:2, :] += jnp.where(is_ctx, dgate, zero)
        acc_ref[2:3, :] += jnp.where(is_ctx, zero, dgate)

    row = pl.BlockSpec((tl, Dm), lambda i: (i, 0))
    return _call(body, name="post_bwd", out_shape=(_sds((T, Dm), BF16), _sds((8, Dm), F32)), grid=(T // tl,),
                 in_specs=[row, pl.BlockSpec((1, Dm), lambda i: (0, 0)), pl.BlockSpec((2, Dm), lambda i: (0, 0)), row],
                 out_specs=(row, pl.BlockSpec((8, Dm), lambda i: (0, 0))), sem=("arbitrary",),
                 vmem_mb=48)(o, g, gate2, dX)


def _loss_kernel(X, target, n_ctx, tl=256):
    T, Dm = X.shape
    nct = n_ctx // tl

    def body(x_ref, t_ref, dx_ref, acc_ref):
        i = pl.program_id(0)

        @pl.when(i == 0)
        def _():
            acc_ref[...] = jnp.zeros_like(acc_ref)

        @pl.when(i < nct)
        def _():
            dx_ref[...] = jnp.zeros_like(dx_ref)

        @pl.when(i >= nct)
        def _():
            d = x_ref[...] - t_ref[...]
            dx_ref[...] = d * (1.0 / Dm)
            acc_ref[...] += jnp.sum(d * d, axis=0, keepdims=True)

    row = pl.BlockSpec((tl, Dm), lambda i: (i, 0))
    trow = pl.BlockSpec((tl, Dm), lambda i: (jnp.maximum(i - nct, 0), 0))
    return _call(body, name="loss", out_shape=(_sds((T, Dm), F32), _sds((1, Dm), F32)), grid=(T // tl,),
                 in_specs=[row, trow], out_specs=(row, pl.BlockSpec((1, Dm), lambda i: (0, 0))),
                 sem=("arbitrary",), vmem_mb=40)(X, target)


CONV_TL = 256
CONV_CB = 512


def _conv_taps(x, pos, row_len, reverse):
    tl = x.shape[0]
    taps = []
    for k in range(CONV_W):
        off = (2 - k) if reverse else (k - 2)
        xs = x if off == 0 else pltpu.roll(x, (-off) % tl, 0)
        valid = jnp.logical_and(pos + off >= 0, pos + off < row_len)
        taps.append(jnp.where(valid, xs, 0.0))
    return taps


def _conv_pos(i, n_ctx, tl):
    row_len = jnp.where(i < n_ctx // tl, n_ctx, GRID_W)
    pos = jnp.bitwise_and(lax.broadcasted_iota(jnp.int32, (tl, 1), 0), row_len - 1)
    return pos, row_len


def _conv_fwd(z, cw, cb, n_ctx):
    T = z.shape[0]
    tl, cbw = CONV_TL, CONV_CB
    assert n_ctx == tl

    def body(z_ref, w_ref, b_ref, o_ref):
        pos, row_len = _conv_pos(pl.program_id(1), n_ctx, tl)
        taps = _conv_taps(z_ref[...].astype(F32), pos, row_len, False)
        pre = b_ref[...] + taps[0] * w_ref[0:1, :]
        for k in range(1, CONV_W):
            pre = pre + taps[k] * w_ref[k:k + 1, :]
        o_ref[...] = _silu(pre).astype(BF16)

    blk = pl.BlockSpec((tl, cbw), lambda j, i: (i, j))
    return _call(body, name="conv_fwd", out_shape=_sds((T, XBC_W), BF16), grid=(XBC_W // cbw, T // tl),
                 in_specs=[blk, pl.BlockSpec((8, cbw), lambda j, i: (0, j)), pl.BlockSpec((1, cbw), lambda j, i: (0, j))],
                 out_specs=blk, sem=("parallel", "parallel"))(z, cw, cb)


def _conv_bwd(z, cw, cb, d0, d1, dz_in, col0, n_ctx):
    T = z.shape[0]
    tl, cbw = CONV_TL, CONV_CB

    def body(z_ref, w_ref, b_ref, d0_ref, d1_ref, dzin_ref, dz_ref, dw_ref, db_ref):
        i = pl.program_id(1)

        @pl.when(i == 0)
        def _():
            dw_ref[...] = jnp.zeros_like(dw_ref)
            db_ref[...] = jnp.zeros_like(db_ref)

        pos, row_len = _conv_pos(i, n_ctx, tl)
        taps = _conv_taps(z_ref[...].astype(F32), pos, row_len, False)
        pre = b_ref[...] + taps[0] * w_ref[0:1, :]
        for k in range(1, CONV_W):
            pre = pre + taps[k] * w_ref[k:k + 1, :]
        s = _sigmoid(pre)
        dpre = (d0_ref[...].astype(F32) + d1_ref[...].astype(F32)) * (s + pre * s * (1.0 - s))
        dtaps = _conv_taps(dpre, pos, row_len, True)
        dx = dtaps[0] * w_ref[0:1, :]
        for k in range(1, CONV_W):
            dx = dx + dtaps[k] * w_ref[k:k + 1, :]
        dz_ref[...] = dx.astype(BF16)
        for k in range(CONV_W):
            dw_ref[k:k + 1, :] += jnp.sum(dpre * taps[k], axis=0, keepdims=True)
        db_ref[0:1, :] += jnp.sum(dpre, axis=0, keepdims=True)

    width = d0.shape[1]
    jo = col0 // cbw
    blk = pl.BlockSpec((tl, cbw), lambda j, i: (i, jo + j))
    dblk = pl.BlockSpec((tl, cbw), lambda j, i: (i, j))
    par = pl.BlockSpec((8, cbw), lambda j, i: (0, jo + j))
    opar = pl.BlockSpec((8, cbw), lambda j, i: (0, j))
    return _call(body, name=f"conv_bwd_{col0}",
                 out_shape=(_sds(dz_in.shape, BF16), _sds((8, width), F32), _sds((8, width), F32)),
                 grid=(width // cbw, T // tl),
                 in_specs=[blk, par, pl.BlockSpec((1, cbw), lambda j, i: (0, jo + j)), dblk, dblk, ANY],
                 out_specs=(blk, opar, opar), sem=("parallel", "arbitrary"), aliases={5: 0})(z, cw, cb, d0, d1, dz_in)


MIX_TL = 128


def _gate_f(yf, yb, zs, g):
    t = (yf + yb) * _silu(zs)
    return t * _rms(t) * g


def _gate_fwd(yf, yb, z, g):
    T = yf.shape[0]
    tl = MIX_TL

    def body(yf_ref, yb_ref, zs_ref, g_ref, o_ref):
        o_ref[...] = _gate_f(yf_ref[...].astype(F32), yb_ref[...].astype(F32), zs_ref[...].astype(F32),
                             g_ref[...]).astype(BF16)

    row = pl.BlockSpec((tl, SSD_W), lambda i: (i, 0))
    return _call(body, name="gate_fwd", out_shape=_sds((T, SSD_W + MLP_W), BF16), grid=(T // tl,),
                 in_specs=[row, row, pl.BlockSpec((tl, SSD_W), lambda i: (i, XBC_W // SSD_W)),
                           pl.BlockSpec((1, SSD_W), lambda i: (0, 0))],
                 out_specs=row, sem=("parallel",))(yf, yb, z, g)


def _gate_bwd(yf, yb, z, g, dycat, dz_in):
    T = yf.shape[0]
    tl = MIX_TL

    def body(yf_ref, yb_ref, zs_ref, g_ref, d_ref, dzin_ref, dy_ref, dz_ref, acc_ref):
        @pl.when(pl.program_id(0) == 0)
        def _():
            acc_ref[...] = jnp.zeros_like(acc_ref)

        _, vjp = jax.vjp(_gate_f, yf_ref[...].astype(F32), yb_ref[...].astype(F32),
                         zs_ref[...].astype(F32), g_ref[...])
        dyf, _, dzs, dg = vjp(d_ref[...].astype(F32))
        dy_ref[...] = dyf.astype(BF16)
        dz_ref[...] = dzs.astype(BF16)
        acc_ref[0:1, :] += dg

    row = pl.BlockSpec((tl, SSD_W), lambda i: (i, 0))
    zs_spec = pl.BlockSpec((tl, SSD_W), lambda i: (i, XBC_W // SSD_W))
    return _call(body, name="gate_bwd",
                 out_shape=(_sds((T, SSD_W), BF16), _sds(dz_in.shape, BF16), _sds((8, SSD_W), F32)),
                 grid=(T // tl,),
                 in_specs=[row, row, zs_spec, pl.BlockSpec((1, SSD_W), lambda i: (0, 0)), row, ANY],
                 out_specs=(row, zs_spec, pl.BlockSpec((8, SSD_W), lambda i: (0, 0))),
                 sem=("arbitrary",), aliases={5: 1}, vmem_mb=40)(yf, yb, z, g, dycat, dz_in)


def _vnorm_f(v, gv):
    return v * _rms(v) * gv


def _mlp_out_f(u, sg, zm, gm):
    t = u * sg * _silu(zm)
    return t * _rms(t) * gm


U_BLK = (XBC_W + SSD_W) // MLP_W


def _mlp_mix(ws_ref, bst_ref, vn_s, sg_s):
    for gi in range(MLP_GROUPS):
        cols = pl.ds(gi * LANES, LANES)
        sg_s[:, cols] = _dot(ws_ref[gi], vn_s[:, cols], NN) + bst_ref[:, gi:gi + 1]


def _mlp_fwd(z, gv, ws, bst, gm, ycat_in):
    T = z.shape[0]
    tl = CHUNK

    def body(u_ref, v_ref, zm_ref, gv_ref, ws_ref, bst_ref, gm_ref, yin_ref, o_ref, vn_s, sg_s):
        vn_s[...] = _vnorm_f(v_ref[...].astype(F32), gv_ref[...]).astype(BF16)
        _mlp_mix(ws_ref, bst_ref, vn_s, sg_s)
        o_ref[...] = _mlp_out_f(u_ref[...].astype(F32), sg_s[...], zm_ref[...].astype(F32),
                                gm_ref[...]).astype(BF16)

    def zblk(b):
        return pl.BlockSpec((tl, MLP_W), lambda i: (i, b))

    vec = pl.BlockSpec((1, MLP_W), lambda i: (0, 0))
    return _call(body, name="mlp_fwd", out_shape=_sds(ycat_in.shape, BF16), grid=(T // tl,),
                 in_specs=[zblk(U_BLK), zblk(U_BLK + 1), zblk(U_BLK + 2), vec,
                           pl.BlockSpec((MLP_GROUPS, CHUNK, CHUNK), lambda i: (0, 0, 0)),
                           pl.BlockSpec((CHUNK, LANES), lambda i: (0, 0)), vec, ANY],
                 out_specs=pl.BlockSpec((tl, MLP_W), lambda i: (i, 1)),
                 scratch=[pltpu.VMEM((tl, MLP_W), BF16), pltpu.VMEM((tl, MLP_W), F32)],
                 sem=("parallel",), aliases={7: 0})(z, z, z, gv, ws, bst, gm, ycat_in)


def _mlp_bwd(z, gv, ws, bst, gm, dycat, dz_in):
    T = z.shape[0]
    tl = CHUNK

    def body(u_ref, v_ref, zm_ref, gv_ref, ws_ref, bst_ref, gm_ref, d_ref, dzin_ref,
             dz_ref, acc_ref, dws_ref, dbst_ref, vn_s, sg_s, dvn_s):
        @pl.when(pl.program_id(0) == 0)
        def _():
            acc_ref[...] = jnp.zeros_like(acc_ref)
            dws_ref[...] = jnp.zeros_like(dws_ref)
            dbst_ref[...] = jnp.zeros_like(dbst_ref)

        v = v_ref[...].astype(F32)
        vn, vjp_v = jax.vjp(_vnorm_f, v, gv_ref[...])
        vn_s[...] = vn.astype(BF16)
        _mlp_mix(ws_ref, bst_ref, vn_s, sg_s)
        _, vjp_o = jax.vjp(_mlp_out_f, u_ref[...].astype(F32), sg_s[...], zm_ref[...].astype(F32), gm_ref[...])
        du, dsg, dzm, dgm = vjp_o(d_ref[...].astype(F32))
        sg_s[...] = dsg
        for gi in range(MLP_GROUPS):
            cols = pl.ds(gi * LANES, LANES)
            dsg_g = sg_s[:, cols]
            dvn_s[:, cols] = _dot(ws_ref[gi], dsg_g, TN)
            dws_ref[gi] += _dot(dsg_g, vn_s[:, cols], NT)
            dbst_ref[:, gi:gi + 1] += jnp.sum(dsg_g, axis=1, keepdims=True)
        dv, dgv = vjp_v(dvn_s[...])
        dz_ref[:, 0:MLP_W] = du.astype(BF16)
        dz_ref[:, MLP_W:2 * MLP_W] = dv.astype(BF16)
        dz_ref[:, 2 * MLP_W:3 * MLP_W] = dzm.astype(BF16)
        acc_ref[0:1, :] += dgv
        acc_ref[1:2, :] += dgm

    def zblk(b):
        return pl.BlockSpec((tl, MLP_W), lambda i: (i, b))

    vec = pl.BlockSpec((1, MLP_W), lambda i: (0, 0))
    ws_spec = pl.BlockSpec((MLP_GROUPS, CHUNK, CHUNK), lambda i: (0, 0, 0))
    bst_spec = pl.BlockSpec((CHUNK, LANES), lambda i: (0, 0))
    return _call(body, name="mlp_bwd",
                 out_shape=(_sds(dz_in.shape, BF16), _sds((8, MLP_W), F32),
                            _sds((MLP_GROUPS, CHUNK, CHUNK), F32), _sds((CHUNK, LANES), F32)),
                 grid=(T // tl,),
                 in_specs=[zblk(U_BLK), zblk(U_BLK + 1), zblk(U_BLK + 2), vec, ws_spec, bst_spec, vec,
                           pl.BlockSpec((tl, MLP_W), lambda i: (i, 1)), ANY],
                 out_specs=(pl.BlockSpec((tl, 3 * MLP_W), lambda i: (i, 1)),
                            pl.BlockSpec((8, MLP_W), lambda i: (0, 0)), ws_spec, bst_spec),
                 scratch=[pltpu.VMEM((tl, MLP_W), BF16), pltpu.VMEM((tl, MLP_W), F32), pltpu.VMEM((tl, MLP_W), F32)],
                 sem=("arbitrary",), aliases={8: 0}, vmem_mb=48)(z, z, z, gv, ws, bst, gm, dycat, dz_in)


def _ssd_constants(direction):
    q = CHUNK
    tri = np.tril(np.ones((q, q), np.float32))
    if direction == 1:
        tri = tri.T
    ex = np.zeros((SSD_GROUPS, LANES, GROUP_COLS), np.float32)
    for g in range(SSD_GROUPS):
        for r in range(HEADS_PER_GROUP):
            ex[g, direction * SSD_HEADS + HEADS_PER_GROUP * g + r, r * SSD_HEAD_DIM:(r + 1) * SSD_HEAD_DIM] = 1.0
    hm = np.zeros((8, GROUP_COLS), np.float32)
    for r in range(HEADS_PER_GROUP):
        hm[r, r * SSD_HEAD_DIM:(r + 1) * SSD_HEAD_DIM] = 1.0
    return (jnp.asarray(tri), jnp.asarray(ex, dtype=BF16),
            jnp.asarray(np.ascontiguousarray(ex.transpose(0, 2, 1)), dtype=BF16), jnp.asarray(hm))


def _dot2(v, m, dims=NN):
    hi = v.astype(BF16)
    lo = (v - hi.astype(F32)).astype(BF16)
    return (lax.dot_general(hi, m, dims, preferred_element_type=F32)
            + lax.dot_general(lo, m, dims, preferred_element_type=F32))


def _ssd_prep(raw, prm):
    T = raw.shape[0]
    nc, Q = T // CHUNK, CHUNK
    tril = jnp.asarray(np.tril(np.ones((Q, Q), np.float32)))
    triu = jnp.asarray(np.triu(np.ones((Q, Q), np.float32)))

    def body(raw_ref, prm_ref, tril_ref, triu_ref, dt_ref, s_ref, e_ref, d_ref, st_ref, etot_ref):
        dt = _softplus(raw_ref[...] + prm_ref[0:1, :])
        a = dt * (-jnp.exp(prm_ref[1:2, :]))
        lane = lax.broadcasted_iota(jnp.int32, (1, LANES), 1)
        s = jnp.where(lane < SSD_HEADS, _dot_hi(tril_ref[...], a), _dot_hi(triu_ref[...], a))
        tot = jnp.sum(a, axis=0, keepdims=True)
        dt_ref[...] = dt
        s_ref[...] = s
        e_ref[...] = jnp.exp(s)
        d_ref[...] = jnp.exp(tot - s)
        st_ref[0] = s.T
        etot_ref[0] = _bcast8(jnp.exp(tot))

    row = pl.BlockSpec((Q, LANES), lambda c: (c, 0))
    cst = lambda shape: pl.BlockSpec(shape, lambda c: (0, 0))
    tl = _sds((T, LANES), F32)
    return _call(body, name="ssd_prep", out_shape=(tl, tl, tl, tl, _sds((nc, LANES, Q), F32), _sds((nc, 8, LANES), F32)),
                 grid=(nc,), in_specs=[row, cst((8, LANES)), cst((Q, Q)), cst((Q, Q))],
                 out_specs=(row, row, row, row, pl.BlockSpec((1, LANES, Q), lambda c: (c, 0, 0)),
                            pl.BlockSpec((1, 8, LANES), lambda c: (c, 0, 0))),
                 sem=("parallel",))(raw, prm, tril, triu)


def _ssd_post(raw, prm, dt, ds0, ds1, ddt0, ddt1, dtot0, dtot1):
    T = raw.shape[0]
    nc, Q = T // CHUNK, CHUNK
    tril = jnp.asarray(np.tril(np.ones((Q, Q), np.float32)))
    triu = jnp.asarray(np.triu(np.ones((Q, Q), np.float32)))

    def body(raw_ref, prm_ref, dt_ref, ds0_ref, ds1_ref, ddt0_ref, ddt1_ref, dtot0_ref, dtot1_ref, tril_ref, triu_ref,
             draw_ref, dprm_ref):
        @pl.when(pl.program_id(0) == 0)
        def _():
            dprm_ref[...] = jnp.zeros_like(dprm_ref)

        pre = raw_ref[...] + prm_ref[0:1, :]
        A = -jnp.exp(prm_ref[1:2, :])
        ds = ds0_ref[...] + ds1_ref[...]
        lane = lax.broadcasted_iota(jnp.int32, (1, LANES), 1)
        da = jnp.where(lane < SSD_HEADS, _dot_hi(triu_ref[...], ds), _dot_hi(tril_ref[...], ds))
        da = da + dtot0_ref[0, 0:1, :] + dtot1_ref[0, 0:1, :]
        draw = (da * A + ddt0_ref[...] + ddt1_ref[...]) * _sigmoid(pre)
        draw_ref[...] = draw
        dprm_ref[0:1, :] += jnp.sum(draw, axis=0, keepdims=True)
        dprm_ref[1:2, :] += jnp.sum(da * dt_ref[...], axis=0, keepdims=True) * A

    row = pl.BlockSpec((Q, LANES), lambda c: (c, 0))
    tot = pl.BlockSpec((1, 8, LANES), lambda c: (c, 0, 0))
    cst = lambda shape: pl.BlockSpec(shape, lambda c: (0, 0))
    return _call(body, name="ssd_post", out_shape=(_sds((T, LANES), F32), _sds((8, LANES), F32)), grid=(nc,),
                 in_specs=[row, cst((8, LANES)), row, row, row, row, row, tot, tot, cst((Q, Q)), cst((Q, Q))],
                 out_specs=(row, cst((8, LANES))), sem=("arbitrary",))(raw, prm, dt, ds0, ds1, ddt0, ddt1, dtot0, dtot1,
                                                                       tril, triu)


def _chunk_order(direction, nc, nctx):
    if direction == 0:
        return lambda c: c
    return lambda c: jnp.where(c < nctx, nctx - 1 - c, nc - 1 + nctx - c)


def _bcast8(row):
    return jnp.broadcast_to(row, (8, row.shape[1]))


GPS = 2


def _cols(ref, k, width):
    return ref[:, k * width:(k + 1) * width]


def _ssd_common(k, x_ref, b_ref, c_ref, dt_ref, s_ref, e_ref, d_ref, st_ref, etot_ref, ex_ref, direction):
    g = pl.program_id(1) * GPS + k
    base = direction * SSD_HEADS + HEADS_PER_GROUP * g
    x = _cols(x_ref, k, GROUP_COLS).astype(F32)
    ex = ex_ref[k]
    dt_x = _dot2(dt_ref[...], ex)
    E_x = _dot2(e_ref[...], ex)
    D_x = _dot2(d_ref[...], ex)
    lane = lax.broadcasted_iota(jnp.int32, (1, LANES), 1)
    etot = etot_ref[0, 0:1, :]
    rowi = lax.broadcasted_iota(jnp.int32, (GROUP_COLS, 1), 0)
    e = [jnp.sum(jnp.where(lane == base + r, etot, 0.0), axis=1, keepdims=True) for r in range(HEADS_PER_GROUP)]
    Etot_c = jnp.where(rowi < SSD_HEAD_DIM, e[0],
                       jnp.where(rowi < 2 * SSD_HEAD_DIM, e[1], jnp.where(rowi < 3 * SSD_HEAD_DIM, e[2], e[3])))
    rows = [st_ref[0, HEADS_PER_GROUP * k + r:HEADS_PER_GROUP * k + r + 1, :] for r in range(HEADS_PER_GROUP)]
    return dict(g=g, base=base, x=x, s=s_ref[...], rows=rows, dt_x=dt_x, lane=lane, X=x * dt_x, E_x=E_x, D_x=D_x,
                Etot_c=Etot_c, Bm=_cols(b_ref, k, SSD_STATE), Cm=_cols(c_ref, k, SSD_STATE))


def _head_decay(q, r, tri_ref):
    sel = q["lane"] == q["base"] + r
    col = jnp.sum(jnp.where(sel, q["s"], 0.0), axis=1, keepdims=True)
    return sel, jnp.exp(jnp.where(tri_ref[...] > 0.5, col - q["rows"][r], NEG_BIG))


def _ssd_in_specs(order, direction):
    Q, GC, N, G = CHUNK, GROUP_COLS, SSD_STATE, SSD_GROUPS
    row = pl.BlockSpec((Q, LANES), lambda c, g: (order(c), 0))
    hrows = HEADS_PER_GROUP * GPS
    return [
        pl.BlockSpec((Q, GPS * GC), lambda c, g: (order(c), g)),
        pl.BlockSpec((Q, GPS * N), lambda c, g: (order(c), SSD_W // (GPS * N) + g)),
        pl.BlockSpec((Q, GPS * N), lambda c, g: (order(c), (SSD_W + G * N) // (GPS * N) + g)),
        row, row, row, row,
        pl.BlockSpec((1, hrows, Q), lambda c, g: (order(c), direction * (SSD_HEADS // hrows) + g, 0)),
        pl.BlockSpec((1, 8, LANES), lambda c, g: (order(c), 0, 0)),
    ]


def _ssd_fwd(xbc, prep, dskx, direction, n_ctx):
    T = xbc.shape[0]
    nc, nctx = T // CHUNK, n_ctx // CHUNK
    order = _chunk_order(direction, nc, nctx)
    tri, ex, ext, hm = _ssd_constants(direction)
    Q, GC, N, G = CHUNK, GROUP_COLS, SSD_STATE, SSD_GROUPS

    def body(x_ref, b_ref, c_ref, dt_ref, s_ref, e_ref, d_ref, st_ref, etot_ref, dsk_ref, tri_ref, ex_ref, hm_ref,
             y_ref, sst_ref, S):
        c = pl.program_id(0)
        for k in range(GPS):
            q = _ssd_common(k, x_ref, b_ref, c_ref, dt_ref, s_ref, e_ref, d_ref, st_ref, etot_ref, ex_ref, direction)
            g = q["g"]

            @pl.when(c == 0)
            def _():
                S[g] = jnp.zeros((GC, N), F32)

            S0 = S[g]
            sst_ref[0, k] = S0
            X, Bm, Cm = q["X"], q["Bm"], q["Cm"]
            y = q["E_x"] * _dot(Cm, S0, NT)
            Gm = _dot(Cm, Bm, NT)
            for r in range(HEADS_PER_GROUP):
                _, Lm = _head_decay(q, r, tri_ref)
                y = y + _dot(Gm * Lm, X * hm_ref[r:r + 1, :], NN)
            if direction == 0:
                y = y + _cols(dsk_ref, k, GC)[0:1, :] * q["x"]
            y_ref[:, k * GC:(k + 1) * GC] = y.astype(BF16)
            S[g] = q["Etot_c"] * S0 + _dot(X * q["D_x"], Bm, TN)

    cst2 = lambda shape: pl.BlockSpec(shape, lambda c, g: (0, 0))
    in_specs = _ssd_in_specs(order, direction) + [
        pl.BlockSpec((8, GPS * GC), lambda c, g: (0, g)), cst2((Q, Q)),
        pl.BlockSpec((GPS, LANES, GC), lambda c, g: (g, 0, 0)), cst2((8, GC))]
    out_specs = (pl.BlockSpec((Q, GPS * GC), lambda c, g: (order(c), g)),
                 pl.BlockSpec((1, GPS, GC, N), lambda c, g: (order(c), g, 0, 0)))
    return _call(body, name=f"ssd_fwd_{direction}",
                 out_shape=(_sds((T, SSD_W), BF16), _sds((nc, G, GC, N), F32)),
                 grid=(nc, G // GPS), in_specs=in_specs, out_specs=out_specs,
                 scratch=[pltpu.VMEM((G, GC, N), F32)], sem=("arbitrary", "arbitrary"),
                 )(xbc, xbc, xbc, *prep, dskx, tri, ex, hm)


def _ssd_bwd(xbc, prep, dskx, sst, dy, direction, n_ctx):
    T = xbc.shape[0]
    nc, nctx = T // CHUNK, n_ctx // CHUNK
    fwd_order = _chunk_order(direction, nc, nctx)
    order = lambda c: fwd_order(nc - 1 - c)
    tri, ex, ext, hm = _ssd_constants(direction)
    ones = jnp.ones((CHUNK, LANES), BF16)
    Q, GC, N, G = CHUNK, GROUP_COLS, SSD_STATE, SSD_GROUPS

    def body(x_ref, b_ref, c_ref, dt_ref, s_ref, e_ref, d_ref, st_ref, etot_ref, dsk_ref, sst_ref, dy_ref, tri_ref,
             ex_ref, ext_ref, hm_ref, ones_ref, dx_ref, db_ref, dc_ref, ds_ref, ddt_ref, dtot_ref, dskg_ref, dS):
        c, gi = pl.program_id(0), pl.program_id(1)

        @pl.when(jnp.logical_and(c == 0, gi == 0))
        def _():
            dskg_ref[...] = jnp.zeros_like(dskg_ref)

        @pl.when(gi == 0)
        def _():
            ds_ref[...] = jnp.zeros_like(ds_ref)
            ddt_ref[...] = jnp.zeros_like(ddt_ref)
            dtot_ref[...] = jnp.zeros_like(dtot_ref)

        for k in range(GPS):
            q = _ssd_common(k, x_ref, b_ref, c_ref, dt_ref, s_ref, e_ref, d_ref, st_ref, etot_ref, ex_ref, direction)
            g = q["g"]

            @pl.when(c == 0)
            def _():
                dS[g] = jnp.zeros((GC, N), F32)

            dS1 = dS[g]
            S0 = sst_ref[0, k]
            x, X, Bm, Cm, ext = q["x"], q["X"], q["Bm"], q["Cm"], ext_ref[k]
            E_x, D_x, Etot_c = q["E_x"], q["D_x"], q["Etot_c"]
            dY = _cols(dy_ref, k, GC).astype(F32)

            CS = _dot(Cm, S0, NT)
            dCS = dY * E_x
            dC = _dot(dCS, S0, NN)
            dS0 = Etot_c * dS1 + _dot(dCS, Cm, TN)
            ds_x = dY * (E_x * CS)
            dtot_c = jnp.sum(dS1 * S0, axis=1, keepdims=True) * Etot_c
            dtot = jnp.sum(dtot_c * ext.astype(F32), axis=0, keepdims=True)
            XD = X * D_x
            dXD = _dot(Bm, dS1, NT)
            dB = _dot(XD, dS1, NN)
            dX = dXD * D_x
            t = dXD * XD
            ds_x = ds_x - t
            dtot_x = jnp.sum(t, axis=0, keepdims=True)
            Gm = _dot(Cm, Bm, NT)
            dG = jnp.zeros((Q, Q), F32)
            ds = jnp.zeros((Q, LANES), F32)
            for r in range(HEADS_PER_GROUP):
                hmr = hm_ref[r:r + 1, :]
                sel, Lm = _head_decay(q, r, tri_ref)
                W = Gm * Lm
                dW = _dot(dY * hmr, X, NT)
                dX = dX + _dot(W, dY, TN) * hmr
                dG = dG + dW * Lm
                Z = dW * W
                ds = ds + jnp.where(sel, _dot2(Z, ones_ref[...], NN) - _dot2(Z, ones_ref[...], TN), 0.0)
            dC = dC + _dot(dG, Bm, NN)
            dB = dB + _dot(dG, Cm, TN)
            ds_ref[...] += ds + _dot2(ds_x, ext)
            dtot_ref[0, 0:1, :] += dtot + _dot2(_bcast8(dtot_x), ext)[0:1, :]
            ddt_ref[...] += _dot(dX * x, ext, NN)
            dx = dX * q["dt_x"]
            if direction == 0:
                dx = dx + dY * _cols(dsk_ref, k, GC)[0:1, :]
                dskg_ref[0:1, :] += jnp.sum(_dot(dY * x, ext, NN), axis=0, keepdims=True)
            dx_ref[:, k * GC:(k + 1) * GC] = dx.astype(BF16)
            db_ref[:, k * N:(k + 1) * N] = dB.astype(BF16)
            dc_ref[:, k * N:(k + 1) * N] = dC.astype(BF16)
            dS[g] = dS0

    cst2 = lambda shape: pl.BlockSpec(shape, lambda c, g: (0, 0))
    in_specs = _ssd_in_specs(order, direction) + [
        pl.BlockSpec((8, GPS * GC), lambda c, g: (0, g)),
        pl.BlockSpec((1, GPS, GC, N), lambda c, g: (order(c), g, 0, 0)),
        pl.BlockSpec((Q, GPS * GC), lambda c, g: (order(c), g)),
        cst2((Q, Q)),
        pl.BlockSpec((GPS, LANES, GC), lambda c, g: (g, 0, 0)),
        pl.BlockSpec((GPS, GC, LANES), lambda c, g: (g, 0, 0)),
        cst2((8, GC)), cst2((Q, LANES)),
    ]
    row = pl.BlockSpec((Q, LANES), lambda c, g: (order(c), 0))
    out_specs = (pl.BlockSpec((Q, GPS * GC), lambda c, g: (order(c), g)),
                 pl.BlockSpec((Q, GPS * N), lambda c, g: (order(c), g)),
                 pl.BlockSpec((Q, GPS * N), lambda c, g: (order(c), g)),
                 row, row, pl.BlockSpec((1, 8, LANES), lambda c, g: (order(c), 0, 0)),
                 cst2((8, LANES)))
    return _call(body, name=f"ssd_bwd_{direction}",
                 out_shape=(_sds((T, SSD_W), BF16), _sds((T, G * N), BF16), _sds((T, G * N), BF16),
                            _sds((T, LANES), F32), _sds((T, LANES), F32), _sds((nc, 8, LANES), F32),
                            _sds((8, LANES), F32)),
                 grid=(nc, G // GPS), in_specs=in_specs, out_specs=out_specs,
                 scratch=[pltpu.VMEM((G, GC, N), F32)], sem=("arbitrary", "arbitrary"),
                 )(xbc, xbc, xbc, *prep, dskx, sst, dy, tri, ex, ext, hm, ones)


ADA_ROWS = 16


def _ada_fwd(c16, w_ada, b_loc):
    depth, Dm, W = w_ada.shape

    def body(c_ref, w_ref, b_ref, o_ref):
        o_ref[0] = _dot_hi(_silu(c_ref[...]), w_ref[0]) + b_ref[0]

    return _call(body, name="ada_fwd", out_shape=_sds((depth, ADA_ROWS, W), F32), grid=(depth,),
                 in_specs=[pl.BlockSpec((ADA_ROWS, Dm), lambda l: (0, 0)),
                           pl.BlockSpec((1, Dm, W), lambda l: (l, 0, 0)),
                           pl.BlockSpec((1, 1, W), lambda l: (l, 0, 0))],
                 out_specs=pl.BlockSpec((1, ADA_ROWS, W), lambda l: (l, 0, 0)),
                 sem=("parallel",), vmem_mb=40)(c16, w_ada, b_loc)


def _ada_bwd(c16, w_ada, dmod):
    depth, Dm, W = w_ada.shape

    def body(c_ref, w_ref, d_ref, gw_ref, dc_ref):
        l = pl.program_id(0)

        @pl.when(l == 0)
        def _():
            dc_ref[...] = jnp.zeros_like(dc_ref)

        cc = c_ref[...]
        sg = _sigmoid(cc)
        gw_ref[0] = _dot_hi(cc * sg, d_ref[0], TN)
        dsc = _dot_hi(d_ref[0], w_ref[0], NT)
        dc_ref[...] += dsc[8:16, :] * (sg + cc * sg * (1.0 - sg))[8:16, :]

    return _call(body, name="ada_bwd", out_shape=(_sds((depth, Dm, W), F32), _sds((8, Dm), F32)), grid=(depth,),
                 in_specs=[pl.BlockSpec((ADA_ROWS, Dm), lambda l: (0, 0)),
                           pl.BlockSpec((1, Dm, W), lambda l: (l, 0, 0)),
                           pl.BlockSpec((1, ADA_ROWS, W), lambda l: (l, 0, 0))],
                 out_specs=(pl.BlockSpec((1, Dm, W), lambda l: (l, 0, 0)), pl.BlockSpec((8, Dm), lambda l: (0, 0))),
                 sem=("arbitrary",), vmem_mb=48)(c16, w_ada, dmod)


def _sum_slabs(g, name, tr=512):
    _, R, C = g.shape
    tr = min(tr, R)

    def body(g_ref, o_ref):
        acc = g_ref[0]
        for k in range(1, N_DEV):
            acc = acc + g_ref[k]
        o_ref[...] = acc

    return _call(body, name=name, out_shape=_sds((R, C), F32), grid=(R // tr,),
                 in_specs=[pl.BlockSpec((N_DEV, tr, C), lambda i: (0, i, 0))],
                 out_specs=pl.BlockSpec((tr, C), lambda i: (i, 0)), sem=("parallel",), vmem_mb=40)(g)


def _adamw_math(w, g, m, v):
    m = ADAM_B1 * m + (1.0 - ADAM_B1) * g
    v = ADAM_B2 * v + (1.0 - ADAM_B2) * (g * g)
    m_hat = m / (1.0 - ADAM_B1 ** ADAM_STEP)
    v_hat = v / (1.0 - ADAM_B2 ** ADAM_STEP)
    delta = -ADAM_LR * (m_hat / (jnp.sqrt(v_hat) + ADAM_EPS) + ADAM_WD * w)
    return delta, m, v


def _adamw(w, g, m, v, name, tr):
    R, C = w.shape
    tr = min(tr, R)
    assert R % tr == 0

    def body(w_ref, g_ref, m_ref, v_ref, d_ref, nm_ref, nv_ref):
        d, nm, nv = _adamw_math(w_ref[...], g_ref[...], m_ref[...], v_ref[...])
        d_ref[...] = d
        nm_ref[...] = nm
        nv_ref[...] = nv

    blk = pl.BlockSpec((tr, C), lambda i: (i, 0))
    out = _sds((R, C), F32)
    return _call(body, name=name, out_shape=(out, out, out), grid=(R // tr,), in_specs=[blk] * 4,
                 out_specs=(blk, blk, blk), sem=("parallel",), vmem_mb=40)(w, g, m, v)


def _adamw_slabs(w, slabs, m, v, name, tr):
    depth, R, C = w.shape
    assert R % tr == 0 and len(slabs) == depth

    def body(w_ref, *rest):
        slab_refs = rest[:depth]
        m_ref, v_ref, g_ref, d_ref, nm_ref, nv_ref = rest[depth:]
        layer = pl.program_id(0)
        for ll in range(depth):
            @pl.when(layer == ll)
            def _(s_ref=slab_refs[ll]):
                g = s_ref[0].astype(F32)
                for k in range(1, N_DEV):
                    g = g + s_ref[k].astype(F32)
                d, nm, nv = _adamw_math(w_ref[0], g, m_ref[0], v_ref[0])
                g_ref[0] = g
                d_ref[0] = d
                nm_ref[0] = nm
                nv_ref[0] = nv

    blk = pl.BlockSpec((1, tr, C), lambda l, i: (l, i, 0))

    def slab_spec(ll):
        return pl.BlockSpec((N_DEV, tr, C), lambda l, i: (0, jnp.where(l == ll, i, 0), 0))

    out = _sds((depth, R, C), F32)
    return _call(body, name=name, out_shape=(out, out, out, out), grid=(depth, R // tr),
                 in_specs=[blk] + [slab_spec(ll) for ll in range(depth)] + [blk, blk],
                 out_specs=(blk, blk, blk, blk), sem=("parallel", "parallel"), vmem_mb=56)(w, *slabs, m, v)


PACK_QUANTUM = 512 * LANES


def _pack(arrays):
    flat = jnp.concatenate([a.reshape(-1).astype(F32) for a in arrays])
    pad = (-flat.shape[0]) % PACK_QUANTUM
    return jnp.pad(flat, (0, pad)).reshape(-1, LANES)


def _unpack(bundle, shapes):
    flat = bundle.reshape(-1)
    out, off = [], 0
    for shp in shapes:
        n = int(np.prod(shp))
        out.append(flat[off:off + n].reshape(shp))
        off += n
    return out


def _row(v):
    return v.reshape(1, -1)


def _pad_rows(a, rows):
    return jnp.pad(a, ((0, rows - a.shape[0]), (0, 0)))


def kernel(x, c, ctx, c_ctx, w_ada, b_ada, g_pre, g_post, w_in, conv_w, conv_b, dt_bias, a_log, d_skip, g_ssd, g_v, w_s, b_s, g_mlp, w_out, loss_target, m_c_ctx, m_w_ada, m_b_ada, m_g_pre, m_g_post, m_w_in, m_conv_w, m_conv_b, m_dt_bias, m_a_log, m_d_skip, m_g_ssd, m_g_v, m_w_s, m_b_s, m_g_mlp, m_w_out, v_c_ctx, v_w_ada, v_b_ada, v_g_pre, v_g_post, v_w_in, v_conv_w, v_conv_b, v_dt_bias, v_a_log, v_d_skip, v_g_ssd, v_g_v, v_w_s, v_b_s, v_g_mlp, v_w_out):
    depth = w_in.shape[0]
    L = x.shape[1]
    n_ctx = ctx.shape[1]
    T = n_ctx + L
    Dm = D_MODEL
    me = _lin(_my_pos())
    ada_w = w_ada.shape[2]
    in_w = w_in.shape[2]
    out_r = w_out.shape[1]
    conv_c = conv_w.shape[2]
    TM = 768

    c_all = _all_gather_small(_pad_rows(c, 8), "gather_c")[:, 0, :]
    c16 = _pad_rows(jnp.concatenate([c_all, _row(c_ctx)], axis=0), ADA_ROWS)
    b_loc = lax.dynamic_slice_in_dim(b_ada, me * ada_w, ada_w, axis=1)[:, None, :]
    mod_loc = _ada_fwd(c16, w_ada, b_loc)
    mod_all = _all_gather_small(mod_loc.reshape(depth * ADA_ROWS, ada_w), "gather_mod")
    mod_all = mod_all.reshape(N_DEV, depth, ADA_ROWS, ada_w)
    mod_me = lax.dynamic_index_in_dim(mod_all, me, axis=2, keepdims=False)
    mod_me = jnp.transpose(mod_me, (1, 0, 2)).reshape(depth, N_DEV * ada_w)
    mod_cx = jnp.transpose(mod_all[:, :, 8, :], (1, 0, 2)).reshape(depth, N_DEV * ada_w)
    shift2 = jnp.stack([mod_cx[:, 0:Dm], mod_me[:, 0:Dm]], axis=1)
    scale2 = jnp.stack([mod_cx[:, Dm:2 * Dm], mod_me[:, Dm:2 * Dm]], axis=1)
    gate2 = jnp.stack([mod_cx[:, 2 * Dm:], mod_me[:, 2 * Dm:]], axis=1)

    w_in_bf = w_in.astype(BF16)
    w_out_bf = w_out.astype(BF16)
    w_in_g = _all_gather_big(w_in_bf[0], "gather_w_in")
    w_out_g = _all_gather_big(w_out_bf[0], "gather_w_out")
    conv_all = _all_gather_small(_pad_rows(conv_w.reshape(depth * CONV_W, conv_c), 24).reshape(24, conv_c),
                                 "gather_conv_w")
    conv_full = jnp.transpose(conv_all[:, :depth * CONV_W, :], (1, 0, 2)).reshape(depth, CONV_W, XBC_W)

    def in_weights(gathered):
        wf = jnp.transpose(gathered, (1, 0, 2)).reshape(Dm, IN_W)
        return (jnp.concatenate([wf[:, :XBC_W], wf[:, XBC_W + DT_W:]], axis=1),
                jnp.pad(wf[:, XBC_W:XBC_W + DT_W], ((0, 0), (0, LANES - DT_W))))

    w_main, w_dt, w_o = [None] * depth, [None] * depth, [None] * depth

    def ssd_prm(l):
        rows = jnp.stack([jnp.pad(dt_bias[l].reshape(-1), (0, LANES - DT_W)),
                          jnp.pad(a_log[l].reshape(-1), (0, LANES - DT_W)),
                          jnp.pad(d_skip[l], (0, LANES - SSD_HEADS))])
        return _pad_rows(rows, 8)

    ws_bf = w_s.astype(BF16)
    bst = jnp.pad(jnp.transpose(b_s, (0, 2, 1)), ((0, 0), (0, 0), (0, LANES - MLP_GROUPS)))

    X = jnp.concatenate([ctx[0], x[0]], axis=0)
    saved = []
    for l in range(depth):
        w_main[l], w_dt[l] = in_weights(w_in_g)
        w_o[l] = w_out_g.reshape(N_DEV * out_r, Dm)
        hx = _prenorm_fwd(X, _row(g_pre[l]), scale2[l], shift2[l], n_ctx)
        if l + 1 < depth:
            z, w_in_g = _mm(hx, w_main[l], "nn", BF16, "in_proj_gather", TM, 512, 2048,
                            exchange=("gather", w_in_bf[l + 1]))
        else:
            z = _mm(hx, w_main[l], "nn", BF16, "in_proj", TM, 512, 2048)
        raw = _mm(hx, w_dt[l], "nn", F32, "dt_proj", TM, LANES, 2048)
        cw = _pad_rows(conv_full[l], 8)
        cb = _row(conv_b[l])
        xbc = _conv_fwd(z, cw, cb, n_ctx)
        prm = ssd_prm(l)
        prep = _ssd_prep(raw, prm)
        dskx = _pad_rows(_row(jnp.repeat(d_skip[l], SSD_HEAD_DIM)), 8)
        y_f, sst_f = _ssd_fwd(xbc, prep, dskx, 0, n_ctx)
        y_b, sst_b = _ssd_fwd(xbc, prep, dskx, 1, n_ctx)
        ycat = _gate_fwd(y_f, y_b, z, _row(g_ssd[l]))
        ycat = _mlp_fwd(z, _row(g_v[l]), ws_bf[l], bst[l], _row(g_mlp[l]), ycat)
        if l + 1 < depth:
            o, w_out_g = _mm(ycat, w_o[l], "nn", F32, "out_proj_gather", TM, 512, 2048,
                             exchange=("gather", w_out_bf[l + 1]))
        else:
            o = _mm(ycat, w_o[l], "nn", F32, "out_proj", TM, 512, 2048)
        saved.append((X, hx, z, raw, cw, cb, xbc, prm, prep, dskx, y_f, sst_f, y_b, sst_b, ycat, o))
        X = _post_fwd(X, o, _row(g_post[l]), gate2[l], n_ctx)

    dX, sq = _loss_kernel(X, loss_target[0], n_ctx)
    loss = lax.psum(0.5 * jnp.sum(sq) / Dm, ("x", "y", "c"))

    g_small = {k: [None] * depth for k in
               ("b_ada", "g_pre", "g_post", "conv_b", "dt_bias", "a_log", "d_skip", "g_ssd", "g_v", "w_s", "b_s",
                "g_mlp", "conv_w", "dmod_c", "dmod")}
    in_recv, out_recv = [None] * depth, [None] * depth
    in_slabs = None
    for l in reversed(range(depth)):
        Xl, hx, z, raw, cw, cb, xbc, prm, prep, dskx, y_f, sst_f, y_b, sst_b, ycat, o = saved[l]
        d_o, acc_post = _post_bwd(o, _row(g_post[l]), gate2[l], dX, n_ctx)
        dycat = _mm(d_o, w_o[l], "nt", BF16, "out_proj_dx", TM, 512, 2048)
        out_slabs = _mm(ycat, d_o, "tn", BF16, "out_proj_dw", 1024, 1024, 768).reshape(N_DEV, out_r, Dm)
        dz = lax.empty((T, Z_MAIN), BF16)
        dy, dz, acc_gate = _gate_bwd(y_f, y_b, z, _row(g_ssd[l]), dycat, dz)
        dz, acc_mlp, dws, dbst = _mlp_bwd(z, _row(g_v[l]), ws_bf[l], bst[l], _row(g_mlp[l]), dycat, dz)
        dx0, db0, dc0, ds0, ddt0, dtot0, dsk0 = _ssd_bwd(xbc, prep, dskx, sst_f, dy, 0, n_ctx)
        dx1, db1, dc1, ds1, ddt1, dtot1, _ = _ssd_bwd(xbc, prep, dskx, sst_b, dy, 1, n_ctx)
        draw, dprm = _ssd_post(raw, prm, prep[0], ds0, ds1, ddt0, ddt1, dtot0, dtot1)
        gn = SSD_GROUPS * SSD_STATE
        dz, dcw_x, dcb_x = _conv_bwd(z, cw, cb, dx0, dx1, dz, 0, n_ctx)
        dz, dcw_b, dcb_b = _conv_bwd(z, cw, cb, db0, db1, dz, SSD_W, n_ctx)
        dz, dcw_c, dcb_c = _conv_bwd(z, cw, cb, dc0, dc1, dz, SSD_W + gn, n_ctx)
        dcw = jnp.concatenate([dcw_x, dcw_b, dcw_c], axis=1)
        dcb = jnp.concatenate([dcb_x, dcb_b, dcb_c], axis=1)
        if in_slabs is not None:
            dhx1, in_recv[l + 1] = _mm(dz, w_main[l], "nt", F32, "in_proj_dx_scatter", TM, 512, 2048,
                                       exchange=("scatter", in_slabs))
        else:
            dhx1 = _mm(dz, w_main[l], "nt", F32, "in_proj_dx", TM, 512, 2048)
        dhx2 = _mm(draw, w_dt[l], "nt", F32, "dt_proj_dx", TM, 512, LANES)
        gw_main, out_recv[l] = _mm(hx, dz, "tn", BF16, "in_proj_dw_scatter", 1024, 1024, 768,
                                   exchange=("scatter", out_slabs))
        gw_dt = _mm(hx, draw, "tn", BF16, "dt_proj_dw", 1024, LANES, 768)
        gw_full = jnp.concatenate([gw_main[:, :XBC_W], gw_dt[:, :DT_W], gw_main[:, XBC_W:]], axis=1)
        in_slabs = jnp.transpose(gw_full.reshape(Dm, N_DEV, in_w), (1, 0, 2))
        dX, acc_pre = _prenorm_bwd(Xl, _row(g_pre[l]), scale2[l], shift2[l], dhx1, dhx2, dX, n_ctx)

        dmod_c = jnp.concatenate([acc_pre[3], acc_pre[1], acc_post[1]])
        dmod_x = jnp.concatenate([acc_pre[4], acc_pre[2], acc_post[2]])
        g_small["dmod_c"][l] = dmod_c
        g_small["dmod"][l] = dmod_x
        g_small["b_ada"][l] = dmod_c + dmod_x
        g_small["g_pre"][l] = acc_pre[0]
        g_small["g_post"][l] = acc_post[0]
        g_small["conv_b"][l] = dcb[0]
        g_small["conv_w"][l] = dcw[:CONV_W]
        g_small["dt_bias"][l] = dprm[0, :DT_W].reshape(2, SSD_HEADS)
        g_small["a_log"][l] = dprm[1, :DT_W].reshape(2, SSD_HEADS)
        g_small["d_skip"][l] = dsk0[0, :SSD_HEADS]
        g_small["g_ssd"][l] = acc_gate[0]
        g_small["g_v"][l] = acc_mlp[0]
        g_small["g_mlp"][l] = acc_mlp[1]
        g_small["w_s"][l] = dws
        g_small["b_s"][l] = jnp.transpose(dbst[:, :MLP_GROUPS])

    grad_x = dX[n_ctx:][None]

    summed_names = ["b_ada", "g_pre", "g_post", "conv_b", "dt_bias", "a_log", "d_skip", "g_ssd", "g_v", "w_s",
                    "b_s", "g_mlp", "conv_w", "dmod_c"]
    parts = [jnp.stack(g_small[k]) for k in summed_names] + [jnp.stack(g_small["dmod"])]
    shapes = [p.shape for p in parts]
    bundle = _pack(parts)
    gathered = _all_gather_big(bundle, "gather_small_grads")
    reduced = _unpack(_sum_slabs(gathered, "sum_small_grads"), shapes)
    gs = dict(zip(summed_names, reduced[:-1]))
    n_el = int(np.prod(shapes[-1]))
    off = sum(int(np.prod(s)) for s in shapes[:-1])
    dmod_all = gathered.reshape(N_DEV, -1)[:, off:off + n_el].reshape(N_DEV, depth, 3 * Dm)

    dmod_rows = jnp.concatenate([jnp.transpose(dmod_all, (1, 0, 2)), gs["dmod_c"][:, None, :]], axis=1)
    dmod_rows = lax.dynamic_slice_in_dim(dmod_rows, me * ada_w, ada_w, axis=2)
    dmod_rows = jnp.pad(dmod_rows, ((0, 0), (0, ADA_ROWS - 9), (0, 0)))
    g_w_ada, dc_part = _ada_bwd(c16, w_ada, dmod_rows)
    dc_all = _all_gather_small(dc_part, "gather_dc")
    g_c_ctx = _sum_slabs(dc_all, "sum_dc")[0]

    grads_small = {"c_ctx": g_c_ctx, "b_ada": gs["b_ada"], "g_pre": gs["g_pre"], "g_post": gs["g_post"],
                   "conv_b": gs["conv_b"], "dt_bias": gs["dt_bias"], "a_log": gs["a_log"], "d_skip": gs["d_skip"],
                   "g_ssd": gs["g_ssd"], "g_v": gs["g_v"], "w_s": gs["w_s"], "b_s": gs["b_s"], "g_mlp": gs["g_mlp"],
                   "conv_w": lax.dynamic_slice_in_dim(gs["conv_w"], me * conv_c, conv_c, axis=2)}

    in_recv[0] = _slab_exchange(in_slabs, "scatter_w_in")

    g_in, d_in, nm_in, nv_in = _adamw_slabs(w_in, in_recv, m_w_in, v_w_in, "adamw_w_in", 128)
    g_out, d_out, nm_out, nv_out = _adamw_slabs(w_out, out_recv, m_w_out, v_w_out, "adamw_w_out", 128)
    d_ada, nm_ada, nv_ada = _adamw(w_ada.reshape(depth * Dm, ada_w), g_w_ada.reshape(depth * Dm, ada_w),
                                   m_w_ada.reshape(depth * Dm, ada_w), v_w_ada.reshape(depth * Dm, ada_w),
                                   "adamw_w_ada", 256)

    small_names = ["c_ctx", "b_ada", "g_pre", "g_post", "conv_w", "conv_b", "dt_bias", "a_log", "d_skip", "g_ssd",
                   "g_v", "w_s", "b_s", "g_mlp"]
    small_w = dict(c_ctx=c_ctx, b_ada=b_ada, g_pre=g_pre, g_post=g_post, conv_w=conv_w, conv_b=conv_b,
                   dt_bias=dt_bias, a_log=a_log, d_skip=d_skip, g_ssd=g_ssd, g_v=g_v, w_s=w_s, b_s=b_s, g_mlp=g_mlp)
    small_m = dict(c_ctx=m_c_ctx, b_ada=m_b_ada, g_pre=m_g_pre, g_post=m_g_post, conv_w=m_conv_w, conv_b=m_conv_b,
                   dt_bias=m_dt_bias, a_log=m_a_log, d_skip=m_d_skip, g_ssd=m_g_ssd, g_v=m_g_v, w_s=m_w_s,
                   b_s=m_b_s, g_mlp=m_g_mlp)
    small_v = dict(c_ctx=v_c_ctx, b_ada=v_b_ada, g_pre=v_g_pre, g_post=v_g_post, conv_w=v_conv_w, conv_b=v_conv_b,
                   dt_bias=v_dt_bias, a_log=v_a_log, d_skip=v_d_skip, g_ssd=v_g_ssd, g_v=v_g_v, w_s=v_w_s,
                   b_s=v_b_s, g_mlp=v_g_mlp)
    s_shapes = [small_w[k].shape for k in small_names]
    d_s, nm_s, nv_s = _adamw(_pack([small_w[k] for k in small_names]),
                             _pack([grads_small[k].reshape(small_w[k].shape) for k in small_names]),
                             _pack([small_m[k] for k in small_names]), _pack([small_v[k] for k in small_names]),
                             "adamw_small", 512)
    d_s = dict(zip(small_names, _unpack(d_s, s_shapes)))
    nm_s = dict(zip(small_names, _unpack(nm_s, s_shapes)))
    nv_s = dict(zip(small_names, _unpack(nv_s, s_shapes)))

    big = {"w_ada": (g_w_ada, d_ada.reshape(w_ada.shape), nm_ada.reshape(w_ada.shape), nv_ada.reshape(w_ada.shape)),
           "w_in": tuple(t.reshape(w_in.shape) for t in (g_in, d_in, nm_in, nv_in)),
           "w_out": tuple(t.reshape(w_out.shape) for t in (g_out, d_out, nm_out, nv_out))}
    order = ["c_ctx", "w_ada", "b_ada", "g_pre", "g_post", "w_in", "conv_w", "conv_b", "dt_bias", "a_log", "d_skip",
             "g_ssd", "g_v", "w_s", "b_s", "g_mlp", "w_out"]

    def pick(k, idx):
        if k in big:
            return big[k][idx]
        return (grads_small[k].reshape(small_w[k].shape), d_s[k], nm_s[k], nv_s[k])[idx]

    return (loss, grad_x, *[pick(k, 0) for k in order], *[pick(k, 1) for k in order],
            *[pick(k, 2) for k in order], *[pick(k, 3) for k in order])
```

```python
import numpy as np
import jax
import jax.numpy as jnp
from jax import lax
from jax.experimental import pallas as pl
from jax.experimental.pallas import tpu as pltpu

F32 = jnp.float32
BF16 = jnp.bfloat16

D_MODEL = 2048
GRID_W = 64
SSD_W = 2048
SSD_HEADS = 32
SSD_HEAD_DIM = 64
SSD_GROUPS = 8
HEADS_PER_GROUP = 4
SSD_STATE = 128
CHUNK = 128
SSD_Q = 256
CONV_W = 5
MLP_W = 2048
MLP_GROUPS = 16
XBC_W = 4096
DT_W = 64
IN_W = 12352
Z_MAIN = IN_W - DT_W
GROUP_COLS = HEADS_PER_GROUP * SSD_HEAD_DIM
EPS = 1e-6
N_DEV = 8

ADAM_LR = 0.001
ADAM_B1 = 0.9
ADAM_B2 = 0.999
ADAM_EPS = 1e-08
ADAM_WD = 0.01
ADAM_STEP = 10

LANES = 128
NEG_BIG = -1e30

NN = (((1,), (0,)), ((), ()))
NT = (((1,), (1,)), ((), ()))
TN = (((0,), (0,)), ((), ()))
HI = lax.Precision.HIGHEST


def _dot(a, b, dims):
    return lax.dot_general(a.astype(BF16), b.astype(BF16), dims, preferred_element_type=F32)


def _dot_hi(a, b, dims=NN):
    return lax.dot_general(a, b, dims, preferred_element_type=F32, precision=HI)


def _sigmoid(x):
    return 1.0 / (1.0 + jnp.exp(-x))


def _silu(x):
    return x * _sigmoid(x)


def _softplus(x):
    return jnp.maximum(x, 0.0) + jnp.log(1.0 + jnp.exp(-jnp.abs(x)))


def _call(body, *, name, out_shape, grid=None, in_specs=None, out_specs=None, scratch=(),
          sem=None, vmem_mb=None, aliases=None):
    params = {}
    if sem is not None:
        params["dimension_semantics"] = sem
    if vmem_mb is not None:
        params["vmem_limit_bytes"] = vmem_mb << 20
    kw = {}
    if grid is not None:
        kw["grid"] = grid
    if in_specs is not None:
        kw["in_specs"] = in_specs
    if out_specs is not None:
        kw["out_specs"] = out_specs
    return pl.pallas_call(body, name=name, out_shape=out_shape, scratch_shapes=list(scratch),
                          input_output_aliases=aliases or {},
                          compiler_params=pltpu.CompilerParams(**params), **kw)


def _sds(shape, dtype):
    return jax.ShapeDtypeStruct(tuple(shape), dtype)


ANY = pl.BlockSpec(memory_space=pl.ANY)


def _my_pos():
    return lax.axis_index("x"), lax.axis_index("y"), lax.axis_index("c")


def _flip(v, bit):
    return 1 - v if bit else v


def _peer(pos, k):
    mx, my, mc = pos
    return (_flip(mx, (k >> 2) & 1), _flip(my, (k >> 1) & 1), _flip(mc, k & 1))


def _lin(pos):
    return 4 * pos[0] + 2 * pos[1] + pos[2]


def _all_gather_small(x, name):
    R, C = x.shape

    def body(x_ref, o_ref, send_sems, recv_sems):
        me = _my_pos()
        o_ref[_lin(me)] = x_ref[...]
        sends = []
        for k in range(1, N_DEV):
            peer = _peer(me, k)
            cp = pltpu.make_async_remote_copy(
                src_ref=x_ref, dst_ref=o_ref.at[_lin(me)], send_sem=send_sems.at[k - 1],
                recv_sem=recv_sems.at[k - 1], device_id=peer, device_id_type=pl.DeviceIdType.MESH)
            cp.start()
            sends.append(cp)
        for k in range(1, N_DEV):
            peer = _peer(me, k)
            pltpu.make_async_remote_copy(
                src_ref=x_ref, dst_ref=o_ref.at[_lin(peer)], send_sem=send_sems.at[k - 1],
                recv_sem=recv_sems.at[k - 1], device_id=peer,
                device_id_type=pl.DeviceIdType.MESH).wait_recv()
        for cp in sends:
            cp.wait_send()

    return _call(body, name=name, out_shape=_sds((N_DEV, R, C), x.dtype),
                 in_specs=[pl.BlockSpec(memory_space=pltpu.VMEM)],
                 out_specs=pl.BlockSpec(memory_space=pltpu.VMEM),
                 scratch=[pltpu.SemaphoreType.DMA((N_DEV - 1,)), pltpu.SemaphoreType.DMA((N_DEV - 1,))],
                 vmem_mb=40)(x)


def _all_gather_big(x, name):
    def body(x_ref, o_ref, send_sems, recv_sems, local_sem):
        mx, my, mc = _my_pos()
        me, sibling = (mx, my, mc), (mx, my, 1 - mc)
        chips = [(1 - mx, my), (mx, 1 - my), (1 - mx, 1 - my)]

        def slot(pos):
            return o_ref.at[_lin(pos)]

        def copy(k, block, to, src=None):
            return pltpu.make_async_remote_copy(
                src_ref=slot(block) if src is None else src, dst_ref=slot(block),
                send_sem=send_sems.at[k], recv_sem=recv_sems.at[k], device_id=to,
                device_id_type=pl.DeviceIdType.MESH)

        mine = pltpu.make_async_copy(x_ref, slot(me), local_sem)
        mine.start()
        first = [copy(0, me, sibling, src=x_ref)]
        first += [copy(1 + j, me, (*chip, mc), src=x_ref) for j, chip in enumerate(chips)]
        for cp in first:
            cp.start()
        passed = [copy(4 + j, (*chip, mc), sibling) for j, chip in enumerate(chips)]
        for j, chip in enumerate(chips):
            copy(1 + j, (*chip, mc), me).wait_recv()
            passed[j].start()
        copy(0, sibling, me).wait_recv()
        for j, chip in enumerate(chips):
            copy(4 + j, (*chip, 1 - mc), me).wait_recv()
        for cp in first + passed:
            cp.wait_send()
        mine.wait()

    return _call(body, name=name, out_shape=_sds((N_DEV,) + x.shape, x.dtype),
                 in_specs=[ANY], out_specs=ANY,
                 scratch=[pltpu.SemaphoreType.DMA((7,)), pltpu.SemaphoreType.DMA((7,)),
                          pltpu.SemaphoreType.DMA])(x)


EXCHANGE_SEMS = [pltpu.SemaphoreType.DMA((N_DEV - 1,)), pltpu.SemaphoreType.DMA((N_DEV - 1,)),
                 pltpu.SemaphoreType.DMA]


def _exchange_copies(kind, src_ref, dst_ref, send_sems, recv_sems, local_sem, with_arrivals=True):
    me = _my_pos()
    mine_src = src_ref if kind == "gather" else src_ref.at[_lin(me)]
    local = pltpu.make_async_copy(mine_src, dst_ref.at[_lin(me)], local_sem)
    sends, recvs = [], []
    for k in range(1, N_DEV):
        peer = _peer(me, k)
        out_src = src_ref if kind == "gather" else src_ref.at[_lin(peer)]
        sends.append(pltpu.make_async_remote_copy(
            src_ref=out_src, dst_ref=dst_ref.at[_lin(me)], send_sem=send_sems.at[k - 1],
            recv_sem=recv_sems.at[k - 1], device_id=peer, device_id_type=pl.DeviceIdType.MESH))
        if with_arrivals:
            recvs.append(pltpu.make_async_remote_copy(
                src_ref=out_src, dst_ref=dst_ref.at[_lin(peer)], send_sem=send_sems.at[k - 1],
                recv_sem=recv_sems.at[k - 1], device_id=peer, device_id_type=pl.DeviceIdType.MESH))
    return sends, recvs, local


def _exchange_start(kind, src_ref, dst_ref, send_sems, recv_sems, local_sem):
    sends, _, local = _exchange_copies(kind, src_ref, dst_ref, send_sems, recv_sems, local_sem, with_arrivals=False)
    local.start()
    for cp in sends:
        cp.start()


def _exchange_wait(kind, src_ref, dst_ref, send_sems, recv_sems, local_sem):
    sends, recvs, local = _exchange_copies(kind, src_ref, dst_ref, send_sems, recv_sems, local_sem)
    for cp in recvs:
        cp.wait_recv()
    for cp in sends:
        cp.wait_send()
    local.wait()


def _exchange_shape(kind, src):
    return _sds((N_DEV,) + src.shape if kind == "gather" else src.shape, src.dtype)


def _mm(a, b, mode, out_dtype, name, tm, tn, tk, exchange=None):
    if mode == "nn":
        (M, K), (K2, N) = a.shape, b.shape
    elif mode == "nt":
        (M, K), (N, K2) = a.shape, b.shape
    else:
        (K, M), (K2, N) = a.shape, b.shape
    assert K == K2
    tm, tn, tk = min(tm, M), min(tn, N), min(tk, K)
    assert M % tm == 0 and N % tn == 0 and K % tk == 0, (name, M, N, K, tm, tn, tk)
    ni, nj, nk = M // tm, N // tn, K // tk
    dims = {"nn": NN, "nt": NT, "tn": TN}[mode]

    def product(a_ref, b_ref, o_ref, acc):
        p = _dot(a_ref[...], b_ref[...], dims)
        if nk == 1:
            o_ref[...] = p.astype(out_dtype)
        else:
            acc_ref = acc[0]
            k = pl.program_id(2)

            @pl.when(k == 0)
            def _():
                acc_ref[...] = p

            @pl.when(k > 0)
            def _():
                acc_ref[...] += p

            @pl.when(k == nk - 1)
            def _():
                o_ref[...] = acc_ref[...].astype(out_dtype)

    if mode == "tn":
        a_spec = pl.BlockSpec((tk, tm), lambda i, j, k: (k, i))
    else:
        a_spec = pl.BlockSpec((tm, tk), lambda i, j, k: (i, k))
    if mode == "nt":
        b_spec = pl.BlockSpec((tn, tk), lambda i, j, k: (j, k))
    else:
        b_spec = pl.BlockSpec((tk, tn), lambda i, j, k: (k, j))
    o_spec = pl.BlockSpec((tm, tn), lambda i, j, k: (i, j))
    acc_scratch = [] if nk == 1 else [pltpu.VMEM((tm, tn), F32)]
    if exchange is None:
        def body(a_ref, b_ref, o_ref, *acc):
            product(a_ref, b_ref, o_ref, acc)

        return _call(body, name=name, out_shape=_sds((M, N), out_dtype), grid=(ni, nj, nk),
                     in_specs=[a_spec, b_spec], out_specs=o_spec, scratch=acc_scratch,
                     sem=("parallel", "parallel", "arbitrary"), vmem_mb=48)(a, b)

    kind, src = exchange

    def body(a_ref, b_ref, src_ref, o_ref, dst_ref, send_sems, recv_sems, local_sem, *acc):
        i, j, k = pl.program_id(0), pl.program_id(1), pl.program_id(2)

        @pl.when(jnp.logical_and(jnp.logical_and(i == 0, j == 0), k == 0))
        def _():
            _exchange_start(kind, src_ref, dst_ref, send_sems, recv_sems, local_sem)

        product(a_ref, b_ref, o_ref, acc)

        @pl.when(jnp.logical_and(jnp.logical_and(i == ni - 1, j == nj - 1), k == nk - 1))
        def _():
            _exchange_wait(kind, src_ref, dst_ref, send_sems, recv_sems, local_sem)

    return _call(body, name=name, out_shape=(_sds((M, N), out_dtype), _exchange_shape(kind, src)), grid=(ni, nj, nk),
                 in_specs=[a_spec, b_spec, ANY], out_specs=(o_spec, ANY), scratch=EXCHANGE_SEMS + acc_scratch,
                 sem=("arbitrary", "arbitrary", "arbitrary"), vmem_mb=48)(a, b, src)


def _rms(x):
    return lax.rsqrt(jnp.mean(x * x, axis=-1, keepdims=True) + EPS)


def _prenorm_f(x, g, sc, sh):
    return (x * _rms(x) * g) * (1.0 + sc) + sh


def _pick(is_ctx, ref):
    return jnp.where(is_ctx, ref[0:1, :], ref[1:2, :])


def _prenorm_fwd(X, g, sc2, sh2, n_ctx, tl=256):
    T, Dm = X.shape
    nct = n_ctx // tl

    def body(x_ref, g_ref, sc_ref, sh_ref, o_ref):
        is_ctx = pl.program_id(0) < nct
        o_ref[...] = _prenorm_f(x_ref[...], g_ref[...], _pick(is_ctx, sc_ref),
                                _pick(is_ctx, sh_ref)).astype(BF16)

    row = pl.BlockSpec((tl, Dm), lambda i: (i, 0))
    return _call(body, name="prenorm_fwd", out_shape=_sds((T, Dm), BF16), grid=(T // tl,),
                 in_specs=[row, pl.BlockSpec((1, Dm), lambda i: (0, 0)),
                           pl.BlockSpec((2, Dm), lambda i: (0, 0)), pl.BlockSpec((2, Dm), lambda i: (0, 0))],
                 out_specs=row, sem=("parallel",), vmem_mb=40)(X, g, sc2, sh2)


def _prenorm_bwd(X, g, sc2, sh2, d1, d2, dres, n_ctx, tl=256):
    T, Dm = X.shape
    nct = n_ctx // tl

    def body(x_ref, g_ref, sc_ref, sh_ref, d1_ref, d2_ref, dres_ref, dx_ref, acc_ref):
        i = pl.program_id(0)
        is_ctx = i < nct

        @pl.when(i == 0)
        def _():
            acc_ref[...] = jnp.zeros_like(acc_ref)

        _, vjp = jax.vjp(_prenorm_f, x_ref[...], g_ref[...], _pick(is_ctx, sc_ref), _pick(is_ctx, sh_ref))
        dx, dg, dsc, dsh = vjp(d1_ref[...] + d2_ref[...])
        dx_ref[...] = dres_ref[...] + dx
        zero = jnp.zeros_like(dsc)
        acc_ref[0:1, :] += dg
        acc_ref[1:2, :] += jnp.where(is_ctx, dsc, zero)
        acc_ref[2:3, :] += jnp.where(is_ctx, zero, dsc)
        acc_ref[3:4, :] += jnp.where(is_ctx, dsh, zero)
        acc_ref[4:5, :] += jnp.where(is_ctx, zero, dsh)

    row = pl.BlockSpec((tl, Dm), lambda i: (i, 0))
    return _call(body, name="prenorm_bwd", out_shape=(_sds((T, Dm), F32), _sds((8, Dm), F32)), grid=(T // tl,),
                 in_specs=[row, pl.BlockSpec((1, Dm), lambda i: (0, 0)),
                           pl.BlockSpec((2, Dm), lambda i: (0, 0)), pl.BlockSpec((2, Dm), lambda i: (0, 0)),
                           row, row, row],
                 out_specs=(row, pl.BlockSpec((8, Dm), lambda i: (0, 0))), sem=("arbitrary",),
                 vmem_mb=48)(X, g, sc2, sh2, d1, d2, dres)


def _post_f(o, g, gate):
    return gate * ((o * _rms(o)) * g)


def _post_fwd(X, o, g, gate2, n_ctx, tl=256):
    T, Dm = X.shape
    nct = n_ctx // tl

    def body(x_ref, o_ref, g_ref, gate_ref, y_ref):
        is_ctx = pl.program_id(0) < nct
        y_ref[...] = x_ref[...] + _post_f(o_ref[...], g_ref[...], _pick(is_ctx, gate_ref))

    row = pl.BlockSpec((tl, Dm), lambda i: (i, 0))
    return _call(body, name="post_fwd", out_shape=_sds((T, Dm), F32), grid=(T // tl,),
                 in_specs=[row, row, pl.BlockSpec((1, Dm), lambda i: (0, 0)), pl.BlockSpec((2, Dm), lambda i: (0, 0))],
                 out_specs=row, sem=("parallel",), vmem_mb=40)(X, o, g, gate2)


def _post_bwd(o, g, gate2, dX, n_ctx, tl=256):
    T, Dm = o.shape
    nct = n_ctx // tl

    def body(o_ref, g_ref, gate_ref, dx_ref, do_ref, acc_ref):
        i = pl.program_id(0)
        is_ctx = i < nct

        @pl.when(i == 0)
        def _():
            acc_ref[...] = jnp.zeros_like(acc_ref)

        _, vjp = jax.vjp(_post_f, o_ref[...], g_ref[...], _pick(is_ctx, gate_ref))
        do, dg, dgate = vjp(dx_ref[...])
        do_ref[...] = do.astype(BF16)
        zero = jnp.zeros_like(dgate)
        acc_ref[0:1, :] += dg
        acc_ref[1:2, :] += jnp.where(is_ctx, dgate, zero)
        acc_ref[2:3, :] += jnp.where(is_ctx, zero, dgate)

    row = pl.BlockSpec((tl, Dm), lambda i: (i, 0))
    return _call(body, name="post_bwd", out_shape=(_sds((T, Dm), BF16), _sds((8, Dm), F32)), grid=(T // tl,),
                 in_specs=[row, pl.BlockSpec((1, Dm), lambda i: (0, 0)), pl.BlockSpec((2, Dm), lambda i: (0, 0)), row],
                 out_specs=(row, pl.BlockSpec((8, Dm), lambda i: (0, 0))), sem=("arbitrary",),
                 vmem_mb=48)(o, g, gate2, dX)


def _loss_kernel(X, target, n_ctx, tl=256):
    T, Dm = X.shape
    nct = n_ctx // tl

    def body(x_ref, t_ref, dx_ref, acc_ref):
        i = pl.program_id(0)

        @pl.when(i == 0)
        def _():
            acc_ref[...] = jnp.zeros_like(acc_ref)

        @pl.when(i < nct)
        def _():
            dx_ref[...] = jnp.zeros_like(dx_ref)

        @pl.when(i >= nct)
        def _():
            d = x_ref[...] - t_ref[...]
            dx_ref[...] = d * (1.0 / Dm)
            acc_ref[...] += jnp.sum(d * d, axis=0, keepdims=True)

    row = pl.BlockSpec((tl, Dm), lambda i: (i, 0))
    trow = pl.BlockSpec((tl, Dm), lambda i: (jnp.maximum(i - nct, 0), 0))
    return _call(body, name="loss", out_shape=(_sds((T, Dm), F32), _sds((1, Dm), F32)), grid=(T // tl,),
                 in_specs=[row, trow], out_specs=(row, pl.BlockSpec((1, Dm), lambda i: (0, 0))),
                 sem=("arbitrary",), vmem_mb=40)(X, target)


CONV_TL = 256
CONV_CB = 512


def _conv_taps(x, pos, row_len, reverse):
    tl = x.shape[0]
    taps = []
    for k in range(CONV_W):
        off = (2 - k) if reverse else (k - 2)
        xs = x if off == 0 else pltpu.roll(x, (-off) % tl, 0)
        valid = jnp.logical_and(pos + off >= 0, pos + off < row_len)
        taps.append(jnp.where(valid, xs, 0.0))
    return taps


def _conv_pos(i, n_ctx, tl):
    row_len = jnp.where(i < n_ctx // tl, n_ctx, GRID_W)
    pos = jnp.bitwise_and(lax.broadcasted_iota(jnp.int32, (tl, 1), 0), row_len - 1)
    return pos, row_len


def _conv_fwd(z, cw, cb, n_ctx):
    T = z.shape[0]
    tl, cbw = CONV_TL, CONV_CB
    assert n_ctx == tl

    def body(z_ref, w_ref, b_ref, o_ref):
        pos, row_len = _conv_pos(pl.program_id(1), n_ctx, tl)
        taps = _conv_taps(z_ref[...].astype(F32), pos, row_len, False)
        pre = b_ref[...] + taps[0] * w_ref[0:1, :]
        for k in range(1, CONV_W):
            pre = pre + taps[k] * w_ref[k:k + 1, :]
        o_ref[...] = _silu(pre).astype(BF16)

    blk = pl.BlockSpec((tl, cbw), lambda j, i: (i, j))
    return _call(body, name="conv_fwd", out_shape=_sds((T, XBC_W), BF16), grid=(XBC_W // cbw, T // tl),
                 in_specs=[blk, pl.BlockSpec((8, cbw), lambda j, i: (0, j)), pl.BlockSpec((1, cbw), lambda j, i: (0, j))],
                 out_specs=blk, sem=("parallel", "parallel"))(z, cw, cb)


def _conv_bwd(z, cw, cb, d0, d1, dz_in, col0, n_ctx):
    T = z.shape[0]
    tl, cbw = CONV_TL, CONV_CB

    def body(z_ref, w_ref, b_ref, d0_ref, d1_ref, dzin_ref, dz_ref, dw_ref, db_ref):
        i = pl.program_id(1)

        @pl.when(i == 0)
        def _():
            dw_ref[...] = jnp.zeros_like(dw_ref)
            db_ref[...] = jnp.zeros_like(db_ref)

        pos, row_len = _conv_pos(i, n_ctx, tl)
        taps = _conv_taps(z_ref[...].astype(F32), pos, row_len, False)
        pre = b_ref[...] + taps[0] * w_ref[0:1, :]
        for k in range(1, CONV_W):
            pre = pre + taps[k] * w_ref[k:k + 1, :]
        s = _sigmoid(pre)
        dpre = (d0_ref[...].astype(F32) + d1_ref[...].astype(F32)) * (s + pre * s * (1.0 - s))
        dtaps = _conv_taps(dpre, pos, row_len, True)
        dx = dtaps[0] * w_ref[0:1, :]
        for k in range(1, CONV_W):
            dx = dx + dtaps[k] * w_ref[k:k + 1, :]
        dz_ref[...] = dx.astype(BF16)
        for k in range(CONV_W):
            dw_ref[k:k + 1, :] += jnp.sum(dpre * taps[k], axis=0, keepdims=True)
        db_ref[0:1, :] += jnp.sum(dpre, axis=0, keepdims=True)

    width = d0.shape[1]
    jo = col0 // cbw
    blk = pl.BlockSpec((tl, cbw), lambda j, i: (i, jo + j))
    dblk = pl.BlockSpec((tl, cbw), lambda j, i: (i, j))
    par = pl.BlockSpec((8, cbw), lambda j, i: (0, jo + j))
    opar = pl.BlockSpec((8, cbw), lambda j, i: (0, j))
    return _call(body, name=f"conv_bwd_{col0}",
                 out_shape=(_sds(dz_in.shape, BF16), _sds((8, width), F32), _sds((8, width), F32)),
                 grid=(width // cbw, T // tl),
                 in_specs=[blk, par, pl.BlockSpec((1, cbw), lambda j, i: (0, jo + j)), dblk, dblk, ANY],
                 out_specs=(blk, opar, opar), sem=("parallel", "arbitrary"), aliases={5: 0})(z, cw, cb, d0, d1, dz_in)


MIX_TL = 128


def _gate_f(yf, yb, zs, g):
    t = (yf + yb) * _silu(zs)
    return t * _rms(t) * g


def _gate_fwd(yf, yb, z, g):
    T = yf.shape[0]
    tl = MIX_TL

    def body(yf_ref, yb_ref, zs_ref, g_ref, o_ref):
        o_ref[...] = _gate_f(yf_ref[...].astype(F32), yb_ref[...].astype(F32), zs_ref[...].astype(F32),
                             g_ref[...]).astype(BF16)

    row = pl.BlockSpec((tl, SSD_W), lambda i: (i, 0))
    return _call(body, name="gate_fwd", out_shape=_sds((T, SSD_W + MLP_W), BF16), grid=(T // tl,),
                 in_specs=[row, row, pl.BlockSpec((tl, SSD_W), lambda i: (i, XBC_W // SSD_W)),
                           pl.BlockSpec((1, SSD_W), lambda i: (0, 0))],
                 out_specs=row, sem=("parallel",))(yf, yb, z, g)


def _gate_bwd(yf, yb, z, g, dycat, dz_in):
    T = yf.shape[0]
    tl = MIX_TL

    def body(yf_ref, yb_ref, zs_ref, g_ref, d_ref, dzin_ref, dy_ref, dz_ref, acc_ref):
        @pl.when(pl.program_id(0) == 0)
        def _():
            acc_ref[...] = jnp.zeros_like(acc_ref)

        _, vjp = jax.vjp(_gate_f, yf_ref[...].astype(F32), yb_ref[...].astype(F32),
                         zs_ref[...].astype(F32), g_ref[...])
        dyf, _, dzs, dg = vjp(d_ref[...].astype(F32))
        dy_ref[...] = dyf.astype(BF16)
        dz_ref[...] = dzs.astype(BF16)
        acc_ref[0:1, :] += dg

    row = pl.BlockSpec((tl, SSD_W), lambda i: (i, 0))
    zs_spec = pl.BlockSpec((tl, SSD_W), lambda i: (i, XBC_W // SSD_W))
    return _call(body, name="gate_bwd",
                 out_shape=(_sds((T, SSD_W), BF16), _sds(dz_in.shape, BF16), _sds((8, SSD_W), F32)),
                 grid=(T // tl,),
                 in_specs=[row, row, zs_spec, pl.BlockSpec((1, SSD_W), lambda i: (0, 0)), row, ANY],
                 out_specs=(row, zs_spec, pl.BlockSpec((8, SSD_W), lambda i: (0, 0))),
                 sem=("arbitrary",), aliases={5: 1}, vmem_mb=40)(yf, yb, z, g, dycat, dz_in)


def _vnorm_f(v, gv):
    return v * _rms(v) * gv


def _mlp_out_f(u, sg, zm, gm):
    t = u * sg * _silu(zm)
    return t * _rms(t) * gm


U_BLK = (XBC_W + SSD_W) // MLP_W


def _mlp_mix(ws_ref, bst_ref, vn_s, sg_s):
    for gi in range(MLP_GROUPS):
        cols = pl.ds(gi * LANES, LANES)
        sg_s[:, cols] = _dot(ws_ref[gi], vn_s[:, cols], NN) + bst_ref[:, gi:gi + 1]


def _mlp_fwd(z, gv, ws, bst, gm, ycat_in):
    T = z.shape[0]
    tl = CHUNK

    def body(u_ref, v_ref, zm_ref, gv_ref, ws_ref, bst_ref, gm_ref, yin_ref, o_ref, vn_s, sg_s):
        vn_s[...] = _vnorm_f(v_ref[...].astype(F32), gv_ref[...]).astype(BF16)
        _mlp_mix(ws_ref, bst_ref, vn_s, sg_s)
        o_ref[...] = _mlp_out_f(u_ref[...].astype(F32), sg_s[...], zm_ref[...].astype(F32),
                                gm_ref[...]).astype(BF16)

    def zblk(b):
        return pl.BlockSpec((tl, MLP_W), lambda i: (i, b))

    vec = pl.BlockSpec((1, MLP_W), lambda i: (0, 0))
    return _call(body, name="mlp_fwd", out_shape=_sds(ycat_in.shape, BF16), grid=(T // tl,),
                 in_specs=[zblk(U_BLK), zblk(U_BLK + 1), zblk(U_BLK + 2), vec,
                           pl.BlockSpec((MLP_GROUPS, CHUNK, CHUNK), lambda i: (0, 0, 0)),
                           pl.BlockSpec((CHUNK, LANES), lambda i: (0, 0)), vec, ANY],
                 out_specs=pl.BlockSpec((tl, MLP_W), lambda i: (i, 1)),
                 scratch=[pltpu.VMEM((tl, MLP_W), BF16), pltpu.VMEM((tl, MLP_W), F32)],
                 sem=("parallel",), aliases={7: 0})(z, z, z, gv, ws, bst, gm, ycat_in)


def _mlp_bwd(z, gv, ws, bst, gm, dycat, dz_in):
    T = z.shape[0]
    tl = CHUNK

    def body(u_ref, v_ref, zm_ref, gv_ref, ws_ref, bst_ref, gm_ref, d_ref, dzin_ref,
             dz_ref, acc_ref, dws_ref, dbst_ref, vn_s, sg_s, dvn_s):
        @pl.when(pl.program_id(0) == 0)
        def _():
            acc_ref[...] = jnp.zeros_like(acc_ref)
            dws_ref[...] = jnp.zeros_like(dws_ref)
            dbst_ref[...] = jnp.zeros_like(dbst_ref)

        v = v_ref[...].astype(F32)
        vn, vjp_v = jax.vjp(_vnorm_f, v, gv_ref[...])
        vn_s[...] = vn.astype(BF16)
        _mlp_mix(ws_ref, bst_ref, vn_s, sg_s)
        _, vjp_o = jax.vjp(_mlp_out_f, u_ref[...].astype(F32), sg_s[...], zm_ref[...].astype(F32), gm_ref[...])
        du, dsg, dzm, dgm = vjp_o(d_ref[...].astype(F32))
        sg_s[...] = dsg
        for gi in range(MLP_GROUPS):
            cols = pl.ds(gi * LANES, LANES)
            dsg_g = sg_s[:, cols]
            dvn_s[:, cols] = _dot(ws_ref[gi], dsg_g, TN)
            dws_ref[gi] += _dot(dsg_g, vn_s[:, cols], NT)
            dbst_ref[:, gi:gi + 1] += jnp.sum(dsg_g, axis=1, keepdims=True)
        dv, dgv = vjp_v(dvn_s[...])
        dz_ref[:, 0:MLP_W] = du.astype(BF16)
        dz_ref[:, MLP_W:2 * MLP_W] = dv.astype(BF16)
        dz_ref[:, 2 * MLP_W:3 * MLP_W] = dzm.astype(BF16)
        acc_ref[0:1, :] += dgv
        acc_ref[1:2, :] += dgm

    def zblk(b):
        return pl.BlockSpec((tl, MLP_W), lambda i: (i, b))

    vec = pl.BlockSpec((1, MLP_W), lambda i: (0, 0))
    ws_spec = pl.BlockSpec((MLP_GROUPS, CHUNK, CHUNK), lambda i: (0, 0, 0))
    bst_spec = pl.BlockSpec((CHUNK, LANES), lambda i: (0, 0))
    return _call(body, name="mlp_bwd",
                 out_shape=(_sds(dz_in.shape, BF16), _sds((8, MLP_W), F32),
                            _sds((MLP_GROUPS, CHUNK, CHUNK), F32), _sds((CHUNK, LANES), F32)),
                 grid=(T // tl,),
                 in_specs=[zblk(U_BLK), zblk(U_BLK + 1), zblk(U_BLK + 2), vec, ws_spec, bst_spec, vec,
                           pl.BlockSpec((tl, MLP_W), lambda i: (i, 1)), ANY],
                 out_specs=(pl.BlockSpec((tl, 3 * MLP_W), lambda i: (i, 1)),
                            pl.BlockSpec((8, MLP_W), lambda i: (0, 0)), ws_spec, bst_spec),
                 scratch=[pltpu.VMEM((tl, MLP_W), BF16), pltpu.VMEM((tl, MLP_W), F32), pltpu.VMEM((tl, MLP_W), F32)],
                 sem=("arbitrary",), aliases={8: 0}, vmem_mb=48)(z, z, z, gv, ws, bst, gm, dycat, dz_in)


def _ssd_constants(direction):
    q = SSD_Q
    tri = np.tril(np.ones((q, q), np.float32))
    if direction == 1:
        tri = tri.T
    ex = np.zeros((SSD_GROUPS, LANES, GROUP_COLS), np.float32)
    for g in range(SSD_GROUPS):
        for r in range(HEADS_PER_GROUP):
            ex[g, direction * SSD_HEADS + HEADS_PER_GROUP * g + r, r * SSD_HEAD_DIM:(r + 1) * SSD_HEAD_DIM] = 1.0
    hm = np.zeros((8, GROUP_COLS), np.float32)
    for r in range(HEADS_PER_GROUP):
        hm[r, r * SSD_HEAD_DIM:(r + 1) * SSD_HEAD_DIM] = 1.0
    return (jnp.asarray(tri), jnp.asarray(ex, dtype=BF16),
            jnp.asarray(np.ascontiguousarray(ex.transpose(0, 2, 1)), dtype=BF16), jnp.asarray(hm))


def _dot2(v, m, dims=NN):
    hi = v.astype(BF16)
    lo = (v - hi.astype(F32)).astype(BF16)
    return (lax.dot_general(hi, m, dims, preferred_element_type=F32)
            + lax.dot_general(lo, m, dims, preferred_element_type=F32))


def _ssd_prep(raw, prm):
    T = raw.shape[0]
    nc, Q = T // SSD_Q, SSD_Q
    tril = jnp.asarray(np.tril(np.ones((Q, Q), np.float32)))
    triu = jnp.asarray(np.triu(np.ones((Q, Q), np.float32)))

    def body(raw_ref, prm_ref, tril_ref, triu_ref, dt_ref, s_ref, e_ref, d_ref, st_ref, etot_ref):
        dt = _softplus(raw_ref[...] + prm_ref[0:1, :])
        a = dt * (-jnp.exp(prm_ref[1:2, :]))
        lane = lax.broadcasted_iota(jnp.int32, (1, LANES), 1)
        s = jnp.where(lane < SSD_HEADS, _dot_hi(tril_ref[...], a), _dot_hi(triu_ref[...], a))
        tot = jnp.sum(a, axis=0, keepdims=True)
        dt_ref[...] = dt
        s_ref[...] = s
        e_ref[...] = jnp.exp(s)
        d_ref[...] = jnp.exp(tot - s)
        st_ref[0] = s.T
        etot_ref[0] = _bcast8(jnp.exp(tot))

    row = pl.BlockSpec((Q, LANES), lambda c: (c, 0))
    cst = lambda shape: pl.BlockSpec(shape, lambda c: (0, 0))
    tl = _sds((T, LANES), F32)
    return _call(body, name="ssd_prep", out_shape=(tl, tl, tl, tl, _sds((nc, LANES, Q), F32), _sds((nc, 8, LANES), F32)),
                 grid=(nc,), in_specs=[row, cst((8, LANES)), cst((Q, Q)), cst((Q, Q))],
                 out_specs=(row, row, row, row, pl.BlockSpec((1, LANES, Q), lambda c: (c, 0, 0)),
                            pl.BlockSpec((1, 8, LANES), lambda c: (c, 0, 0))),
                 sem=("parallel",))(raw, prm, tril, triu)


def _ssd_post(raw, prm, dt, ds0, ds1, ddt0, ddt1, dtot0, dtot1, daz0, daz1):
    T = raw.shape[0]
    nc, Q = T // SSD_Q, SSD_Q
    tril = jnp.asarray(np.tril(np.ones((Q, Q), np.float32)))
    triu = jnp.asarray(np.triu(np.ones((Q, Q), np.float32)))

    def body(raw_ref, prm_ref, dt_ref, ds0_ref, ds1_ref, ddt0_ref, ddt1_ref, dtot0_ref, dtot1_ref, daz0_ref, daz1_ref,
             tril_ref, triu_ref, draw_ref, dprm_ref):
        @pl.when(pl.program_id(0) == 0)
        def _():
            dprm_ref[...] = jnp.zeros_like(dprm_ref)

        pre = raw_ref[...] + prm_ref[0:1, :]
        A = -jnp.exp(prm_ref[1:2, :])
        ds = ds0_ref[...] + ds1_ref[...]
        lane = lax.broadcasted_iota(jnp.int32, (1, LANES), 1)
        da = jnp.where(lane < SSD_HEADS, _dot_hi(triu_ref[...], ds), _dot_hi(tril_ref[...], ds))
        da = da + dtot0_ref[0, 0:1, :] + dtot1_ref[0, 0:1, :]
        daz_rows = jnp.concatenate([daz0_ref[0], daz1_ref[0], jnp.zeros((LANES - 2 * SSD_HEADS, Q), F32)], axis=0)
        da = da + daz_rows.T
        draw = (da * A + ddt0_ref[...] + ddt1_ref[...]) * _sigmoid(pre)
        draw_ref[...] = draw
        dprm_ref[0:1, :] += jnp.sum(draw, axis=0, keepdims=True)
        dprm_ref[1:2, :] += jnp.sum(da * dt_ref[...], axis=0, keepdims=True) * A

    row = pl.BlockSpec((Q, LANES), lambda c: (c, 0))
    tot = pl.BlockSpec((1, 8, LANES), lambda c: (c, 0, 0))
    daz = pl.BlockSpec((1, SSD_HEADS, Q), lambda c: (c, 0, 0))
    cst = lambda shape: pl.BlockSpec(shape, lambda c: (0, 0))
    return _call(body, name="ssd_post", out_shape=(_sds((T, LANES), F32), _sds((8, LANES), F32)), grid=(nc,),
                 in_specs=[row, cst((8, LANES)), row, row, row, row, row, tot, tot, daz, daz, cst((Q, Q)), cst((Q, Q))],
                 out_specs=(row, cst((8, LANES))), sem=("arbitrary",))(raw, prm, dt, ds0, ds1, ddt0, ddt1, dtot0, dtot1,
                                                                       daz0, daz1, tril, triu)


def _chunk_order(direction, nc, nctx):
    if direction == 0:
        return lambda c: c
    return lambda c: jnp.where(c < nctx, nctx - 1 - c, nc - 1 + nctx - c)


def _bcast8(row):
    return jnp.broadcast_to(row, (8, row.shape[1]))


GPS = 2


def _cols(ref, k, width):
    return ref[:, k * width:(k + 1) * width]


def _ssd_common(k, x_ref, b_ref, c_ref, dt_ref, s_ref, e_ref, d_ref, st_ref, etot_ref, ex_ref, direction):
    g = pl.program_id(1) * GPS + k
    base = direction * SSD_HEADS + HEADS_PER_GROUP * g
    x = _cols(x_ref, k, GROUP_COLS).astype(F32)
    ex = ex_ref[k]
    dt_x = _dot2(dt_ref[...], ex)
    E_x = _dot2(e_ref[...], ex)
    D_x = _dot2(d_ref[...], ex)
    lane = lax.broadcasted_iota(jnp.int32, (1, LANES), 1)
    etot = etot_ref[0, 0:1, :]
    rowi = lax.broadcasted_iota(jnp.int32, (GROUP_COLS, 1), 0)
    e = [jnp.sum(jnp.where(lane == base + r, etot, 0.0), axis=1, keepdims=True) for r in range(HEADS_PER_GROUP)]
    Etot_c = jnp.where(rowi < SSD_HEAD_DIM, e[0],
                       jnp.where(rowi < 2 * SSD_HEAD_DIM, e[1], jnp.where(rowi < 3 * SSD_HEAD_DIM, e[2], e[3])))
    rows = [st_ref[0, HEADS_PER_GROUP * k + r:HEADS_PER_GROUP * k + r + 1, :] for r in range(HEADS_PER_GROUP)]
    return dict(g=g, base=base, x=x, s=s_ref[...], rows=rows, dt_x=dt_x, lane=lane, X=x * dt_x, E_x=E_x, D_x=D_x,
                Etot_c=Etot_c, Bm=_cols(b_ref, k, SSD_STATE), Cm=_cols(c_ref, k, SSD_STATE))


def _head_decay(q, r, tri_ref):
    sel = q["lane"] == q["base"] + r
    col = jnp.sum(jnp.where(sel, q["s"], 0.0), axis=1, keepdims=True)
    return sel, jnp.exp(jnp.where(tri_ref[...] > 0.5, col - q["rows"][r], NEG_BIG))


def _ssd_in_specs(order, direction):
    Q, GC, N, G = SSD_Q, GROUP_COLS, SSD_STATE, SSD_GROUPS
    row = pl.BlockSpec((Q, LANES), lambda c, g: (order(c), 0))
    hrows = HEADS_PER_GROUP * GPS
    return [
        pl.BlockSpec((Q, GPS * GC), lambda c, g: (order(c), g)),
        pl.BlockSpec((Q, GPS * N), lambda c, g: (order(c), SSD_W // (GPS * N) + g)),
        pl.BlockSpec((Q, GPS * N), lambda c, g: (order(c), (SSD_W + G * N) // (GPS * N) + g)),
        row, row, row, row,
        pl.BlockSpec((1, hrows, Q), lambda c, g: (order(c), direction * (SSD_HEADS // hrows) + g, 0)),
        pl.BlockSpec((1, 8, LANES), lambda c, g: (order(c), 0, 0)),
    ]


def _ssd_fwd(xbc, prep, dskx, direction, n_ctx):
    T = xbc.shape[0]
    nc, nctx = T // SSD_Q, n_ctx // SSD_Q
    order = _chunk_order(direction, nc, nctx)
    tri, ex, ext, hm = _ssd_constants(direction)
    Q, GC, N, G = SSD_Q, GROUP_COLS, SSD_STATE, SSD_GROUPS

    def body(x_ref, b_ref, c_ref, dt_ref, s_ref, e_ref, d_ref, st_ref, etot_ref, dsk_ref, tri_ref, ex_ref, hm_ref,
             y_ref, sst_ref, S):
        c = pl.program_id(0)
        for k in range(GPS):
            q = _ssd_common(k, x_ref, b_ref, c_ref, dt_ref, s_ref, e_ref, d_ref, st_ref, etot_ref, ex_ref, direction)
            g = q["g"]

            @pl.when(c == 0)
            def _():
                S[g] = jnp.zeros((GC, N), F32)

            S0 = S[g]
            sst_ref[0, k] = S0
            X, Bm, Cm = q["X"], q["Bm"], q["Cm"]
            y = q["E_x"] * _dot(Cm, S0, NT)
            Gm = _dot(Cm, Bm, NT)
            for r in range(HEADS_PER_GROUP):
                _, Lm = _head_decay(q, r, tri_ref)
                y = y + _dot(Gm * Lm, X * hm_ref[r:r + 1, :], NN)
            if direction == 0:
                y = y + _cols(dsk_ref, k, GC)[0:1, :] * q["x"]
            y_ref[:, k * GC:(k + 1) * GC] = y.astype(BF16)
            S[g] = q["Etot_c"] * S0 + _dot(X * q["D_x"], Bm, TN)

    cst2 = lambda shape: pl.BlockSpec(shape, lambda c, g: (0, 0))
    in_specs = _ssd_in_specs(order, direction) + [
        pl.BlockSpec((8, GPS * GC), lambda c, g: (0, g)), cst2((Q, Q)),
        pl.BlockSpec((GPS, LANES, GC), lambda c, g: (g, 0, 0)), cst2((8, GC))]
    out_specs = (pl.BlockSpec((Q, GPS * GC), lambda c, g: (order(c), g)),
                 pl.BlockSpec((1, GPS, GC, N), lambda c, g: (order(c), g, 0, 0)))
    return _call(body, name=f"ssd_fwd_{direction}",
                 out_shape=(_sds((T, SSD_W), BF16), _sds((nc, G, GC, N), F32)),
                 grid=(nc, G // GPS), in_specs=in_specs, out_specs=out_specs,
                 scratch=[pltpu.VMEM((G, GC, N), F32)], sem=("arbitrary", "arbitrary"),
                 )(xbc, xbc, xbc, *prep, dskx, tri, ex, hm)


def _ssd_bwd(xbc, prep, dskx, sst, dy, direction, n_ctx):
    T = xbc.shape[0]
    nc, nctx = T // SSD_Q, n_ctx // SSD_Q
    fwd_order = _chunk_order(direction, nc, nctx)
    order = lambda c: fwd_order(nc - 1 - c)
    tri, ex, ext, hm = _ssd_constants(direction)
    ntri = (1.0 - tri).astype(BF16)
    Q, GC, N, G = SSD_Q, GROUP_COLS, SSD_STATE, SSD_GROUPS

    def body(x_ref, b_ref, c_ref, dt_ref, s_ref, e_ref, d_ref, st_ref, etot_ref, dsk_ref, sst_ref, dy_ref, tri_ref,
             ex_ref, ext_ref, hm_ref, ntri_ref, dx_ref, db_ref, dc_ref, ds_ref, ddt_ref, dtot_ref, daz_ref, dskg_ref,
             dS):
        c, gi = pl.program_id(0), pl.program_id(1)

        @pl.when(jnp.logical_and(c == 0, gi == 0))
        def _():
            dskg_ref[...] = jnp.zeros_like(dskg_ref)

        @pl.when(gi == 0)
        def _():
            ds_ref[...] = jnp.zeros_like(ds_ref)
            ddt_ref[...] = jnp.zeros_like(ddt_ref)
            dtot_ref[...] = jnp.zeros_like(dtot_ref)

        row8 = lax.broadcasted_iota(jnp.int32, (HEADS_PER_GROUP * GPS, 1), 0)
        daz = jnp.zeros((HEADS_PER_GROUP * GPS, Q), F32)
        for k in range(GPS):
            q = _ssd_common(k, x_ref, b_ref, c_ref, dt_ref, s_ref, e_ref, d_ref, st_ref, etot_ref, ex_ref, direction)
            g = q["g"]

            @pl.when(c == 0)
            def _():
                dS[g] = jnp.zeros((GC, N), F32)

            dS1 = dS[g]
            S0 = sst_ref[0, k]
            x, X, Bm, Cm, ext = q["x"], q["X"], q["Bm"], q["Cm"], ext_ref[k]
            E_x, D_x, Etot_c = q["E_x"], q["D_x"], q["Etot_c"]
            dY = _cols(dy_ref, k, GC).astype(F32)

            CS = _dot(Cm, S0, NT)
            dCS = dY * E_x
            dC = _dot(dCS, S0, NN)
            dS0 = Etot_c * dS1 + _dot(dCS, Cm, TN)
            ds_x = dY * (E_x * CS)
            dtot_c = jnp.sum(dS1 * S0, axis=1, keepdims=True) * Etot_c
            dtot = jnp.sum(dtot_c * ext.astype(F32), axis=0, keepdims=True)
            XD = X * D_x
            dXD = _dot(Bm, dS1, NT)
            dB = _dot(XD, dS1, NN)
            dX = dXD * D_x
            t = dXD * XD
            ds_x = ds_x - t
            dtot_x = jnp.sum(t, axis=0, keepdims=True)
            Gm = _dot(Cm, Bm, NT)
            dG = jnp.zeros((Q, Q), F32)
            for r in range(HEADS_PER_GROUP):
                hmr = hm_ref[r:r + 1, :]
                _, Lm = _head_decay(q, r, tri_ref)
                W = Gm * Lm
                dW = _dot(dY * hmr, X, NT)
                dX = dX + _dot(W, dY, TN) * hmr
                dG = dG + dW * Lm
                P = _dot(dW * W, ntri_ref[...], NN)
                da_row = jnp.sum(jnp.where(tri_ref[...] > 0.5, P, 0.0), axis=0, keepdims=True)
                daz = jnp.where(row8 == HEADS_PER_GROUP * k + r, da_row, daz)
            dC = dC + _dot(dG, Bm, NN)
            dB = dB + _dot(dG, Cm, TN)
            ds_ref[...] += _dot2(ds_x, ext)
            dtot_ref[0, 0:1, :] += dtot + _dot2(_bcast8(dtot_x), ext)[0:1, :]
            ddt_ref[...] += _dot(dX * x, ext, NN)
            dx = dX * q["dt_x"]
            if direction == 0:
                dx = dx + dY * _cols(dsk_ref, k, GC)[0:1, :]
                dskg_ref[0:1, :] += jnp.sum(_dot(dY * x, ext, NN), axis=0, keepdims=True)
            dx_ref[:, k * GC:(k + 1) * GC] = dx.astype(BF16)
            db_ref[:, k * N:(k + 1) * N] = dB.astype(BF16)
            dc_ref[:, k * N:(k + 1) * N] = dC.astype(BF16)
            dS[g] = dS0
        daz_ref[0] = daz

    cst2 = lambda shape: pl.BlockSpec(shape, lambda c, g: (0, 0))
    in_specs = _ssd_in_specs(order, direction) + [
        pl.BlockSpec((8, GPS * GC), lambda c, g: (0, g)),
        pl.BlockSpec((1, GPS, GC, N), lambda c, g: (order(c), g, 0, 0)),
        pl.BlockSpec((Q, GPS * GC), lambda c, g: (order(c), g)),
        cst2((Q, Q)),
        pl.BlockSpec((GPS, LANES, GC), lambda c, g: (g, 0, 0)),
        pl.BlockSpec((GPS, GC, LANES), lambda c, g: (g, 0, 0)),
        cst2((8, GC)), cst2((Q, Q)),
    ]
    row = pl.BlockSpec((Q, LANES), lambda c, g: (order(c), 0))
    hrows = HEADS_PER_GROUP * GPS
    out_specs = (pl.BlockSpec((Q, GPS * GC), lambda c, g: (order(c), g)),
                 pl.BlockSpec((Q, GPS * N), lambda c, g: (order(c), g)),
                 pl.BlockSpec((Q, GPS * N), lambda c, g: (order(c), g)),
                 row, row, pl.BlockSpec((1, 8, LANES), lambda c, g: (order(c), 0, 0)),
                 pl.BlockSpec((1, hrows, Q), lambda c, g: (order(c), g, 0)),
                 cst2((8, LANES)))
    return _call(body, name=f"ssd_bwd_{direction}",
                 out_shape=(_sds((T, SSD_W), BF16), _sds((T, G * N), BF16), _sds((T, G * N), BF16),
                            _sds((T, LANES), F32), _sds((T, LANES), F32), _sds((nc, 8, LANES), F32),
                            _sds((nc, SSD_HEADS, Q), F32), _sds((8, LANES), F32)),
                 grid=(nc, G // GPS), in_specs=in_specs, out_specs=out_specs,
                 scratch=[pltpu.VMEM((G, GC, N), F32)], sem=("arbitrary", "arbitrary"),
                 )(xbc, xbc, xbc, *prep, dskx, sst, dy, tri, ex, ext, hm, ntri)


ADA_ROWS = 16


def _ada_fwd(c16, w_ada, b_loc):
    depth, Dm, W = w_ada.shape

    def body(c_ref, w_ref, b_ref, o_ref):
        o_ref[0] = _dot_hi(_silu(c_ref[...]), w_ref[0]) + b_ref[0]

    return _call(body, name="ada_fwd", out_shape=_sds((depth, ADA_ROWS, W), F32), grid=(depth,),
                 in_specs=[pl.BlockSpec((ADA_ROWS, Dm), lambda l: (0, 0)),
                           pl.BlockSpec((1, Dm, W), lambda l: (l, 0, 0)),
                           pl.BlockSpec((1, 1, W), lambda l: (l, 0, 0))],
                 out_specs=pl.BlockSpec((1, ADA_ROWS, W), lambda l: (l, 0, 0)),
                 sem=("parallel",), vmem_mb=40)(c16, w_ada, b_loc)


def _ada_bwd(c16, w_ada, dmod):
    depth, Dm, W = w_ada.shape

    def body(c_ref, w_ref, d_ref, gw_ref, dc_ref):
        l = pl.program_id(0)

        @pl.when(l == 0)
        def _():
            dc_ref[...] = jnp.zeros_like(dc_ref)

        cc = c_ref[...]
        sg = _sigmoid(cc)
        gw_ref[0] = _dot_hi(cc * sg, d_ref[0], TN)
        dsc = _dot_hi(d_ref[0], w_ref[0], NT)
        dc_ref[...] += dsc[8:16, :] * (sg + cc * sg * (1.0 - sg))[8:16, :]

    return _call(body, name="ada_bwd", out_shape=(_sds((depth, Dm, W), F32), _sds((8, Dm), F32)), grid=(depth,),
                 in_specs=[pl.BlockSpec((ADA_ROWS, Dm), lambda l: (0, 0)),
                           pl.BlockSpec((1, Dm, W), lambda l: (l, 0, 0)),
                           pl.BlockSpec((1, ADA_ROWS, W), lambda l: (l, 0, 0))],
                 out_specs=(pl.BlockSpec((1, Dm, W), lambda l: (l, 0, 0)), pl.BlockSpec((8, Dm), lambda l: (0, 0))),
                 sem=("arbitrary",), vmem_mb=48)(c16, w_ada, dmod)


def _sum_slabs(g, name, tr=512):
    _, R, C = g.shape
    tr = min(tr, R)

    def body(g_ref, o_ref):
        acc = g_ref[0]
        for k in range(1, N_DEV):
            acc = acc + g_ref[k]
        o_ref[...] = acc

    return _call(body, name=name, out_shape=_sds((R, C), F32), grid=(R // tr,),
                 in_specs=[pl.BlockSpec((N_DEV, tr, C), lambda i: (0, i, 0))],
                 out_specs=pl.BlockSpec((tr, C), lambda i: (i, 0)), sem=("parallel",), vmem_mb=40)(g)


def _adamw_math(w, g, m, v):
    m = ADAM_B1 * m + (1.0 - ADAM_B1) * g
    v = ADAM_B2 * v + (1.0 - ADAM_B2) * (g * g)
    m_hat = m / (1.0 - ADAM_B1 ** ADAM_STEP)
    v_hat = v / (1.0 - ADAM_B2 ** ADAM_STEP)
    delta = -ADAM_LR * (m_hat / (jnp.sqrt(v_hat) + ADAM_EPS) + ADAM_WD * w)
    return delta, m, v


def _adamw(w, g, m, v, name, tr):
    R, C = w.shape
    tr = min(tr, R)
    assert R % tr == 0

    def body(w_ref, g_ref, m_ref, v_ref, d_ref, nm_ref, nv_ref):
        d, nm, nv = _adamw_math(w_ref[...], g_ref[...], m_ref[...], v_ref[...])
        d_ref[...] = d
        nm_ref[...] = nm
        nv_ref[...] = nv

    blk = pl.BlockSpec((tr, C), lambda i: (i, 0))
    out = _sds((R, C), F32)
    return _call(body, name=name, out_shape=(out, out, out), grid=(R // tr,), in_specs=[blk] * 4,
                 out_specs=(blk, blk, blk), sem=("parallel",), vmem_mb=40)(w, g, m, v)


def _adamw_slabs(w, slabs, m, v, name, tr):
    depth, R, C = w.shape
    assert R % tr == 0 and len(slabs) == depth

    def body(w_ref, *rest):
        slab_refs = rest[:depth]
        m_ref, v_ref, g_ref, d_ref, nm_ref, nv_ref = rest[depth:]
        layer = pl.program_id(0)
        for ll in range(depth):
            @pl.when(layer == ll)
            def _(s_ref=slab_refs[ll]):
                g = s_ref[0].astype(F32)
                for k in range(1, N_DEV):
                    g = g + s_ref[k].astype(F32)
                d, nm, nv = _adamw_math(w_ref[0], g, m_ref[0], v_ref[0])
                g_ref[0] = g
                d_ref[0] = d
                nm_ref[0] = nm
                nv_ref[0] = nv

    blk = pl.BlockSpec((1, tr, C), lambda l, i: (l, i, 0))

    def slab_spec(ll):
        return pl.BlockSpec((N_DEV, tr, C), lambda l, i: (0, jnp.where(l == ll, i, 0), 0))

    out = _sds((depth, R, C), F32)
    return _call(body, name=name, out_shape=(out, out, out, out), grid=(depth, R // tr),
                 in_specs=[blk] + [slab_spec(ll) for ll in range(depth)] + [blk, blk],
                 out_specs=(blk, blk, blk, blk), sem=("parallel", "parallel"), vmem_mb=56)(w, *slabs, m, v)


PACK_QUANTUM = 512 * LANES


def _pack(arrays):
    flat = jnp.concatenate([a.reshape(-1).astype(F32) for a in arrays])
    pad = (-flat.shape[0]) % PACK_QUANTUM
    return jnp.pad(flat, (0, pad)).reshape(-1, LANES)


def _unpack(bundle, shapes):
    flat = bundle.reshape(-1)
    out, off = [], 0
    for shp in shapes:
        n = int(np.prod(shp))
        out.append(flat[off:off + n].reshape(shp))
        off += n
    return out


def _row(v):
    return v.reshape(1, -1)


def _pad_rows(a, rows):
    return jnp.pad(a, ((0, rows - a.shape[0]), (0, 0)))


def kernel(x, c, ctx, c_ctx, w_ada, b_ada, g_pre, g_post, w_in, conv_w, conv_b, dt_bias, a_log, d_skip, g_ssd, g_v, w_s, b_s, g_mlp, w_out, loss_target, m_c_ctx, m_w_ada, m_b_ada, m_g_pre, m_g_post, m_w_in, m_conv_w, m_conv_b, m_dt_bias, m_a_log, m_d_skip, m_g_ssd, m_g_v, m_w_s, m_b_s, m_g_mlp, m_w_out, v_c_ctx, v_w_ada, v_b_ada, v_g_pre, v_g_post, v_w_in, v_conv_w, v_conv_b, v_dt_bias, v_a_log, v_d_skip, v_g_ssd, v_g_v, v_w_s, v_b_s, v_g_mlp, v_w_out):
    depth = w_in.shape[0]
    L = x.shape[1]
    n_ctx = ctx.shape[1]
    T = n_ctx + L
    Dm = D_MODEL
    me = _lin(_my_pos())
    ada_w = w_ada.shape[2]
    in_w = w_in.shape[2]
    out_r = w_out.shape[1]
    conv_c = conv_w.shape[2]
    TM = 768

    c_all = _all_gather_small(_pad_rows(c, 8), "gather_c")[:, 0, :]
    c16 = _pad_rows(jnp.concatenate([c_all, _row(c_ctx)], axis=0), ADA_ROWS)
    b_loc = lax.dynamic_slice_in_dim(b_ada, me * ada_w, ada_w, axis=1)[:, None, :]
    mod_loc = _ada_fwd(c16, w_ada, b_loc)
    mod_all = _all_gather_small(mod_loc.reshape(depth * ADA_ROWS, ada_w), "gather_mod")
    mod_all = mod_all.reshape(N_DEV, depth, ADA_ROWS, ada_w)
    mod_me = lax.dynamic_index_in_dim(mod_all, me, axis=2, keepdims=False)
    mod_me = jnp.transpose(mod_me, (1, 0, 2)).reshape(depth, N_DEV * ada_w)
    mod_cx = jnp.transpose(mod_all[:, :, 8, :], (1, 0, 2)).reshape(depth, N_DEV * ada_w)
    shift2 = jnp.stack([mod_cx[:, 0:Dm], mod_me[:, 0:Dm]], axis=1)
    scale2 = jnp.stack([mod_cx[:, Dm:2 * Dm], mod_me[:, Dm:2 * Dm]], axis=1)
    gate2 = jnp.stack([mod_cx[:, 2 * Dm:], mod_me[:, 2 * Dm:]], axis=1)

    w_in_bf = w_in.astype(BF16)
    w_out_bf = w_out.astype(BF16)
    w_in_g = _all_gather_big(w_in_bf[0], "gather_w_in")
    w_out_g = _all_gather_big(w_out_bf[0], "gather_w_out")
    conv_all = _all_gather_small(_pad_rows(conv_w.reshape(depth * CONV_W, conv_c), 24).reshape(24, conv_c),
                                 "gather_conv_w")
    conv_full = jnp.transpose(conv_all[:, :depth * CONV_W, :], (1, 0, 2)).reshape(depth, CONV_W, XBC_W)

    def in_weights(gathered):
        wf = jnp.transpose(gathered, (1, 0, 2)).reshape(Dm, IN_W)
        return (jnp.concatenate([wf[:, :XBC_W], wf[:, XBC_W + DT_W:]], axis=1),
                jnp.pad(wf[:, XBC_W:XBC_W + DT_W], ((0, 0), (0, LANES - DT_W))))

    w_main, w_dt, w_o = [None] * depth, [None] * depth, [None] * depth

    def ssd_prm(l):
        rows = jnp.stack([jnp.pad(dt_bias[l].reshape(-1), (0, LANES - DT_W)),
                          jnp.pad(a_log[l].reshape(-1), (0, LANES - DT_W)),
                          jnp.pad(d_skip[l], (0, LANES - SSD_HEADS))])
        return _pad_rows(rows, 8)

    ws_bf = w_s.astype(BF16)
    bst = jnp.pad(jnp.transpose(b_s, (0, 2, 1)), ((0, 0), (0, 0), (0, LANES - MLP_GROUPS)))

    X = jnp.concatenate([ctx[0], x[0]], axis=0)
    saved = []
    for l in range(depth):
        w_main[l], w_dt[l] = in_weights(w_in_g)
        w_o[l] = w_out_g.reshape(N_DEV * out_r, Dm)
        hx = _prenorm_fwd(X, _row(g_pre[l]), scale2[l], shift2[l], n_ctx)
        if l + 1 < depth:
            z, w_in_g = _mm(hx, w_main[l], "nn", BF16, "in_proj_gather", TM, 512, 2048,
                            exchange=("gather", w_in_bf[l + 1]))
        else:
            z = _mm(hx, w_main[l], "nn", BF16, "in_proj", TM, 512, 2048)
        raw = _mm(hx, w_dt[l], "nn", F32, "dt_proj", TM, LANES, 2048)
        cw = _pad_rows(conv_full[l], 8)
        cb = _row(conv_b[l])
        xbc = _conv_fwd(z, cw, cb, n_ctx)
        prm = ssd_prm(l)
        prep = _ssd_prep(raw, prm)
        dskx = _pad_rows(_row(jnp.repeat(d_skip[l], SSD_HEAD_DIM)), 8)
        y_f, sst_f = _ssd_fwd(xbc, prep, dskx, 0, n_ctx)
        y_b, sst_b = _ssd_fwd(xbc, prep, dskx, 1, n_ctx)
        ycat = _gate_fwd(y_f, y_b, z, _row(g_ssd[l]))
        ycat = _mlp_fwd(z, _row(g_v[l]), ws_bf[l], bst[l], _row(g_mlp[l]), ycat)
        if l + 1 < depth:
            o, w_out_g = _mm(ycat, w_o[l], "nn", F32, "out_proj_gather", TM, 512, 2048,
                             exchange=("gather", w_out_bf[l + 1]))
        else:
            o = _mm(ycat, w_o[l], "nn", F32, "out_proj", TM, 512, 2048)
        saved.append((X, hx, z, raw, cw, cb, xbc, prm, prep, dskx, y_f, sst_f, y_b, sst_b, ycat, o))
        X = _post_fwd(X, o, _row(g_post[l]), gate2[l], n_ctx)

    dX, sq = _loss_kernel(X, loss_target[0], n_ctx)
    loss = lax.psum(0.5 * jnp.sum(sq) / Dm, ("x", "y", "c"))

    g_small = {k: [None] * depth for k in
               ("b_ada", "g_pre", "g_post", "conv_b", "dt_bias", "a_log", "d_skip", "g_ssd", "g_v", "w_s", "b_s",
                "g_mlp", "conv_w", "dmod_c", "dmod")}
    in_recv, out_recv = [None] * depth, [None] * depth
    for l in reversed(range(depth)):
        Xl, hx, z, raw, cw, cb, xbc, prm, prep, dskx, y_f, sst_f, y_b, sst_b, ycat, o = saved[l]
        d_o, acc_post = _post_bwd(o, _row(g_post[l]), gate2[l], dX, n_ctx)
        dycat = _mm(d_o, w_o[l], "nt", BF16, "out_proj_dx", TM, 512, 2048)
        out_slabs = _mm(ycat, d_o, "tn", BF16, "out_proj_dw", 1024, 1024, 768).reshape(N_DEV, out_r, Dm)
        dz = lax.empty((T, Z_MAIN), BF16)
        dy, dz, acc_gate = _gate_bwd(y_f, y_b, z, _row(g_ssd[l]), dycat, dz)
        dz, acc_mlp, dws, dbst = _mlp_bwd(z, _row(g_v[l]), ws_bf[l], bst[l], _row(g_mlp[l]), dycat, dz)
        dx0, db0, dc0, ds0, ddt0, dtot0, daz0, dsk0 = _ssd_bwd(xbc, prep, dskx, sst_f, dy, 0, n_ctx)
        dx1, db1, dc1, ds1, ddt1, dtot1, daz1, _ = _ssd_bwd(xbc, prep, dskx, sst_b, dy, 1, n_ctx)
        draw, dprm = _ssd_post(raw, prm, prep[0], ds0, ds1, ddt0, ddt1, dtot0, dtot1, daz0, daz1)
        gn = SSD_GROUPS * SSD_STATE
        dz, dcw_x, dcb_x = _conv_bwd(z, cw, cb, dx0, dx1, dz, 0, n_ctx)
        dz, dcw_b, dcb_b = _conv_bwd(z, cw, cb, db0, db1, dz, SSD_W, n_ctx)
        dz, dcw_c, dcb_c = _conv_bwd(z, cw, cb, dc0, dc1, dz, SSD_W + gn, n_ctx)
        dcw = jnp.concatenate([dcw_x, dcw_b, dcw_c], axis=1)
        dcb = jnp.concatenate([dcb_x, dcb_b, dcb_c], axis=1)
        gw_main, out_recv[l] = _mm(hx, dz, "tn", BF16, "in_proj_dw_scatter", 1024, 1024, 768,
                                   exchange=("scatter", out_slabs))
        gw_dt = _mm(hx, draw, "tn", BF16, "dt_proj_dw", 1024, LANES, 768)
        gw_full = jnp.concatenate([gw_main[:, :XBC_W], gw_dt[:, :DT_W], gw_main[:, XBC_W:]], axis=1)
        in_slabs = jnp.transpose(gw_full.reshape(Dm, N_DEV, in_w), (1, 0, 2))
        dhx1, in_recv[l] = _mm(dz, w_main[l], "nt", F32, "in_proj_dx_scatter", TM, 512, 2048,
                               exchange=("scatter", in_slabs))
        dhx2 = _mm(draw, w_dt[l], "nt", F32, "dt_proj_dx", TM, 512, LANES)
        dX, acc_pre = _prenorm_bwd(Xl, _row(g_pre[l]), scale2[l], shift2[l], dhx1, dhx2, dX, n_ctx)

        dmod_c = jnp.concatenate([acc_pre[3], acc_pre[1], acc_post[1]])
        dmod_x = jnp.concatenate([acc_pre[4], acc_pre[2], acc_post[2]])
        g_small["dmod_c"][l] = dmod_c
        g_small["dmod"][l] = dmod_x
        g_small["b_ada"][l] = dmod_c + dmod_x
        g_small["g_pre"][l] = acc_pre[0]
        g_small["g_post"][l] = acc_post[0]
        g_small["conv_b"][l] = dcb[0]
        g_small["conv_w"][l] = dcw[:CONV_W]
        g_small["dt_bias"][l] = dprm[0, :DT_W].reshape(2, SSD_HEADS)
        g_small["a_log"][l] = dprm[1, :DT_W].reshape(2, SSD_HEADS)
        g_small["d_skip"][l] = dsk0[0, :SSD_HEADS]
        g_small["g_ssd"][l] = acc_gate[0]
        g_small["g_v"][l] = acc_mlp[0]
        g_small["g_mlp"][l] = acc_mlp[1]
        g_small["w_s"][l] = dws
        g_small["b_s"][l] = jnp.transpose(dbst[:, :MLP_GROUPS])

    grad_x = dX[n_ctx:][None]

    summed_names = ["b_ada", "g_pre", "g_post", "conv_b", "dt_bias", "a_log", "d_skip", "g_ssd", "g_v", "w_s",
                    "b_s", "g_mlp", "conv_w", "dmod_c"]
    parts = [jnp.stack(g_small[k]) for k in summed_names] + [jnp.stack(g_small["dmod"])]
    shapes = [p.shape for p in parts]
    bundle = _pack(parts)
    gathered = _all_gather_big(bundle, "gather_small_grads")
    reduced = _unpack(_sum_slabs(gathered, "sum_small_grads"), shapes)
    gs = dict(zip(summed_names, reduced[:-1]))
    n_el = int(np.prod(shapes[-1]))
    off = sum(int(np.prod(s)) for s in shapes[:-1])
    dmod_all = gathered.reshape(N_DEV, -1)[:, off:off + n_el].reshape(N_DEV, depth, 3 * Dm)

    dmod_rows = jnp.concatenate([jnp.transpose(dmod_all, (1, 0, 2)), gs["dmod_c"][:, None, :]], axis=1)
    dmod_rows = lax.dynamic_slice_in_dim(dmod_rows, me * ada_w, ada_w, axis=2)
    dmod_rows = jnp.pad(dmod_rows, ((0, 0), (0, ADA_ROWS - 9), (0, 0)))
    g_w_ada, dc_part = _ada_bwd(c16, w_ada, dmod_rows)
    dc_all = _all_gather_small(dc_part, "gather_dc")
    g_c_ctx = _sum_slabs(dc_all, "sum_dc")[0]

    grads_small = {"c_ctx": g_c_ctx, "b_ada": gs["b_ada"], "g_pre": gs["g_pre"], "g_post": gs["g_post"],
                   "conv_b": gs["conv_b"], "dt_bias": gs["dt_bias"], "a_log": gs["a_log"], "d_skip": gs["d_skip"],
                   "g_ssd": gs["g_ssd"], "g_v": gs["g_v"], "w_s": gs["w_s"], "b_s": gs["b_s"], "g_mlp": gs["g_mlp"],
                   "conv_w": lax.dynamic_slice_in_dim(gs["conv_w"], me * conv_c, conv_c, axis=2)}

    g_in, d_in, nm_in, nv_in = _adamw_slabs(w_in, in_recv, m_w_in, v_w_in, "adamw_w_in", 128)
    g_out, d_out, nm_out, nv_out = _adamw_slabs(w_out, out_recv, m_w_out, v_w_out, "adamw_w_out", 128)
    d_ada, nm_ada, nv_ada = _adamw(w_ada.reshape(depth * Dm, ada_w), g_w_ada.reshape(depth * Dm, ada_w),
                                   m_w_ada.reshape(depth * Dm, ada_w), v_w_ada.reshape(depth * Dm, ada_w),
                                   "adamw_w_ada", 256)

    small_names = ["c_ctx", "b_ada", "g_pre", "g_post", "conv_w", "conv_b", "dt_bias", "a_log", "d_skip", "g_ssd",
                   "g_v", "w_s", "b_s", "g_mlp"]
    small_w = dict(c_ctx=c_ctx, b_ada=b_ada, g_pre=g_pre, g_post=g_post, conv_w=conv_w, conv_b=conv_b,
                   dt_bias=dt_bias, a_log=a_log, d_skip=d_skip, g_ssd=g_ssd, g_v=g_v, w_s=w_s, b_s=b_s, g_mlp=g_mlp)
    small_m = dict(c_ctx=m_c_ctx, b_ada=m_b_ada, g_pre=m_g_pre, g_post=m_g_post, conv_w=m_conv_w, conv_b=m_conv_b,
                   dt_bias=m_dt_bias, a_log=m_a_log, d_skip=m_d_skip, g_ssd=m_g_ssd, g_v=m_g_v, w_s=m_w_s,
                   b_s=m_b_s, g_mlp=m_g_mlp)
    small_v = dict(c_ctx=v_c_ctx, b_ada=v_b_ada, g_pre=v_g_pre, g_post=v_g_post, conv_w=v_conv_w, conv_b=v_conv_b,
                   dt_bias=v_dt_bias, a_log=v_a_log, d_skip=v_d_skip, g_ssd=v_g_ssd, g_v=v_g_v, w_s=v_w_s,
                   b_s=v_b_s, g_mlp=v_g_mlp)
    s_shapes = [small_w[k].shape for k in small_names]
    d_s, nm_s, nv_s = _adamw(_pack([small_w[k] for k in small_names]),
                             _pack([grads_small[k].reshape(small_w[k].shape) for k in small_names]),
                             _pack([small_m[k] for k in small_names]), _pack([small_v[k] for k in small_names]),
                             "adamw_small", 512)
    d_s = dict(zip(small_names, _unpack(d_s, s_shapes)))
    nm_s = dict(zip(small_names, _unpack(nm_s, s_shapes)))
    nv_s = dict(zip(small_names, _unpack(nv_s, s_shapes)))

    big = {"w_ada": (g_w_ada, d_ada.reshape(w_ada.shape), nm_ada.reshape(w_ada.shape), nv_ada.reshape(w_ada.shape)),
           "w_in": tuple(t.reshape(w_in.shape) for t in (g_in, d_in, nm_in, nv_in)),
           "w_out": tuple(t.reshape(w_out.shape) for t in (g_out, d_out, nm_out, nv_out))}
    order = ["c_ctx", "w_ada", "b_ada", "g_pre", "g_post", "w_in", "conv_w", "conv_b", "dt_bias", "a_log", "d_skip",
             "g_ssd", "g_v", "w_s", "b_s", "g_mlp", "w_out"]

    def pick(k, idx):
        if k in big:
            return big[k][idx]
        return (grads_small[k].reshape(small_w[k].shape), d_s[k], nm_s[k], nv_s[k])[idx]

    return (loss, grad_x, *[pick(k, 0) for k in order], *[pick(k, 1) for k in order],
            *[pick(k, 2) for k in order], *[pick(k, 3) for k in order])
```

```python
import numpy as np
import jax
import jax.numpy as jnp
from jax import lax
from jax.experimental import pallas as pl
from jax.experimental.pallas import tpu as pltpu

F32 = jnp.float32
BF16 = jnp.bfloat16

D_MODEL = 2048
GRID_W = 64
SSD_W = 2048
SSD_HEADS = 32
SSD_HEAD_DIM = 64
SSD_GROUPS = 8
HEADS_PER_GROUP = 4
SSD_STATE = 128
CHUNK = 128
SSD_Q = 256
CONV_W = 5
MLP_W = 2048
MLP_GROUPS = 16
XBC_W = 4096
DT_W = 64
IN_W = 12352
Z_MAIN = IN_W - DT_W
GROUP_COLS = HEADS_PER_GROUP * SSD_HEAD_DIM
EPS = 1e-6
N_DEV = 8

ADAM_LR = 0.001
ADAM_B1 = 0.9
ADAM_B2 = 0.999
ADAM_EPS = 1e-08
ADAM_WD = 0.01
ADAM_STEP = 10

LANES = 128
NEG_BIG = -1e30

NN = (((1,), (0,)), ((), ()))
NT = (((1,), (1,)), ((), ()))
TN = (((0,), (0,)), ((), ()))
HI = lax.Precision.HIGHEST


def _dot(a, b, dims):
    return lax.dot_general(a.astype(BF16), b.astype(BF16), dims, preferred_element_type=F32)


def _dot_hi(a, b, dims=NN):
    return lax.dot_general(a, b, dims, preferred_element_type=F32, precision=HI)


def _sigmoid(x):
    return 1.0 / (1.0 + jnp.exp(-x))


def _silu(x):
    return x * _sigmoid(x)


def _softplus(x):
    return jnp.maximum(x, 0.0) + jnp.log(1.0 + jnp.exp(-jnp.abs(x)))


def _call(body, *, name, out_shape, grid=None, in_specs=None, out_specs=None, scratch=(),
          sem=None, vmem_mb=None, aliases=None):
    params = {}
    if sem is not None:
        params["dimension_semantics"] = sem
    if vmem_mb is not None:
        params["vmem_limit_bytes"] = vmem_mb << 20
    kw = {}
    if grid is not None:
        kw["grid"] = grid
    if in_specs is not None:
        kw["in_specs"] = in_specs
    if out_specs is not None:
        kw["out_specs"] = out_specs
    return pl.pallas_call(body, name=name, out_shape=out_shape, scratch_shapes=list(scratch),
                          input_output_aliases=aliases or {},
                          compiler_params=pltpu.CompilerParams(**params), **kw)


def _sds(shape, dtype):
    return jax.ShapeDtypeStruct(tuple(shape), dtype)


ANY = pl.BlockSpec(memory_space=pl.ANY)


def _my_pos():
    return lax.axis_index("x"), lax.axis_index("y"), lax.axis_index("c")


def _flip(v, bit):
    return 1 - v if bit else v


def _peer(pos, k):
    mx, my, mc = pos
    return (_flip(mx, (k >> 2) & 1), _flip(my, (k >> 1) & 1), _flip(mc, k & 1))


def _lin(pos):
    return 4 * pos[0] + 2 * pos[1] + pos[2]


def _all_gather_small(x, name):
    R, C = x.shape

    def body(x_ref, o_ref, send_sems, recv_sems):
        me = _my_pos()
        o_ref[_lin(me)] = x_ref[...]
        sends = []
        for k in range(1, N_DEV):
            peer = _peer(me, k)
            cp = pltpu.make_async_remote_copy(
                src_ref=x_ref, dst_ref=o_ref.at[_lin(me)], send_sem=send_sems.at[k - 1],
                recv_sem=recv_sems.at[k - 1], device_id=peer, device_id_type=pl.DeviceIdType.MESH)
            cp.start()
            sends.append(cp)
        for k in range(1, N_DEV):
            peer = _peer(me, k)
            pltpu.make_async_remote_copy(
                src_ref=x_ref, dst_ref=o_ref.at[_lin(peer)], send_sem=send_sems.at[k - 1],
                recv_sem=recv_sems.at[k - 1], device_id=peer,
                device_id_type=pl.DeviceIdType.MESH).wait_recv()
        for cp in sends:
            cp.wait_send()

    return _call(body, name=name, out_shape=_sds((N_DEV, R, C), x.dtype),
                 in_specs=[pl.BlockSpec(memory_space=pltpu.VMEM)],
                 out_specs=pl.BlockSpec(memory_space=pltpu.VMEM),
                 scratch=[pltpu.SemaphoreType.DMA((N_DEV - 1,)), pltpu.SemaphoreType.DMA((N_DEV - 1,))],
                 vmem_mb=40)(x)


def _all_gather_big(x, name):
    def body(x_ref, o_ref, send_sems, recv_sems, local_sem):
        mx, my, mc = _my_pos()
        me, sibling = (mx, my, mc), (mx, my, 1 - mc)
        chips = [(1 - mx, my), (mx, 1 - my), (1 - mx, 1 - my)]

        def slot(pos):
            return o_ref.at[_lin(pos)]

        def copy(k, block, to, src=None):
            return pltpu.make_async_remote_copy(
                src_ref=slot(block) if src is None else src, dst_ref=slot(block),
                send_sem=send_sems.at[k], recv_sem=recv_sems.at[k], device_id=to,
                device_id_type=pl.DeviceIdType.MESH)

        mine = pltpu.make_async_copy(x_ref, slot(me), local_sem)
        mine.start()
        first = [copy(0, me, sibling, src=x_ref)]
        first += [copy(1 + j, me, (*chip, mc), src=x_ref) for j, chip in enumerate(chips)]
        for cp in first:
            cp.start()
        passed = [copy(4 + j, (*chip, mc), sibling) for j, chip in enumerate(chips)]
        for j, chip in enumerate(chips):
            copy(1 + j, (*chip, mc), me).wait_recv()
            passed[j].start()
        copy(0, sibling, me).wait_recv()
        for j, chip in enumerate(chips):
            copy(4 + j, (*chip, 1 - mc), me).wait_recv()
        for cp in first + passed:
            cp.wait_send()
        mine.wait()

    return _call(body, name=name, out_shape=_sds((N_DEV,) + x.shape, x.dtype),
                 in_specs=[ANY], out_specs=ANY,
                 scratch=[pltpu.SemaphoreType.DMA((7,)), pltpu.SemaphoreType.DMA((7,)),
                          pltpu.SemaphoreType.DMA])(x)


EXCHANGE_SEMS = [pltpu.SemaphoreType.DMA((N_DEV - 1,)), pltpu.SemaphoreType.DMA((N_DEV - 1,)),
                 pltpu.SemaphoreType.DMA]


def _exchange_copies(kind, src_ref, dst_ref, send_sems, recv_sems, local_sem, with_arrivals=True):
    me = _my_pos()
    mine_src = src_ref if kind == "gather" else src_ref.at[_lin(me)]
    local = pltpu.make_async_copy(mine_src, dst_ref.at[_lin(me)], local_sem)
    sends, recvs = [], []
    for k in range(1, N_DEV):
        peer = _peer(me, k)
        out_src = src_ref if kind == "gather" else src_ref.at[_lin(peer)]
        sends.append(pltpu.make_async_remote_copy(
            src_ref=out_src, dst_ref=dst_ref.at[_lin(me)], send_sem=send_sems.at[k - 1],
            recv_sem=recv_sems.at[k - 1], device_id=peer, device_id_type=pl.DeviceIdType.MESH))
        if with_arrivals:
            recvs.append(pltpu.make_async_remote_copy(
                src_ref=out_src, dst_ref=dst_ref.at[_lin(peer)], send_sem=send_sems.at[k - 1],
                recv_sem=recv_sems.at[k - 1], device_id=peer, device_id_type=pl.DeviceIdType.MESH))
    return sends, recvs, local


def _exchange_start(kind, src_ref, dst_ref, send_sems, recv_sems, local_sem):
    sends, _, local = _exchange_copies(kind, src_ref, dst_ref, send_sems, recv_sems, local_sem, with_arrivals=False)
    local.start()
    for cp in sends:
        cp.start()


def _exchange_wait(kind, src_ref, dst_ref, send_sems, recv_sems, local_sem):
    sends, recvs, local = _exchange_copies(kind, src_ref, dst_ref, send_sems, recv_sems, local_sem)
    for cp in recvs:
        cp.wait_recv()
    for cp in sends:
        cp.wait_send()
    local.wait()


def _exchange_shape(kind, src):
    return _sds((N_DEV,) + src.shape if kind == "gather" else src.shape, src.dtype)


def _mm(a, b, mode, out_dtype, name, tm, tn, tk, exchange=None):
    if mode == "nn":
        (M, K), (K2, N) = a.shape, b.shape
    elif mode == "nt":
        (M, K), (N, K2) = a.shape, b.shape
    else:
        (K, M), (K2, N) = a.shape, b.shape
    assert K == K2
    tm, tn, tk = min(tm, M), min(tn, N), min(tk, K)
    assert M % tm == 0 and N % tn == 0 and K % tk == 0, (name, M, N, K, tm, tn, tk)
    ni, nj, nk = M // tm, N // tn, K // tk
    dims = {"nn": NN, "nt": NT, "tn": TN}[mode]

    def product(a_ref, b_ref, o_ref, acc):
        p = _dot(a_ref[...], b_ref[...], dims)
        if nk == 1:
            o_ref[...] = p.astype(out_dtype)
        else:
            acc_ref = acc[0]
            k = pl.program_id(2)

            @pl.when(k == 0)
            def _():
                acc_ref[...] = p

            @pl.when(k > 0)
            def _():
                acc_ref[...] += p

            @pl.when(k == nk - 1)
            def _():
                o_ref[...] = acc_ref[...].astype(out_dtype)

    if mode == "tn":
        a_spec = pl.BlockSpec((tk, tm), lambda i, j, k: (k, i))
    else:
        a_spec = pl.BlockSpec((tm, tk), lambda i, j, k: (i, k))
    if mode == "nt":
        b_spec = pl.BlockSpec((tn, tk), lambda i, j, k: (j, k))
    else:
        b_spec = pl.BlockSpec((tk, tn), lambda i, j, k: (k, j))
    o_spec = pl.BlockSpec((tm, tn), lambda i, j, k: (i, j))
    acc_scratch = [] if nk == 1 else [pltpu.VMEM((tm, tn), F32)]
    if exchange is None:
        def body(a_ref, b_ref, o_ref, *acc):
            product(a_ref, b_ref, o_ref, acc)

        return _call(body, name=name, out_shape=_sds((M, N), out_dtype), grid=(ni, nj, nk),
                     in_specs=[a_spec, b_spec], out_specs=o_spec, scratch=acc_scratch,
                     sem=("parallel", "parallel", "arbitrary"), vmem_mb=48)(a, b)

    kind, src = exchange

    def body(a_ref, b_ref, src_ref, o_ref, dst_ref, send_sems, recv_sems, local_sem, *acc):
        i, j, k = pl.program_id(0), pl.program_id(1), pl.program_id(2)

        @pl.when(jnp.logical_and(jnp.logical_and(i == 0, j == 0), k == 0))
        def _():
            _exchange_start(kind, src_ref, dst_ref, send_sems, recv_sems, local_sem)

        product(a_ref, b_ref, o_ref, acc)

        @pl.when(jnp.logical_and(jnp.logical_and(i == ni - 1, j == nj - 1), k == nk - 1))
        def _():
            _exchange_wait(kind, src_ref, dst_ref, send_sems, recv_sems, local_sem)

    return _call(body, name=name, out_shape=(_sds((M, N), out_dtype), _exchange_shape(kind, src)), grid=(ni, nj, nk),
                 in_specs=[a_spec, b_spec, ANY], out_specs=(o_spec, ANY), scratch=EXCHANGE_SEMS + acc_scratch,
                 sem=("arbitrary", "arbitrary", "arbitrary"), vmem_mb=48)(a, b, src)


def _rms(x):
    return lax.rsqrt(jnp.mean(x * x, axis=-1, keepdims=True) + EPS)


def _prenorm_f(x, g, sc, sh):
    return (x * _rms(x) * g) * (1.0 + sc) + sh


def _pick(is_ctx, ref):
    return jnp.where(is_ctx, ref[0:1, :], ref[1:2, :])


def _prenorm_fwd(X, g, sc2, sh2, n_ctx, tl=256):
    T, Dm = X.shape
    nct = n_ctx // tl

    def body(x_ref, g_ref, sc_ref, sh_ref, o_ref):
        is_ctx = pl.program_id(0) < nct
        o_ref[...] = _prenorm_f(x_ref[...], g_ref[...], _pick(is_ctx, sc_ref),
                                _pick(is_ctx, sh_ref)).astype(BF16)

    row = pl.BlockSpec((tl, Dm), lambda i: (i, 0))
    return _call(body, name="prenorm_fwd", out_shape=_sds((T, Dm), BF16), grid=(T // tl,),
                 in_specs=[row, pl.BlockSpec((1, Dm), lambda i: (0, 0)),
                           pl.BlockSpec((2, Dm), lambda i: (0, 0)), pl.BlockSpec((2, Dm), lambda i: (0, 0))],
                 out_specs=row, sem=("parallel",), vmem_mb=40)(X, g, sc2, sh2)


def _prenorm_bwd(X, g, sc2, sh2, d1, d2, dres, n_ctx, tl=256):
    T, Dm = X.shape
    nct = n_ctx // tl

    def body(x_ref, g_ref, sc_ref, sh_ref, d1_ref, d2_ref, dres_ref, dx_ref, acc_ref):
        i = pl.program_id(0)
        is_ctx = i < nct

        @pl.when(i == 0)
        def _():
            acc_ref[...] = jnp.zeros_like(acc_ref)

        _, vjp = jax.vjp(_prenorm_f, x_ref[...], g_ref[...], _pick(is_ctx, sc_ref), _pick(is_ctx, sh_ref))
        dx, dg, dsc, dsh = vjp(d1_ref[...] + d2_ref[...])
        dx_ref[...] = dres_ref[...] + dx
        zero = jnp.zeros_like(dsc)
        acc_ref[0:1, :] += dg
        acc_ref[1:2, :] += jnp.where(is_ctx, dsc, zero)
        acc_ref[2:3, :] += jnp.where(is_ctx, zero, dsc)
        acc_ref[3:4, :] += jnp.where(is_ctx, dsh, zero)
        acc_ref[4:5, :] += jnp.where(is_ctx, zero, dsh)

    row = pl.BlockSpec((tl, Dm), lambda i: (i, 0))
    return _call(body, name="prenorm_bwd", out_shape=(_sds((T, Dm), F32), _sds((8, Dm), F32)), grid=(T // tl,),
                 in_specs=[row, pl.BlockSpec((1, Dm), lambda i: (0, 0)),
                           pl.BlockSpec((2, Dm), lambda i: (0, 0)), pl.BlockSpec((2, Dm), lambda i: (0, 0)),
                           row, row, row],
                 out_specs=(row, pl.BlockSpec((8, Dm), lambda i: (0, 0))), sem=("arbitrary",),
                 vmem_mb=48)(X, g, sc2, sh2, d1, d2, dres)


def _post_f(o, g, gate):
    return gate * ((o * _rms(o)) * g)


def _post_fwd(X, o, g, gate2, n_ctx, tl=256):
    T, Dm = X.shape
    nct = n_ctx // tl

    def body(x_ref, o_ref, g_ref, gate_ref, y_ref):
        is_ctx = pl.program_id(0) < nct
        y_ref[...] = x_ref[...] + _post_f(o_ref[...], g_ref[...], _pick(is_ctx, gate_ref))

    row = pl.BlockSpec((tl, Dm), lambda i: (i, 0))
    return _call(body, name="post_fwd", out_shape=_sds((T, Dm), F32), grid=(T // tl,),
                 in_specs=[row, row, pl.BlockSpec((1, Dm), lambda i: (0, 0)), pl.BlockSpec((2, Dm), lambda i: (0, 0))],
                 out_specs=row, sem=("parallel",), vmem_mb=40)(X, o, g, gate2)


def _post_bwd(o, g, gate2, dX, n_ctx, tl=256):
    T, Dm = o.shape
    nct = n_ctx // tl

    def body(o_ref, g_ref, gate_ref, dx_ref, do_ref, acc_ref):
        i = pl.program_id(0)
        is_ctx = i < nct

        @pl.when(i == 0)
        def _():
            acc_ref[...] = jnp.zeros_like(acc_ref)

        _, vjp = jax.vjp(_post_f, o_ref[...], g_ref[...], _pick(is_ctx, gate_ref))
        do, dg, dgate = vjp(dx_ref[...])
        do_ref[...] = do.astype(BF16)
        zero = jnp.zeros_like(dgate)
        acc_ref[0:1, :] += dg
        acc_ref[1:2, :] += jnp.where(is_ctx, dgate, zero)
        acc_ref[2:3, :] += jnp.where(is_ctx, zero, dgate)

    row = pl.BlockSpec((tl, Dm), lambda i: (i, 0))
    return _call(body, name="post_bwd", out_shape=(_sds((T, Dm), BF16), _sds((8, Dm), F32)), grid=(T // tl,),
                 in_specs=[row, pl.BlockSpec((1, Dm), lambda i: (0, 0)), pl.BlockSpec((2, Dm), lambda i: (0, 0)), row],
                 out_specs=(row, pl.BlockSpec((8, Dm), lambda i: (0, 0))), sem=("arbitrary",),
                 vmem_mb=48)(o, g, gate2, dX)


def _loss_kernel(X, target, n_ctx, tl=256):
    T, Dm = X.shape
    nct = n_ctx // tl

    def body(x_ref, t_ref, dx_ref, acc_ref):
        i = pl.program_id(0)

        @pl.when(i == 0)
        def _():
            acc_ref[...] = jnp.zeros_like(acc_ref)

        @pl.when(i < nct)
        def _():
            dx_ref[...] = jnp.zeros_like(dx_ref)

        @pl.when(i >= nct)
        def _():
            d = x_ref[...] - t_ref[...]
            dx_ref[...] = d * (1.0 / Dm)
            acc_ref[...] += jnp.sum(d * d, axis=0, keepdims=True)

    row = pl.BlockSpec((tl, Dm), lambda i: (i, 0))
    trow = pl.BlockSpec((tl, Dm), lambda i: (jnp.maximum(i - nct, 0), 0))
    return _call(body, name="loss", out_shape=(_sds((T, Dm), F32), _sds((1, Dm), F32)), grid=(T // tl,),
                 in_specs=[row, trow], out_specs=(row, pl.BlockSpec((1, Dm), lambda i: (0, 0))),
                 sem=("arbitrary",), vmem_mb=40)(X, target)


CONV_TL = 256
CONV_CB = 512


def _conv_taps(x, reverse):
    rows = x.shape[0]
    pos = lax.broadcasted_iota(jnp.int32, (rows, 1), 0)
    taps = []
    for k in range(CONV_W):
        off = (2 - k) if reverse else (k - 2)
        if off == 0:
            taps.append(x)
            continue
        xs = pltpu.roll(x, (-off) % rows, 0)
        valid = (pos < rows - off) if off > 0 else (pos >= -off)
        taps.append(jnp.where(valid, xs, 0.0))
    return taps


def _conv_pre(taps, w, b):
    pre = b + taps[0] * w[0:1, :]
    for k in range(1, CONV_W):
        pre = pre + taps[k] * w[k:k + 1, :]
    return pre


def _conv_subtiles(tl, cbw, row_len):
    return [(r0, c0) for c0 in range(0, cbw, LANES) for r0 in range(0, tl, row_len)]


def _conv_fwd(z, cw, cb, n_ctx):
    T = z.shape[0]
    tl, cbw = CONV_TL, CONV_CB
    assert n_ctx == tl

    def body(z_ref, w_ref, b_ref, o_ref):
        def run(row_len):
            for r0, c0 in _conv_subtiles(tl, cbw, row_len):
                rs, cs = pl.ds(r0, row_len), pl.ds(c0, LANES)
                pre = _conv_pre(_conv_taps(z_ref[rs, cs].astype(F32), False), w_ref[:, cs], b_ref[:, cs])
                o_ref[rs, cs] = _silu(pre).astype(BF16)

        @pl.when(pl.program_id(1) == 0)
        def _():
            run(n_ctx)

        @pl.when(pl.program_id(1) > 0)
        def _():
            run(GRID_W)

    blk = pl.BlockSpec((tl, cbw), lambda j, i: (i, j))
    return _call(body, name="conv_fwd", out_shape=_sds((T, XBC_W), BF16), grid=(XBC_W // cbw, T // tl),
                 in_specs=[blk, pl.BlockSpec((8, cbw), lambda j, i: (0, j)), pl.BlockSpec((1, cbw), lambda j, i: (0, j))],
                 out_specs=blk, sem=("parallel", "parallel"))(z, cw, cb)


def _conv_bwd(z, cw, cb, d0, d1, dz_in, col0, n_ctx):
    T = z.shape[0]
    tl, cbw = CONV_TL, CONV_CB

    def body(z_ref, w_ref, b_ref, d0_ref, d1_ref, dzin_ref, dz_ref, dw_ref, db_ref):
        i = pl.program_id(1)

        @pl.when(i == 0)
        def _():
            dw_ref[...] = jnp.zeros_like(dw_ref)
            db_ref[...] = jnp.zeros_like(db_ref)

        def run(row_len):
            for c0 in range(0, cbw, LANES):
                cs = pl.ds(c0, LANES)
                w = w_ref[:, cs]
                dw = [jnp.zeros((1, LANES), F32) for _ in range(CONV_W)]
                db = jnp.zeros((1, LANES), F32)
                for r0 in range(0, tl, row_len):
                    rs = pl.ds(r0, row_len)
                    taps = _conv_taps(z_ref[rs, cs].astype(F32), False)
                    pre = _conv_pre(taps, w, b_ref[:, cs])
                    s = _sigmoid(pre)
                    dpre = (d0_ref[rs, cs].astype(F32) + d1_ref[rs, cs].astype(F32)) * (s + pre * s * (1.0 - s))
                    dz_ref[rs, cs] = _conv_pre(_conv_taps(dpre, True), w, 0.0).astype(BF16)
                    for k in range(CONV_W):
                        dw[k] = dw[k] + jnp.sum(dpre * taps[k], axis=0, keepdims=True)
                    db = db + jnp.sum(dpre, axis=0, keepdims=True)
                for k in range(CONV_W):
                    dw_ref[k:k + 1, cs] += dw[k]
                db_ref[0:1, cs] += db

        @pl.when(i == 0)
        def _():
            run(n_ctx)

        @pl.when(i > 0)
        def _():
            run(GRID_W)

    width = d0.shape[1]
    jo = col0 // cbw
    blk = pl.BlockSpec((tl, cbw), lambda j, i: (i, jo + j))
    dblk = pl.BlockSpec((tl, cbw), lambda j, i: (i, j))
    par = pl.BlockSpec((8, cbw), lambda j, i: (0, jo + j))
    opar = pl.BlockSpec((8, cbw), lambda j, i: (0, j))
    return _call(body, name=f"conv_bwd_{col0}",
                 out_shape=(_sds(dz_in.shape, BF16), _sds((8, width), F32), _sds((8, width), F32)),
                 grid=(width // cbw, T // tl),
                 in_specs=[blk, par, pl.BlockSpec((1, cbw), lambda j, i: (0, jo + j)), dblk, dblk, ANY],
                 out_specs=(blk, opar, opar), sem=("parallel", "arbitrary"), aliases={5: 0})(z, cw, cb, d0, d1, dz_in)


MIX_TL = 128


def _gate_f(yf, yb, zs, g):
    t = (yf + yb) * _silu(zs)
    return t * _rms(t) * g


def _gate_fwd(yf, yb, z, g):
    T = yf.shape[0]
    tl = MIX_TL

    def body(yf_ref, yb_ref, zs_ref, g_ref, o_ref):
        o_ref[...] = _gate_f(yf_ref[...].astype(F32), yb_ref[...].astype(F32), zs_ref[...].astype(F32),
                             g_ref[...]).astype(BF16)

    row = pl.BlockSpec((tl, SSD_W), lambda i: (i, 0))
    return _call(body, name="gate_fwd", out_shape=_sds((T, SSD_W + MLP_W), BF16), grid=(T // tl,),
                 in_specs=[row, row, pl.BlockSpec((tl, SSD_W), lambda i: (i, XBC_W // SSD_W)),
                           pl.BlockSpec((1, SSD_W), lambda i: (0, 0))],
                 out_specs=row, sem=("parallel",))(yf, yb, z, g)


def _gate_bwd(yf, yb, z, g, dycat, dz_in):
    T = yf.shape[0]
    tl = MIX_TL

    def body(yf_ref, yb_ref, zs_ref, g_ref, d_ref, dzin_ref, dy_ref, dz_ref, acc_ref):
        @pl.when(pl.program_id(0) == 0)
        def _():
            acc_ref[...] = jnp.zeros_like(acc_ref)

        _, vjp = jax.vjp(_gate_f, yf_ref[...].astype(F32), yb_ref[...].astype(F32),
                         zs_ref[...].astype(F32), g_ref[...])
        dyf, _, dzs, dg = vjp(d_ref[...].astype(F32))
        dy_ref[...] = dyf.astype(BF16)
        dz_ref[...] = dzs.astype(BF16)
        acc_ref[0:1, :] += dg

    row = pl.BlockSpec((tl, SSD_W), lambda i: (i, 0))
    zs_spec = pl.BlockSpec((tl, SSD_W), lambda i: (i, XBC_W // SSD_W))
    return _call(body, name="gate_bwd",
                 out_shape=(_sds((T, SSD_W), BF16), _sds(dz_in.shape, BF16), _sds((8, SSD_W), F32)),
                 grid=(T // tl,),
                 in_specs=[row, row, zs_spec, pl.BlockSpec((1, SSD_W), lambda i: (0, 0)), row, ANY],
                 out_specs=(row, zs_spec, pl.BlockSpec((8, SSD_W), lambda i: (0, 0))),
                 sem=("arbitrary",), aliases={5: 1}, vmem_mb=40)(yf, yb, z, g, dycat, dz_in)


def _vnorm_f(v, gv):
    return v * _rms(v) * gv


def _mlp_out_f(u, sg, zm, gm):
    t = u * sg * _silu(zm)
    return t * _rms(t) * gm


U_BLK = (XBC_W + SSD_W) // MLP_W


def _mlp_mix(ws_ref, bst_ref, vn_s, sg_s):
    for gi in range(MLP_GROUPS):
        cols = pl.ds(gi * LANES, LANES)
        sg_s[:, cols] = _dot(ws_ref[gi], vn_s[:, cols], NN) + bst_ref[:, gi:gi + 1]


def _mlp_fwd(z, gv, ws, bst, gm, ycat_in):
    T = z.shape[0]
    tl = CHUNK

    def body(u_ref, v_ref, zm_ref, gv_ref, ws_ref, bst_ref, gm_ref, yin_ref, o_ref, vn_s, sg_s):
        vn_s[...] = _vnorm_f(v_ref[...].astype(F32), gv_ref[...]).astype(BF16)
        _mlp_mix(ws_ref, bst_ref, vn_s, sg_s)
        o_ref[...] = _mlp_out_f(u_ref[...].astype(F32), sg_s[...], zm_ref[...].astype(F32),
                                gm_ref[...]).astype(BF16)

    def zblk(b):
        return pl.BlockSpec((tl, MLP_W), lambda i: (i, b))

    vec = pl.BlockSpec((1, MLP_W), lambda i: (0, 0))
    return _call(body, name="mlp_fwd", out_shape=_sds(ycat_in.shape, BF16), grid=(T // tl,),
                 in_specs=[zblk(U_BLK), zblk(U_BLK + 1), zblk(U_BLK + 2), vec,
                           pl.BlockSpec((MLP_GROUPS, CHUNK, CHUNK), lambda i: (0, 0, 0)),
                           pl.BlockSpec((CHUNK, LANES), lambda i: (0, 0)), vec, ANY],
                 out_specs=pl.BlockSpec((tl, MLP_W), lambda i: (i, 1)),
                 scratch=[pltpu.VMEM((tl, MLP_W), BF16), pltpu.VMEM((tl, MLP_W), F32)],
                 sem=("parallel",), aliases={7: 0})(z, z, z, gv, ws, bst, gm, ycat_in)


def _mlp_bwd(z, gv, ws, bst, gm, dycat, dz_in):
    T = z.shape[0]
    tl = CHUNK

    def body(u_ref, v_ref, zm_ref, gv_ref, ws_ref, bst_ref, gm_ref, d_ref, dzin_ref,
             dz_ref, acc_ref, dws_ref, dbst_ref, vn_s, sg_s, dvn_s):
        @pl.when(pl.program_id(0) == 0)
        def _():
            acc_ref[...] = jnp.zeros_like(acc_ref)
            dws_ref[...] = jnp.zeros_like(dws_ref)
            dbst_ref[...] = jnp.zeros_like(dbst_ref)

        v = v_ref[...].astype(F32)
        vn, vjp_v = jax.vjp(_vnorm_f, v, gv_ref[...])
        vn_s[...] = vn.astype(BF16)
        _mlp_mix(ws_ref, bst_ref, vn_s, sg_s)
        _, vjp_o = jax.vjp(_mlp_out_f, u_ref[...].astype(F32), sg_s[...], zm_ref[...].astype(F32), gm_ref[...])
        du, dsg, dzm, dgm = vjp_o(d_ref[...].astype(F32))
        sg_s[...] = dsg
        for gi in range(MLP_GROUPS):
            cols = pl.ds(gi * LANES, LANES)
            dsg_g = sg_s[:, cols]
            dvn_s[:, cols] = _dot(ws_ref[gi], dsg_g, TN)
            dws_ref[gi] += _dot(dsg_g, vn_s[:, cols], NT)
            dbst_ref[:, gi:gi + 1] += jnp.sum(dsg_g, axis=1, keepdims=True)
        dv, dgv = vjp_v(dvn_s[...])
        dz_ref[:, 0:MLP_W] = du.astype(BF16)
        dz_ref[:, MLP_W:2 * MLP_W] = dv.astype(BF16)
        dz_ref[:, 2 * MLP_W:3 * MLP_W] = dzm.astype(BF16)
        acc_ref[0:1, :] += dgv
        acc_ref[1:2, :] += dgm

    def zblk(b):
        return pl.BlockSpec((tl, MLP_W), lambda i: (i, b))

    vec = pl.BlockSpec((1, MLP_W), lambda i: (0, 0))
    ws_spec = pl.BlockSpec((MLP_GROUPS, CHUNK, CHUNK), lambda i: (0, 0, 0))
    bst_spec = pl.BlockSpec((CHUNK, LANES), lambda i: (0, 0))
    return _call(body, name="mlp_bwd",
                 out_shape=(_sds(dz_in.shape, BF16), _sds((8, MLP_W), F32),
                            _sds((MLP_GROUPS, CHUNK, CHUNK), F32), _sds((CHUNK, LANES), F32)),
                 grid=(T // tl,),
                 in_specs=[zblk(U_BLK), zblk(U_BLK + 1), zblk(U_BLK + 2), vec, ws_spec, bst_spec, vec,
                           pl.BlockSpec((tl, MLP_W), lambda i: (i, 1)), ANY],
                 out_specs=(pl.BlockSpec((tl, 3 * MLP_W), lambda i: (i, 1)),
                            pl.BlockSpec((8, MLP_W), lambda i: (0, 0)), ws_spec, bst_spec),
                 scratch=[pltpu.VMEM((tl, MLP_W), BF16), pltpu.VMEM((tl, MLP_W), F32), pltpu.VMEM((tl, MLP_W), F32)],
                 sem=("arbitrary",), aliases={8: 0}, vmem_mb=48)(z, z, z, gv, ws, bst, gm, dycat, dz_in)


def _ssd_constants(direction):
    q = SSD_Q
    tri = np.tril(np.ones((q, q), np.float32))
    if direction == 1:
        tri = tri.T
    ex = np.zeros((SSD_GROUPS, LANES, GROUP_COLS), np.float32)
    for g in range(SSD_GROUPS):
        for r in range(HEADS_PER_GROUP):
            ex[g, direction * SSD_HEADS + HEADS_PER_GROUP * g + r, r * SSD_HEAD_DIM:(r + 1) * SSD_HEAD_DIM] = 1.0
    hm = np.zeros((8, GROUP_COLS), np.float32)
    for r in range(HEADS_PER_GROUP):
        hm[r, r * SSD_HEAD_DIM:(r + 1) * SSD_HEAD_DIM] = 1.0
    return (jnp.asarray(tri), jnp.asarray(np.concatenate([ex, ex], axis=1), dtype=BF16),
            jnp.asarray(np.ascontiguousarray(ex.transpose(0, 2, 1)), dtype=BF16), jnp.asarray(hm, dtype=BF16))


def _dot2(v, m, dims=NN):
    hi = v.astype(BF16)
    lo = (v - hi.astype(F32)).astype(BF16)
    return (lax.dot_general(hi, m, dims, preferred_element_type=F32)
            + lax.dot_general(lo, m, dims, preferred_element_type=F32))


def _ssd_prep(raw, prm):
    T = raw.shape[0]
    nc, Q = T // SSD_Q, SSD_Q
    tril = jnp.asarray(np.tril(np.ones((Q, Q), np.float32)))
    triu = jnp.asarray(np.triu(np.ones((Q, Q), np.float32)))

    def body(raw_ref, prm_ref, tril_ref, triu_ref, dt_ref, s_ref, pk_ref, st_ref, etot_ref):
        dt = _softplus(raw_ref[...] + prm_ref[0:1, :])
        a = dt * (-jnp.exp(prm_ref[1:2, :]))
        lane = lax.broadcasted_iota(jnp.int32, (1, LANES), 1)
        s = jnp.where(lane < SSD_HEADS, _dot_hi(tril_ref[...], a), _dot_hi(triu_ref[...], a))
        tot = jnp.sum(a, axis=0, keepdims=True)
        dt_ref[...] = dt
        s_ref[...] = s
        for n, v in enumerate((dt, jnp.exp(s), jnp.exp(tot - s))):
            hi = v.astype(BF16)
            pk_ref[:, 2 * n * LANES:(2 * n + 1) * LANES] = hi
            pk_ref[:, (2 * n + 1) * LANES:(2 * n + 2) * LANES] = (v - hi.astype(F32)).astype(BF16)
        st_ref[0] = s.T
        etot_ref[0] = _bcast8(jnp.exp(tot))

    row = pl.BlockSpec((Q, LANES), lambda c: (c, 0))
    cst = lambda shape: pl.BlockSpec(shape, lambda c: (0, 0))
    tl = _sds((T, LANES), F32)
    return _call(body, name="ssd_prep",
                 out_shape=(tl, tl, _sds((T, 6 * LANES), BF16), _sds((nc, LANES, Q), F32), _sds((nc, 8, LANES), F32)),
                 grid=(nc,), in_specs=[row, cst((8, LANES)), cst((Q, Q)), cst((Q, Q))],
                 out_specs=(row, row, pl.BlockSpec((Q, 6 * LANES), lambda c: (c, 0)),
                            pl.BlockSpec((1, LANES, Q), lambda c: (c, 0, 0)),
                            pl.BlockSpec((1, 8, LANES), lambda c: (c, 0, 0))),
                 sem=("parallel",))(raw, prm, tril, triu)


def _ssd_post(raw, prm, dt, ds0, ds1, ddt0, ddt1, dtot0, dtot1, daz0, daz1):
    T = raw.shape[0]
    nc, Q = T // SSD_Q, SSD_Q
    tril = jnp.asarray(np.tril(np.ones((Q, Q), np.float32)))
    triu = jnp.asarray(np.triu(np.ones((Q, Q), np.float32)))

    def body(raw_ref, prm_ref, dt_ref, ds0_ref, ds1_ref, ddt0_ref, ddt1_ref, dtot0_ref, dtot1_ref, daz0_ref, daz1_ref,
             tril_ref, triu_ref, draw_ref, dprm_ref):
        @pl.when(pl.program_id(0) == 0)
        def _():
            dprm_ref[...] = jnp.zeros_like(dprm_ref)

        pre = raw_ref[...] + prm_ref[0:1, :]
        A = -jnp.exp(prm_ref[1:2, :])
        ds = ds0_ref[...] + ds1_ref[...]
        lane = lax.broadcasted_iota(jnp.int32, (1, LANES), 1)
        da = jnp.where(lane < SSD_HEADS, _dot_hi(triu_ref[...], ds), _dot_hi(tril_ref[...], ds))
        da = da + dtot0_ref[0, 0:1, :] + dtot1_ref[0, 0:1, :]
        daz_rows = jnp.concatenate([daz0_ref[0], daz1_ref[0], jnp.zeros((LANES - 2 * SSD_HEADS, Q), F32)], axis=0)
        da = da + daz_rows.T
        draw = (da * A + ddt0_ref[...] + ddt1_ref[...]) * _sigmoid(pre)
        draw_ref[...] = draw
        dprm_ref[0:1, :] += jnp.sum(draw, axis=0, keepdims=True)
        dprm_ref[1:2, :] += jnp.sum(da * dt_ref[...], axis=0, keepdims=True) * A

    row = pl.BlockSpec((Q, LANES), lambda c: (c, 0))
    tot = pl.BlockSpec((1, 8, LANES), lambda c: (c, 0, 0))
    daz = pl.BlockSpec((1, SSD_HEADS, Q), lambda c: (c, 0, 0))
    cst = lambda shape: pl.BlockSpec(shape, lambda c: (0, 0))
    return _call(body, name="ssd_post", out_shape=(_sds((T, LANES), F32), _sds((8, LANES), F32)), grid=(nc,),
                 in_specs=[row, cst((8, LANES)), row, row, row, row, row, tot, tot, daz, daz, cst((Q, Q)), cst((Q, Q))],
                 out_specs=(row, cst((8, LANES))), sem=("arbitrary",))(raw, prm, dt, ds0, ds1, ddt0, ddt1, dtot0, dtot1,
                                                                       daz0, daz1, tril, triu)


def _chunk_order(direction, nc, nctx):
    if direction == 0:
        return lambda c: c
    return lambda c: jnp.where(c < nctx, nctx - 1 - c, nc - 1 + nctx - c)


def _bcast8(row):
    return jnp.broadcast_to(row, (8, row.shape[1]))


GPS = 2


def _cols(ref, k, width):
    return ref[:, k * width:(k + 1) * width]


def _ssd_common(k, x_ref, b_ref, c_ref, s_ref, pk_ref, st_ref, etot_ref, ex_ref, direction):
    g = pl.program_id(1) * GPS + k
    base = direction * SSD_HEADS + HEADS_PER_GROUP * g
    x = _cols(x_ref, k, GROUP_COLS).astype(F32)
    ex = ex_ref[k]

    def spread(n):
        return lax.dot_general(_cols(pk_ref, n, 2 * LANES), ex, NN, preferred_element_type=F32)

    dt_x, E_x, D_x = spread(0), spread(1), spread(2)
    lane = lax.broadcasted_iota(jnp.int32, (1, LANES), 1)
    etot = etot_ref[0, 0:1, :]
    rowi = lax.broadcasted_iota(jnp.int32, (GROUP_COLS, 1), 0)
    e = [jnp.sum(jnp.where(lane == base + r, etot, 0.0), axis=1, keepdims=True) for r in range(HEADS_PER_GROUP)]
    Etot_c = jnp.where(rowi < SSD_HEAD_DIM, e[0],
                       jnp.where(rowi < 2 * SSD_HEAD_DIM, e[1], jnp.where(rowi < 3 * SSD_HEAD_DIM, e[2], e[3])))
    rows = [st_ref[0, HEADS_PER_GROUP * k + r:HEADS_PER_GROUP * k + r + 1, :] for r in range(HEADS_PER_GROUP)]
    return dict(g=g, base=base, x=x, s=s_ref[...], rows=rows, dt_x=dt_x, lane=lane, X=x * dt_x, E_x=E_x, D_x=D_x,
                Etot_c=Etot_c, Bm=_cols(b_ref, k, SSD_STATE), Cm=_cols(c_ref, k, SSD_STATE))


def _head_decay(q, r, mask):
    sel = q["lane"] == q["base"] + r
    col = jnp.sum(jnp.where(sel, q["s"], 0.0), axis=1, keepdims=True)
    return jnp.exp(jnp.where(mask, col - q["rows"][r], NEG_BIG))


def _ssd_in_specs(order, direction):
    Q, GC, N, G = SSD_Q, GROUP_COLS, SSD_STATE, SSD_GROUPS
    row = pl.BlockSpec((Q, LANES), lambda c, g: (order(c), 0))
    hrows = HEADS_PER_GROUP * GPS
    return [
        pl.BlockSpec((Q, GPS * GC), lambda c, g: (order(c), g)),
        pl.BlockSpec((Q, GPS * N), lambda c, g: (order(c), SSD_W // (GPS * N) + g)),
        pl.BlockSpec((Q, GPS * N), lambda c, g: (order(c), (SSD_W + G * N) // (GPS * N) + g)),
        row, pl.BlockSpec((Q, 6 * LANES), lambda c, g: (order(c), 0)),
        pl.BlockSpec((1, hrows, Q), lambda c, g: (order(c), direction * (SSD_HEADS // hrows) + g, 0)),
        pl.BlockSpec((1, 8, LANES), lambda c, g: (order(c), 0, 0)),
    ]


def _ssd_fwd(xbc, prep, dskx, direction, n_ctx):
    T = xbc.shape[0]
    nc, nctx = T // SSD_Q, n_ctx // SSD_Q
    order = _chunk_order(direction, nc, nctx)
    tri, ex, ext, hm = _ssd_constants(direction)
    Q, GC, N, G = SSD_Q, GROUP_COLS, SSD_STATE, SSD_GROUPS

    def body(x_ref, b_ref, c_ref, s_ref, pk_ref, st_ref, etot_ref, dsk_ref, tri_ref, ex_ref, hm_ref,
             y_ref, sst_ref, S):
        c = pl.program_id(0)
        mask = tri_ref[...] > 0.5
        for k in range(GPS):
            q = _ssd_common(k, x_ref, b_ref, c_ref, s_ref, pk_ref, st_ref, etot_ref, ex_ref, direction)
            g = q["g"]

            @pl.when(c == 0)
            def _():
                S[g] = jnp.zeros((GC, N), F32)

            S0 = S[g]
            sst_ref[0, k] = S0
            X, Bm, Cm = q["X"], q["Bm"], q["Cm"]
            Xb = X.astype(BF16)
            y = q["E_x"] * _dot(Cm, S0, NT)
            Gm = _dot(Cm, Bm, NT)
            for r in range(HEADS_PER_GROUP):
                y = y + _dot(Gm * _head_decay(q, r, mask), Xb * hm_ref[r:r + 1, :], NN)
            if direction == 0:
                y = y + _cols(dsk_ref, k, GC)[0:1, :] * q["x"]
            y_ref[:, k * GC:(k + 1) * GC] = y.astype(BF16)
            S[g] = q["Etot_c"] * S0 + _dot(X * q["D_x"], Bm, TN)

    cst2 = lambda shape: pl.BlockSpec(shape, lambda c, g: (0, 0))
    in_specs = _ssd_in_specs(order, direction) + [
        pl.BlockSpec((8, GPS * GC), lambda c, g: (0, g)), cst2((Q, Q)),
        pl.BlockSpec((GPS, 2 * LANES, GC), lambda c, g: (g, 0, 0)), cst2((8, GC))]
    out_specs = (pl.BlockSpec((Q, GPS * GC), lambda c, g: (order(c), g)),
                 pl.BlockSpec((1, GPS, GC, N), lambda c, g: (order(c), g, 0, 0)))
    return _call(body, name=f"ssd_fwd_{direction}",
                 out_shape=(_sds((T, SSD_W), BF16), _sds((nc, G, GC, N), F32)),
                 grid=(nc, G // GPS), in_specs=in_specs, out_specs=out_specs,
                 scratch=[pltpu.VMEM((G, GC, N), F32)], sem=("arbitrary", "arbitrary"),
                 )(xbc, xbc, xbc, *prep[1:], dskx, tri, ex, hm)


def _ssd_bwd(xbc, prep, dskx, sst, dy, direction, n_ctx):
    T = xbc.shape[0]
    nc, nctx = T // SSD_Q, n_ctx // SSD_Q
    fwd_order = _chunk_order(direction, nc, nctx)
    order = lambda c: fwd_order(nc - 1 - c)
    tri, ex, ext, hm = _ssd_constants(direction)
    ntri = (1.0 - tri).astype(BF16)
    Q, GC, N, G = SSD_Q, GROUP_COLS, SSD_STATE, SSD_GROUPS

    def body(x_ref, b_ref, c_ref, s_ref, pk_ref, st_ref, etot_ref, dsk_ref, sst_ref, dy_ref, tri_ref,
             ex_ref, ext_ref, hm_ref, ntri_ref, dx_ref, db_ref, dc_ref, ds_ref, ddt_ref, dtot_ref, daz_ref, dskg_ref,
             dS):
        c, gi = pl.program_id(0), pl.program_id(1)
        mask = tri_ref[...] > 0.5

        @pl.when(jnp.logical_and(c == 0, gi == 0))
        def _():
            dskg_ref[...] = jnp.zeros_like(dskg_ref)

        @pl.when(gi == 0)
        def _():
            ds_ref[...] = jnp.zeros_like(ds_ref)
            ddt_ref[...] = jnp.zeros_like(ddt_ref)
            dtot_ref[...] = jnp.zeros_like(dtot_ref)

        row8 = lax.broadcasted_iota(jnp.int32, (HEADS_PER_GROUP * GPS, 1), 0)
        daz = jnp.zeros((HEADS_PER_GROUP * GPS, Q), F32)
        for k in range(GPS):
            q = _ssd_common(k, x_ref, b_ref, c_ref, s_ref, pk_ref, st_ref, etot_ref, ex_ref, direction)
            g = q["g"]

            @pl.when(c == 0)
            def _():
                dS[g] = jnp.zeros((GC, N), F32)

            dS1 = dS[g]
            S0 = sst_ref[0, k]
            x, X, Bm, Cm, ext = q["x"], q["X"], q["Bm"], q["Cm"], ext_ref[k]
            E_x, D_x, Etot_c = q["E_x"], q["D_x"], q["Etot_c"]
            dYb = _cols(dy_ref, k, GC)
            dY = dYb.astype(F32)
            Xb = X.astype(BF16)

            CS = _dot(Cm, S0, NT)
            dCS = dY * E_x
            dC = _dot(dCS, S0, NN)
            dS0 = Etot_c * dS1 + _dot(dCS, Cm, TN)
            ds_x = dY * (E_x * CS)
            dtot_c = jnp.sum(dS1 * S0, axis=1, keepdims=True) * Etot_c
            dtot = jnp.sum(dtot_c * ext.astype(F32), axis=0, keepdims=True)
            XD = X * D_x
            dXD = _dot(Bm, dS1, NT)
            dB = _dot(XD, dS1, NN)
            dX = dXD * D_x
            t = dXD * XD
            ds_x = ds_x - t
            dtot_x = jnp.sum(t, axis=0, keepdims=True)
            Gm = _dot(Cm, Bm, NT)
            dG = jnp.zeros((Q, Q), F32)
            for r in range(HEADS_PER_GROUP):
                hmr = hm_ref[r:r + 1, :]
                Lm = _head_decay(q, r, mask)
                W = Gm * Lm
                dW = _dot(dYb * hmr, Xb, NT)
                dX = dX + _dot(W, dYb, TN) * hmr.astype(F32)
                dG = dG + dW * Lm
                P = _dot(dW * W, ntri_ref[...], NN)
                da_row = jnp.sum(jnp.where(mask, P, 0.0), axis=0, keepdims=True)
                daz = jnp.where(row8 == HEADS_PER_GROUP * k + r, da_row, daz)
            dC = dC + _dot(dG, Bm, NN)
            dB = dB + _dot(dG, Cm, TN)
            ds_ref[...] += _dot2(ds_x, ext)
            dtot_ref[0, 0:1, :] += dtot + _dot2(_bcast8(dtot_x), ext)[0:1, :]
            ddt_ref[...] += _dot(dX * x, ext, NN)
            dx = dX * q["dt_x"]
            if direction == 0:
                dx = dx + dY * _cols(dsk_ref, k, GC)[0:1, :]
                dskg_ref[0:1, :] += jnp.sum(_dot(dY * x, ext, NN), axis=0, keepdims=True)
            dx_ref[:, k * GC:(k + 1) * GC] = dx.astype(BF16)
            db_ref[:, k * N:(k + 1) * N] = dB.astype(BF16)
            dc_ref[:, k * N:(k + 1) * N] = dC.astype(BF16)
            dS[g] = dS0
        daz_ref[0] = daz

    cst2 = lambda shape: pl.BlockSpec(shape, lambda c, g: (0, 0))
    in_specs = _ssd_in_specs(order, direction) + [
        pl.BlockSpec((8, GPS * GC), lambda c, g: (0, g)),
        pl.BlockSpec((1, GPS, GC, N), lambda c, g: (order(c), g, 0, 0)),
        pl.BlockSpec((Q, GPS * GC), lambda c, g: (order(c), g)),
        cst2((Q, Q)),
        pl.BlockSpec((GPS, 2 * LANES, GC), lambda c, g: (g, 0, 0)),
        pl.BlockSpec((GPS, GC, LANES), lambda c, g: (g, 0, 0)),
        cst2((8, GC)), cst2((Q, Q)),
    ]
    row = pl.BlockSpec((Q, LANES), lambda c, g: (order(c), 0))
    hrows = HEADS_PER_GROUP * GPS
    out_specs = (pl.BlockSpec((Q, GPS * GC), lambda c, g: (order(c), g)),
                 pl.BlockSpec((Q, GPS * N), lambda c, g: (order(c), g)),
                 pl.BlockSpec((Q, GPS * N), lambda c, g: (order(c), g)),
                 row, row, pl.BlockSpec((1, 8, LANES), lambda c, g: (order(c), 0, 0)),
                 pl.BlockSpec((1, hrows, Q), lambda c, g: (order(c), g, 0)),
                 cst2((8, LANES)))
    return _call(body, name=f"ssd_bwd_{direction}",
                 out_shape=(_sds((T, SSD_W), BF16), _sds((T, G * N), BF16), _sds((T, G * N), BF16),
                            _sds((T, LANES), F32), _sds((T, LANES), F32), _sds((nc, 8, LANES), F32),
                            _sds((nc, SSD_HEADS, Q), F32), _sds((8, LANES), F32)),
                 grid=(nc, G // GPS), in_specs=in_specs, out_specs=out_specs,
                 scratch=[pltpu.VMEM((G, GC, N), F32)], sem=("arbitrary", "arbitrary"),
                 )(xbc, xbc, xbc, *prep[1:], dskx, sst, dy, tri, ex, ext, hm, ntri)


ADA_ROWS = 16


def _ada_fwd(c16, w_ada, b_loc):
    depth, Dm, W = w_ada.shape

    def body(c_ref, w_ref, b_ref, o_ref):
        o_ref[0] = _dot_hi(_silu(c_ref[...]), w_ref[0]) + b_ref[0]

    return _call(body, name="ada_fwd", out_shape=_sds((depth, ADA_ROWS, W), F32), grid=(depth,),
                 in_specs=[pl.BlockSpec((ADA_ROWS, Dm), lambda l: (0, 0)),
                           pl.BlockSpec((1, Dm, W), lambda l: (l, 0, 0)),
                           pl.BlockSpec((1, 1, W), lambda l: (l, 0, 0))],
                 out_specs=pl.BlockSpec((1, ADA_ROWS, W), lambda l: (l, 0, 0)),
                 sem=("parallel",), vmem_mb=40)(c16, w_ada, b_loc)


def _ada_bwd(c16, w_ada, dmod):
    depth, Dm, W = w_ada.shape

    def body(c_ref, w_ref, d_ref, gw_ref, dc_ref):
        l = pl.program_id(0)

        @pl.when(l == 0)
        def _():
            dc_ref[...] = jnp.zeros_like(dc_ref)

        cc = c_ref[...]
        sg = _sigmoid(cc)
        gw_ref[0] = _dot_hi(cc * sg, d_ref[0], TN)
        dsc = _dot_hi(d_ref[0], w_ref[0], NT)
        dc_ref[...] += dsc[8:16, :] * (sg + cc * sg * (1.0 - sg))[8:16, :]

    return _call(body, name="ada_bwd", out_shape=(_sds((depth, Dm, W), F32), _sds((8, Dm), F32)), grid=(depth,),
                 in_specs=[pl.BlockSpec((ADA_ROWS, Dm), lambda l: (0, 0)),
                           pl.BlockSpec((1, Dm, W), lambda l: (l, 0, 0)),
                           pl.BlockSpec((1, ADA_ROWS, W), lambda l: (l, 0, 0))],
                 out_specs=(pl.BlockSpec((1, Dm, W), lambda l: (l, 0, 0)), pl.BlockSpec((8, Dm), lambda l: (0, 0))),
                 sem=("arbitrary",), vmem_mb=48)(c16, w_ada, dmod)


def _sum_slabs(g, name, tr=512):
    _, R, C = g.shape
    tr = min(tr, R)

    def body(g_ref, o_ref):
        acc = g_ref[0]
        for k in range(1, N_DEV):
            acc = acc + g_ref[k]
        o_ref[...] = acc

    return _call(body, name=name, out_shape=_sds((R, C), F32), grid=(R // tr,),
                 in_specs=[pl.BlockSpec((N_DEV, tr, C), lambda i: (0, i, 0))],
                 out_specs=pl.BlockSpec((tr, C), lambda i: (i, 0)), sem=("parallel",), vmem_mb=40)(g)


def _adamw_math(w, g, m, v):
    m = ADAM_B1 * m + (1.0 - ADAM_B1) * g
    v = ADAM_B2 * v + (1.0 - ADAM_B2) * (g * g)
    m_hat = m / (1.0 - ADAM_B1 ** ADAM_STEP)
    v_hat = v / (1.0 - ADAM_B2 ** ADAM_STEP)
    delta = -ADAM_LR * (m_hat / (jnp.sqrt(v_hat) + ADAM_EPS) + ADAM_WD * w)
    return delta, m, v


def _adamw(w, g, m, v, name, tr):
    R, C = w.shape
    tr = min(tr, R)
    assert R % tr == 0

    def body(w_ref, g_ref, m_ref, v_ref, d_ref, nm_ref, nv_ref):
        d, nm, nv = _adamw_math(w_ref[...], g_ref[...], m_ref[...], v_ref[...])
        d_ref[...] = d
        nm_ref[...] = nm
        nv_ref[...] = nv

    blk = pl.BlockSpec((tr, C), lambda i: (i, 0))
    out = _sds((R, C), F32)
    return _call(body, name=name, out_shape=(out, out, out), grid=(R // tr,), in_specs=[blk] * 4,
                 out_specs=(blk, blk, blk), sem=("parallel",), vmem_mb=40)(w, g, m, v)


def _adamw_slabs(w, slabs, m, v, name, tr):
    depth, R, C = w.shape
    assert R % tr == 0 and len(slabs) == depth

    def body(w_ref, *rest):
        slab_refs = rest[:depth]
        m_ref, v_ref, g_ref, d_ref, nm_ref, nv_ref = rest[depth:]
        layer = pl.program_id(0)
        for ll in range(depth):
            @pl.when(layer == ll)
            def _(s_ref=slab_refs[ll]):
                g = s_ref[0].astype(F32)
                for k in range(1, N_DEV):
                    g = g + s_ref[k].astype(F32)
                d, nm, nv = _adamw_math(w_ref[0], g, m_ref[0], v_ref[0])
                g_ref[0] = g
                d_ref[0] = d
                nm_ref[0] = nm
                nv_ref[0] = nv

    blk = pl.BlockSpec((1, tr, C), lambda l, i: (l, i, 0))

    def slab_spec(ll):
        return pl.BlockSpec((N_DEV, tr, C), lambda l, i: (0, jnp.where(l == ll, i, 0), 0))

    out = _sds((depth, R, C), F32)
    return _call(body, name=name, out_shape=(out, out, out, out), grid=(depth, R // tr),
                 in_specs=[blk] + [slab_spec(ll) for ll in range(depth)] + [blk, blk],
                 out_specs=(blk, blk, blk, blk), sem=("parallel", "parallel"), vmem_mb=56)(w, *slabs, m, v)


PACK_QUANTUM = 512 * LANES


def _pack(arrays):
    flat = jnp.concatenate([a.reshape(-1).astype(F32) for a in arrays])
    pad = (-flat.shape[0]) % PACK_QUANTUM
    return jnp.pad(flat, (0, pad)).reshape(-1, LANES)


def _unpack(bundle, shapes):
    flat = bundle.reshape(-1)
    out, off = [], 0
    for shp in shapes:
        n = int(np.prod(shp))
        out.append(flat[off:off + n].reshape(shp))
        off += n
    return out


def _row(v):
    return v.reshape(1, -1)


def _pad_rows(a, rows):
    return jnp.pad(a, ((0, rows - a.shape[0]), (0, 0)))


def kernel(x, c, ctx, c_ctx, w_ada, b_ada, g_pre, g_post, w_in, conv_w, conv_b, dt_bias, a_log, d_skip, g_ssd, g_v, w_s, b_s, g_mlp, w_out, loss_target, m_c_ctx, m_w_ada, m_b_ada, m_g_pre, m_g_post, m_w_in, m_conv_w, m_conv_b, m_dt_bias, m_a_log, m_d_skip, m_g_ssd, m_g_v, m_w_s, m_b_s, m_g_mlp, m_w_out, v_c_ctx, v_w_ada, v_b_ada, v_g_pre, v_g_post, v_w_in, v_conv_w, v_conv_b, v_dt_bias, v_a_log, v_d_skip, v_g_ssd, v_g_v, v_w_s, v_b_s, v_g_mlp, v_w_out):
    depth = w_in.shape[0]
    L = x.shape[1]
    n_ctx = ctx.shape[1]
    T = n_ctx + L
    Dm = D_MODEL
    me = _lin(_my_pos())
    ada_w = w_ada.shape[2]
    in_w = w_in.shape[2]
    out_r = w_out.shape[1]
    conv_c = conv_w.shape[2]
    TM = 768

    c_all = _all_gather_small(_pad_rows(c, 8), "gather_c")[:, 0, :]
    c16 = _pad_rows(jnp.concatenate([c_all, _row(c_ctx)], axis=0), ADA_ROWS)
    b_loc = lax.dynamic_slice_in_dim(b_ada, me * ada_w, ada_w, axis=1)[:, None, :]
    mod_loc = _ada_fwd(c16, w_ada, b_loc)
    mod_all = _all_gather_small(mod_loc.reshape(depth * ADA_ROWS, ada_w), "gather_mod")
    mod_all = mod_all.reshape(N_DEV, depth, ADA_ROWS, ada_w)
    mod_me = lax.dynamic_index_in_dim(mod_all, me, axis=2, keepdims=False)
    mod_me = jnp.transpose(mod_me, (1, 0, 2)).reshape(depth, N_DEV * ada_w)
    mod_cx = jnp.transpose(mod_all[:, :, 8, :], (1, 0, 2)).reshape(depth, N_DEV * ada_w)
    shift2 = jnp.stack([mod_cx[:, 0:Dm], mod_me[:, 0:Dm]], axis=1)
    scale2 = jnp.stack([mod_cx[:, Dm:2 * Dm], mod_me[:, Dm:2 * Dm]], axis=1)
    gate2 = jnp.stack([mod_cx[:, 2 * Dm:], mod_me[:, 2 * Dm:]], axis=1)

    w_in_bf = w_in.astype(BF16)
    w_out_bf = w_out.astype(BF16)
    w_in_g = _all_gather_big(w_in_bf[0], "gather_w_in")
    w_out_g = _all_gather_big(w_out_bf[0], "gather_w_out")
    conv_all = _all_gather_small(_pad_rows(conv_w.reshape(depth * CONV_W, conv_c), 24).reshape(24, conv_c),
                                 "gather_conv_w")
    conv_full = jnp.transpose(conv_all[:, :depth * CONV_W, :], (1, 0, 2)).reshape(depth, CONV_W, XBC_W)

    def in_weights(gathered):
        wf = jnp.transpose(gathered, (1, 0, 2)).reshape(Dm, IN_W)
        return (jnp.concatenate([wf[:, :XBC_W], wf[:, XBC_W + DT_W:]], axis=1),
                jnp.pad(wf[:, XBC_W:XBC_W + DT_W], ((0, 0), (0, LANES - DT_W))))

    w_main, w_dt, w_o = [None] * depth, [None] * depth, [None] * depth

    def ssd_prm(l):
        rows = jnp.stack([jnp.pad(dt_bias[l].reshape(-1), (0, LANES - DT_W)),
                          jnp.pad(a_log[l].reshape(-1), (0, LANES - DT_W)),
                          jnp.pad(d_skip[l], (0, LANES - SSD_HEADS))])
        return _pad_rows(rows, 8)

    ws_bf = w_s.astype(BF16)
    bst = jnp.pad(jnp.transpose(b_s, (0, 2, 1)), ((0, 0), (0, 0), (0, LANES - MLP_GROUPS)))

    X = jnp.concatenate([ctx[0], x[0]], axis=0)
    saved = []
    for l in range(depth):
        w_main[l], w_dt[l] = in_weights(w_in_g)
        w_o[l] = w_out_g.reshape(N_DEV * out_r, Dm)
        hx = _prenorm_fwd(X, _row(g_pre[l]), scale2[l], shift2[l], n_ctx)
        if l + 1 < depth:
            z, w_in_g = _mm(hx, w_main[l], "nn", BF16, "in_proj_gather", TM, 1024, 2048,
                            exchange=("gather", w_in_bf[l + 1]))
        else:
            z = _mm(hx, w_main[l], "nn", BF16, "in_proj", TM, 1024, 2048)
        raw = _mm(hx, w_dt[l], "nn", F32, "dt_proj", TM, LANES, 2048)
        cw = _pad_rows(conv_full[l], 8)
        cb = _row(conv_b[l])
        xbc = _conv_fwd(z, cw, cb, n_ctx)
        prm = ssd_prm(l)
        prep = _ssd_prep(raw, prm)
        dskx = _pad_rows(_row(jnp.repeat(d_skip[l], SSD_HEAD_DIM)), 8)
        y_f, sst_f = _ssd_fwd(xbc, prep, dskx, 0, n_ctx)
        y_b, sst_b = _ssd_fwd(xbc, prep, dskx, 1, n_ctx)
        ycat = _gate_fwd(y_f, y_b, z, _row(g_ssd[l]))
        ycat = _mlp_fwd(z, _row(g_v[l]), ws_bf[l], bst[l], _row(g_mlp[l]), ycat)
        if l + 1 < depth:
            o, w_out_g = _mm(ycat, w_o[l], "nn", F32, "out_proj_gather", TM, 1024, 4096,
                             exchange=("gather", w_out_bf[l + 1]))
        else:
            o = _mm(ycat, w_o[l], "nn", F32, "out_proj", TM, 1024, 4096)
        saved.append((X, hx, z, raw, cw, cb, xbc, prm, prep, dskx, y_f, sst_f, y_b, sst_b, ycat, o))
        X = _post_fwd(X, o, _row(g_post[l]), gate2[l], n_ctx)

    dX, sq = _loss_kernel(X, loss_target[0], n_ctx)
    loss = lax.psum(0.5 * jnp.sum(sq) / Dm, ("x", "y", "c"))

    g_small = {k: [None] * depth for k in
               ("b_ada", "g_pre", "g_post", "conv_b", "dt_bias", "a_log", "d_skip", "g_ssd", "g_v", "w_s", "b_s",
                "g_mlp", "conv_w", "dmod_c", "dmod")}
    in_recv, out_recv = [None] * depth, [None] * depth
    for l in reversed(range(depth)):
        Xl, hx, z, raw, cw, cb, xbc, prm, prep, dskx, y_f, sst_f, y_b, sst_b, ycat, o = saved[l]
        d_o, acc_post = _post_bwd(o, _row(g_post[l]), gate2[l], dX, n_ctx)
        dycat = _mm(d_o, w_o[l], "nt", BF16, "out_proj_dx", TM, 1024, 2048)
        out_slabs = _mm(ycat, d_o, "tn", BF16, "out_proj_dw", 1024, 1024, 2816).reshape(N_DEV, out_r, Dm)
        dz = lax.empty((T, Z_MAIN), BF16)
        dy, dz, acc_gate = _gate_bwd(y_f, y_b, z, _row(g_ssd[l]), dycat, dz)
        dz, acc_mlp, dws, dbst = _mlp_bwd(z, _row(g_v[l]), ws_bf[l], bst[l], _row(g_mlp[l]), dycat, dz)
        dx0, db0, dc0, ds0, ddt0, dtot0, daz0, dsk0 = _ssd_bwd(xbc, prep, dskx, sst_f, dy, 0, n_ctx)
        dx1, db1, dc1, ds1, ddt1, dtot1, daz1, _ = _ssd_bwd(xbc, prep, dskx, sst_b, dy, 1, n_ctx)
        draw, dprm = _ssd_post(raw, prm, prep[0], ds0, ds1, ddt0, ddt1, dtot0, dtot1, daz0, daz1)
        gn = SSD_GROUPS * SSD_STATE
        dz, dcw_x, dcb_x = _conv_bwd(z, cw, cb, dx0, dx1, dz, 0, n_ctx)
        dz, dcw_b, dcb_b = _conv_bwd(z, cw, cb, db0, db1, dz, SSD_W, n_ctx)
        dz, dcw_c, dcb_c = _conv_bwd(z, cw, cb, dc0, dc1, dz, SSD_W + gn, n_ctx)
        dcw = jnp.concatenate([dcw_x, dcw_b, dcw_c], axis=1)
        dcb = jnp.concatenate([dcb_x, dcb_b, dcb_c], axis=1)
        gw_main, out_recv[l] = _mm(hx, dz, "tn", BF16, "in_proj_dw_scatter", 1024, 1024, 2816,
                                   exchange=("scatter", out_slabs))
        gw_dt = _mm(hx, draw, "tn", BF16, "dt_proj_dw", 1024, LANES, 2816)
        gw_full = jnp.concatenate([gw_main[:, :XBC_W], gw_dt[:, :DT_W], gw_main[:, XBC_W:]], axis=1)
        in_slabs = jnp.transpose(gw_full.reshape(Dm, N_DEV, in_w), (1, 0, 2))
        dhx1, in_recv[l] = _mm(dz, w_main[l], "nt", F32, "in_proj_dx_scatter", TM, 1024, 4096,
                               exchange=("scatter", in_slabs))
        dhx2 = _mm(draw, w_dt[l], "nt", F32, "dt_proj_dx", TM, 512, LANES)
        dX, acc_pre = _prenorm_bwd(Xl, _row(g_pre[l]), scale2[l], shift2[l], dhx1, dhx2, dX, n_ctx)

        dmod_c = jnp.concatenate([acc_pre[3], acc_pre[1], acc_post[1]])
        dmod_x = jnp.concatenate([acc_pre[4], acc_pre[2], acc_post[2]])
        g_small["dmod_c"][l] = dmod_c
        g_small["dmod"][l] = dmod_x
        g_small["b_ada"][l] = dmod_c + dmod_x
        g_small["g_pre"][l] = acc_pre[0]
        g_small["g_post"][l] = acc_post[0]
        g_small["conv_b"][l] = dcb[0]
        g_small["conv_w"][l] = dcw[:CONV_W]
        g_small["dt_bias"][l] = dprm[0, :DT_W].reshape(2, SSD_HEADS)
        g_small["a_log"][l] = dprm[1, :DT_W].reshape(2, SSD_HEADS)
        g_small["d_skip"][l] = dsk0[0, :SSD_HEADS]
        g_small["g_ssd"][l] = acc_gate[0]
        g_small["g_v"][l] = acc_mlp[0]
        g_small["g_mlp"][l] = acc_mlp[1]
        g_small["w_s"][l] = dws
        g_small["b_s"][l] = jnp.transpose(dbst[:, :MLP_GROUPS])

    grad_x = dX[n_ctx:][None]

    summed_names = ["b_ada", "g_pre", "g_post", "conv_b", "dt_bias", "a_log", "d_skip", "g_ssd", "g_v", "w_s",
                    "b_s", "g_mlp", "conv_w", "dmod_c"]
    parts = [jnp.stack(g_small[k]) for k in summed_names] + [jnp.stack(g_small["dmod"])]
    shapes = [p.shape for p in parts]
    bundle = _pack(parts)
    gathered = _all_gather_big(bundle, "gather_small_grads")
    reduced = _unpack(_sum_slabs(gathered, "sum_small_grads"), shapes)
    gs = dict(zip(summed_names, reduced[:-1]))
    n_el = int(np.prod(shapes[-1]))
    off = sum(int(np.prod(s)) for s in shapes[:-1])
    dmod_all = gathered.reshape(N_DEV, -1)[:, off:off + n_el].reshape(N_DEV, depth, 3 * Dm)

    dmod_rows = jnp.concatenate([jnp.transpose(dmod_all, (1, 0, 2)), gs["dmod_c"][:, None, :]], axis=1)
    dmod_rows = lax.dynamic_slice_in_dim(dmod_rows, me * ada_w, ada_w, axis=2)
    dmod_rows = jnp.pad(dmod_rows, ((0, 0), (0, ADA_ROWS - 9), (0, 0)))
    g_w_ada, dc_part = _ada_bwd(c16, w_ada, dmod_rows)
    dc_all = _all_gather_small(dc_part, "gather_dc")
    g_c_ctx = _sum_slabs(dc_all, "sum_dc")[0]

    grads_small = {"c_ctx": g_c_ctx, "b_ada": gs["b_ada"], "g_pre": gs["g_pre"], "g_post": gs["g_post"],
                   "conv_b": gs["conv_b"], "dt_bias": gs["dt_bias"], "a_log": gs["a_log"], "d_skip": gs["d_skip"],
                   "g_ssd": gs["g_ssd"], "g_v": gs["g_v"], "w_s": gs["w_s"], "b_s": gs["b_s"], "g_mlp": gs["g_mlp"],
                   "conv_w": lax.dynamic_slice_in_dim(gs["conv_w"], me * conv_c, conv_c, axis=2)}

    g_in, d_in, nm_in, nv_in = _adamw_slabs(w_in, in_recv, m_w_in, v_w_in, "adamw_w_in", 128)
    g_out, d_out, nm_out, nv_out = _adamw_slabs(w_out, out_recv, m_w_out, v_w_out, "adamw_w_out", 128)
    d_ada, nm_ada, nv_ada = _adamw(w_ada.reshape(depth * Dm, ada_w), g_w_ada.reshape(depth * Dm, ada_w),
                                   m_w_ada.reshape(depth * Dm, ada_w), v_w_ada.reshape(depth * Dm, ada_w),
                                   "adamw_w_ada", 256)

    small_names = ["c_ctx", "b_ada", "g_pre", "g_post", "conv_w", "conv_b", "dt_bias", "a_log", "d_skip", "g_ssd",
                   "g_v", "w_s", "b_s", "g_mlp"]
    small_w = dict(c_ctx=c_ctx, b_ada=b_ada, g_pre=g_pre, g_post=g_post, conv_w=conv_w, conv_b=conv_b,
                   dt_bias=dt_bias, a_log=a_log, d_skip=d_skip, g_ssd=g_ssd, g_v=g_v, w_s=w_s, b_s=b_s, g_mlp=g_mlp)
    small_m = dict(c_ctx=m_c_ctx, b_ada=m_b_ada, g_pre=m_g_pre, g_post=m_g_post, conv_w=m_conv_w, conv_b=m_conv_b,
                   dt_bias=m_dt_bias, a_log=m_a_log, d_skip=m_d_skip, g_ssd=m_g_ssd, g_v=m_g_v, w_s=m_w_s,
                   b_s=m_b_s, g_mlp=m_g_mlp)
    small_v = dict(c_ctx=v_c_ctx, b_ada=v_b_ada, g_pre=v_g_pre, g_post=v_g_post, conv_w=v_conv_w, conv_b=v_conv_b,
                   dt_bias=v_dt_bias, a_log=v_a_log, d_skip=v_d_skip, g_ssd=v_g_ssd, g_v=v_g_v, w_s=v_w_s,
                   b_s=v_b_s, g_mlp=v_g_mlp)
    s_shapes = [small_w[k].shape for k in small_names]
    d_s, nm_s, nv_s = _adamw(_pack([small_w[k] for k in small_names]),
                             _pack([grads_small[k].reshape(small_w[k].shape) for k in small_names]),
                             _pack([small_m[k] for k in small_names]), _pack([small_v[k] for k in small_names]),
                             "adamw_small", 512)
    d_s = dict(zip(small_names, _unpack(d_s, s_shapes)))
    nm_s = dict(zip(small_names, _unpack(nm_s, s_shapes)))
    nv_s = dict(zip(small_names, _unpack(nv_s, s_shapes)))

    big = {"w_ada": (g_w_ada, d_ada.reshape(w_ada.shape), nm_ada.reshape(w_ada.shape), nv_ada.reshape(w_ada.shape)),
           "w_in": tuple(t.reshape(w_in.shape) for t in (g_in, d_in, nm_in, nv_in)),
           "w_out": tuple(t.reshape(w_out.shape) for t in (g_out, d_out, nm_out, nv_out))}
    order = ["c_ctx", "w_ada", "b_ada", "g_pre", "g_post", "w_in", "conv_w", "conv_b", "dt_bias", "a_log", "d_skip",
             "g_ssd", "g_v", "w_s", "b_s", "g_mlp", "w_out"]

    def pick(k, idx):
        if k in big:
            return big[k][idx]
        return (grads_small[k].reshape(small_w[k].shape), d_s[k], nm_s[k], nv_s[k])[idx]

    return (loss, grad_x, *[pick(k, 0) for k in order], *[pick(k, 1) for k in order],
            *[pick(k, 2) for k in order], *[pick(k, 3) for k in order])
```

```python
import numpy as np
import jax
import jax.numpy as jnp
from jax import lax
from jax.experimental import pallas as pl
from jax.experimental.pallas import tpu as pltpu

F32 = jnp.float32
BF16 = jnp.bfloat16

D_MODEL = 2048
GRID_W = 64
SSD_W = 2048
SSD_HEADS = 32
SSD_HEAD_DIM = 64
SSD_GROUPS = 8
HEADS_PER_GROUP = 4
SSD_STATE = 128
CHUNK = 128
SSD_Q = 256
CONV_W = 5
MLP_W = 2048
MLP_GROUPS = 16
XBC_W = 4096
DT_W = 64
IN_W = 12352
Z_MAIN = IN_W - DT_W
GROUP_COLS = HEADS_PER_GROUP * SSD_HEAD_DIM
EPS = 1e-6
N_DEV = 8

ADAM_LR = 0.001
ADAM_B1 = 0.9
ADAM_B2 = 0.999
ADAM_EPS = 1e-08
ADAM_WD = 0.01
ADAM_STEP = 10

LANES = 128
NEG_BIG = -1e30

NN = (((1,), (0,)), ((), ()))
NT = (((1,), (1,)), ((), ()))
TN = (((0,), (0,)), ((), ()))
HI = lax.Precision.HIGHEST


def _dot(a, b, dims):
    return lax.dot_general(a.astype(BF16), b.astype(BF16), dims, preferred_element_type=F32)


def _dot_hi(a, b, dims=NN):
    return lax.dot_general(a, b, dims, preferred_element_type=F32, precision=HI)


def _sigmoid(x):
    return 1.0 / (1.0 + jnp.exp(-x))


def _silu(x):
    return x * _sigmoid(x)


def _softplus(x):
    return jnp.maximum(x, 0.0) + jnp.log(1.0 + jnp.exp(-jnp.abs(x)))


def _call(body, *, name, out_shape, grid=None, in_specs=None, out_specs=None, scratch=(),
          sem=None, vmem_mb=None, aliases=None):
    params = {}
    if sem is not None:
        params["dimension_semantics"] = sem
    if vmem_mb is not None:
        params["vmem_limit_bytes"] = vmem_mb << 20
    kw = {}
    if grid is not None:
        kw["grid"] = grid
    if in_specs is not None:
        kw["in_specs"] = in_specs
    if out_specs is not None:
        kw["out_specs"] = out_specs
    return pl.pallas_call(body, name=name, out_shape=out_shape, scratch_shapes=list(scratch),
                          input_output_aliases=aliases or {},
                          compiler_params=pltpu.CompilerParams(**params), **kw)


def _sds(shape, dtype):
    return jax.ShapeDtypeStruct(tuple(shape), dtype)


ANY = pl.BlockSpec(memory_space=pl.ANY)


def _my_pos():
    return lax.axis_index("x"), lax.axis_index("y"), lax.axis_index("c")


def _flip(v, bit):
    return 1 - v if bit else v


def _peer(pos, k):
    mx, my, mc = pos
    return (_flip(mx, (k >> 2) & 1), _flip(my, (k >> 1) & 1), _flip(mc, k & 1))


def _lin(pos):
    return 4 * pos[0] + 2 * pos[1] + pos[2]


def _all_gather_small(x, name):
    R, C = x.shape

    def body(x_ref, o_ref, send_sems, recv_sems):
        me = _my_pos()
        o_ref[_lin(me)] = x_ref[...]
        sends = []
        for k in range(1, N_DEV):
            peer = _peer(me, k)
            cp = pltpu.make_async_remote_copy(
                src_ref=x_ref, dst_ref=o_ref.at[_lin(me)], send_sem=send_sems.at[k - 1],
                recv_sem=recv_sems.at[k - 1], device_id=peer, device_id_type=pl.DeviceIdType.MESH)
            cp.start()
            sends.append(cp)
        for k in range(1, N_DEV):
            peer = _peer(me, k)
            pltpu.make_async_remote_copy(
                src_ref=x_ref, dst_ref=o_ref.at[_lin(peer)], send_sem=send_sems.at[k - 1],
                recv_sem=recv_sems.at[k - 1], device_id=peer,
                device_id_type=pl.DeviceIdType.MESH).wait_recv()
        for cp in sends:
            cp.wait_send()

    return _call(body, name=name, out_shape=_sds((N_DEV, R, C), x.dtype),
                 in_specs=[pl.BlockSpec(memory_space=pltpu.VMEM)],
                 out_specs=pl.BlockSpec(memory_space=pltpu.VMEM),
                 scratch=[pltpu.SemaphoreType.DMA((N_DEV - 1,)), pltpu.SemaphoreType.DMA((N_DEV - 1,))],
                 vmem_mb=40)(x)


def _all_gather_big(x, name):
    def body(x_ref, o_ref, send_sems, recv_sems, local_sem):
        mx, my, mc = _my_pos()
        me, sibling = (mx, my, mc), (mx, my, 1 - mc)
        chips = [(1 - mx, my), (mx, 1 - my), (1 - mx, 1 - my)]

        def slot(pos):
            return o_ref.at[_lin(pos)]

        def copy(k, block, to, src=None):
            return pltpu.make_async_remote_copy(
                src_ref=slot(block) if src is None else src, dst_ref=slot(block),
                send_sem=send_sems.at[k], recv_sem=recv_sems.at[k], device_id=to,
                device_id_type=pl.DeviceIdType.MESH)

        mine = pltpu.make_async_copy(x_ref, slot(me), local_sem)
        mine.start()
        first = [copy(0, me, sibling, src=x_ref)]
        first += [copy(1 + j, me, (*chip, mc), src=x_ref) for j, chip in enumerate(chips)]
        for cp in first:
            cp.start()
        passed = [copy(4 + j, (*chip, mc), sibling) for j, chip in enumerate(chips)]
        for j, chip in enumerate(chips):
            copy(1 + j, (*chip, mc), me).wait_recv()
            passed[j].start()
        copy(0, sibling, me).wait_recv()
        for j, chip in enumerate(chips):
            copy(4 + j, (*chip, 1 - mc), me).wait_recv()
        for cp in first + passed:
            cp.wait_send()
        mine.wait()

    return _call(body, name=name, out_shape=_sds((N_DEV,) + x.shape, x.dtype),
                 in_specs=[ANY], out_specs=ANY,
                 scratch=[pltpu.SemaphoreType.DMA((7,)), pltpu.SemaphoreType.DMA((7,)),
                          pltpu.SemaphoreType.DMA])(x)


EXCHANGE_SEMS = [pltpu.SemaphoreType.DMA((N_DEV - 1,)), pltpu.SemaphoreType.DMA((N_DEV - 1,)),
                 pltpu.SemaphoreType.DMA]


def _exchange_copies(kind, src_ref, dst_ref, send_sems, recv_sems, local_sem, with_arrivals=True):
    me = _my_pos()
    mine_src = src_ref if kind == "gather" else src_ref.at[_lin(me)]
    local = pltpu.make_async_copy(mine_src, dst_ref.at[_lin(me)], local_sem)
    sends, recvs = [], []
    for k in range(1, N_DEV):
        peer = _peer(me, k)
        out_src = src_ref if kind == "gather" else src_ref.at[_lin(peer)]
        sends.append(pltpu.make_async_remote_copy(
            src_ref=out_src, dst_ref=dst_ref.at[_lin(me)], send_sem=send_sems.at[k - 1],
            recv_sem=recv_sems.at[k - 1], device_id=peer, device_id_type=pl.DeviceIdType.MESH))
        if with_arrivals:
            recvs.append(pltpu.make_async_remote_copy(
                src_ref=out_src, dst_ref=dst_ref.at[_lin(peer)], send_sem=send_sems.at[k - 1],
                recv_sem=recv_sems.at[k - 1], device_id=peer, device_id_type=pl.DeviceIdType.MESH))
    return sends, recvs, local


def _exchange_start(kind, src_ref, dst_ref, send_sems, recv_sems, local_sem):
    sends, _, local = _exchange_copies(kind, src_ref, dst_ref, send_sems, recv_sems, local_sem, with_arrivals=False)
    local.start()
    for cp in sends:
        cp.start()


def _exchange_wait(kind, src_ref, dst_ref, send_sems, recv_sems, local_sem):
    sends, recvs, local = _exchange_copies(kind, src_ref, dst_ref, send_sems, recv_sems, local_sem)
    for cp in recvs:
        cp.wait_recv()
    for cp in sends:
        cp.wait_send()
    local.wait()


def _exchange_shape(kind, src):
    return _sds((N_DEV,) + src.shape if kind == "gather" else src.shape, src.dtype)


def _mm(a, b, mode, out_dtype, name, tm, tn, tk, exchange=None, add=None, side=None):
    if mode == "nn":
        (M, K), (K2, N) = a.shape, b.shape
    elif mode == "nt":
        (M, K), (N, K2) = a.shape, b.shape
    else:
        (K, M), (K2, N) = a.shape, b.shape
    assert K == K2
    tm, tn, tk = min(tm, M), min(tn, N), min(tk, K)
    assert M % tm == 0 and N % tn == 0 and K % tk == 0, (name, M, N, K, tm, tn, tk)
    ni, nj, nk = M // tm, N // tn, K // tk
    dims = {"nn": NN, "nt": NT, "tn": TN}[mode]
    assert side is None or (nk == 1 and mode != "tn")

    if mode == "tn":
        a_spec = pl.BlockSpec((tk, tm), lambda i, j, k: (k, i))
    else:
        a_spec = pl.BlockSpec((tm, tk), lambda i, j, k: (i, k))
    if mode == "nt":
        b_spec = pl.BlockSpec((tn, tk), lambda i, j, k: (j, k))
    else:
        b_spec = pl.BlockSpec((tk, tn), lambda i, j, k: (k, j))
    operands, in_specs = [a, b], [a_spec, b_spec]
    out_shape, out_specs = [_sds((M, N), out_dtype)], [pl.BlockSpec((tm, tn), lambda i, j, k: (i, j))]
    if add is not None:
        k2 = add[0].shape[1]
        operands += list(add)
        in_specs += [pl.BlockSpec((tm, k2), lambda i, j, k: (i, 0)), pl.BlockSpec((k2, tn), lambda i, j, k: (0, j))]
    if side is not None:
        n3 = side.shape[0]
        operands.append(side)
        in_specs.append(pl.BlockSpec((n3, tk), lambda i, j, k: (0, 0)))
        out_shape.append(_sds((M, n3), F32))
        out_specs.append(pl.BlockSpec((tm, n3), lambda i, j, k: (i, 0)))
    scratch = [] if nk == 1 else [pltpu.VMEM((tm, tn), F32)]
    if exchange is not None:
        kind, src = exchange
        operands.append(src)
        in_specs.append(ANY)
        out_shape.append(_exchange_shape(kind, src))
        out_specs.append(ANY)
        scratch = EXCHANGE_SEMS + scratch
    n_in, n_out = len(operands), len(out_shape)

    def body(*refs):
        ins, outs, scr = list(refs[:n_in]), list(refs[n_in:n_in + n_out]), list(refs[n_in + n_out:])
        a_ref, b_ref, o_ref = ins[0], ins[1], outs[0]
        add_refs = ins[2:4] if add is not None else None
        side_ref = ins[2 + (2 if add is not None else 0)] if side is not None else None
        i, j, k = pl.program_id(0), pl.program_id(1), pl.program_id(2)
        if exchange is not None:
            sems, scr = scr[:3], scr[3:]

            @pl.when(jnp.logical_and(jnp.logical_and(i == 0, j == 0), k == 0))
            def _():
                _exchange_start(kind, ins[-1], outs[-1], *sems)

        p = _dot(a_ref[...], b_ref[...], dims)
        if nk == 1:
            if add_refs is not None:
                p = p + _dot(add_refs[0][...], add_refs[1][...], NN)
            o_ref[...] = p.astype(out_dtype)
        else:
            acc_ref = scr[0]

            @pl.when(k == 0)
            def _():
                first = p if add_refs is None else p + _dot(add_refs[0][...], add_refs[1][...], NN)
                acc_ref[...] = first

            @pl.when(k > 0)
            def _():
                acc_ref[...] += p

            @pl.when(k == nk - 1)
            def _():
                o_ref[...] = acc_ref[...].astype(out_dtype)

        if side is not None:
            @pl.when(j == 0)
            def _():
                outs[1][...] = _dot(a_ref[...], side_ref[...], NT)

        if exchange is not None:
            @pl.when(jnp.logical_and(jnp.logical_and(i == ni - 1, j == nj - 1), k == nk - 1))
            def _():
                _exchange_wait(kind, ins[-1], outs[-1], *sems)

    plain = exchange is None and side is None
    res = _call(body, name=name, out_shape=tuple(out_shape), grid=(ni, nj, nk), in_specs=in_specs,
                out_specs=tuple(out_specs), scratch=scratch,
                sem=("parallel", "parallel", "arbitrary") if plain else ("arbitrary",) * 3, vmem_mb=48)(*operands)
    return res[0] if n_out == 1 else res


def _rms(x):
    return lax.rsqrt(jnp.mean(x * x, axis=-1, keepdims=True) + EPS)


def _prenorm_f(x, g, sc, sh):
    return (x * _rms(x) * g) * (1.0 + sc) + sh


def _pick(is_ctx, ref):
    return jnp.where(is_ctx, ref[0:1, :], ref[1:2, :])


def _prenorm_fwd(X, g, sc2, sh2, n_ctx, tl=256):
    T, Dm = X.shape
    nct = n_ctx // tl

    def body(x_ref, g_ref, sc_ref, sh_ref, o_ref):
        is_ctx = pl.program_id(0) < nct
        o_ref[...] = _prenorm_f(x_ref[...], g_ref[...], _pick(is_ctx, sc_ref),
                                _pick(is_ctx, sh_ref)).astype(BF16)

    row = pl.BlockSpec((tl, Dm), lambda i: (i, 0))
    return _call(body, name="prenorm_fwd", out_shape=_sds((T, Dm), BF16), grid=(T // tl,),
                 in_specs=[row, pl.BlockSpec((1, Dm), lambda i: (0, 0)),
                           pl.BlockSpec((2, Dm), lambda i: (0, 0)), pl.BlockSpec((2, Dm), lambda i: (0, 0))],
                 out_specs=row, sem=("parallel",), vmem_mb=40)(X, g, sc2, sh2)


def _prenorm_bwd(X, g, sc2, sh2, d1, dres, n_ctx, tl=256):
    T, Dm = X.shape
    nct = n_ctx // tl

    def body(x_ref, g_ref, sc_ref, sh_ref, d1_ref, dres_ref, dx_ref, acc_ref):
        i = pl.program_id(0)
        is_ctx = i < nct

        @pl.when(i == 0)
        def _():
            acc_ref[...] = jnp.zeros_like(acc_ref)

        _, vjp = jax.vjp(_prenorm_f, x_ref[...], g_ref[...], _pick(is_ctx, sc_ref), _pick(is_ctx, sh_ref))
        dx, dg, dsc, dsh = vjp(d1_ref[...])
        dx_ref[...] = dres_ref[...] + dx
        zero = jnp.zeros_like(dsc)
        acc_ref[0:1, :] += dg
        acc_ref[1:2, :] += jnp.where(is_ctx, dsc, zero)
        acc_ref[2:3, :] += jnp.where(is_ctx, zero, dsc)
        acc_ref[3:4, :] += jnp.where(is_ctx, dsh, zero)
        acc_ref[4:5, :] += jnp.where(is_ctx, zero, dsh)

    row = pl.BlockSpec((tl, Dm), lambda i: (i, 0))
    return _call(body, name="prenorm_bwd", out_shape=(_sds((T, Dm), F32), _sds((8, Dm), F32)), grid=(T // tl,),
                 in_specs=[row, pl.BlockSpec((1, Dm), lambda i: (0, 0)),
                           pl.BlockSpec((2, Dm), lambda i: (0, 0)), pl.BlockSpec((2, Dm), lambda i: (0, 0)),
                           row, row],
                 out_specs=(row, pl.BlockSpec((8, Dm), lambda i: (0, 0))), sem=("arbitrary",),
                 vmem_mb=48)(X, g, sc2, sh2, d1, dres)


def _post_f(o, g, gate):
    return gate * ((o * _rms(o)) * g)


def _post_fwd(X, o, g, gate2, n_ctx, tl=256):
    T, Dm = X.shape
    nct = n_ctx // tl

    def body(x_ref, o_ref, g_ref, gate_ref, y_ref):
        is_ctx = pl.program_id(0) < nct
        y_ref[...] = x_ref[...] + _post_f(o_ref[...], g_ref[...], _pick(is_ctx, gate_ref))

    row = pl.BlockSpec((tl, Dm), lambda i: (i, 0))
    return _call(body, name="post_fwd", out_shape=_sds((T, Dm), F32), grid=(T // tl,),
                 in_specs=[row, row, pl.BlockSpec((1, Dm), lambda i: (0, 0)), pl.BlockSpec((2, Dm), lambda i: (0, 0))],
                 out_specs=row, sem=("parallel",), vmem_mb=40)(X, o, g, gate2)


def _post_bwd(o, g, gate2, dX, n_ctx, tl=256):
    T, Dm = o.shape
    nct = n_ctx // tl

    def body(o_ref, g_ref, gate_ref, dx_ref, do_ref, acc_ref):
        i = pl.program_id(0)
        is_ctx = i < nct

        @pl.when(i == 0)
        def _():
            acc_ref[...] = jnp.zeros_like(acc_ref)

        _, vjp = jax.vjp(_post_f, o_ref[...], g_ref[...], _pick(is_ctx, gate_ref))
        do, dg, dgate = vjp(dx_ref[...])
        do_ref[...] = do.astype(BF16)
        zero = jnp.zeros_like(dgate)
        acc_ref[0:1, :] += dg
        acc_ref[1:2, :] += jnp.where(is_ctx, dgate, zero)
        acc_ref[2:3, :] += jnp.where(is_ctx, zero, dgate)

    row = pl.BlockSpec((tl, Dm), lambda i: (i, 0))
    return _call(body, name="post_bwd", out_shape=(_sds((T, Dm), BF16), _sds((8, Dm), F32)), grid=(T // tl,),
                 in_specs=[row, pl.BlockSpec((1, Dm), lambda i: (0, 0)), pl.BlockSpec((2, Dm), lambda i: (0, 0)), row],
                 out_specs=(row, pl.BlockSpec((8, Dm), lambda i: (0, 0))), sem=("arbitrary",),
                 vmem_mb=48)(o, g, gate2, dX)


def _loss_kernel(X, target, n_ctx, tl=256):
    T, Dm = X.shape
    nct = n_ctx // tl

    def body(x_ref, t_ref, dx_ref, acc_ref):
        i = pl.program_id(0)

        @pl.when(i == 0)
        def _():
            acc_ref[...] = jnp.zeros_like(acc_ref)

        @pl.when(i < nct)
        def _():
            dx_ref[...] = jnp.zeros_like(dx_ref)

        @pl.when(i >= nct)
        def _():
            d = x_ref[...] - t_ref[...]
            dx_ref[...] = d * (1.0 / Dm)
            acc_ref[...] += jnp.sum(d * d, axis=0, keepdims=True)

    row = pl.BlockSpec((tl, Dm), lambda i: (i, 0))
    trow = pl.BlockSpec((tl, Dm), lambda i: (jnp.maximum(i - nct, 0), 0))
    return _call(body, name="loss", out_shape=(_sds((T, Dm), F32), _sds((1, Dm), F32)), grid=(T // tl,),
                 in_specs=[row, trow], out_specs=(row, pl.BlockSpec((1, Dm), lambda i: (0, 0))),
                 sem=("arbitrary",), vmem_mb=40)(X, target)


CONV_TL = 256
CONV_CB = 1024


def _conv_taps(x, reverse):
    rows = x.shape[0]
    pos = lax.broadcasted_iota(jnp.int32, (rows, 1), 0)
    taps = []
    for k in range(CONV_W):
        off = (2 - k) if reverse else (k - 2)
        if off == 0:
            taps.append(x)
            continue
        xs = pltpu.roll(x, (-off) % rows, 0)
        valid = (pos < rows - off) if off > 0 else (pos >= -off)
        taps.append(jnp.where(valid, xs, 0.0))
    return taps


def _conv_pre(taps, w, b):
    pre = b + taps[0] * w[0:1, :]
    for k in range(1, CONV_W):
        pre = pre + taps[k] * w[k:k + 1, :]
    return pre


def _conv_subtiles(tl, cbw, row_len):
    return [(r0, c0) for c0 in range(0, cbw, LANES) for r0 in range(0, tl, row_len)]


def _conv_fwd(z, cw, cb, n_ctx):
    T = z.shape[0]
    tl, cbw = CONV_TL, CONV_CB
    assert n_ctx == tl

    def body(z_ref, w_ref, b_ref, o_ref):
        def run(row_len):
            for r0, c0 in _conv_subtiles(tl, cbw, row_len):
                rs, cs = pl.ds(r0, row_len), pl.ds(c0, LANES)
                pre = _conv_pre(_conv_taps(z_ref[rs, cs].astype(F32), False), w_ref[:, cs], b_ref[:, cs])
                o_ref[rs, cs] = _silu(pre).astype(BF16)

        @pl.when(pl.program_id(1) == 0)
        def _():
            run(n_ctx)

        @pl.when(pl.program_id(1) > 0)
        def _():
            run(GRID_W)

    blk = pl.BlockSpec((tl, cbw), lambda j, i: (i, j))
    return _call(body, name="conv_fwd", out_shape=_sds((T, XBC_W), BF16), grid=(XBC_W // cbw, T // tl),
                 in_specs=[blk, pl.BlockSpec((8, cbw), lambda j, i: (0, j)), pl.BlockSpec((1, cbw), lambda j, i: (0, j))],
                 out_specs=blk, sem=("parallel", "parallel"))(z, cw, cb)


def _conv_bwd(z, cw, cb, d0, d1, dz_in, col0, n_ctx):
    T = z.shape[0]
    tl, cbw = CONV_TL, CONV_CB

    def body(z_ref, w_ref, b_ref, d0_ref, d1_ref, dzin_ref, dz_ref, dw_ref, db_ref):
        i = pl.program_id(1)

        @pl.when(i == 0)
        def _():
            dw_ref[...] = jnp.zeros_like(dw_ref)
            db_ref[...] = jnp.zeros_like(db_ref)

        def run(row_len):
            for c0 in range(0, cbw, LANES):
                cs = pl.ds(c0, LANES)
                w = w_ref[:, cs]
                dw = [jnp.zeros((1, LANES), F32) for _ in range(CONV_W)]
                db = jnp.zeros((1, LANES), F32)
                for r0 in range(0, tl, row_len):
                    rs = pl.ds(r0, row_len)
                    taps = _conv_taps(z_ref[rs, cs].astype(F32), False)
                    pre = _conv_pre(taps, w, b_ref[:, cs])
                    s = _sigmoid(pre)
                    dpre = (d0_ref[rs, cs].astype(F32) + d1_ref[rs, cs].astype(F32)) * (s + pre * s * (1.0 - s))
                    dz_ref[rs, cs] = _conv_pre(_conv_taps(dpre, True), w, 0.0).astype(BF16)
                    for k in range(CONV_W):
                        dw[k] = dw[k] + jnp.sum(dpre * taps[k], axis=0, keepdims=True)
                    db = db + jnp.sum(dpre, axis=0, keepdims=True)
                for k in range(CONV_W):
                    dw_ref[k:k + 1, cs] += dw[k]
                db_ref[0:1, cs] += db

        @pl.when(i == 0)
        def _():
            run(n_ctx)

        @pl.when(i > 0)
        def _():
            run(GRID_W)

    width = d0.shape[1]
    jo = col0 // cbw
    blk = pl.BlockSpec((tl, cbw), lambda j, i: (i, jo + j))
    dblk = pl.BlockSpec((tl, cbw), lambda j, i: (i, j))
    par = pl.BlockSpec((8, cbw), lambda j, i: (0, jo + j))
    opar = pl.BlockSpec((8, cbw), lambda j, i: (0, j))
    return _call(body, name=f"conv_bwd_{col0}",
                 out_shape=(_sds(dz_in.shape, BF16), _sds((8, width), F32), _sds((8, width), F32)),
                 grid=(width // cbw, T // tl),
                 in_specs=[blk, par, pl.BlockSpec((1, cbw), lambda j, i: (0, jo + j)), dblk, dblk, ANY],
                 out_specs=(blk, opar, opar), sem=("parallel", "arbitrary"), aliases={5: 0})(z, cw, cb, d0, d1, dz_in)


MIX_TL = 128


def _gate_f(yf, yb, zs, g):
    t = (yf + yb) * _silu(zs)
    return t * _rms(t) * g


def _gate_fwd(yf, yb, z, g):
    T = yf.shape[0]
    tl = MIX_TL

    def body(yf_ref, yb_ref, zs_ref, g_ref, o_ref):
        o_ref[...] = _gate_f(yf_ref[...].astype(F32), yb_ref[...].astype(F32), zs_ref[...].astype(F32),
                             g_ref[...]).astype(BF16)

    row = pl.BlockSpec((tl, SSD_W), lambda i: (i, 0))
    return _call(body, name="gate_fwd", out_shape=_sds((T, SSD_W + MLP_W), BF16), grid=(T // tl,),
                 in_specs=[row, row, pl.BlockSpec((tl, SSD_W), lambda i: (i, XBC_W // SSD_W)),
                           pl.BlockSpec((1, SSD_W), lambda i: (0, 0))],
                 out_specs=row, sem=("parallel",))(yf, yb, z, g)


def _gate_bwd(yf, yb, z, g, dycat, dz_in):
    T = yf.shape[0]
    tl = MIX_TL

    def body(yf_ref, yb_ref, zs_ref, g_ref, d_ref, dzin_ref, dy_ref, dz_ref, acc_ref):
        @pl.when(pl.program_id(0) == 0)
        def _():
            acc_ref[...] = jnp.zeros_like(acc_ref)

        _, vjp = jax.vjp(_gate_f, yf_ref[...].astype(F32), yb_ref[...].astype(F32),
                         zs_ref[...].astype(F32), g_ref[...])
        dyf, _, dzs, dg = vjp(d_ref[...].astype(F32))
        dy_ref[...] = dyf.astype(BF16)
        dz_ref[...] = dzs.astype(BF16)
        acc_ref[0:1, :] += dg

    row = pl.BlockSpec((tl, SSD_W), lambda i: (i, 0))
    zs_spec = pl.BlockSpec((tl, SSD_W), lambda i: (i, XBC_W // SSD_W))
    return _call(body, name="gate_bwd",
                 out_shape=(_sds((T, SSD_W), BF16), _sds(dz_in.shape, BF16), _sds((8, SSD_W), F32)),
                 grid=(T // tl,),
                 in_specs=[row, row, zs_spec, pl.BlockSpec((1, SSD_W), lambda i: (0, 0)), row, ANY],
                 out_specs=(row, zs_spec, pl.BlockSpec((8, SSD_W), lambda i: (0, 0))),
                 sem=("arbitrary",), aliases={5: 1}, vmem_mb=40)(yf, yb, z, g, dycat, dz_in)


def _vnorm_f(v, gv):
    return v * _rms(v) * gv


def _mlp_out_f(u, sg, zm, gm):
    t = u * sg * _silu(zm)
    return t * _rms(t) * gm


U_BLK = (XBC_W + SSD_W) // MLP_W


def _mlp_mix(ws_ref, bst_ref, vn_s, sg_s):
    for gi in range(MLP_GROUPS):
        cols = pl.ds(gi * LANES, LANES)
        sg_s[:, cols] = _dot(ws_ref[gi], vn_s[:, cols], NN) + bst_ref[:, gi:gi + 1]


def _mlp_fwd(z, gv, ws, bst, gm, ycat_in):
    T = z.shape[0]
    tl = CHUNK

    def body(u_ref, v_ref, zm_ref, gv_ref, ws_ref, bst_ref, gm_ref, yin_ref, o_ref, vn_s, sg_s):
        vn_s[...] = _vnorm_f(v_ref[...].astype(F32), gv_ref[...]).astype(BF16)
        _mlp_mix(ws_ref, bst_ref, vn_s, sg_s)
        o_ref[...] = _mlp_out_f(u_ref[...].astype(F32), sg_s[...], zm_ref[...].astype(F32),
                                gm_ref[...]).astype(BF16)

    def zblk(b):
        return pl.BlockSpec((tl, MLP_W), lambda i: (i, b))

    vec = pl.BlockSpec((1, MLP_W), lambda i: (0, 0))
    return _call(body, name="mlp_fwd", out_shape=_sds(ycat_in.shape, BF16), grid=(T // tl,),
                 in_specs=[zblk(U_BLK), zblk(U_BLK + 1), zblk(U_BLK + 2), vec,
                           pl.BlockSpec((MLP_GROUPS, CHUNK, CHUNK), lambda i: (0, 0, 0)),
                           pl.BlockSpec((CHUNK, LANES), lambda i: (0, 0)), vec, ANY],
                 out_specs=pl.BlockSpec((tl, MLP_W), lambda i: (i, 1)),
                 scratch=[pltpu.VMEM((tl, MLP_W), BF16), pltpu.VMEM((tl, MLP_W), F32)],
                 sem=("parallel",), aliases={7: 0})(z, z, z, gv, ws, bst, gm, ycat_in)


def _mlp_bwd(z, gv, ws, bst, gm, dycat, dz_in):
    T = z.shape[0]
    tl = CHUNK

    def body(u_ref, v_ref, zm_ref, gv_ref, ws_ref, bst_ref, gm_ref, d_ref, dzin_ref,
             dz_ref, acc_ref, dws_ref, dbst_ref, vn_s, sg_s, dvn_s):
        @pl.when(pl.program_id(0) == 0)
        def _():
            acc_ref[...] = jnp.zeros_like(acc_ref)
            dws_ref[...] = jnp.zeros_like(dws_ref)
            dbst_ref[...] = jnp.zeros_like(dbst_ref)

        v = v_ref[...].astype(F32)
        vn, vjp_v = jax.vjp(_vnorm_f, v, gv_ref[...])
        vn_s[...] = vn.astype(BF16)
        _mlp_mix(ws_ref, bst_ref, vn_s, sg_s)
        _, vjp_o = jax.vjp(_mlp_out_f, u_ref[...].astype(F32), sg_s[...], zm_ref[...].astype(F32), gm_ref[...])
        du, dsg, dzm, dgm = vjp_o(d_ref[...].astype(F32))
        sg_s[...] = dsg
        for gi in range(MLP_GROUPS):
            cols = pl.ds(gi * LANES, LANES)
            dsg_g = sg_s[:, cols]
            dvn_s[:, cols] = _dot(ws_ref[gi], dsg_g, TN)
            dws_ref[gi] += _dot(dsg_g, vn_s[:, cols], NT)
            dbst_ref[:, gi:gi + 1] += jnp.sum(dsg_g, axis=1, keepdims=True)
        dv, dgv = vjp_v(dvn_s[...])
        dz_ref[:, 0:MLP_W] = du.astype(BF16)
        dz_ref[:, MLP_W:2 * MLP_W] = dv.astype(BF16)
        dz_ref[:, 2 * MLP_W:3 * MLP_W] = dzm.astype(BF16)
        acc_ref[0:1, :] += dgv
        acc_ref[1:2, :] += dgm

    def zblk(b):
        return pl.BlockSpec((tl, MLP_W), lambda i: (i, b))

    vec = pl.BlockSpec((1, MLP_W), lambda i: (0, 0))
    ws_spec = pl.BlockSpec((MLP_GROUPS, CHUNK, CHUNK), lambda i: (0, 0, 0))
    bst_spec = pl.BlockSpec((CHUNK, LANES), lambda i: (0, 0))
    return _call(body, name="mlp_bwd",
                 out_shape=(_sds(dz_in.shape, BF16), _sds((8, MLP_W), F32),
                            _sds((MLP_GROUPS, CHUNK, CHUNK), F32), _sds((CHUNK, LANES), F32)),
                 grid=(T // tl,),
                 in_specs=[zblk(U_BLK), zblk(U_BLK + 1), zblk(U_BLK + 2), vec, ws_spec, bst_spec, vec,
                           pl.BlockSpec((tl, MLP_W), lambda i: (i, 1)), ANY],
                 out_specs=(pl.BlockSpec((tl, 3 * MLP_W), lambda i: (i, 1)),
                            pl.BlockSpec((8, MLP_W), lambda i: (0, 0)), ws_spec, bst_spec),
                 scratch=[pltpu.VMEM((tl, MLP_W), BF16), pltpu.VMEM((tl, MLP_W), F32), pltpu.VMEM((tl, MLP_W), F32)],
                 sem=("arbitrary",), aliases={8: 0}, vmem_mb=48)(z, z, z, gv, ws, bst, gm, dycat, dz_in)


def _ssd_constants(direction):
    q = SSD_Q
    tri = np.tril(np.ones((q, q), np.float32))
    if direction == 1:
        tri = tri.T
    ex = np.zeros((SSD_GROUPS, LANES, GROUP_COLS), np.float32)
    for g in range(SSD_GROUPS):
        for r in range(HEADS_PER_GROUP):
            ex[g, direction * SSD_HEADS + HEADS_PER_GROUP * g + r, r * SSD_HEAD_DIM:(r + 1) * SSD_HEAD_DIM] = 1.0
    hm = np.zeros((8, GROUP_COLS), np.float32)
    for r in range(HEADS_PER_GROUP):
        hm[r, r * SSD_HEAD_DIM:(r + 1) * SSD_HEAD_DIM] = 1.0
    return (jnp.asarray(tri), jnp.asarray(np.concatenate([ex, ex], axis=1), dtype=BF16),
            jnp.asarray(np.ascontiguousarray(ex.transpose(0, 2, 1)), dtype=BF16), jnp.asarray(hm, dtype=BF16))


def _dot2(v, m, dims=NN):
    hi = v.astype(BF16)
    lo = (v - hi.astype(F32)).astype(BF16)
    return (lax.dot_general(hi, m, dims, preferred_element_type=F32)
            + lax.dot_general(lo, m, dims, preferred_element_type=F32))


def _ssd_prep(raw, prm):
    T = raw.shape[0]
    nc, Q = T // SSD_Q, SSD_Q
    tril = jnp.asarray(np.tril(np.ones((Q, Q), np.float32)))
    triu = jnp.asarray(np.triu(np.ones((Q, Q), np.float32)))

    def body(raw_ref, prm_ref, tril_ref, triu_ref, dt_ref, s_ref, pk_ref, st_ref, etot_ref):
        dt = _softplus(raw_ref[...] + prm_ref[0:1, :])
        a = dt * (-jnp.exp(prm_ref[1:2, :]))
        lane = lax.broadcasted_iota(jnp.int32, (1, LANES), 1)
        s = jnp.where(lane < SSD_HEADS, _dot_hi(tril_ref[...], a), _dot_hi(triu_ref[...], a))
        tot = jnp.sum(a, axis=0, keepdims=True)
        dt_ref[...] = dt
        s_ref[...] = s
        for n, v in enumerate((dt, jnp.exp(s), jnp.exp(tot - s))):
            hi = v.astype(BF16)
            pk_ref[:, 2 * n * LANES:(2 * n + 1) * LANES] = hi
            pk_ref[:, (2 * n + 1) * LANES:(2 * n + 2) * LANES] = (v - hi.astype(F32)).astype(BF16)
        st_ref[0] = s.T
        etot_ref[0] = _bcast8(jnp.exp(tot))

    row = pl.BlockSpec((Q, LANES), lambda c: (c, 0))
    cst = lambda shape: pl.BlockSpec(shape, lambda c: (0, 0))
    tl = _sds((T, LANES), F32)
    return _call(body, name="ssd_prep",
                 out_shape=(tl, tl, _sds((T, 6 * LANES), BF16), _sds((nc, LANES, Q), F32), _sds((nc, 8, LANES), F32)),
                 grid=(nc,), in_specs=[row, cst((8, LANES)), cst((Q, Q)), cst((Q, Q))],
                 out_specs=(row, row, pl.BlockSpec((Q, 6 * LANES), lambda c: (c, 0)),
                            pl.BlockSpec((1, LANES, Q), lambda c: (c, 0, 0)),
                            pl.BlockSpec((1, 8, LANES), lambda c: (c, 0, 0))),
                 sem=("parallel",))(raw, prm, tril, triu)


def _ssd_post(raw, prm, dt, ds0, ds1, ddt0, ddt1, dtot0, dtot1, daz0, daz1):
    T = raw.shape[0]
    nc, Q = T // SSD_Q, SSD_Q
    tril = jnp.asarray(np.tril(np.ones((Q, Q), np.float32)))
    triu = jnp.asarray(np.triu(np.ones((Q, Q), np.float32)))

    def body(raw_ref, prm_ref, dt_ref, ds0_ref, ds1_ref, ddt0_ref, ddt1_ref, dtot0_ref, dtot1_ref, daz0_ref, daz1_ref,
             tril_ref, triu_ref, draw_ref, dprm_ref):
        @pl.when(pl.program_id(0) == 0)
        def _():
            dprm_ref[...] = jnp.zeros_like(dprm_ref)

        pre = raw_ref[...] + prm_ref[0:1, :]
        A = -jnp.exp(prm_ref[1:2, :])
        ds = ds0_ref[...] + ds1_ref[...]
        lane = lax.broadcasted_iota(jnp.int32, (1, LANES), 1)
        da = jnp.where(lane < SSD_HEADS, _dot_hi(triu_ref[...], ds), _dot_hi(tril_ref[...], ds))
        da = da + dtot0_ref[0, 0:1, :] + dtot1_ref[0, 0:1, :]
        daz_rows = jnp.concatenate([daz0_ref[0], daz1_ref[0], jnp.zeros((LANES - 2 * SSD_HEADS, Q), F32)], axis=0)
        da = da + daz_rows.T
        draw = (da * A + ddt0_ref[...] + ddt1_ref[...]) * _sigmoid(pre)
        draw_ref[...] = draw
        dprm_ref[0:1, :] += jnp.sum(draw, axis=0, keepdims=True)
        dprm_ref[1:2, :] += jnp.sum(da * dt_ref[...], axis=0, keepdims=True) * A

    row = pl.BlockSpec((Q, LANES), lambda c: (c, 0))
    tot = pl.BlockSpec((1, 8, LANES), lambda c: (c, 0, 0))
    daz = pl.BlockSpec((1, SSD_HEADS, Q), lambda c: (c, 0, 0))
    cst = lambda shape: pl.BlockSpec(shape, lambda c: (0, 0))
    return _call(body, name="ssd_post", out_shape=(_sds((T, LANES), F32), _sds((8, LANES), F32)), grid=(nc,),
                 in_specs=[row, cst((8, LANES)), row, row, row, row, row, tot, tot, daz, daz, cst((Q, Q)), cst((Q, Q))],
                 out_specs=(row, cst((8, LANES))), sem=("arbitrary",))(raw, prm, dt, ds0, ds1, ddt0, ddt1, dtot0, dtot1,
                                                                       daz0, daz1, tril, triu)


def _chunk_order(direction, nc, nctx):
    if direction == 0:
        return lambda c: c
    return lambda c: jnp.where(c < nctx, nctx - 1 - c, nc - 1 + nctx - c)


def _bcast8(row):
    return jnp.broadcast_to(row, (8, row.shape[1]))


GPS = 8


def _cols(ref, k, width):
    return ref[:, k * width:(k + 1) * width]


def _ssd_common(k, x_ref, b_ref, c_ref, s_ref, pk_ref, st_ref, etot_ref, ex_ref, direction):
    g = pl.program_id(1) * GPS + k
    base = direction * SSD_HEADS + HEADS_PER_GROUP * g
    x = _cols(x_ref, k, GROUP_COLS).astype(F32)
    ex = ex_ref[k]

    def spread(n):
        return lax.dot_general(_cols(pk_ref, n, 2 * LANES), ex, NN, preferred_element_type=F32)

    dt_x, E_x, D_x = spread(0), spread(1), spread(2)
    lane = lax.broadcasted_iota(jnp.int32, (1, LANES), 1)
    etot = etot_ref[0, 0:1, :]
    rowi = lax.broadcasted_iota(jnp.int32, (GROUP_COLS, 1), 0)
    e = [jnp.sum(jnp.where(lane == base + r, etot, 0.0), axis=1, keepdims=True) for r in range(HEADS_PER_GROUP)]
    Etot_c = jnp.where(rowi < SSD_HEAD_DIM, e[0],
                       jnp.where(rowi < 2 * SSD_HEAD_DIM, e[1], jnp.where(rowi < 3 * SSD_HEAD_DIM, e[2], e[3])))
    rows = [st_ref[0, HEADS_PER_GROUP * k + r:HEADS_PER_GROUP * k + r + 1, :] for r in range(HEADS_PER_GROUP)]
    return dict(g=g, base=base, x=x, s=s_ref[...], rows=rows, dt_x=dt_x, lane=lane, X=x * dt_x, E_x=E_x, D_x=D_x,
                Etot_c=Etot_c, Bm=_cols(b_ref, k, SSD_STATE), Cm=_cols(c_ref, k, SSD_STATE))


def _head_decay(q, r, mask):
    sel = q["lane"] == q["base"] + r
    col = jnp.sum(jnp.where(sel, q["s"], 0.0), axis=1, keepdims=True)
    return jnp.exp(jnp.where(mask, col - q["rows"][r], NEG_BIG))


def _ssd_in_specs(order, direction):
    Q, GC, N, G = SSD_Q, GROUP_COLS, SSD_STATE, SSD_GROUPS
    row = pl.BlockSpec((Q, LANES), lambda c, g: (order(c), 0))
    hrows = HEADS_PER_GROUP * GPS
    return [
        pl.BlockSpec((Q, GPS * GC), lambda c, g: (order(c), g)),
        pl.BlockSpec((Q, GPS * N), lambda c, g: (order(c), SSD_W // (GPS * N) + g)),
        pl.BlockSpec((Q, GPS * N), lambda c, g: (order(c), (SSD_W + G * N) // (GPS * N) + g)),
        row, pl.BlockSpec((Q, 6 * LANES), lambda c, g: (order(c), 0)),
        pl.BlockSpec((1, hrows, Q), lambda c, g: (order(c), direction * (SSD_HEADS // hrows) + g, 0)),
        pl.BlockSpec((1, 8, LANES), lambda c, g: (order(c), 0, 0)),
    ]


def _ssd_fwd(xbc, prep, dskx, direction, n_ctx):
    T = xbc.shape[0]
    nc, nctx = T // SSD_Q, n_ctx // SSD_Q
    order = _chunk_order(direction, nc, nctx)
    tri, ex, ext, hm = _ssd_constants(direction)
    Q, GC, N, G = SSD_Q, GROUP_COLS, SSD_STATE, SSD_GROUPS

    def body(x_ref, b_ref, c_ref, s_ref, pk_ref, st_ref, etot_ref, dsk_ref, tri_ref, ex_ref, hm_ref,
             y_ref, sst_ref, S):
        c = pl.program_id(0)
        mask = tri_ref[...] > 0.5
        for k in range(GPS):
            q = _ssd_common(k, x_ref, b_ref, c_ref, s_ref, pk_ref, st_ref, etot_ref, ex_ref, direction)
            g = q["g"]

            @pl.when(c == 0)
            def _():
                S[g] = jnp.zeros((GC, N), F32)

            S0 = S[g]
            sst_ref[0, k] = S0
            X, Bm, Cm = q["X"], q["Bm"], q["Cm"]
            Xb = X.astype(BF16)
            y = q["E_x"] * _dot(Cm, S0, NT)
            Gm = _dot(Cm, Bm, NT)
            for r in range(HEADS_PER_GROUP):
                y = y + _dot(Gm * _head_decay(q, r, mask), Xb * hm_ref[r:r + 1, :], NN)
            if direction == 0:
                y = y + _cols(dsk_ref, k, GC)[0:1, :] * q["x"]
            y_ref[:, k * GC:(k + 1) * GC] = y.astype(BF16)
            S[g] = q["Etot_c"] * S0 + _dot(X * q["D_x"], Bm, TN)

    cst2 = lambda shape: pl.BlockSpec(shape, lambda c, g: (0, 0))
    in_specs = _ssd_in_specs(order, direction) + [
        pl.BlockSpec((8, GPS * GC), lambda c, g: (0, g)), cst2((Q, Q)),
        pl.BlockSpec((GPS, 2 * LANES, GC), lambda c, g: (g, 0, 0)), cst2((8, GC))]
    out_specs = (pl.BlockSpec((Q, GPS * GC), lambda c, g: (order(c), g)),
                 pl.BlockSpec((1, GPS, GC, N), lambda c, g: (order(c), g, 0, 0)))
    return _call(body, name=f"ssd_fwd_{direction}",
                 out_shape=(_sds((T, SSD_W), BF16), _sds((nc, G, GC, N), F32)),
                 grid=(nc, G // GPS), in_specs=in_specs, out_specs=out_specs,
                 scratch=[pltpu.VMEM((G, GC, N), F32)], sem=("arbitrary", "arbitrary"),
                 )(xbc, xbc, xbc, *prep[1:], dskx, tri, ex, hm)


def _ssd_bwd(xbc, prep, dskx, sst, dy, direction, n_ctx):
    T = xbc.shape[0]
    nc, nctx = T // SSD_Q, n_ctx // SSD_Q
    fwd_order = _chunk_order(direction, nc, nctx)
    order = lambda c: fwd_order(nc - 1 - c)
    tri, ex, ext, hm = _ssd_constants(direction)
    ntri = (1.0 - tri).astype(BF16)
    Q, GC, N, G = SSD_Q, GROUP_COLS, SSD_STATE, SSD_GROUPS

    def body(x_ref, b_ref, c_ref, s_ref, pk_ref, st_ref, etot_ref, dsk_ref, sst_ref, dy_ref, tri_ref,
             ex_ref, ext_ref, hm_ref, ntri_ref, dx_ref, db_ref, dc_ref, ds_ref, ddt_ref, dtot_ref, daz_ref, dskg_ref,
             dS):
        c, gi = pl.program_id(0), pl.program_id(1)
        mask = tri_ref[...] > 0.5

        @pl.when(jnp.logical_and(c == 0, gi == 0))
        def _():
            dskg_ref[...] = jnp.zeros_like(dskg_ref)

        @pl.when(gi == 0)
        def _():
            ds_ref[...] = jnp.zeros_like(ds_ref)
            ddt_ref[...] = jnp.zeros_like(ddt_ref)
            dtot_ref[...] = jnp.zeros_like(dtot_ref)

        row8 = lax.broadcasted_iota(jnp.int32, (HEADS_PER_GROUP * GPS, 1), 0)
        daz = jnp.zeros((HEADS_PER_GROUP * GPS, Q), F32)
        for k in range(GPS):
            q = _ssd_common(k, x_ref, b_ref, c_ref, s_ref, pk_ref, st_ref, etot_ref, ex_ref, direction)
            g = q["g"]

            @pl.when(c == 0)
            def _():
                dS[g] = jnp.zeros((GC, N), F32)

            dS1 = dS[g]
            S0 = sst_ref[0, k]
            x, X, Bm, Cm, ext = q["x"], q["X"], q["Bm"], q["Cm"], ext_ref[k]
            E_x, D_x, Etot_c = q["E_x"], q["D_x"], q["Etot_c"]
            dYb = _cols(dy_ref, k, GC)
            dY = dYb.astype(F32)
            Xb = X.astype(BF16)

            CS = _dot(Cm, S0, NT)
            dCS = dY * E_x
            dC = _dot(dCS, S0, NN)
            dS0 = Etot_c * dS1 + _dot(dCS, Cm, TN)
            ds_x = dY * (E_x * CS)
            dtot_c = jnp.sum(dS1 * S0, axis=1, keepdims=True) * Etot_c
            dtot = jnp.sum(dtot_c * ext.astype(F32), axis=0, keepdims=True)
            XD = X * D_x
            dXD = _dot(Bm, dS1, NT)
            dB = _dot(XD, dS1, NN)
            dX = dXD * D_x
            t = dXD * XD
            ds_x = ds_x - t
            dtot_x = jnp.sum(t, axis=0, keepdims=True)
            Gm = _dot(Cm, Bm, NT)
            dG = jnp.zeros((Q, Q), F32)
            for r in range(HEADS_PER_GROUP):
                hmr = hm_ref[r:r + 1, :]
                Lm = _head_decay(q, r, mask)
                W = Gm * Lm
                dW = _dot(dYb * hmr, Xb, NT)
                dX = dX + _dot(W, dYb, TN) * hmr.astype(F32)
                dG = dG + dW * Lm
                P = _dot(dW * W, ntri_ref[...], NN)
                da_row = jnp.sum(jnp.where(mask, P, 0.0), axis=0, keepdims=True)
                daz = jnp.where(row8 == HEADS_PER_GROUP * k + r, da_row, daz)
            dC = dC + _dot(dG, Bm, NN)
            dB = dB + _dot(dG, Cm, TN)
            ds_ref[...] += _dot2(ds_x, ext)
            dtot_ref[0, 0:1, :] += dtot + _dot2(_bcast8(dtot_x), ext)[0:1, :]
            ddt_ref[...] += _dot(dX * x, ext, NN)
            dx = dX * q["dt_x"]
            if direction == 0:
                dx = dx + dY * _cols(dsk_ref, k, GC)[0:1, :]
                dskg_ref[0:1, :] += jnp.sum(_dot(dY * x, ext, NN), axis=0, keepdims=True)
            dx_ref[:, k * GC:(k + 1) * GC] = dx.astype(BF16)
            db_ref[:, k * N:(k + 1) * N] = dB.astype(BF16)
            dc_ref[:, k * N:(k + 1) * N] = dC.astype(BF16)
            dS[g] = dS0
        daz_ref[0] = daz

    cst2 = lambda shape: pl.BlockSpec(shape, lambda c, g: (0, 0))
    in_specs = _ssd_in_specs(order, direction) + [
        pl.BlockSpec((8, GPS * GC), lambda c, g: (0, g)),
        pl.BlockSpec((1, GPS, GC, N), lambda c, g: (order(c), g, 0, 0)),
        pl.BlockSpec((Q, GPS * GC), lambda c, g: (order(c), g)),
        cst2((Q, Q)),
        pl.BlockSpec((GPS, 2 * LANES, GC), lambda c, g: (g, 0, 0)),
        pl.BlockSpec((GPS, GC, LANES), lambda c, g: (g, 0, 0)),
        cst2((8, GC)), cst2((Q, Q)),
    ]
    row = pl.BlockSpec((Q, LANES), lambda c, g: (order(c), 0))
    hrows = HEADS_PER_GROUP * GPS
    out_specs = (pl.BlockSpec((Q, GPS * GC), lambda c, g: (order(c), g)),
                 pl.BlockSpec((Q, GPS * N), lambda c, g: (order(c), g)),
                 pl.BlockSpec((Q, GPS * N), lambda c, g: (order(c), g)),
                 row, row, pl.BlockSpec((1, 8, LANES), lambda c, g: (order(c), 0, 0)),
                 pl.BlockSpec((1, hrows, Q), lambda c, g: (order(c), g, 0)),
                 cst2((8, LANES)))
    return _call(body, name=f"ssd_bwd_{direction}",
                 out_shape=(_sds((T, SSD_W), BF16), _sds((T, G * N), BF16), _sds((T, G * N), BF16),
                            _sds((T, LANES), F32), _sds((T, LANES), F32), _sds((nc, 8, LANES), F32),
                            _sds((nc, SSD_HEADS, Q), F32), _sds((8, LANES), F32)),
                 grid=(nc, G // GPS), in_specs=in_specs, out_specs=out_specs,
                 scratch=[pltpu.VMEM((G, GC, N), F32)], sem=("arbitrary", "arbitrary"),
                 )(xbc, xbc, xbc, *prep[1:], dskx, sst, dy, tri, ex, ext, hm, ntri)


ADA_ROWS = 16


def _ada_fwd(c16, w_ada, b_loc):
    depth, Dm, W = w_ada.shape

    def body(c_ref, w_ref, b_ref, o_ref):
        o_ref[0] = _dot_hi(_silu(c_ref[...]), w_ref[0]) + b_ref[0]

    return _call(body, name="ada_fwd", out_shape=_sds((depth, ADA_ROWS, W), F32), grid=(depth,),
                 in_specs=[pl.BlockSpec((ADA_ROWS, Dm), lambda l: (0, 0)),
                           pl.BlockSpec((1, Dm, W), lambda l: (l, 0, 0)),
                           pl.BlockSpec((1, 1, W), lambda l: (l, 0, 0))],
                 out_specs=pl.BlockSpec((1, ADA_ROWS, W), lambda l: (l, 0, 0)),
                 sem=("parallel",), vmem_mb=40)(c16, w_ada, b_loc)


def _ada_bwd(c16, w_ada, dmod):
    depth, Dm, W = w_ada.shape

    def body(c_ref, w_ref, d_ref, gw_ref, dc_ref):
        l = pl.program_id(0)

        @pl.when(l == 0)
        def _():
            dc_ref[...] = jnp.zeros_like(dc_ref)

        cc = c_ref[...]
        sg = _sigmoid(cc)
        gw_ref[0] = _dot_hi(cc * sg, d_ref[0], TN)
        dsc = _dot_hi(d_ref[0], w_ref[0], NT)
        dc_ref[...] += dsc[8:16, :] * (sg + cc * sg * (1.0 - sg))[8:16, :]

    return _call(body, name="ada_bwd", out_shape=(_sds((depth, Dm, W), F32), _sds((8, Dm), F32)), grid=(depth,),
                 in_specs=[pl.BlockSpec((ADA_ROWS, Dm), lambda l: (0, 0)),
                           pl.BlockSpec((1, Dm, W), lambda l: (l, 0, 0)),
                           pl.BlockSpec((1, ADA_ROWS, W), lambda l: (l, 0, 0))],
                 out_specs=(pl.BlockSpec((1, Dm, W), lambda l: (l, 0, 0)), pl.BlockSpec((8, Dm), lambda l: (0, 0))),
                 sem=("arbitrary",), vmem_mb=48)(c16, w_ada, dmod)


def _sum_slabs(g, name, tr=512):
    _, R, C = g.shape
    tr = min(tr, R)

    def body(g_ref, o_ref):
        acc = g_ref[0]
        for k in range(1, N_DEV):
            acc = acc + g_ref[k]
        o_ref[...] = acc

    return _call(body, name=name, out_shape=_sds((R, C), F32), grid=(R // tr,),
                 in_specs=[pl.BlockSpec((N_DEV, tr, C), lambda i: (0, i, 0))],
                 out_specs=pl.BlockSpec((tr, C), lambda i: (i, 0)), sem=("parallel",), vmem_mb=40)(g)


def _adamw_math(w, g, m, v):
    m = ADAM_B1 * m + (1.0 - ADAM_B1) * g
    v = ADAM_B2 * v + (1.0 - ADAM_B2) * (g * g)
    m_hat = m / (1.0 - ADAM_B1 ** ADAM_STEP)
    v_hat = v / (1.0 - ADAM_B2 ** ADAM_STEP)
    delta = -ADAM_LR * (m_hat / (jnp.sqrt(v_hat) + ADAM_EPS) + ADAM_WD * w)
    return delta, m, v


def _adamw(w, g, m, v, name, tr):
    R, C = w.shape
    tr = min(tr, R)
    assert R % tr == 0

    def body(w_ref, g_ref, m_ref, v_ref, d_ref, nm_ref, nv_ref):
        d, nm, nv = _adamw_math(w_ref[...], g_ref[...], m_ref[...], v_ref[...])
        d_ref[...] = d
        nm_ref[...] = nm
        nv_ref[...] = nv

    blk = pl.BlockSpec((tr, C), lambda i: (i, 0))
    out = _sds((R, C), F32)
    return _call(body, name=name, out_shape=(out, out, out), grid=(R // tr,), in_specs=[blk] * 4,
                 out_specs=(blk, blk, blk), sem=("parallel",), vmem_mb=40)(w, g, m, v)


def _adamw_slabs(w, slabs, m, v, name, tr):
    depth, R, C = w.shape
    assert R % tr == 0 and len(slabs) == depth

    def body(w_ref, *rest):
        slab_refs = rest[:depth]
        m_ref, v_ref, g_ref, d_ref, nm_ref, nv_ref = rest[depth:]
        layer = pl.program_id(0)
        for ll in range(depth):
            @pl.when(layer == ll)
            def _(s_ref=slab_refs[ll]):
                g = s_ref[0].astype(F32)
                for k in range(1, N_DEV):
                    g = g + s_ref[k].astype(F32)
                d, nm, nv = _adamw_math(w_ref[0], g, m_ref[0], v_ref[0])
                g_ref[0] = g
                d_ref[0] = d
                nm_ref[0] = nm
                nv_ref[0] = nv

    blk = pl.BlockSpec((1, tr, C), lambda l, i: (l, i, 0))

    def slab_spec(ll):
        return pl.BlockSpec((N_DEV, tr, C), lambda l, i: (0, jnp.where(l == ll, i, 0), 0))

    out = _sds((depth, R, C), F32)
    return _call(body, name=name, out_shape=(out, out, out, out), grid=(depth, R // tr),
                 in_specs=[blk] + [slab_spec(ll) for ll in range(depth)] + [blk, blk],
                 out_specs=(blk, blk, blk, blk), sem=("parallel", "parallel"), vmem_mb=56)(w, *slabs, m, v)


PACK_QUANTUM = 512 * LANES


def _pack(arrays):
    flat = jnp.concatenate([a.reshape(-1).astype(F32) for a in arrays])
    pad = (-flat.shape[0]) % PACK_QUANTUM
    return jnp.pad(flat, (0, pad)).reshape(-1, LANES)


def _unpack(bundle, shapes):
    flat = bundle.reshape(-1)
    out, off = [], 0
    for shp in shapes:
        n = int(np.prod(shp))
        out.append(flat[off:off + n].reshape(shp))
        off += n
    return out


def _row(v):
    return v.reshape(1, -1)


def _pad_rows(a, rows):
    return jnp.pad(a, ((0, rows - a.shape[0]), (0, 0)))


def kernel(x, c, ctx, c_ctx, w_ada, b_ada, g_pre, g_post, w_in, conv_w, conv_b, dt_bias, a_log, d_skip, g_ssd, g_v, w_s, b_s, g_mlp, w_out, loss_target, m_c_ctx, m_w_ada, m_b_ada, m_g_pre, m_g_post, m_w_in, m_conv_w, m_conv_b, m_dt_bias, m_a_log, m_d_skip, m_g_ssd, m_g_v, m_w_s, m_b_s, m_g_mlp, m_w_out, v_c_ctx, v_w_ada, v_b_ada, v_g_pre, v_g_post, v_w_in, v_conv_w, v_conv_b, v_dt_bias, v_a_log, v_d_skip, v_g_ssd, v_g_v, v_w_s, v_b_s, v_g_mlp, v_w_out):
    depth = w_in.shape[0]
    L = x.shape[1]
    n_ctx = ctx.shape[1]
    T = n_ctx + L
    Dm = D_MODEL
    me = _lin(_my_pos())
    ada_w = w_ada.shape[2]
    in_w = w_in.shape[2]
    out_r = w_out.shape[1]
    conv_c = conv_w.shape[2]
    TM = 768

    c_all = _all_gather_small(_pad_rows(c, 8), "gather_c")[:, 0, :]
    c16 = _pad_rows(jnp.concatenate([c_all, _row(c_ctx)], axis=0), ADA_ROWS)
    b_loc = lax.dynamic_slice_in_dim(b_ada, me * ada_w, ada_w, axis=1)[:, None, :]
    mod_loc = _ada_fwd(c16, w_ada, b_loc)
    mod_all = _all_gather_small(mod_loc.reshape(depth * ADA_ROWS, ada_w), "gather_mod")
    mod_all = mod_all.reshape(N_DEV, depth, ADA_ROWS, ada_w)
    mod_me = lax.dynamic_index_in_dim(mod_all, me, axis=2, keepdims=False)
    mod_me = jnp.transpose(mod_me, (1, 0, 2)).reshape(depth, N_DEV * ada_w)
    mod_cx = jnp.transpose(mod_all[:, :, 8, :], (1, 0, 2)).reshape(depth, N_DEV * ada_w)
    shift2 = jnp.stack([mod_cx[:, 0:Dm], mod_me[:, 0:Dm]], axis=1)
    scale2 = jnp.stack([mod_cx[:, Dm:2 * Dm], mod_me[:, Dm:2 * Dm]], axis=1)
    gate2 = jnp.stack([mod_cx[:, 2 * Dm:], mod_me[:, 2 * Dm:]], axis=1)

    w_in_t, m_w_in_t, v_w_in_t = (jnp.swapaxes(a, 1, 2) for a in (w_in, m_w_in, v_w_in))
    w_in_bf = w_in_t.astype(BF16)
    w_out_bf = w_out.astype(BF16)
    w_in_g = _all_gather_big(w_in_bf[0], "gather_w_in")
    w_out_g = _all_gather_big(w_out_bf[0], "gather_w_out")
    conv_all = _all_gather_small(_pad_rows(conv_w.reshape(depth * CONV_W, conv_c), 24).reshape(24, conv_c),
                                 "gather_conv_w")
    conv_full = jnp.transpose(conv_all[:, :depth * CONV_W, :], (1, 0, 2)).reshape(depth, CONV_W, XBC_W)

    def in_weights(gathered):
        wf = gathered.reshape(IN_W, Dm)
        return (jnp.concatenate([wf[:XBC_W], wf[XBC_W + DT_W:]], axis=0),
                jnp.pad(wf[XBC_W:XBC_W + DT_W], ((0, LANES - DT_W), (0, 0))))

    w_main, w_dt, w_o = [None] * depth, [None] * depth, [None] * depth

    def ssd_prm(l):
        rows = jnp.stack([jnp.pad(dt_bias[l].reshape(-1), (0, LANES - DT_W)),
                          jnp.pad(a_log[l].reshape(-1), (0, LANES - DT_W)),
                          jnp.pad(d_skip[l], (0, LANES - SSD_HEADS))])
        return _pad_rows(rows, 8)

    ws_bf = w_s.astype(BF16)
    bst = jnp.pad(jnp.transpose(b_s, (0, 2, 1)), ((0, 0), (0, 0), (0, LANES - MLP_GROUPS)))

    X = jnp.concatenate([ctx[0], x[0]], axis=0)
    saved = []
    for l in range(depth):
        w_main[l], w_dt[l] = in_weights(w_in_g)
        w_o[l] = w_out_g.reshape(N_DEV * out_r, Dm)
        hx = _prenorm_fwd(X, _row(g_pre[l]), scale2[l], shift2[l], n_ctx)
        if l + 1 < depth:
            z, raw, w_in_g = _mm(hx, w_main[l], "nt", BF16, "in_proj_gather", TM, 1024, 2048,
                                 exchange=("gather", w_in_bf[l + 1]), side=w_dt[l])
        else:
            z, raw = _mm(hx, w_main[l], "nt", BF16, "in_proj", TM, 1024, 2048, side=w_dt[l])
        cw = _pad_rows(conv_full[l], 8)
        cb = _row(conv_b[l])
        xbc = _conv_fwd(z, cw, cb, n_ctx)
        prm = ssd_prm(l)
        prep = _ssd_prep(raw, prm)
        dskx = _pad_rows(_row(jnp.repeat(d_skip[l], SSD_HEAD_DIM)), 8)
        y_f, sst_f = _ssd_fwd(xbc, prep, dskx, 0, n_ctx)
        y_b, sst_b = _ssd_fwd(xbc, prep, dskx, 1, n_ctx)
        ycat = _gate_fwd(y_f, y_b, z, _row(g_ssd[l]))
        ycat = _mlp_fwd(z, _row(g_v[l]), ws_bf[l], bst[l], _row(g_mlp[l]), ycat)
        if l + 1 < depth:
            o, w_out_g = _mm(ycat, w_o[l], "nn", F32, "out_proj_gather", TM, 1024, 4096,
                             exchange=("gather", w_out_bf[l + 1]))
        else:
            o = _mm(ycat, w_o[l], "nn", F32, "out_proj", TM, 1024, 4096)
        saved.append((X, hx, z, raw, cw, cb, xbc, prm, prep, dskx, y_f, sst_f, y_b, sst_b, ycat, o))
        X = _post_fwd(X, o, _row(g_post[l]), gate2[l], n_ctx)

    dX, sq = _loss_kernel(X, loss_target[0], n_ctx)
    loss = lax.psum(0.5 * jnp.sum(sq) / Dm, ("x", "y", "c"))

    g_small = {k: [None] * depth for k in
               ("b_ada", "g_pre", "g_post", "conv_b", "dt_bias", "a_log", "d_skip", "g_ssd", "g_v", "w_s", "b_s",
                "g_mlp", "conv_w", "dmod_c", "dmod")}
    in_recv, out_recv = [None] * depth, [None] * depth
    for l in reversed(range(depth)):
        Xl, hx, z, raw, cw, cb, xbc, prm, prep, dskx, y_f, sst_f, y_b, sst_b, ycat, o = saved[l]
        d_o, acc_post = _post_bwd(o, _row(g_post[l]), gate2[l], dX, n_ctx)
        dycat = _mm(d_o, w_o[l], "nt", BF16, "out_proj_dx", TM, 1024, 2048)
        out_slabs = _mm(ycat, d_o, "tn", BF16, "out_proj_dw", 1024, 1024, 2816).reshape(N_DEV, out_r, Dm)
        dz = lax.empty((T, Z_MAIN), BF16)
        dy, dz, acc_gate = _gate_bwd(y_f, y_b, z, _row(g_ssd[l]), dycat, dz)
        dz, acc_mlp, dws, dbst = _mlp_bwd(z, _row(g_v[l]), ws_bf[l], bst[l], _row(g_mlp[l]), dycat, dz)
        dx0, db0, dc0, ds0, ddt0, dtot0, daz0, dsk0 = _ssd_bwd(xbc, prep, dskx, sst_f, dy, 0, n_ctx)
        dx1, db1, dc1, ds1, ddt1, dtot1, daz1, _ = _ssd_bwd(xbc, prep, dskx, sst_b, dy, 1, n_ctx)
        draw, dprm = _ssd_post(raw, prm, prep[0], ds0, ds1, ddt0, ddt1, dtot0, dtot1, daz0, daz1)
        gn = SSD_GROUPS * SSD_STATE
        dz, dcw_x, dcb_x = _conv_bwd(z, cw, cb, dx0, dx1, dz, 0, n_ctx)
        dz, dcw_b, dcb_b = _conv_bwd(z, cw, cb, db0, db1, dz, SSD_W, n_ctx)
        dz, dcw_c, dcb_c = _conv_bwd(z, cw, cb, dc0, dc1, dz, SSD_W + gn, n_ctx)
        dcw = jnp.concatenate([dcw_x, dcw_b, dcw_c], axis=1)
        dcb = jnp.concatenate([dcb_x, dcb_b, dcb_c], axis=1)
        gw_main, out_recv[l] = _mm(dz, hx, "tn", BF16, "in_proj_dw_scatter", 1024, 1024, 2816,
                                   exchange=("scatter", out_slabs))
        gw_dt = _mm(draw, hx, "tn", BF16, "dt_proj_dw", LANES, 1024, 2816)
        in_slabs = jnp.concatenate([gw_main[:XBC_W], gw_dt[:DT_W], gw_main[XBC_W:]], axis=0).reshape(N_DEV, in_w, Dm)
        dhx, in_recv[l] = _mm(dz, w_main[l], "nn", F32, "in_proj_dx_scatter", TM, 1024, 4096,
                              exchange=("scatter", in_slabs), add=(draw, w_dt[l]))
        dX, acc_pre = _prenorm_bwd(Xl, _row(g_pre[l]), scale2[l], shift2[l], dhx, dX, n_ctx)

        dmod_c = jnp.concatenate([acc_pre[3], acc_pre[1], acc_post[1]])
        dmod_x = jnp.concatenate([acc_pre[4], acc_pre[2], acc_post[2]])
        g_small["dmod_c"][l] = dmod_c
        g_small["dmod"][l] = dmod_x
        g_small["b_ada"][l] = dmod_c + dmod_x
        g_small["g_pre"][l] = acc_pre[0]
        g_small["g_post"][l] = acc_post[0]
        g_small["conv_b"][l] = dcb[0]
        g_small["conv_w"][l] = dcw[:CONV_W]
        g_small["dt_bias"][l] = dprm[0, :DT_W].reshape(2, SSD_HEADS)
        g_small["a_log"][l] = dprm[1, :DT_W].reshape(2, SSD_HEADS)
        g_small["d_skip"][l] = dsk0[0, :SSD_HEADS]
        g_small["g_ssd"][l] = acc_gate[0]
        g_small["g_v"][l] = acc_mlp[0]
        g_small["g_mlp"][l] = acc_mlp[1]
        g_small["w_s"][l] = dws
        g_small["b_s"][l] = jnp.transpose(dbst[:, :MLP_GROUPS])

    grad_x = dX[n_ctx:][None]

    summed_names = ["b_ada", "g_pre", "g_post", "conv_b", "dt_bias", "a_log", "d_skip", "g_ssd", "g_v", "w_s",
                    "b_s", "g_mlp", "conv_w", "dmod_c"]
    parts = [jnp.stack(g_small[k]) for k in summed_names] + [jnp.stack(g_small["dmod"])]
    shapes = [p.shape for p in parts]
    bundle = _pack(parts)
    gathered = _all_gather_big(bundle, "gather_small_grads")
    reduced = _unpack(_sum_slabs(gathered, "sum_small_grads"), shapes)
    gs = dict(zip(summed_names, reduced[:-1]))
    n_el = int(np.prod(shapes[-1]))
    off = sum(int(np.prod(s)) for s in shapes[:-1])
    dmod_all = gathered.reshape(N_DEV, -1)[:, off:off + n_el].reshape(N_DEV, depth, 3 * Dm)

    dmod_rows = jnp.concatenate([jnp.transpose(dmod_all, (1, 0, 2)), gs["dmod_c"][:, None, :]], axis=1)
    dmod_rows = lax.dynamic_slice_in_dim(dmod_rows, me * ada_w, ada_w, axis=2)
    dmod_rows = jnp.pad(dmod_rows, ((0, 0), (0, ADA_ROWS - 9), (0, 0)))
    g_w_ada, dc_part = _ada_bwd(c16, w_ada, dmod_rows)
    dc_all = _all_gather_small(dc_part, "gather_dc")
    g_c_ctx = _sum_slabs(dc_all, "sum_dc")[0]

    grads_small = {"c_ctx": g_c_ctx, "b_ada": gs["b_ada"], "g_pre": gs["g_pre"], "g_post": gs["g_post"],
                   "conv_b": gs["conv_b"], "dt_bias": gs["dt_bias"], "a_log": gs["a_log"], "d_skip": gs["d_skip"],
                   "g_ssd": gs["g_ssd"], "g_v": gs["g_v"], "w_s": gs["w_s"], "b_s": gs["b_s"], "g_mlp": gs["g_mlp"],
                   "conv_w": lax.dynamic_slice_in_dim(gs["conv_w"], me * conv_c, conv_c, axis=2)}

    g_in, d_in, nm_in, nv_in = (jnp.swapaxes(t, 1, 2) for t in
                                _adamw_slabs(w_in_t, in_recv, m_w_in_t, v_w_in_t, "adamw_w_in", 8))
    g_out, d_out, nm_out, nv_out = _adamw_slabs(w_out, out_recv, m_w_out, v_w_out, "adamw_w_out", 128)
    d_ada, nm_ada, nv_ada = _adamw(w_ada.reshape(depth * Dm, ada_w), g_w_ada.reshape(depth * Dm, ada_w),
                                   m_w_ada.reshape(depth * Dm, ada_w), v_w_ada.reshape(depth * Dm, ada_w),
                                   "adamw_w_ada", 256)

    small_names = ["c_ctx", "b_ada", "g_pre", "g_post", "conv_w", "conv_b", "dt_bias", "a_log", "d_skip", "g_ssd",
                   "g_v", "w_s", "b_s", "g_mlp"]
    small_w = dict(c_ctx=c_ctx, b_ada=b_ada, g_pre=g_pre, g_post=g_post, conv_w=conv_w, conv_b=conv_b,
                   dt_bias=dt_bias, a_log=a_log, d_skip=d_skip, g_ssd=g_ssd, g_v=g_v, w_s=w_s, b_s=b_s, g_mlp=g_mlp)
    small_m = dict(c_ctx=m_c_ctx, b_ada=m_b_ada, g_pre=m_g_pre, g_post=m_g_post, conv_w=m_conv_w, conv_b=m_conv_b,
                   dt_bias=m_dt_bias, a_log=m_a_log, d_skip=m_d_skip, g_ssd=m_g_ssd, g_v=m_g_v, w_s=m_w_s,
                   b_s=m_b_s, g_mlp=m_g_mlp)
    small_v = dict(c_ctx=v_c_ctx, b_ada=v_b_ada, g_pre=v_g_pre, g_post=v_g_post, conv_w=v_conv_w, conv_b=v_conv_b,
                   dt_bias=v_dt_bias, a_log=v_a_log, d_skip=v_d_skip, g_ssd=v_g_ssd, g_v=v_g_v, w_s=v_w_s,
                   b_s=v_b_s, g_mlp=v_g_mlp)
    s_shapes = [small_w[k].shape for k in small_names]
    d_s, nm_s, nv_s = _adamw(_pack([small_w[k] for k in small_names]),
                             _pack([grads_small[k].reshape(small_w[k].shape) for k in small_names]),
                             _pack([small_m[k] for k in small_names]), _pack([small_v[k] for k in small_names]),
                             "adamw_small", 512)
    d_s = dict(zip(small_names, _unpack(d_s, s_shapes)))
    nm_s = dict(zip(small_names, _unpack(nm_s, s_shapes)))
    nv_s = dict(zip(small_names, _unpack(nv_s, s_shapes)))

    big = {"w_ada": (g_w_ada, d_ada.reshape(w_ada.shape), nm_ada.reshape(w_ada.shape), nv_ada.reshape(w_ada.shape)),
           "w_in": tuple(t.reshape(w_in.shape) for t in (g_in, d_in, nm_in, nv_in)),
           "w_out": tuple(t.reshape(w_out.shape) for t in (g_out, d_out, nm_out, nv_out))}
    order = ["c_ctx", "w_ada", "b_ada", "g_pre", "g_post", "w_in", "conv_w", "conv_b", "dt_bias", "a_log", "d_skip",
             "g_ssd", "g_v", "w_s", "b_s", "g_mlp", "w_out"]

    def pick(k, idx):
        if k in big:
            return big[k][idx]
        return (grads_small[k].reshape(small_w[k].shape), d_s[k], nm_s[k], nv_s[k])[idx]

    return (loss, grad_x, *[pick(k, 0) for k in order], *[pick(k, 1) for k in order],
            *[pick(k, 2) for k in order], *[pick(k, 3) for k in order])
```

```python
import numpy as np
import jax
import jax.numpy as jnp
from jax import lax
from jax.experimental import pallas as pl
from jax.experimental.pallas import tpu as pltpu

F32 = jnp.float32
BF16 = jnp.bfloat16

D_MODEL = 2048
GRID_W = 64
SSD_W = 2048
SSD_HEADS = 32
SSD_HEAD_DIM = 64
SSD_GROUPS = 8
HEADS_PER_GROUP = 4
SSD_STATE = 128
CHUNK = 128
SSD_Q = 256
CONV_W = 5
MLP_W = 2048
MLP_GROUPS = 16
XBC_W = 4096
DT_W = 64
IN_W = 12352
Z_MAIN = IN_W - DT_W
GROUP_COLS = HEADS_PER_GROUP * SSD_HEAD_DIM
EPS = 1e-6
N_DEV = 8

ADAM_LR = 0.001
ADAM_B1 = 0.9
ADAM_B2 = 0.999
ADAM_EPS = 1e-08
ADAM_WD = 0.01
ADAM_STEP = 10

LANES = 128
NEG_BIG = -1e30

NN = (((1,), (0,)), ((), ()))
NT = (((1,), (1,)), ((), ()))
TN = (((0,), (0,)), ((), ()))
HI = lax.Precision.HIGHEST


def _dot(a, b, dims):
    return lax.dot_general(a.astype(BF16), b.astype(BF16), dims, preferred_element_type=F32)


def _dot_hi(a, b, dims=NN):
    return lax.dot_general(a, b, dims, preferred_element_type=F32, precision=HI)


def _sigmoid(x):
    return 1.0 / (1.0 + jnp.exp(-x))


def _silu(x):
    return x * _sigmoid(x)


def _softplus(x):
    return jnp.maximum(x, 0.0) + jnp.log(1.0 + jnp.exp(-jnp.abs(x)))


def _call(body, *, name, out_shape, grid=None, in_specs=None, out_specs=None, scratch=(),
          sem=None, vmem_mb=None, aliases=None):
    params = {}
    if sem is not None:
        params["dimension_semantics"] = sem
    if vmem_mb is not None:
        params["vmem_limit_bytes"] = vmem_mb << 20
    kw = {}
    if grid is not None:
        kw["grid"] = grid
    if in_specs is not None:
        kw["in_specs"] = in_specs
    if out_specs is not None:
        kw["out_specs"] = out_specs
    return pl.pallas_call(body, name=name, out_shape=out_shape, scratch_shapes=list(scratch),
                          input_output_aliases=aliases or {},
                          compiler_params=pltpu.CompilerParams(**params), **kw)


def _sds(shape, dtype):
    return jax.ShapeDtypeStruct(tuple(shape), dtype)


ANY = pl.BlockSpec(memory_space=pl.ANY)


def _my_pos():
    return lax.axis_index("x"), lax.axis_index("y"), lax.axis_index("c")


def _flip(v, bit):
    return 1 - v if bit else v


def _peer(pos, k):
    mx, my, mc = pos
    return (_flip(mx, (k >> 2) & 1), _flip(my, (k >> 1) & 1), _flip(mc, k & 1))


def _lin(pos):
    return 4 * pos[0] + 2 * pos[1] + pos[2]


def _all_gather_small(x, name):
    R, C = x.shape

    def body(x_ref, o_ref, send_sems, recv_sems):
        me = _my_pos()
        o_ref[_lin(me)] = x_ref[...]
        sends = []
        for k in range(1, N_DEV):
            peer = _peer(me, k)
            cp = pltpu.make_async_remote_copy(
                src_ref=x_ref, dst_ref=o_ref.at[_lin(me)], send_sem=send_sems.at[k - 1],
                recv_sem=recv_sems.at[k - 1], device_id=peer, device_id_type=pl.DeviceIdType.MESH)
            cp.start()
            sends.append(cp)
        for k in range(1, N_DEV):
            peer = _peer(me, k)
            pltpu.make_async_remote_copy(
                src_ref=x_ref, dst_ref=o_ref.at[_lin(peer)], send_sem=send_sems.at[k - 1],
                recv_sem=recv_sems.at[k - 1], device_id=peer,
                device_id_type=pl.DeviceIdType.MESH).wait_recv()
        for cp in sends:
            cp.wait_send()

    return _call(body, name=name, out_shape=_sds((N_DEV, R, C), x.dtype),
                 in_specs=[pl.BlockSpec(memory_space=pltpu.VMEM)],
                 out_specs=pl.BlockSpec(memory_space=pltpu.VMEM),
                 scratch=[pltpu.SemaphoreType.DMA((N_DEV - 1,)), pltpu.SemaphoreType.DMA((N_DEV - 1,))],
                 vmem_mb=40)(x)


def _all_gather_big(x, name):
    def body(x_ref, o_ref, send_sems, recv_sems, local_sem):
        mx, my, mc = _my_pos()
        me, sibling = (mx, my, mc), (mx, my, 1 - mc)
        chips = [(1 - mx, my), (mx, 1 - my), (1 - mx, 1 - my)]

        def slot(pos):
            return o_ref.at[_lin(pos)]

        def copy(k, block, to, src=None):
            return pltpu.make_async_remote_copy(
                src_ref=slot(block) if src is None else src, dst_ref=slot(block),
                send_sem=send_sems.at[k], recv_sem=recv_sems.at[k], device_id=to,
                device_id_type=pl.DeviceIdType.MESH)

        mine = pltpu.make_async_copy(x_ref, slot(me), local_sem)
        mine.start()
        first = [copy(0, me, sibling, src=x_ref)]
        first += [copy(1 + j, me, (*chip, mc), src=x_ref) for j, chip in enumerate(chips)]
        for cp in first:
            cp.start()
        passed = [copy(4 + j, (*chip, mc), sibling) for j, chip in enumerate(chips)]
        for j, chip in enumerate(chips):
            copy(1 + j, (*chip, mc), me).wait_recv()
            passed[j].start()
        copy(0, sibling, me).wait_recv()
        for j, chip in enumerate(chips):
            copy(4 + j, (*chip, 1 - mc), me).wait_recv()
        for cp in first + passed:
            cp.wait_send()
        mine.wait()

    return _call(body, name=name, out_shape=_sds((N_DEV,) + x.shape, x.dtype),
                 in_specs=[ANY], out_specs=ANY,
                 scratch=[pltpu.SemaphoreType.DMA((7,)), pltpu.SemaphoreType.DMA((7,)),
                          pltpu.SemaphoreType.DMA])(x)


EXCHANGE_SEMS = [pltpu.SemaphoreType.DMA((N_DEV - 1,)), pltpu.SemaphoreType.DMA((N_DEV - 1,)),
                 pltpu.SemaphoreType.DMA]


def _exchange_copies(kind, src_ref, dst_ref, send_sems, recv_sems, local_sem, with_arrivals=True):
    me = _my_pos()
    mine_src = src_ref if kind == "gather" else src_ref.at[_lin(me)]
    local = pltpu.make_async_copy(mine_src, dst_ref.at[_lin(me)], local_sem)
    sends, recvs = [], []
    for k in range(1, N_DEV):
        peer = _peer(me, k)
        out_src = src_ref if kind == "gather" else src_ref.at[_lin(peer)]
        sends.append(pltpu.make_async_remote_copy(
            src_ref=out_src, dst_ref=dst_ref.at[_lin(me)], send_sem=send_sems.at[k - 1],
            recv_sem=recv_sems.at[k - 1], device_id=peer, device_id_type=pl.DeviceIdType.MESH))
        if with_arrivals:
            recvs.append(pltpu.make_async_remote_copy(
                src_ref=out_src, dst_ref=dst_ref.at[_lin(peer)], send_sem=send_sems.at[k - 1],
                recv_sem=recv_sems.at[k - 1], device_id=peer, device_id_type=pl.DeviceIdType.MESH))
    return sends, recvs, local


def _exchange_start(kind, src_ref, dst_ref, send_sems, recv_sems, local_sem):
    sends, _, local = _exchange_copies(kind, src_ref, dst_ref, send_sems, recv_sems, local_sem, with_arrivals=False)
    local.start()
    for cp in sends:
        cp.start()


def _exchange_wait(kind, src_ref, dst_ref, send_sems, recv_sems, local_sem):
    sends, recvs, local = _exchange_copies(kind, src_ref, dst_ref, send_sems, recv_sems, local_sem)
    for cp in recvs:
        cp.wait_recv()
    for cp in sends:
        cp.wait_send()
    local.wait()


def _exchange_shape(kind, src):
    return _sds((N_DEV,) + src.shape if kind == "gather" else src.shape, src.dtype)


def _mm(a, b, mode, out_dtype, name, tm, tn, tk, exchange=None, add=None, side=None):
    if mode == "nn":
        (M, K), (K2, N) = a.shape, b.shape
    elif mode == "nt":
        (M, K), (N, K2) = a.shape, b.shape
    else:
        (K, M), (K2, N) = a.shape, b.shape
    assert K == K2
    tm, tn, tk = min(tm, M), min(tn, N), min(tk, K)
    assert M % tm == 0 and N % tn == 0 and K % tk == 0, (name, M, N, K, tm, tn, tk)
    ni, nj, nk = M // tm, N // tn, K // tk
    dims = {"nn": NN, "nt": NT, "tn": TN}[mode]
    assert side is None or (nk == 1 and mode != "tn")

    if mode == "tn":
        a_spec = pl.BlockSpec((tk, tm), lambda i, j, k: (k, i))
    else:
        a_spec = pl.BlockSpec((tm, tk), lambda i, j, k: (i, k))
    if mode == "nt":
        b_spec = pl.BlockSpec((tn, tk), lambda i, j, k: (j, k))
    else:
        b_spec = pl.BlockSpec((tk, tn), lambda i, j, k: (k, j))
    operands, in_specs = [a, b], [a_spec, b_spec]
    out_shape, out_specs = [_sds((M, N), out_dtype)], [pl.BlockSpec((tm, tn), lambda i, j, k: (i, j))]
    if add is not None:
        k2 = add[0].shape[1]
        operands += list(add)
        in_specs += [pl.BlockSpec((tm, k2), lambda i, j, k: (i, 0)), pl.BlockSpec((k2, tn), lambda i, j, k: (0, j))]
    if side is not None:
        n3 = side.shape[0]
        operands.append(side)
        in_specs.append(pl.BlockSpec((n3, tk), lambda i, j, k: (0, 0)))
        out_shape.append(_sds((M, n3), F32))
        out_specs.append(pl.BlockSpec((tm, n3), lambda i, j, k: (i, 0)))
    scratch = [] if nk == 1 else [pltpu.VMEM((tm, tn), F32)]
    aliases = {}
    if exchange is not None:
        kind, src = exchange[:2]
        operands.append(src)
        in_specs.append(ANY)
        if len(exchange) == 3:
            into, into_idx = exchange[2]
            aliases = {len(operands): len(out_shape)}
            operands.append(into)
            in_specs.append(ANY)
            out_shape.append(_sds(into.shape, into.dtype))
        else:
            into_idx = None
            out_shape.append(_exchange_shape(kind, src))
        out_specs.append(ANY)
        scratch = EXCHANGE_SEMS + scratch
    n_in, n_out = len(operands), len(out_shape)

    def body(*refs):
        ins, outs, scr = list(refs[:n_in]), list(refs[n_in:n_in + n_out]), list(refs[n_in + n_out:])
        a_ref, b_ref, o_ref = ins[0], ins[1], outs[0]
        add_refs = ins[2:4] if add is not None else None
        side_ref = ins[2 + (2 if add is not None else 0)] if side is not None else None
        i, j, k = pl.program_id(0), pl.program_id(1), pl.program_id(2)
        if exchange is not None:
            sems, scr = scr[:3], scr[3:]
            src_ref = ins[-1] if into_idx is None else ins[-2]
            dst_ref = outs[-1] if into_idx is None else outs[-1].at[into_idx]

            @pl.when(jnp.logical_and(jnp.logical_and(i == 0, j == 0), k == 0))
            def _():
                _exchange_start(kind, src_ref, dst_ref, *sems)

        p = _dot(a_ref[...], b_ref[...], dims)
        if nk == 1:
            if add_refs is not None:
                p = p + _dot(add_refs[0][...], add_refs[1][...], NN)
            o_ref[...] = p.astype(out_dtype)
        else:
            acc_ref = scr[0]

            @pl.when(k == 0)
            def _():
                first = p if add_refs is None else p + _dot(add_refs[0][...], add_refs[1][...], NN)
                acc_ref[...] = first

            @pl.when(k > 0)
            def _():
                acc_ref[...] += p

            @pl.when(k == nk - 1)
            def _():
                o_ref[...] = acc_ref[...].astype(out_dtype)

        if side is not None:
            @pl.when(j == 0)
            def _():
                outs[1][...] = _dot(a_ref[...], side_ref[...], NT)

        if exchange is not None:
            @pl.when(jnp.logical_and(jnp.logical_and(i == ni - 1, j == nj - 1), k == nk - 1))
            def _():
                _exchange_wait(kind, src_ref, dst_ref, *sems)

    plain = exchange is None and side is None
    res = _call(body, name=name, out_shape=tuple(out_shape), grid=(ni, nj, nk), in_specs=in_specs,
                out_specs=tuple(out_specs), scratch=scratch, aliases=aliases,
                sem=("parallel", "parallel", "arbitrary") if plain else ("arbitrary",) * 3, vmem_mb=48)(*operands)
    return res[0] if n_out == 1 else res


def _rms(x):
    return lax.rsqrt(jnp.mean(x * x, axis=-1, keepdims=True) + EPS)


def _prenorm_f(x, g, sc, sh):
    return (x * _rms(x) * g) * (1.0 + sc) + sh


def _pick(is_ctx, ref):
    return jnp.where(is_ctx, ref[0:1, :], ref[1:2, :])


def _prenorm_fwd(X, g, sc2, sh2, n_ctx, tl=256):
    T, Dm = X.shape
    nct = n_ctx // tl

    def body(x_ref, g_ref, sc_ref, sh_ref, o_ref):
        is_ctx = pl.program_id(0) < nct
        o_ref[...] = _prenorm_f(x_ref[...], g_ref[...], _pick(is_ctx, sc_ref),
                                _pick(is_ctx, sh_ref)).astype(BF16)

    row = pl.BlockSpec((tl, Dm), lambda i: (i, 0))
    return _call(body, name="prenorm_fwd", out_shape=_sds((T, Dm), BF16), grid=(T // tl,),
                 in_specs=[row, pl.BlockSpec((1, Dm), lambda i: (0, 0)),
                           pl.BlockSpec((2, Dm), lambda i: (0, 0)), pl.BlockSpec((2, Dm), lambda i: (0, 0))],
                 out_specs=row, sem=("parallel",), vmem_mb=40)(X, g, sc2, sh2)


def _prenorm_bwd(X, g, sc2, sh2, d1, dres, n_ctx, tl=256):
    T, Dm = X.shape
    nct = n_ctx // tl

    def body(x_ref, g_ref, sc_ref, sh_ref, d1_ref, dres_ref, dx_ref, acc_ref):
        i = pl.program_id(0)
        is_ctx = i < nct

        @pl.when(i == 0)
        def _():
            acc_ref[...] = jnp.zeros_like(acc_ref)

        _, vjp = jax.vjp(_prenorm_f, x_ref[...], g_ref[...], _pick(is_ctx, sc_ref), _pick(is_ctx, sh_ref))
        dx, dg, dsc, dsh = vjp(d1_ref[...])
        dx_ref[...] = dres_ref[...] + dx
        zero = jnp.zeros_like(dsc)
        acc_ref[0:1, :] += dg
        acc_ref[1:2, :] += jnp.where(is_ctx, dsc, zero)
        acc_ref[2:3, :] += jnp.where(is_ctx, zero, dsc)
        acc_ref[3:4, :] += jnp.where(is_ctx, dsh, zero)
        acc_ref[4:5, :] += jnp.where(is_ctx, zero, dsh)

    row = pl.BlockSpec((tl, Dm), lambda i: (i, 0))
    return _call(body, name="prenorm_bwd", out_shape=(_sds((T, Dm), F32), _sds((8, Dm), F32)), grid=(T // tl,),
                 in_specs=[row, pl.BlockSpec((1, Dm), lambda i: (0, 0)),
                           pl.BlockSpec((2, Dm), lambda i: (0, 0)), pl.BlockSpec((2, Dm), lambda i: (0, 0)),
                           row, row],
                 out_specs=(row, pl.BlockSpec((8, Dm), lambda i: (0, 0))), sem=("arbitrary",),
                 vmem_mb=48)(X, g, sc2, sh2, d1, dres)


def _post_f(o, g, gate):
    return gate * ((o * _rms(o)) * g)


def _post_fwd(X, o, g, gate2, n_ctx, tl=256):
    T, Dm = X.shape
    nct = n_ctx // tl

    def body(x_ref, o_ref, g_ref, gate_ref, y_ref):
        is_ctx = pl.program_id(0) < nct
        y_ref[...] = x_ref[...] + _post_f(o_ref[...], g_ref[...], _pick(is_ctx, gate_ref))

    row = pl.BlockSpec((tl, Dm), lambda i: (i, 0))
    return _call(body, name="post_fwd", out_shape=_sds((T, Dm), F32), grid=(T // tl,),
                 in_specs=[row, row, pl.BlockSpec((1, Dm), lambda i: (0, 0)), pl.BlockSpec((2, Dm), lambda i: (0, 0))],
                 out_specs=row, sem=("parallel",), vmem_mb=40)(X, o, g, gate2)


def _post_bwd(o, g, gate2, dX, n_ctx, tl=256):
    T, Dm = o.shape
    nct = n_ctx // tl

    def body(o_ref, g_ref, gate_ref, dx_ref, do_ref, acc_ref):
        i = pl.program_id(0)
        is_ctx = i < nct

        @pl.when(i == 0)
        def _():
            acc_ref[...] = jnp.zeros_like(acc_ref)

        _, vjp = jax.vjp(_post_f, o_ref[...], g_ref[...], _pick(is_ctx, gate_ref))
        do, dg, dgate = vjp(dx_ref[...])
        do_ref[...] = do.astype(BF16)
        zero = jnp.zeros_like(dgate)
        acc_ref[0:1, :] += dg
        acc_ref[1:2, :] += jnp.where(is_ctx, dgate, zero)
        acc_ref[2:3, :] += jnp.where(is_ctx, zero, dgate)

    row = pl.BlockSpec((tl, Dm), lambda i: (i, 0))
    return _call(body, name="post_bwd", out_shape=(_sds((T, Dm), BF16), _sds((8, Dm), F32)), grid=(T // tl,),
                 in_specs=[row, pl.BlockSpec((1, Dm), lambda i: (0, 0)), pl.BlockSpec((2, Dm), lambda i: (0, 0)), row],
                 out_specs=(row, pl.BlockSpec((8, Dm), lambda i: (0, 0))), sem=("arbitrary",),
                 vmem_mb=48)(o, g, gate2, dX)


def _loss_kernel(X, target, n_ctx, tl=256):
    T, Dm = X.shape
    nct = n_ctx // tl

    def body(x_ref, t_ref, dx_ref, acc_ref):
        i = pl.program_id(0)

        @pl.when(i == 0)
        def _():
            acc_ref[...] = jnp.zeros_like(acc_ref)

        @pl.when(i < nct)
        def _():
            dx_ref[...] = jnp.zeros_like(dx_ref)

        @pl.when(i >= nct)
        def _():
            d = x_ref[...] - t_ref[...]
            dx_ref[...] = d * (1.0 / Dm)
            acc_ref[...] += jnp.sum(d * d, axis=0, keepdims=True)

    row = pl.BlockSpec((tl, Dm), lambda i: (i, 0))
    trow = pl.BlockSpec((tl, Dm), lambda i: (jnp.maximum(i - nct, 0), 0))
    return _call(body, name="loss", out_shape=(_sds((T, Dm), F32), _sds((1, Dm), F32)), grid=(T // tl,),
                 in_specs=[row, trow], out_specs=(row, pl.BlockSpec((1, Dm), lambda i: (0, 0))),
                 sem=("arbitrary",), vmem_mb=40)(X, target)


CONV_TL = 256
CONV_CB = 1024


def _conv_taps(x, reverse):
    rows = x.shape[0]
    pos = lax.broadcasted_iota(jnp.int32, (rows, 1), 0)
    taps = []
    for k in range(CONV_W):
        off = (2 - k) if reverse else (k - 2)
        if off == 0:
            taps.append(x)
            continue
        xs = pltpu.roll(x, (-off) % rows, 0)
        valid = (pos < rows - off) if off > 0 else (pos >= -off)
        taps.append(jnp.where(valid, xs, 0.0))
    return taps


def _conv_pre(taps, w, b):
    pre = b + taps[0] * w[0:1, :]
    for k in range(1, CONV_W):
        pre = pre + taps[k] * w[k:k + 1, :]
    return pre


def _conv_subtiles(tl, cbw, row_len):
    return [(r0, c0) for c0 in range(0, cbw, LANES) for r0 in range(0, tl, row_len)]


def _conv_fwd(z, cw, cb, n_ctx):
    T = z.shape[0]
    tl, cbw = CONV_TL, CONV_CB
    assert n_ctx == tl

    def body(z_ref, w_ref, b_ref, o_ref):
        def run(row_len):
            for r0, c0 in _conv_subtiles(tl, cbw, row_len):
                rs, cs = pl.ds(r0, row_len), pl.ds(c0, LANES)
                pre = _conv_pre(_conv_taps(z_ref[rs, cs].astype(F32), False), w_ref[:, cs], b_ref[:, cs])
                o_ref[rs, cs] = _silu(pre).astype(BF16)

        @pl.when(pl.program_id(1) == 0)
        def _():
            run(n_ctx)

        @pl.when(pl.program_id(1) > 0)
        def _():
            run(GRID_W)

    blk = pl.BlockSpec((tl, cbw), lambda j, i: (i, j))
    return _call(body, name="conv_fwd", out_shape=_sds((T, XBC_W), BF16), grid=(XBC_W // cbw, T // tl),
                 in_specs=[blk, pl.BlockSpec((8, cbw), lambda j, i: (0, j)), pl.BlockSpec((1, cbw), lambda j, i: (0, j))],
                 out_specs=blk, sem=("parallel", "parallel"))(z, cw, cb)


def _conv_bwd(z, cw, cb, d0, d1, dz_in, col0, n_ctx):
    T = z.shape[0]
    tl, cbw = CONV_TL, CONV_CB

    def body(z_ref, w_ref, b_ref, d0_ref, d1_ref, dzin_ref, dz_ref, dw_ref, db_ref):
        i = pl.program_id(1)

        @pl.when(i == 0)
        def _():
            dw_ref[...] = jnp.zeros_like(dw_ref)
            db_ref[...] = jnp.zeros_like(db_ref)

        def run(row_len):
            for c0 in range(0, cbw, LANES):
                cs = pl.ds(c0, LANES)
                w = w_ref[:, cs]
                dw = [jnp.zeros((1, LANES), F32) for _ in range(CONV_W)]
                db = jnp.zeros((1, LANES), F32)
                for r0 in range(0, tl, row_len):
                    rs = pl.ds(r0, row_len)
                    taps = _conv_taps(z_ref[rs, cs].astype(F32), False)
                    pre = _conv_pre(taps, w, b_ref[:, cs])
                    s = _sigmoid(pre)
                    dpre = (d0_ref[rs, cs].astype(F32) + d1_ref[rs, cs].astype(F32)) * (s + pre * s * (1.0 - s))
                    dz_ref[rs, cs] = _conv_pre(_conv_taps(dpre, True), w, 0.0).astype(BF16)
                    for k in range(CONV_W):
                        dw[k] = dw[k] + jnp.sum(dpre * taps[k], axis=0, keepdims=True)
                    db = db + jnp.sum(dpre, axis=0, keepdims=True)
                for k in range(CONV_W):
                    dw_ref[k:k + 1, cs] += dw[k]
                db_ref[0:1, cs] += db

        @pl.when(i == 0)
        def _():
            run(n_ctx)

        @pl.when(i > 0)
        def _():
            run(GRID_W)

    width = d0.shape[1]
    jo = col0 // cbw
    blk = pl.BlockSpec((tl, cbw), lambda j, i: (i, jo + j))
    dblk = pl.BlockSpec((tl, cbw), lambda j, i: (i, j))
    par = pl.BlockSpec((8, cbw), lambda j, i: (0, jo + j))
    opar = pl.BlockSpec((8, cbw), lambda j, i: (0, j))
    return _call(body, name=f"conv_bwd_{col0}",
                 out_shape=(_sds(dz_in.shape, BF16), _sds((8, width), F32), _sds((8, width), F32)),
                 grid=(width // cbw, T // tl),
                 in_specs=[blk, par, pl.BlockSpec((1, cbw), lambda j, i: (0, jo + j)), dblk, dblk, ANY],
                 out_specs=(blk, opar, opar), sem=("parallel", "arbitrary"), aliases={5: 0})(z, cw, cb, d0, d1, dz_in)


MIX_TL = 128
ROW_BLOCK = 16
COL_CHUNK = 512


def _for_row_blocks(n_rows, fn):
    def step(i, carry):
        fn(pl.ds(pl.multiple_of(i * ROW_BLOCK, ROW_BLOCK), ROW_BLOCK))
        return carry

    lax.fori_loop(0, n_rows // ROW_BLOCK, step, 0)


def _col_chunks(width):
    return [pl.ds(c0, COL_CHUNK) for c0 in range(0, width, COL_CHUNK)]


def _gate_fwd(yf, yb, z, g):
    T = yf.shape[0]
    tl = MIX_TL

    def body(yf_ref, yb_ref, zs_ref, g_ref, o_ref):
        t = (yf_ref[...].astype(F32) + yb_ref[...].astype(F32)) * _silu(zs_ref[...].astype(F32))
        o_ref[...] = (t * _rms(t) * g_ref[...]).astype(BF16)

    row = pl.BlockSpec((tl, SSD_W), lambda i: (i, 0))
    return _call(body, name="gate_fwd", out_shape=_sds((T, SSD_W + MLP_W), BF16), grid=(T // tl,),
                 in_specs=[row, row, pl.BlockSpec((tl, SSD_W), lambda i: (i, XBC_W // SSD_W)),
                           pl.BlockSpec((1, SSD_W), lambda i: (0, 0))],
                 out_specs=row, sem=("parallel",))(yf, yb, z, g)


def _gate_bwd(yf, yb, z, g, dycat, dz_in):
    T = yf.shape[0]
    tl = MIX_TL

    def body(yf_ref, yb_ref, zs_ref, g_ref, d_ref, dzin_ref, dy_ref, dz_ref, acc_ref):
        @pl.when(pl.program_id(0) == 0)
        def _():
            acc_ref[...] = jnp.zeros_like(acc_ref)

        def block(rs):
            def parts(cs):
                y = yf_ref[rs, cs].astype(F32) + yb_ref[rs, cs].astype(F32)
                zs = zs_ref[rs, cs].astype(F32)
                sg = _sigmoid(zs)
                return y, zs, sg, y * (zs * sg), d_ref[rs, cs].astype(F32)

            ss = jnp.zeros((ROW_BLOCK, 1), F32)
            su = jnp.zeros((ROW_BLOCK, 1), F32)
            for cs in _col_chunks(SSD_W):
                _, _, _, t, dd = parts(cs)
                ss = ss + jnp.sum(t * t, axis=1, keepdims=True)
                su = su + jnp.sum(t * (dd * g_ref[:, cs]), axis=1, keepdims=True)
            r = lax.rsqrt(ss * (1.0 / SSD_W) + EPS)
            c = r * r * r * (su * (1.0 / SSD_W))
            for cs in _col_chunks(SSD_W):
                y, zs, sg, t, dd = parts(cs)
                dt = r * (dd * g_ref[:, cs]) - t * c
                dy_ref[rs, cs] = (dt * (zs * sg)).astype(BF16)
                dz_ref[rs, cs] = (dt * y * (sg * (1.0 + zs * (1.0 - sg)))).astype(BF16)
                acc_ref[0:1, cs] += jnp.sum(dd * t * r, axis=0, keepdims=True)

        _for_row_blocks(tl, block)

    row = pl.BlockSpec((tl, SSD_W), lambda i: (i, 0))
    zs_spec = pl.BlockSpec((tl, SSD_W), lambda i: (i, XBC_W // SSD_W))
    return _call(body, name="gate_bwd",
                 out_shape=(_sds((T, SSD_W), BF16), _sds(dz_in.shape, BF16), _sds((8, SSD_W), F32)),
                 grid=(T // tl,),
                 in_specs=[row, row, zs_spec, pl.BlockSpec((1, SSD_W), lambda i: (0, 0)), row, ANY],
                 out_specs=(row, zs_spec, pl.BlockSpec((8, SSD_W), lambda i: (0, 0))),
                 sem=("arbitrary",), aliases={5: 1}, vmem_mb=40)(yf, yb, z, g, dycat, dz_in)


def _vnorm_f(v, gv):
    return v * _rms(v) * gv


def _mlp_out_f(u, sg, zm, gm):
    t = u * sg * _silu(zm)
    return t * _rms(t) * gm


U_BLK = (XBC_W + SSD_W) // MLP_W


def _mlp_mix(ws_ref, bst_ref, vn_s, sg_s):
    for gi in range(MLP_GROUPS):
        cols = pl.ds(gi * LANES, LANES)
        sg_s[:, cols] = _dot(ws_ref[gi], vn_s[:, cols], NN) + bst_ref[:, gi:gi + 1]


def _mlp_fwd(z, gv, ws, bst, gm, ycat_in):
    T = z.shape[0]
    tl = CHUNK

    def body(u_ref, v_ref, zm_ref, gv_ref, ws_ref, bst_ref, gm_ref, yin_ref, o_ref, vn_s, sg_s):
        vn_s[...] = _vnorm_f(v_ref[...].astype(F32), gv_ref[...]).astype(BF16)
        _mlp_mix(ws_ref, bst_ref, vn_s, sg_s)
        o_ref[...] = _mlp_out_f(u_ref[...].astype(F32), sg_s[...], zm_ref[...].astype(F32),
                                gm_ref[...]).astype(BF16)

    def zblk(b):
        return pl.BlockSpec((tl, MLP_W), lambda i: (i, b))

    vec = pl.BlockSpec((1, MLP_W), lambda i: (0, 0))
    return _call(body, name="mlp_fwd", out_shape=_sds(ycat_in.shape, BF16), grid=(T // tl,),
                 in_specs=[zblk(U_BLK), zblk(U_BLK + 1), zblk(U_BLK + 2), vec,
                           pl.BlockSpec((MLP_GROUPS, CHUNK, CHUNK), lambda i: (0, 0, 0)),
                           pl.BlockSpec((CHUNK, LANES), lambda i: (0, 0)), vec, ANY],
                 out_specs=pl.BlockSpec((tl, MLP_W), lambda i: (i, 1)),
                 scratch=[pltpu.VMEM((tl, MLP_W), BF16), pltpu.VMEM((tl, MLP_W), F32)],
                 sem=("parallel",), aliases={7: 0})(z, z, z, gv, ws, bst, gm, ycat_in)


def _mlp_bwd(z, gv, ws, bst, gm, dycat, dz_in):
    T = z.shape[0]
    tl = CHUNK

    def body(u_ref, v_ref, zm_ref, gv_ref, ws_ref, bst_ref, gm_ref, d_ref, dzin_ref,
             dz_ref, acc_ref, dws_ref, dbst_ref, vn_s, sg_s, dvn_s):
        @pl.when(pl.program_id(0) == 0)
        def _():
            acc_ref[...] = jnp.zeros_like(acc_ref)
            dws_ref[...] = jnp.zeros_like(dws_ref)
            dbst_ref[...] = jnp.zeros_like(dbst_ref)

        v = v_ref[...].astype(F32)
        vn, vjp_v = jax.vjp(_vnorm_f, v, gv_ref[...])
        vn_s[...] = vn.astype(BF16)
        _mlp_mix(ws_ref, bst_ref, vn_s, sg_s)
        _, vjp_o = jax.vjp(_mlp_out_f, u_ref[...].astype(F32), sg_s[...], zm_ref[...].astype(F32), gm_ref[...])
        du, dsg, dzm, dgm = vjp_o(d_ref[...].astype(F32))
        sg_s[...] = dsg
        for gi in range(MLP_GROUPS):
            cols = pl.ds(gi * LANES, LANES)
            dsg_g = sg_s[:, cols]
            dvn_s[:, cols] = _dot(ws_ref[gi], dsg_g, TN)
            dws_ref[gi] += _dot(dsg_g, vn_s[:, cols], NT)
            dbst_ref[:, gi:gi + 1] += jnp.sum(dsg_g, axis=1, keepdims=True)
        dv, dgv = vjp_v(dvn_s[...])
        dz_ref[:, 0:MLP_W] = du.astype(BF16)
        dz_ref[:, MLP_W:2 * MLP_W] = dv.astype(BF16)
        dz_ref[:, 2 * MLP_W:3 * MLP_W] = dzm.astype(BF16)
        acc_ref[0:1, :] += dgv
        acc_ref[1:2, :] += dgm

    def zblk(b):
        return pl.BlockSpec((tl, MLP_W), lambda i: (i, b))

    vec = pl.BlockSpec((1, MLP_W), lambda i: (0, 0))
    ws_spec = pl.BlockSpec((MLP_GROUPS, CHUNK, CHUNK), lambda i: (0, 0, 0))
    bst_spec = pl.BlockSpec((CHUNK, LANES), lambda i: (0, 0))
    return _call(body, name="mlp_bwd",
                 out_shape=(_sds(dz_in.shape, BF16), _sds((8, MLP_W), F32),
                            _sds((MLP_GROUPS, CHUNK, CHUNK), F32), _sds((CHUNK, LANES), F32)),
                 grid=(T // tl,),
                 in_specs=[zblk(U_BLK), zblk(U_BLK + 1), zblk(U_BLK + 2), vec, ws_spec, bst_spec, vec,
                           pl.BlockSpec((tl, MLP_W), lambda i: (i, 1)), ANY],
                 out_specs=(pl.BlockSpec((tl, 3 * MLP_W), lambda i: (i, 1)),
                            pl.BlockSpec((8, MLP_W), lambda i: (0, 0)), ws_spec, bst_spec),
                 scratch=[pltpu.VMEM((tl, MLP_W), BF16), pltpu.VMEM((tl, MLP_W), F32), pltpu.VMEM((tl, MLP_W), F32)],
                 sem=("arbitrary",), aliases={8: 0}, vmem_mb=48)(z, z, z, gv, ws, bst, gm, dycat, dz_in)


def _ssd_constants(direction):
    q = SSD_Q
    tri = np.tril(np.ones((q, q), np.float32))
    if direction == 1:
        tri = tri.T
    ex = np.zeros((SSD_GROUPS, LANES, GROUP_COLS), np.float32)
    for g in range(SSD_GROUPS):
        for r in range(HEADS_PER_GROUP):
            ex[g, direction * SSD_HEADS + HEADS_PER_GROUP * g + r, r * SSD_HEAD_DIM:(r + 1) * SSD_HEAD_DIM] = 1.0
    hm = np.zeros((8, GROUP_COLS), np.float32)
    for r in range(HEADS_PER_GROUP):
        hm[r, r * SSD_HEAD_DIM:(r + 1) * SSD_HEAD_DIM] = 1.0
    return (jnp.asarray(tri), jnp.asarray(np.concatenate([ex, ex], axis=1), dtype=BF16),
            jnp.asarray(np.ascontiguousarray(ex.transpose(0, 2, 1)), dtype=BF16), jnp.asarray(hm, dtype=BF16))


def _dot2(v, m, dims=NN):
    hi = v.astype(BF16)
    lo = (v - hi.astype(F32)).astype(BF16)
    return (lax.dot_general(hi, m, dims, preferred_element_type=F32)
            + lax.dot_general(lo, m, dims, preferred_element_type=F32))


def _ssd_prep(raw, prm):
    T = raw.shape[0]
    nc, Q = T // SSD_Q, SSD_Q
    tril = jnp.asarray(np.tril(np.ones((Q, Q), np.float32)))
    triu = jnp.asarray(np.triu(np.ones((Q, Q), np.float32)))

    def body(raw_ref, prm_ref, tril_ref, triu_ref, dt_ref, s_ref, pk_ref, st_ref, etot_ref):
        dt = _softplus(raw_ref[...] + prm_ref[0:1, :])
        a = dt * (-jnp.exp(prm_ref[1:2, :]))
        lane = lax.broadcasted_iota(jnp.int32, (1, LANES), 1)
        s = jnp.where(lane < SSD_HEADS, _dot_hi(tril_ref[...], a), _dot_hi(triu_ref[...], a))
        tot = jnp.sum(a, axis=0, keepdims=True)
        dt_ref[...] = dt
        s_ref[...] = s
        for n, v in enumerate((dt, jnp.exp(s), jnp.exp(tot - s))):
            hi = v.astype(BF16)
            pk_ref[:, 2 * n * LANES:(2 * n + 1) * LANES] = hi
            pk_ref[:, (2 * n + 1) * LANES:(2 * n + 2) * LANES] = (v - hi.astype(F32)).astype(BF16)
        st_ref[0] = s.T
        etot_ref[0] = _bcast8(jnp.exp(tot))

    row = pl.BlockSpec((Q, LANES), lambda c: (c, 0))
    cst = lambda shape: pl.BlockSpec(shape, lambda c: (0, 0))
    tl = _sds((T, LANES), F32)
    return _call(body, name="ssd_prep",
                 out_shape=(tl, tl, _sds((T, 6 * LANES), BF16), _sds((nc, LANES, Q), F32), _sds((nc, 8, LANES), F32)),
                 grid=(nc,), in_specs=[row, cst((8, LANES)), cst((Q, Q)), cst((Q, Q))],
                 out_specs=(row, row, pl.BlockSpec((Q, 6 * LANES), lambda c: (c, 0)),
                            pl.BlockSpec((1, LANES, Q), lambda c: (c, 0, 0)),
                            pl.BlockSpec((1, 8, LANES), lambda c: (c, 0, 0))),
                 sem=("parallel",))(raw, prm, tril, triu)


def _ssd_post(raw, prm, dt, ds0, ds1, ddt0, ddt1, dtot0, dtot1, daz0, daz1):
    T = raw.shape[0]
    nc, Q = T // SSD_Q, SSD_Q
    tril = jnp.asarray(np.tril(np.ones((Q, Q), np.float32)))
    triu = jnp.asarray(np.triu(np.ones((Q, Q), np.float32)))

    def body(raw_ref, prm_ref, dt_ref, ds0_ref, ds1_ref, ddt0_ref, ddt1_ref, dtot0_ref, dtot1_ref, daz0_ref, daz1_ref,
             tril_ref, triu_ref, draw_ref, dprm_ref):
        @pl.when(pl.program_id(0) == 0)
        def _():
            dprm_ref[...] = jnp.zeros_like(dprm_ref)

        pre = raw_ref[...] + prm_ref[0:1, :]
        A = -jnp.exp(prm_ref[1:2, :])
        ds = ds0_ref[...] + ds1_ref[...]
        lane = lax.broadcasted_iota(jnp.int32, (1, LANES), 1)
        da = jnp.where(lane < SSD_HEADS, _dot_hi(triu_ref[...], ds), _dot_hi(tril_ref[...], ds))
        da = da + dtot0_ref[0, 0:1, :] + dtot1_ref[0, 0:1, :]
        daz_rows = jnp.concatenate([daz0_ref[0], daz1_ref[0], jnp.zeros((LANES - 2 * SSD_HEADS, Q), F32)], axis=0)
        da = da + daz_rows.T
        draw = (da * A + ddt0_ref[...] + ddt1_ref[...]) * _sigmoid(pre)
        draw_ref[...] = draw
        dprm_ref[0:1, :] += jnp.sum(draw, axis=0, keepdims=True)
        dprm_ref[1:2, :] += jnp.sum(da * dt_ref[...], axis=0, keepdims=True) * A

    row = pl.BlockSpec((Q, LANES), lambda c: (c, 0))
    tot = pl.BlockSpec((1, 8, LANES), lambda c: (c, 0, 0))
    daz = pl.BlockSpec((1, SSD_HEADS, Q), lambda c: (c, 0, 0))
    cst = lambda shape: pl.BlockSpec(shape, lambda c: (0, 0))
    return _call(body, name="ssd_post", out_shape=(_sds((T, LANES), F32), _sds((8, LANES), F32)), grid=(nc,),
                 in_specs=[row, cst((8, LANES)), row, row, row, row, row, tot, tot, daz, daz, cst((Q, Q)), cst((Q, Q))],
                 out_specs=(row, cst((8, LANES))), sem=("arbitrary",))(raw, prm, dt, ds0, ds1, ddt0, ddt1, dtot0, dtot1,
                                                                       daz0, daz1, tril, triu)


def _chunk_order(direction, nc, nctx):
    if direction == 0:
        return lambda c: c
    return lambda c: jnp.where(c < nctx, nctx - 1 - c, nc - 1 + nctx - c)


def _bcast8(row):
    return jnp.broadcast_to(row, (8, row.shape[1]))


GPS = 8


def _cols(ref, k, width):
    return ref[:, k * width:(k + 1) * width]


def _ssd_common(k, x_ref, b_ref, c_ref, s_ref, pk_ref, st_ref, etot_ref, ex_ref, direction):
    g = pl.program_id(1) * GPS + k
    base = direction * SSD_HEADS + HEADS_PER_GROUP * g
    x = _cols(x_ref, k, GROUP_COLS).astype(F32)
    ex = ex_ref[k]

    def spread(n):
        return lax.dot_general(_cols(pk_ref, n, 2 * LANES), ex, NN, preferred_element_type=F32)

    dt_x, E_x, D_x = spread(0), spread(1), spread(2)
    lane = lax.broadcasted_iota(jnp.int32, (1, LANES), 1)
    etot = etot_ref[0, 0:1, :]
    rowi = lax.broadcasted_iota(jnp.int32, (GROUP_COLS, 1), 0)
    e = [jnp.sum(jnp.where(lane == base + r, etot, 0.0), axis=1, keepdims=True) for r in range(HEADS_PER_GROUP)]
    Etot_c = jnp.where(rowi < SSD_HEAD_DIM, e[0],
                       jnp.where(rowi < 2 * SSD_HEAD_DIM, e[1], jnp.where(rowi < 3 * SSD_HEAD_DIM, e[2], e[3])))
    rows = [st_ref[0, HEADS_PER_GROUP * k + r:HEADS_PER_GROUP * k + r + 1, :] for r in range(HEADS_PER_GROUP)]
    return dict(g=g, base=base, x=x, s=s_ref[...], rows=rows, dt_x=dt_x, lane=lane, X=x * dt_x, E_x=E_x, D_x=D_x,
                Etot_c=Etot_c, Bm=_cols(b_ref, k, SSD_STATE), Cm=_cols(c_ref, k, SSD_STATE))


def _head_decay(q, r, mask):
    sel = q["lane"] == q["base"] + r
    col = jnp.sum(jnp.where(sel, q["s"], 0.0), axis=1, keepdims=True)
    return jnp.exp(jnp.where(mask, col - q["rows"][r], NEG_BIG))


def _ssd_in_specs(order, direction):
    Q, GC, N, G = SSD_Q, GROUP_COLS, SSD_STATE, SSD_GROUPS
    row = pl.BlockSpec((Q, LANES), lambda c, g: (order(c), 0))
    hrows = HEADS_PER_GROUP * GPS
    return [
        pl.BlockSpec((Q, GPS * GC), lambda c, g: (order(c), g)),
        pl.BlockSpec((Q, GPS * N), lambda c, g: (order(c), SSD_W // (GPS * N) + g)),
        pl.BlockSpec((Q, GPS * N), lambda c, g: (order(c), (SSD_W + G * N) // (GPS * N) + g)),
        row, pl.BlockSpec((Q, 6 * LANES), lambda c, g: (order(c), 0)),
        pl.BlockSpec((1, hrows, Q), lambda c, g: (order(c), direction * (SSD_HEADS // hrows) + g, 0)),
        pl.BlockSpec((1, 8, LANES), lambda c, g: (order(c), 0, 0)),
    ]


def _ssd_fwd(xbc, prep, dskx, direction, n_ctx):
    T = xbc.shape[0]
    nc, nctx = T // SSD_Q, n_ctx // SSD_Q
    order = _chunk_order(direction, nc, nctx)
    tri, ex, ext, hm = _ssd_constants(direction)
    Q, GC, N, G = SSD_Q, GROUP_COLS, SSD_STATE, SSD_GROUPS

    def body(x_ref, b_ref, c_ref, s_ref, pk_ref, st_ref, etot_ref, dsk_ref, tri_ref, ex_ref, hm_ref,
             y_ref, sst_ref, S):
        c = pl.program_id(0)
        mask = tri_ref[...] > 0.5
        for k in range(GPS):
            q = _ssd_common(k, x_ref, b_ref, c_ref, s_ref, pk_ref, st_ref, etot_ref, ex_ref, direction)
            g = q["g"]

            @pl.when(c == 0)
            def _():
                S[g] = jnp.zeros((GC, N), F32)

            S0 = S[g]
            sst_ref[0, k] = S0
            X, Bm, Cm = q["X"], q["Bm"], q["Cm"]
            Xb = X.astype(BF16)
            y = q["E_x"] * _dot(Cm, S0, NT)
            Gm = _dot(Cm, Bm, NT)
            for r in range(HEADS_PER_GROUP):
                y = y + _dot(Gm * _head_decay(q, r, mask), Xb * hm_ref[r:r + 1, :], NN)
            if direction == 0:
                y = y + _cols(dsk_ref, k, GC)[0:1, :] * q["x"]
            y_ref[:, k * GC:(k + 1) * GC] = y.astype(BF16)
            S[g] = q["Etot_c"] * S0 + _dot(X * q["D_x"], Bm, TN)

    cst2 = lambda shape: pl.BlockSpec(shape, lambda c, g: (0, 0))
    in_specs = _ssd_in_specs(order, direction) + [
        pl.BlockSpec((8, GPS * GC), lambda c, g: (0, g)), cst2((Q, Q)),
        pl.BlockSpec((GPS, 2 * LANES, GC), lambda c, g: (g, 0, 0)), cst2((8, GC))]
    out_specs = (pl.BlockSpec((Q, GPS * GC), lambda c, g: (order(c), g)),
                 pl.BlockSpec((1, GPS, GC, N), lambda c, g: (order(c), g, 0, 0)))
    return _call(body, name=f"ssd_fwd_{direction}",
                 out_shape=(_sds((T, SSD_W), BF16), _sds((nc, G, GC, N), F32)),
                 grid=(nc, G // GPS), in_specs=in_specs, out_specs=out_specs,
                 scratch=[pltpu.VMEM((G, GC, N), F32)], sem=("arbitrary", "arbitrary"),
                 )(xbc, xbc, xbc, *prep[1:], dskx, tri, ex, hm)


def _ssd_bwd(xbc, prep, dskx, sst, dy, direction, n_ctx):
    T = xbc.shape[0]
    nc, nctx = T // SSD_Q, n_ctx // SSD_Q
    fwd_order = _chunk_order(direction, nc, nctx)
    order = lambda c: fwd_order(nc - 1 - c)
    tri, ex, ext, hm = _ssd_constants(direction)
    ntri = (1.0 - tri).astype(BF16)
    Q, GC, N, G = SSD_Q, GROUP_COLS, SSD_STATE, SSD_GROUPS

    def body(x_ref, b_ref, c_ref, s_ref, pk_ref, st_ref, etot_ref, dsk_ref, sst_ref, dy_ref, tri_ref,
             ex_ref, ext_ref, hm_ref, ntri_ref, dx_ref, db_ref, dc_ref, ds_ref, ddt_ref, dtot_ref, daz_ref, dskg_ref,
             dS):
        c, gi = pl.program_id(0), pl.program_id(1)
        mask = tri_ref[...] > 0.5

        @pl.when(jnp.logical_and(c == 0, gi == 0))
        def _():
            dskg_ref[...] = jnp.zeros_like(dskg_ref)

        @pl.when(gi == 0)
        def _():
            ds_ref[...] = jnp.zeros_like(ds_ref)
            ddt_ref[...] = jnp.zeros_like(ddt_ref)
            dtot_ref[...] = jnp.zeros_like(dtot_ref)

        row8 = lax.broadcasted_iota(jnp.int32, (HEADS_PER_GROUP * GPS, 1), 0)
        daz = jnp.zeros((HEADS_PER_GROUP * GPS, Q), F32)
        for k in range(GPS):
            q = _ssd_common(k, x_ref, b_ref, c_ref, s_ref, pk_ref, st_ref, etot_ref, ex_ref, direction)
            g = q["g"]

            @pl.when(c == 0)
            def _():
                dS[g] = jnp.zeros((GC, N), F32)

            dS1 = dS[g]
            S0 = sst_ref[0, k]
            x, X, Bm, Cm, ext = q["x"], q["X"], q["Bm"], q["Cm"], ext_ref[k]
            E_x, D_x, Etot_c = q["E_x"], q["D_x"], q["Etot_c"]
            dYb = _cols(dy_ref, k, GC)
            dY = dYb.astype(F32)
            Xb = X.astype(BF16)

            CS = _dot(Cm, S0, NT)
            dCS = dY * E_x
            dC = _dot(dCS, S0, NN)
            dS0 = Etot_c * dS1 + _dot(dCS, Cm, TN)
            ds_x = dY * (E_x * CS)
            dtot_c = jnp.sum(dS1 * S0, axis=1, keepdims=True) * Etot_c
            dtot = jnp.sum(dtot_c * ext.astype(F32), axis=0, keepdims=True)
            XD = X * D_x
            dXD = _dot(Bm, dS1, NT)
            dB = _dot(XD, dS1, NN)
            dX = dXD * D_x
            t = dXD * XD
            ds_x = ds_x - t
            dtot_x = jnp.sum(t, axis=0, keepdims=True)
            Gm = _dot(Cm, Bm, NT)
            dG = jnp.zeros((Q, Q), F32)
            for r in range(HEADS_PER_GROUP):
                hmr = hm_ref[r:r + 1, :]
                Lm = _head_decay(q, r, mask)
                W = Gm * Lm
                dW = _dot(dYb * hmr, Xb, NT)
                dX = dX + _dot(W, dYb, TN) * hmr.astype(F32)
                dG = dG + dW * Lm
                P = _dot(dW * W, ntri_ref[...], NN)
                da_row = jnp.sum(jnp.where(mask, P, 0.0), axis=0, keepdims=True)
                daz = jnp.where(row8 == HEADS_PER_GROUP * k + r, da_row, daz)
            dC = dC + _dot(dG, Bm, NN)
            dB = dB + _dot(dG, Cm, TN)
            ds_ref[...] += _dot2(ds_x, ext)
            dtot_ref[0, 0:1, :] += dtot + _dot2(_bcast8(dtot_x), ext)[0:1, :]
            ddt_ref[...] += _dot(dX * x, ext, NN)
            dx = dX * q["dt_x"]
            if direction == 0:
                dx = dx + dY * _cols(dsk_ref, k, GC)[0:1, :]
                dskg_ref[0:1, :] += jnp.sum(_dot(dY * x, ext, NN), axis=0, keepdims=True)
            dx_ref[:, k * GC:(k + 1) * GC] = dx.astype(BF16)
            db_ref[:, k * N:(k + 1) * N] = dB.astype(BF16)
            dc_ref[:, k * N:(k + 1) * N] = dC.astype(BF16)
            dS[g] = dS0
        daz_ref[0] = daz

    cst2 = lambda shape: pl.BlockSpec(shape, lambda c, g: (0, 0))
    in_specs = _ssd_in_specs(order, direction) + [
        pl.BlockSpec((8, GPS * GC), lambda c, g: (0, g)),
        pl.BlockSpec((1, GPS, GC, N), lambda c, g: (order(c), g, 0, 0)),
        pl.BlockSpec((Q, GPS * GC), lambda c, g: (order(c), g)),
        cst2((Q, Q)),
        pl.BlockSpec((GPS, 2 * LANES, GC), lambda c, g: (g, 0, 0)),
        pl.BlockSpec((GPS, GC, LANES), lambda c, g: (g, 0, 0)),
        cst2((8, GC)), cst2((Q, Q)),
    ]
    row = pl.BlockSpec((Q, LANES), lambda c, g: (order(c), 0))
    hrows = HEADS_PER_GROUP * GPS
    out_specs = (pl.BlockSpec((Q, GPS * GC), lambda c, g: (order(c), g)),
                 pl.BlockSpec((Q, GPS * N), lambda c, g: (order(c), g)),
                 pl.BlockSpec((Q, GPS * N), lambda c, g: (order(c), g)),
                 row, row, pl.BlockSpec((1, 8, LANES), lambda c, g: (order(c), 0, 0)),
                 pl.BlockSpec((1, hrows, Q), lambda c, g: (order(c), g, 0)),
                 cst2((8, LANES)))
    return _call(body, name=f"ssd_bwd_{direction}",
                 out_shape=(_sds((T, SSD_W), BF16), _sds((T, G * N), BF16), _sds((T, G * N), BF16),
                            _sds((T, LANES), F32), _sds((T, LANES), F32), _sds((nc, 8, LANES), F32),
                            _sds((nc, SSD_HEADS, Q), F32), _sds((8, LANES), F32)),
                 grid=(nc, G // GPS), in_specs=in_specs, out_specs=out_specs,
                 scratch=[pltpu.VMEM((G, GC, N), F32)], sem=("arbitrary", "arbitrary"),
                 )(xbc, xbc, xbc, *prep[1:], dskx, sst, dy, tri, ex, ext, hm, ntri)


ADA_ROWS = 16


def _ada_fwd(c16, w_ada, b_loc):
    depth, Dm, W = w_ada.shape

    def body(c_ref, w_ref, b_ref, o_ref):
        o_ref[0] = _dot_hi(_silu(c_ref[...]), w_ref[0]) + b_ref[0]

    return _call(body, name="ada_fwd", out_shape=_sds((depth, ADA_ROWS, W), F32), grid=(depth,),
                 in_specs=[pl.BlockSpec((ADA_ROWS, Dm), lambda l: (0, 0)),
                           pl.BlockSpec((1, Dm, W), lambda l: (l, 0, 0)),
                           pl.BlockSpec((1, 1, W), lambda l: (l, 0, 0))],
                 out_specs=pl.BlockSpec((1, ADA_ROWS, W), lambda l: (l, 0, 0)),
                 sem=("parallel",), vmem_mb=40)(c16, w_ada, b_loc)


def _ada_bwd(c16, w_ada, dmod):
    depth, Dm, W = w_ada.shape

    def body(c_ref, w_ref, d_ref, gw_ref, dc_ref):
        l = pl.program_id(0)

        @pl.when(l == 0)
        def _():
            dc_ref[...] = jnp.zeros_like(dc_ref)

        cc = c_ref[...]
        sg = _sigmoid(cc)
        gw_ref[0] = _dot_hi(cc * sg, d_ref[0], TN)
        dsc = _dot_hi(d_ref[0], w_ref[0], NT)
        dc_ref[...] += dsc[8:16, :] * (sg + cc * sg * (1.0 - sg))[8:16, :]

    return _call(body, name="ada_bwd", out_shape=(_sds((depth, Dm, W), F32), _sds((8, Dm), F32)), grid=(depth,),
                 in_specs=[pl.BlockSpec((ADA_ROWS, Dm), lambda l: (0, 0)),
                           pl.BlockSpec((1, Dm, W), lambda l: (l, 0, 0)),
                           pl.BlockSpec((1, ADA_ROWS, W), lambda l: (l, 0, 0))],
                 out_specs=(pl.BlockSpec((1, Dm, W), lambda l: (l, 0, 0)), pl.BlockSpec((8, Dm), lambda l: (0, 0))),
                 sem=("arbitrary",), vmem_mb=48)(c16, w_ada, dmod)


def _sum_slabs(g, name, tr=512):
    _, R, C = g.shape
    tr = min(tr, R)

    def body(g_ref, o_ref):
        acc = g_ref[0]
        for k in range(1, N_DEV):
            acc = acc + g_ref[k]
        o_ref[...] = acc

    return _call(body, name=name, out_shape=_sds((R, C), F32), grid=(R // tr,),
                 in_specs=[pl.BlockSpec((N_DEV, tr, C), lambda i: (0, i, 0))],
                 out_specs=pl.BlockSpec((tr, C), lambda i: (i, 0)), sem=("parallel",), vmem_mb=40)(g)


def _adamw_math(w, g, m, v):
    m = ADAM_B1 * m + (1.0 - ADAM_B1) * g
    v = ADAM_B2 * v + (1.0 - ADAM_B2) * (g * g)
    m_hat = m / (1.0 - ADAM_B1 ** ADAM_STEP)
    v_hat = v / (1.0 - ADAM_B2 ** ADAM_STEP)
    delta = -ADAM_LR * (m_hat / (jnp.sqrt(v_hat) + ADAM_EPS) + ADAM_WD * w)
    return delta, m, v


def _adamw(w, g, m, v, name, tr):
    R, C = w.shape
    tr = min(tr, R)
    assert R % tr == 0

    def body(w_ref, g_ref, m_ref, v_ref, d_ref, nm_ref, nv_ref):
        d, nm, nv = _adamw_math(w_ref[...], g_ref[...], m_ref[...], v_ref[...])
        d_ref[...] = d
        nm_ref[...] = nm
        nv_ref[...] = nv

    blk = pl.BlockSpec((tr, C), lambda i: (i, 0))
    out = _sds((R, C), F32)
    return _call(body, name=name, out_shape=(out, out, out), grid=(R // tr,), in_specs=[blk] * 4,
                 out_specs=(blk, blk, blk), sem=("parallel",), vmem_mb=40)(w, g, m, v)


def _adamw_slabs(w, slabs, m, v, name, tr, tc):
    depth, R, C = w.shape
    assert R % tr == 0 and C % tc == 0

    def body(w_ref, s_ref, m_ref, v_ref, g_ref, d_ref, nm_ref, nv_ref):
        g = s_ref[0, 0].astype(F32)
        for k in range(1, N_DEV):
            g = g + s_ref[0, k].astype(F32)
        d, nm, nv = _adamw_math(w_ref[0], g, m_ref[0], v_ref[0])
        g_ref[0] = g
        d_ref[0] = d
        nm_ref[0] = nm
        nv_ref[0] = nv

    blk = pl.BlockSpec((1, tr, tc), lambda l, i, j: (l, i, j))
    out = _sds((depth, R, C), F32)
    return _call(body, name=name, out_shape=(out, out, out, out), grid=(depth, R // tr, C // tc),
                 in_specs=[blk, pl.BlockSpec((1, N_DEV, tr, tc), lambda l, i, j: (l, 0, i, j)), blk, blk],
                 out_specs=(blk, blk, blk, blk), sem=("parallel", "parallel", "parallel"), vmem_mb=56)(w, slabs, m, v)


PACK_QUANTUM = 512 * LANES


def _pack(arrays):
    flat = jnp.concatenate([a.reshape(-1).astype(F32) for a in arrays])
    pad = (-flat.shape[0]) % PACK_QUANTUM
    return jnp.pad(flat, (0, pad)).reshape(-1, LANES)


def _unpack(bundle, shapes):
    flat = bundle.reshape(-1)
    out, off = [], 0
    for shp in shapes:
        n = int(np.prod(shp))
        out.append(flat[off:off + n].reshape(shp))
        off += n
    return out


def _row(v):
    return v.reshape(1, -1)


def _pad_rows(a, rows):
    return jnp.pad(a, ((0, rows - a.shape[0]), (0, 0)))


def kernel(x, c, ctx, c_ctx, w_ada, b_ada, g_pre, g_post, w_in, conv_w, conv_b, dt_bias, a_log, d_skip, g_ssd, g_v, w_s, b_s, g_mlp, w_out, loss_target, m_c_ctx, m_w_ada, m_b_ada, m_g_pre, m_g_post, m_w_in, m_conv_w, m_conv_b, m_dt_bias, m_a_log, m_d_skip, m_g_ssd, m_g_v, m_w_s, m_b_s, m_g_mlp, m_w_out, v_c_ctx, v_w_ada, v_b_ada, v_g_pre, v_g_post, v_w_in, v_conv_w, v_conv_b, v_dt_bias, v_a_log, v_d_skip, v_g_ssd, v_g_v, v_w_s, v_b_s, v_g_mlp, v_w_out):
    depth = w_in.shape[0]
    L = x.shape[1]
    n_ctx = ctx.shape[1]
    T = n_ctx + L
    Dm = D_MODEL
    me = _lin(_my_pos())
    ada_w = w_ada.shape[2]
    in_w = w_in.shape[2]
    out_r = w_out.shape[1]
    conv_c = conv_w.shape[2]
    TM = 768

    c_all = _all_gather_small(_pad_rows(c, 8), "gather_c")[:, 0, :]
    c16 = _pad_rows(jnp.concatenate([c_all, _row(c_ctx)], axis=0), ADA_ROWS)
    b_loc = lax.dynamic_slice_in_dim(b_ada, me * ada_w, ada_w, axis=1)[:, None, :]
    mod_loc = _ada_fwd(c16, w_ada, b_loc)
    mod_all = _all_gather_small(mod_loc.reshape(depth * ADA_ROWS, ada_w), "gather_mod")
    mod_all = mod_all.reshape(N_DEV, depth, ADA_ROWS, ada_w)
    mod_me = lax.dynamic_index_in_dim(mod_all, me, axis=2, keepdims=False)
    mod_me = jnp.transpose(mod_me, (1, 0, 2)).reshape(depth, N_DEV * ada_w)
    mod_cx = jnp.transpose(mod_all[:, :, 8, :], (1, 0, 2)).reshape(depth, N_DEV * ada_w)
    shift2 = jnp.stack([mod_cx[:, 0:Dm], mod_me[:, 0:Dm]], axis=1)
    scale2 = jnp.stack([mod_cx[:, Dm:2 * Dm], mod_me[:, Dm:2 * Dm]], axis=1)
    gate2 = jnp.stack([mod_cx[:, 2 * Dm:], mod_me[:, 2 * Dm:]], axis=1)

    w_in_t, m_w_in_t, v_w_in_t = (jnp.swapaxes(a, 1, 2) for a in (w_in, m_w_in, v_w_in))
    w_in_bf = w_in_t.astype(BF16)
    w_out_bf = w_out.astype(BF16)
    w_in_g = _all_gather_big(w_in_bf[0], "gather_w_in")
    w_out_g = _all_gather_big(w_out_bf[0], "gather_w_out")
    conv_all = _all_gather_small(_pad_rows(conv_w.reshape(depth * CONV_W, conv_c), 24).reshape(24, conv_c),
                                 "gather_conv_w")
    conv_full = jnp.transpose(conv_all[:, :depth * CONV_W, :], (1, 0, 2)).reshape(depth, CONV_W, XBC_W)

    def in_weights(gathered):
        wf = gathered.reshape(IN_W, Dm)
        return (jnp.concatenate([wf[:XBC_W], wf[XBC_W + DT_W:]], axis=0),
                jnp.pad(wf[XBC_W:XBC_W + DT_W], ((0, LANES - DT_W), (0, 0))))

    w_main, w_dt, w_o = [None] * depth, [None] * depth, [None] * depth

    def ssd_prm(l):
        rows = jnp.stack([jnp.pad(dt_bias[l].reshape(-1), (0, LANES - DT_W)),
                          jnp.pad(a_log[l].reshape(-1), (0, LANES - DT_W)),
                          jnp.pad(d_skip[l], (0, LANES - SSD_HEADS))])
        return _pad_rows(rows, 8)

    ws_bf = w_s.astype(BF16)
    bst = jnp.pad(jnp.transpose(b_s, (0, 2, 1)), ((0, 0), (0, 0), (0, LANES - MLP_GROUPS)))

    X = jnp.concatenate([ctx[0], x[0]], axis=0)
    saved = []
    for l in range(depth):
        w_main[l], w_dt[l] = in_weights(w_in_g)
        w_o[l] = w_out_g.reshape(N_DEV * out_r, Dm)
        hx = _prenorm_fwd(X, _row(g_pre[l]), scale2[l], shift2[l], n_ctx)
        if l + 1 < depth:
            z, raw, w_in_g = _mm(hx, w_main[l], "nt", BF16, "in_proj_gather", TM, 1024, 2048,
                                 exchange=("gather", w_in_bf[l + 1]), side=w_dt[l])
        else:
            z, raw = _mm(hx, w_main[l], "nt", BF16, "in_proj", TM, 1024, 2048, side=w_dt[l])
        cw = _pad_rows(conv_full[l], 8)
        cb = _row(conv_b[l])
        xbc = _conv_fwd(z, cw, cb, n_ctx)
        prm = ssd_prm(l)
        prep = _ssd_prep(raw, prm)
        dskx = _pad_rows(_row(jnp.repeat(d_skip[l], SSD_HEAD_DIM)), 8)
        y_f, sst_f = _ssd_fwd(xbc, prep, dskx, 0, n_ctx)
        y_b, sst_b = _ssd_fwd(xbc, prep, dskx, 1, n_ctx)
        ycat = _gate_fwd(y_f, y_b, z, _row(g_ssd[l]))
        ycat = _mlp_fwd(z, _row(g_v[l]), ws_bf[l], bst[l], _row(g_mlp[l]), ycat)
        if l + 1 < depth:
            o, w_out_g = _mm(ycat, w_o[l], "nn", F32, "out_proj_gather", TM, 1024, 4096,
                             exchange=("gather", w_out_bf[l + 1]))
        else:
            o = _mm(ycat, w_o[l], "nn", F32, "out_proj", TM, 1024, 4096)
        saved.append((X, hx, z, raw, cw, cb, xbc, prm, prep, dskx, y_f, sst_f, y_b, sst_b, ycat, o))
        X = _post_fwd(X, o, _row(g_post[l]), gate2[l], n_ctx)

    dX, sq = _loss_kernel(X, loss_target[0], n_ctx)
    loss = lax.psum(0.5 * jnp.sum(sq) / Dm, ("x", "y", "c"))

    g_small = {k: [None] * depth for k in
               ("b_ada", "g_pre", "g_post", "conv_b", "dt_bias", "a_log", "d_skip", "g_ssd", "g_v", "w_s", "b_s",
                "g_mlp", "conv_w", "dmod_c", "dmod")}
    in_recv = lax.empty((depth, N_DEV, in_w, Dm), BF16)
    out_recv = lax.empty((depth, N_DEV, out_r, Dm), BF16)
    for l in reversed(range(depth)):
        Xl, hx, z, raw, cw, cb, xbc, prm, prep, dskx, y_f, sst_f, y_b, sst_b, ycat, o = saved[l]
        d_o, acc_post = _post_bwd(o, _row(g_post[l]), gate2[l], dX, n_ctx)
        dycat = _mm(d_o, w_o[l], "nt", BF16, "out_proj_dx", TM, 1024, 2048)
        out_slabs = _mm(ycat, d_o, "tn", BF16, "out_proj_dw", 1024, 1024, 2816).reshape(N_DEV, out_r, Dm)
        dz = lax.empty((T, Z_MAIN), BF16)
        dy, dz, acc_gate = _gate_bwd(y_f, y_b, z, _row(g_ssd[l]), dycat, dz)
        dz, acc_mlp, dws, dbst = _mlp_bwd(z, _row(g_v[l]), ws_bf[l], bst[l], _row(g_mlp[l]), dycat, dz)
        dx0, db0, dc0, ds0, ddt0, dtot0, daz0, dsk0 = _ssd_bwd(xbc, prep, dskx, sst_f, dy, 0, n_ctx)
        dx1, db1, dc1, ds1, ddt1, dtot1, daz1, _ = _ssd_bwd(xbc, prep, dskx, sst_b, dy, 1, n_ctx)
        draw, dprm = _ssd_post(raw, prm, prep[0], ds0, ds1, ddt0, ddt1, dtot0, dtot1, daz0, daz1)
        gn = SSD_GROUPS * SSD_STATE
        dz, dcw_x, dcb_x = _conv_bwd(z, cw, cb, dx0, dx1, dz, 0, n_ctx)
        dz, dcw_b, dcb_b = _conv_bwd(z, cw, cb, db0, db1, dz, SSD_W, n_ctx)
        dz, dcw_c, dcb_c = _conv_bwd(z, cw, cb, dc0, dc1, dz, SSD_W + gn, n_ctx)
        dcw = jnp.concatenate([dcw_x, dcw_b, dcw_c], axis=1)
        dcb = jnp.concatenate([dcb_x, dcb_b, dcb_c], axis=1)
        gw_main, out_recv = _mm(dz, hx, "tn", BF16, "in_proj_dw_scatter", 1024, 1024, 2816,
                                exchange=("scatter", out_slabs, (out_recv, l)))
        gw_dt = _mm(draw, hx, "tn", BF16, "dt_proj_dw", LANES, 1024, 2816)
        in_slabs = jnp.concatenate([gw_main[:XBC_W], gw_dt[:DT_W], gw_main[XBC_W:]], axis=0).reshape(N_DEV, in_w, Dm)
        dhx, in_recv = _mm(dz, w_main[l], "nn", F32, "in_proj_dx_scatter", TM, 1024, 4096,
                           exchange=("scatter", in_slabs, (in_recv, l)), add=(draw, w_dt[l]))
        dX, acc_pre = _prenorm_bwd(Xl, _row(g_pre[l]), scale2[l], shift2[l], dhx, dX, n_ctx)

        dmod_c = jnp.concatenate([acc_pre[3], acc_pre[1], acc_post[1]])
        dmod_x = jnp.concatenate([acc_pre[4], acc_pre[2], acc_post[2]])
        g_small["dmod_c"][l] = dmod_c
        g_small["dmod"][l] = dmod_x
        g_small["b_ada"][l] = dmod_c + dmod_x
        g_small["g_pre"][l] = acc_pre[0]
        g_small["g_post"][l] = acc_post[0]
        g_small["conv_b"][l] = dcb[0]
        g_small["conv_w"][l] = dcw[:CONV_W]
        g_small["dt_bias"][l] = dprm[0, :DT_W].reshape(2, SSD_HEADS)
        g_small["a_log"][l] = dprm[1, :DT_W].reshape(2, SSD_HEADS)
        g_small["d_skip"][l] = dsk0[0, :SSD_HEADS]
        g_small["g_ssd"][l] = acc_gate[0]
        g_small["g_v"][l] = acc_mlp[0]
        g_small["g_mlp"][l] = acc_mlp[1]
        g_small["w_s"][l] = dws
        g_small["b_s"][l] = jnp.transpose(dbst[:, :MLP_GROUPS])

    grad_x = dX[n_ctx:][None]

    summed_names = ["b_ada", "g_pre", "g_post", "conv_b", "dt_bias", "a_log", "d_skip", "g_ssd", "g_v", "w_s",
                    "b_s", "g_mlp", "conv_w", "dmod_c"]
    parts = [jnp.stack(g_small[k]) for k in summed_names] + [jnp.stack(g_small["dmod"])]
    shapes = [p.shape for p in parts]
    bundle = _pack(parts)
    gathered = _all_gather_big(bundle, "gather_small_grads")
    reduced = _unpack(_sum_slabs(gathered, "sum_small_grads"), shapes)
    gs = dict(zip(summed_names, reduced[:-1]))
    n_el = int(np.prod(shapes[-1]))
    off = sum(int(np.prod(s)) for s in shapes[:-1])
    dmod_all = gathered.reshape(N_DEV, -1)[:, off:off + n_el].reshape(N_DEV, depth, 3 * Dm)

    dmod_rows = jnp.concatenate([jnp.transpose(dmod_all, (1, 0, 2)), gs["dmod_c"][:, None, :]], axis=1)
    dmod_rows = lax.dynamic_slice_in_dim(dmod_rows, me * ada_w, ada_w, axis=2)
    dmod_rows = jnp.pad(dmod_rows, ((0, 0), (0, ADA_ROWS - 9), (0, 0)))
    g_w_ada, dc_part = _ada_bwd(c16, w_ada, dmod_rows)
    dc_all = _all_gather_small(dc_part, "gather_dc")
    g_c_ctx = _sum_slabs(dc_all, "sum_dc")[0]

    grads_small = {"c_ctx": g_c_ctx, "b_ada": gs["b_ada"], "g_pre": gs["g_pre"], "g_post": gs["g_post"],
                   "conv_b": gs["conv_b"], "dt_bias": gs["dt_bias"], "a_log": gs["a_log"], "d_skip": gs["d_skip"],
                   "g_ssd": gs["g_ssd"], "g_v": gs["g_v"], "w_s": gs["w_s"], "b_s": gs["b_s"], "g_mlp": gs["g_mlp"],
                   "conv_w": lax.dynamic_slice_in_dim(gs["conv_w"], me * conv_c, conv_c, axis=2)}

    g_in, d_in, nm_in, nv_in = (jnp.swapaxes(t, 1, 2) for t in
                                _adamw_slabs(w_in_t, in_recv, m_w_in_t, v_w_in_t, "adamw_w_in", in_w, 256))
    g_out, d_out, nm_out, nv_out = _adamw_slabs(w_out, out_recv, m_w_out, v_w_out, "adamw_w_out", 128, Dm)
    d_ada, nm_ada, nv_ada = _adamw(w_ada.reshape(depth * Dm, ada_w), g_w_ada.reshape(depth * Dm, ada_w),
                                   m_w_ada.reshape(depth * Dm, ada_w), v_w_ada.reshape(depth * Dm, ada_w),
                                   "adamw_w_ada", 256)

    small_names = ["c_ctx", "b_ada", "g_pre", "g_post", "conv_w", "conv_b", "dt_bias", "a_log", "d_skip", "g_ssd",
                   "g_v", "w_s", "b_s", "g_mlp"]
    small_w = dict(c_ctx=c_ctx, b_ada=b_ada, g_pre=g_pre, g_post=g_post, conv_w=conv_w, conv_b=conv_b,
                   dt_bias=dt_bias, a_log=a_log, d_skip=d_skip, g_ssd=g_ssd, g_v=g_v, w_s=w_s, b_s=b_s, g_mlp=g_mlp)
    small_m = dict(c_ctx=m_c_ctx, b_ada=m_b_ada, g_pre=m_g_pre, g_post=m_g_post, conv_w=m_conv_w, conv_b=m_conv_b,
                   dt_bias=m_dt_bias, a_log=m_a_log, d_skip=m_d_skip, g_ssd=m_g_ssd, g_v=m_g_v, w_s=m_w_s,
                   b_s=m_b_s, g_mlp=m_g_mlp)
    small_v = dict(c_ctx=v_c_ctx, b_ada=v_b_ada, g_pre=v_g_pre, g_post=v_g_post, conv_w=v_conv_w, conv_b=v_conv_b,
                   dt_bias=v_dt_bias, a_log=v_a_log, d_skip=v_d_skip, g_ssd=v_g_ssd, g_v=v_g_v, w_s=v_w_s,
                   b_s=v_b_s, g_mlp=v_g_mlp)
    s_shapes = [small_w[k].shape for k in small_names]
    d_s, nm_s, nv_s = _adamw(_pack([small_w[k] for k in small_names]),
                             _pack([grads_small[k].reshape(small_w[k].shape) for k in small_names]),
                             _pack([small_m[k] for k in small_names]), _pack([small_v[k] for k in small_names]),
                             "adamw_small", 512)
    d_s = dict(zip(small_names, _unpack(d_s, s_shapes)))
    nm_s = dict(zip(small_names, _unpack(nm_s, s_shapes)))
    nv_s = dict(zip(small_names, _unpack(nv_s, s_shapes)))

    big = {"w_ada": (g_w_ada, d_ada.reshape(w_ada.shape), nm_ada.reshape(w_ada.shape), nv_ada.reshape(w_ada.shape)),
           "w_in": tuple(t.reshape(w_in.shape) for t in (g_in, d_in, nm_in, nv_in)),
           "w_out": tuple(t.reshape(w_out.shape) for t in (g_out, d_out, nm_out, nv_out))}
    order = ["c_ctx", "w_ada", "b_ada", "g_pre", "g_post", "w_in", "conv_w", "conv_b", "dt_bias", "a_log", "d_skip",
             "g_ssd", "g_v", "w_s", "b_s", "g_mlp", "w_out"]

    def pick(k, idx):
        if k in big:
            return big[k][idx]
        return (grads_small[k].reshape(small_w[k].shape), d_s[k], nm_s[k], nv_s[k])[idx]

    return (loss, grad_x, *[pick(k, 0) for k in order], *[pick(k, 1) for k in order],
            *[pick(k, 2) for k in order], *[pick(k, 3) for k in order])
```

```python
import numpy as np
import jax
import jax.numpy as jnp
from jax import lax
from jax.experimental import pallas as pl
from jax.experimental.pallas import tpu as pltpu

F32 = jnp.float32
BF16 = jnp.bfloat16

D_MODEL = 2048
GRID_W = 64
SSD_W = 2048
SSD_HEADS = 32
SSD_HEAD_DIM = 64
SSD_GROUPS = 8
HEADS_PER_GROUP = 4
SSD_STATE = 128
CHUNK = 128
SSD_Q = 256
CONV_W = 5
MLP_W = 2048
MLP_GROUPS = 16
XBC_W = 4096
DT_W = 64
IN_W = 12352
Z_MAIN = IN_W - DT_W
GROUP_COLS = HEADS_PER_GROUP * SSD_HEAD_DIM
EPS = 1e-6
N_DEV = 8

ADAM_LR = 0.001
ADAM_B1 = 0.9
ADAM_B2 = 0.999
ADAM_EPS = 1e-08
ADAM_WD = 0.01
ADAM_STEP = 10

LANES = 128
NEG_BIG = -1e30

NN = (((1,), (0,)), ((), ()))
NT = (((1,), (1,)), ((), ()))
TN = (((0,), (0,)), ((), ()))
HI = lax.Precision.HIGHEST


def _dot(a, b, dims):
    return lax.dot_general(a.astype(BF16), b.astype(BF16), dims, preferred_element_type=F32)


def _dot_hi(a, b, dims=NN):
    return lax.dot_general(a, b, dims, preferred_element_type=F32, precision=HI)


def _sigmoid(x):
    return 1.0 / (1.0 + jnp.exp(-x))


def _silu(x):
    return x * _sigmoid(x)


def _softplus(x):
    return jnp.maximum(x, 0.0) + jnp.log(1.0 + jnp.exp(-jnp.abs(x)))


def _call(body, *, name, out_shape, grid=None, in_specs=None, out_specs=None, scratch=(),
          sem=None, vmem_mb=None, aliases=None):
    params = {}
    if sem is not None:
        params["dimension_semantics"] = sem
    if vmem_mb is not None:
        params["vmem_limit_bytes"] = vmem_mb << 20
    kw = {}
    if grid is not None:
        kw["grid"] = grid
    if in_specs is not None:
        kw["in_specs"] = in_specs
    if out_specs is not None:
        kw["out_specs"] = out_specs
    return pl.pallas_call(body, name=name, out_shape=out_shape, scratch_shapes=list(scratch),
                          input_output_aliases=aliases or {},
                          compiler_params=pltpu.CompilerParams(**params), **kw)


def _sds(shape, dtype):
    return jax.ShapeDtypeStruct(tuple(shape), dtype)


ANY = pl.BlockSpec(memory_space=pl.ANY)


def _my_pos():
    return lax.axis_index("x"), lax.axis_index("y"), lax.axis_index("c")


def _flip(v, bit):
    return 1 - v if bit else v


def _peer(pos, k):
    mx, my, mc = pos
    return (_flip(mx, (k >> 2) & 1), _flip(my, (k >> 1) & 1), _flip(mc, k & 1))


def _lin(pos):
    return 4 * pos[0] + 2 * pos[1] + pos[2]


def _all_gather_small(x, name):
    R, C = x.shape

    def body(x_ref, o_ref, send_sems, recv_sems):
        me = _my_pos()
        o_ref[_lin(me)] = x_ref[...]
        sends = []
        for k in range(1, N_DEV):
            peer = _peer(me, k)
            cp = pltpu.make_async_remote_copy(
                src_ref=x_ref, dst_ref=o_ref.at[_lin(me)], send_sem=send_sems.at[k - 1],
                recv_sem=recv_sems.at[k - 1], device_id=peer, device_id_type=pl.DeviceIdType.MESH)
            cp.start()
            sends.append(cp)
        for k in range(1, N_DEV):
            peer = _peer(me, k)
            pltpu.make_async_remote_copy(
                src_ref=x_ref, dst_ref=o_ref.at[_lin(peer)], send_sem=send_sems.at[k - 1],
                recv_sem=recv_sems.at[k - 1], device_id=peer,
                device_id_type=pl.DeviceIdType.MESH).wait_recv()
        for cp in sends:
            cp.wait_send()

    return _call(body, name=name, out_shape=_sds((N_DEV, R, C), x.dtype),
                 in_specs=[pl.BlockSpec(memory_space=pltpu.VMEM)],
                 out_specs=pl.BlockSpec(memory_space=pltpu.VMEM),
                 scratch=[pltpu.SemaphoreType.DMA((N_DEV - 1,)), pltpu.SemaphoreType.DMA((N_DEV - 1,))],
                 vmem_mb=40)(x)


def _all_gather_big(x, name):
    def body(x_ref, o_ref, send_sems, recv_sems, local_sem):
        mx, my, mc = _my_pos()
        me, sibling = (mx, my, mc), (mx, my, 1 - mc)
        chips = [(1 - mx, my), (mx, 1 - my), (1 - mx, 1 - my)]

        def slot(pos):
            return o_ref.at[_lin(pos)]

        def copy(k, block, to, src=None):
            return pltpu.make_async_remote_copy(
                src_ref=slot(block) if src is None else src, dst_ref=slot(block),
                send_sem=send_sems.at[k], recv_sem=recv_sems.at[k], device_id=to,
                device_id_type=pl.DeviceIdType.MESH)

        mine = pltpu.make_async_copy(x_ref, slot(me), local_sem)
        mine.start()
        first = [copy(0, me, sibling, src=x_ref)]
        first += [copy(1 + j, me, (*chip, mc), src=x_ref) for j, chip in enumerate(chips)]
        for cp in first:
            cp.start()
        passed = [copy(4 + j, (*chip, mc), sibling) for j, chip in enumerate(chips)]
        for j, chip in enumerate(chips):
            copy(1 + j, (*chip, mc), me).wait_recv()
            passed[j].start()
        copy(0, sibling, me).wait_recv()
        for j, chip in enumerate(chips):
            copy(4 + j, (*chip, 1 - mc), me).wait_recv()
        for cp in first + passed:
            cp.wait_send()
        mine.wait()

    return _call(body, name=name, out_shape=_sds((N_DEV,) + x.shape, x.dtype),
                 in_specs=[ANY], out_specs=ANY,
                 scratch=[pltpu.SemaphoreType.DMA((7,)), pltpu.SemaphoreType.DMA((7,)),
                          pltpu.SemaphoreType.DMA])(x)


EXCHANGE_SEMS = [pltpu.SemaphoreType.DMA((N_DEV - 1,)), pltpu.SemaphoreType.DMA((N_DEV - 1,)),
                 pltpu.SemaphoreType.DMA]


def _exchange_copies(kind, src_ref, dst_ref, send_sems, recv_sems, local_sem, with_arrivals=True):
    me = _my_pos()
    mine_src = src_ref if kind == "gather" else src_ref.at[_lin(me)]
    local = pltpu.make_async_copy(mine_src, dst_ref.at[_lin(me)], local_sem)
    sends, recvs = [], []
    for k in range(1, N_DEV):
        peer = _peer(me, k)
        out_src = src_ref if kind == "gather" else src_ref.at[_lin(peer)]
        sends.append(pltpu.make_async_remote_copy(
            src_ref=out_src, dst_ref=dst_ref.at[_lin(me)], send_sem=send_sems.at[k - 1],
            recv_sem=recv_sems.at[k - 1], device_id=peer, device_id_type=pl.DeviceIdType.MESH))
        if with_arrivals:
            recvs.append(pltpu.make_async_remote_copy(
                src_ref=out_src, dst_ref=dst_ref.at[_lin(peer)], send_sem=send_sems.at[k - 1],
                recv_sem=recv_sems.at[k - 1], device_id=peer, device_id_type=pl.DeviceIdType.MESH))
    return sends, recvs, local


def _exchange_start(kind, src_ref, dst_ref, send_sems, recv_sems, local_sem):
    sends, _, local = _exchange_copies(kind, src_ref, dst_ref, send_sems, recv_sems, local_sem, with_arrivals=False)
    local.start()
    for cp in sends:
        cp.start()


def _exchange_wait(kind, src_ref, dst_ref, send_sems, recv_sems, local_sem):
    sends, recvs, local = _exchange_copies(kind, src_ref, dst_ref, send_sems, recv_sems, local_sem)
    for cp in recvs:
        cp.wait_recv()
    for cp in sends:
        cp.wait_send()
    local.wait()


def _exchange_shape(kind, src):
    return _sds((N_DEV,) + src.shape if kind == "gather" else src.shape, src.dtype)


def _mm(a, b, mode, out_dtype, name, tm, tn, tk, exchange=None, add=None, side=None):
    if mode == "nn":
        (M, K), (K2, N) = a.shape, b.shape
    elif mode == "nt":
        (M, K), (N, K2) = a.shape, b.shape
    else:
        (K, M), (K2, N) = a.shape, b.shape
    assert K == K2
    tm, tn, tk = min(tm, M), min(tn, N), min(tk, K)
    assert M % tm == 0 and N % tn == 0 and K % tk == 0, (name, M, N, K, tm, tn, tk)
    ni, nj, nk = M // tm, N // tn, K // tk
    dims = {"nn": NN, "nt": NT, "tn": TN}[mode]
    assert side is None or (nk == 1 and mode != "tn")

    if mode == "tn":
        a_spec = pl.BlockSpec((tk, tm), lambda i, j, k: (k, i))
    else:
        a_spec = pl.BlockSpec((tm, tk), lambda i, j, k: (i, k))
    if mode == "nt":
        b_spec = pl.BlockSpec((tn, tk), lambda i, j, k: (j, k))
    else:
        b_spec = pl.BlockSpec((tk, tn), lambda i, j, k: (k, j))
    operands, in_specs = [a, b], [a_spec, b_spec]
    out_shape, out_specs = [_sds((M, N), out_dtype)], [pl.BlockSpec((tm, tn), lambda i, j, k: (i, j))]
    if add is not None:
        k2 = add[0].shape[1]
        operands += list(add)
        in_specs += [pl.BlockSpec((tm, k2), lambda i, j, k: (i, 0)), pl.BlockSpec((k2, tn), lambda i, j, k: (0, j))]
    if side is not None:
        n3 = side.shape[0]
        operands.append(side)
        in_specs.append(pl.BlockSpec((n3, tk), lambda i, j, k: (0, 0)))
        out_shape.append(_sds((M, n3), F32))
        out_specs.append(pl.BlockSpec((tm, n3), lambda i, j, k: (i, 0)))
    scratch = [] if nk == 1 else [pltpu.VMEM((tm, tn), F32)]
    aliases = {}
    if exchange is not None:
        kind, src = exchange[:2]
        operands.append(src)
        in_specs.append(ANY)
        if len(exchange) == 3:
            into, into_idx = exchange[2]
            aliases = {len(operands): len(out_shape)}
            operands.append(into)
            in_specs.append(ANY)
            out_shape.append(_sds(into.shape, into.dtype))
        else:
            into_idx = None
            out_shape.append(_exchange_shape(kind, src))
        out_specs.append(ANY)
        scratch = EXCHANGE_SEMS + scratch
    n_in, n_out = len(operands), len(out_shape)

    def body(*refs):
        ins, outs, scr = list(refs[:n_in]), list(refs[n_in:n_in + n_out]), list(refs[n_in + n_out:])
        a_ref, b_ref, o_ref = ins[0], ins[1], outs[0]
        add_refs = ins[2:4] if add is not None else None
        side_ref = ins[2 + (2 if add is not None else 0)] if side is not None else None
        i, j, k = pl.program_id(0), pl.program_id(1), pl.program_id(2)
        if exchange is not None:
            sems, scr = scr[:3], scr[3:]
            src_ref = ins[-1] if into_idx is None else ins[-2]
            dst_ref = outs[-1] if into_idx is None else outs[-1].at[into_idx]

            @pl.when(jnp.logical_and(jnp.logical_and(i == 0, j == 0), k == 0))
            def _():
                _exchange_start(kind, src_ref, dst_ref, *sems)

        p = _dot(a_ref[...], b_ref[...], dims)
        if nk == 1:
            if add_refs is not None:
                p = p + _dot(add_refs[0][...], add_refs[1][...], NN)
            o_ref[...] = p.astype(out_dtype)
        else:
            acc_ref = scr[0]

            @pl.when(k == 0)
            def _():
                first = p if add_refs is None else p + _dot(add_refs[0][...], add_refs[1][...], NN)
                acc_ref[...] = first

            @pl.when(k > 0)
            def _():
                acc_ref[...] += p

            @pl.when(k == nk - 1)
            def _():
                o_ref[...] = acc_ref[...].astype(out_dtype)

        if side is not None:
            @pl.when(j == 0)
            def _():
                outs[1][...] = _dot(a_ref[...], side_ref[...], NT)

        if exchange is not None:
            @pl.when(jnp.logical_and(jnp.logical_and(i == ni - 1, j == nj - 1), k == nk - 1))
            def _():
                _exchange_wait(kind, src_ref, dst_ref, *sems)

    plain = exchange is None and side is None
    res = _call(body, name=name, out_shape=tuple(out_shape), grid=(ni, nj, nk), in_specs=in_specs,
                out_specs=tuple(out_specs), scratch=scratch, aliases=aliases,
                sem=("parallel", "parallel", "arbitrary") if plain else ("arbitrary",) * 3, vmem_mb=48)(*operands)
    return res[0] if n_out == 1 else res


def _rms(x):
    return lax.rsqrt(jnp.mean(x * x, axis=-1, keepdims=True) + EPS)


def _prenorm_f(x, g, sc, sh):
    return (x * _rms(x) * g) * (1.0 + sc) + sh


def _pick(is_ctx, ref):
    return jnp.where(is_ctx, ref[0:1, :], ref[1:2, :])


def _prenorm_fwd(X, g, sc2, sh2, n_ctx, tl=256):
    T, Dm = X.shape
    nct = n_ctx // tl

    def body(x_ref, g_ref, sc_ref, sh_ref, o_ref):
        is_ctx = pl.program_id(0) < nct
        o_ref[...] = _prenorm_f(x_ref[...], g_ref[...], _pick(is_ctx, sc_ref),
                                _pick(is_ctx, sh_ref)).astype(BF16)

    row = pl.BlockSpec((tl, Dm), lambda i: (i, 0))
    return _call(body, name="prenorm_fwd", out_shape=_sds((T, Dm), BF16), grid=(T // tl,),
                 in_specs=[row, pl.BlockSpec((1, Dm), lambda i: (0, 0)),
                           pl.BlockSpec((2, Dm), lambda i: (0, 0)), pl.BlockSpec((2, Dm), lambda i: (0, 0))],
                 out_specs=row, sem=("parallel",), vmem_mb=40)(X, g, sc2, sh2)


def _prenorm_bwd(X, g, sc2, sh2, d1, dres, n_ctx, tl=256):
    T, Dm = X.shape
    nct = n_ctx // tl

    def body(x_ref, g_ref, sc_ref, sh_ref, d1_ref, dres_ref, dx_ref, acc_ref):
        i = pl.program_id(0)
        is_ctx = i < nct

        @pl.when(i == 0)
        def _():
            acc_ref[...] = jnp.zeros_like(acc_ref)

        _, vjp = jax.vjp(_prenorm_f, x_ref[...], g_ref[...], _pick(is_ctx, sc_ref), _pick(is_ctx, sh_ref))
        dx, dg, dsc, dsh = vjp(d1_ref[...])
        dx_ref[...] = dres_ref[...] + dx
        zero = jnp.zeros_like(dsc)
        acc_ref[0:1, :] += dg
        acc_ref[1:2, :] += jnp.where(is_ctx, dsc, zero)
        acc_ref[2:3, :] += jnp.where(is_ctx, zero, dsc)
        acc_ref[3:4, :] += jnp.where(is_ctx, dsh, zero)
        acc_ref[4:5, :] += jnp.where(is_ctx, zero, dsh)

    row = pl.BlockSpec((tl, Dm), lambda i: (i, 0))
    return _call(body, name="prenorm_bwd", out_shape=(_sds((T, Dm), F32), _sds((8, Dm), F32)), grid=(T // tl,),
                 in_specs=[row, pl.BlockSpec((1, Dm), lambda i: (0, 0)),
                           pl.BlockSpec((2, Dm), lambda i: (0, 0)), pl.BlockSpec((2, Dm), lambda i: (0, 0)),
                           row, row],
                 out_specs=(row, pl.BlockSpec((8, Dm), lambda i: (0, 0))), sem=("arbitrary",),
                 vmem_mb=48)(X, g, sc2, sh2, d1, dres)


def _post_f(o, g, gate):
    return gate * ((o * _rms(o)) * g)


def _post_fwd(X, o, g, gate2, n_ctx, tl=256):
    T, Dm = X.shape
    nct = n_ctx // tl

    def body(x_ref, o_ref, g_ref, gate_ref, y_ref):
        is_ctx = pl.program_id(0) < nct
        y_ref[...] = x_ref[...] + _post_f(o_ref[...], g_ref[...], _pick(is_ctx, gate_ref))

    row = pl.BlockSpec((tl, Dm), lambda i: (i, 0))
    return _call(body, name="post_fwd", out_shape=_sds((T, Dm), F32), grid=(T // tl,),
                 in_specs=[row, row, pl.BlockSpec((1, Dm), lambda i: (0, 0)), pl.BlockSpec((2, Dm), lambda i: (0, 0))],
                 out_specs=row, sem=("parallel",), vmem_mb=40)(X, o, g, gate2)


def _post_bwd(o, g, gate2, dX, n_ctx, tl=256):
    T, Dm = o.shape
    nct = n_ctx // tl

    def body(o_ref, g_ref, gate_ref, dx_ref, do_ref, acc_ref):
        i = pl.program_id(0)
        is_ctx = i < nct

        @pl.when(i == 0)
        def _():
            acc_ref[...] = jnp.zeros_like(acc_ref)

        _, vjp = jax.vjp(_post_f, o_ref[...], g_ref[...], _pick(is_ctx, gate_ref))
        do, dg, dgate = vjp(dx_ref[...])
        do_ref[...] = do.astype(BF16)
        zero = jnp.zeros_like(dgate)
        acc_ref[0:1, :] += dg
        acc_ref[1:2, :] += jnp.where(is_ctx, dgate, zero)
        acc_ref[2:3, :] += jnp.where(is_ctx, zero, dgate)

    row = pl.BlockSpec((tl, Dm), lambda i: (i, 0))
    return _call(body, name="post_bwd", out_shape=(_sds((T, Dm), BF16), _sds((8, Dm), F32)), grid=(T // tl,),
                 in_specs=[row, pl.BlockSpec((1, Dm), lambda i: (0, 0)), pl.BlockSpec((2, Dm), lambda i: (0, 0)), row],
                 out_specs=(row, pl.BlockSpec((8, Dm), lambda i: (0, 0))), sem=("arbitrary",),
                 vmem_mb=48)(o, g, gate2, dX)


def _loss_kernel(X, target, n_ctx, tl=256):
    T, Dm = X.shape
    nct = n_ctx // tl

    def body(x_ref, t_ref, dx_ref, acc_ref):
        i = pl.program_id(0)

        @pl.when(i == 0)
        def _():
            acc_ref[...] = jnp.zeros_like(acc_ref)

        @pl.when(i < nct)
        def _():
            dx_ref[...] = jnp.zeros_like(dx_ref)

        @pl.when(i >= nct)
        def _():
            d = x_ref[...] - t_ref[...]
            dx_ref[...] = d * (1.0 / Dm)
            acc_ref[...] += jnp.sum(d * d, axis=0, keepdims=True)

    row = pl.BlockSpec((tl, Dm), lambda i: (i, 0))
    trow = pl.BlockSpec((tl, Dm), lambda i: (jnp.maximum(i - nct, 0), 0))
    return _call(body, name="loss", out_shape=(_sds((T, Dm), F32), _sds((1, Dm), F32)), grid=(T // tl,),
                 in_specs=[row, trow], out_specs=(row, pl.BlockSpec((1, Dm), lambda i: (0, 0))),
                 sem=("arbitrary",), vmem_mb=40)(X, target)


CONV_TL = 256
CONV_CB = 1024


def _conv_taps(x, reverse):
    rows = x.shape[0]
    pos = lax.broadcasted_iota(jnp.int32, (rows, 1), 0)
    taps = []
    for k in range(CONV_W):
        off = (2 - k) if reverse else (k - 2)
        if off == 0:
            taps.append(x)
            continue
        xs = pltpu.roll(x, (-off) % rows, 0)
        valid = (pos < rows - off) if off > 0 else (pos >= -off)
        taps.append(jnp.where(valid, xs, 0.0))
    return taps


def _conv_pre(taps, w, b):
    pre = b + taps[0] * w[0:1, :]
    for k in range(1, CONV_W):
        pre = pre + taps[k] * w[k:k + 1, :]
    return pre


def _conv_subtiles(tl, cbw, row_len):
    return [(r0, c0) for c0 in range(0, cbw, LANES) for r0 in range(0, tl, row_len)]


def _conv_fwd(z, cw, cb, n_ctx):
    T = z.shape[0]
    tl, cbw = CONV_TL, CONV_CB
    assert n_ctx == tl

    def body(z_ref, w_ref, b_ref, o_ref):
        def run(row_len):
            for r0, c0 in _conv_subtiles(tl, cbw, row_len):
                rs, cs = pl.ds(r0, row_len), pl.ds(c0, LANES)
                pre = _conv_pre(_conv_taps(z_ref[rs, cs].astype(F32), False), w_ref[:, cs], b_ref[:, cs])
                o_ref[rs, cs] = _silu(pre).astype(BF16)

        @pl.when(pl.program_id(1) == 0)
        def _():
            run(n_ctx)

        @pl.when(pl.program_id(1) > 0)
        def _():
            run(GRID_W)

    blk = pl.BlockSpec((tl, cbw), lambda j, i: (i, j))
    return _call(body, name="conv_fwd", out_shape=_sds((T, XBC_W), BF16), grid=(XBC_W // cbw, T // tl),
                 in_specs=[blk, pl.BlockSpec((8, cbw), lambda j, i: (0, j)), pl.BlockSpec((1, cbw), lambda j, i: (0, j))],
                 out_specs=blk, sem=("parallel", "parallel"))(z, cw, cb)


def _conv_bwd(z, cw, cb, d0, d1, dz_in, col0, n_ctx):
    T = z.shape[0]
    tl, cbw = CONV_TL, CONV_CB

    def body(z_ref, w_ref, b_ref, d0_ref, d1_ref, dzin_ref, dz_ref, dw_ref, db_ref):
        i = pl.program_id(1)

        @pl.when(i == 0)
        def _():
            dw_ref[...] = jnp.zeros_like(dw_ref)
            db_ref[...] = jnp.zeros_like(db_ref)

        def run(row_len):
            for c0 in range(0, cbw, LANES):
                cs = pl.ds(c0, LANES)
                w = w_ref[:, cs]
                dw = [jnp.zeros((1, LANES), F32) for _ in range(CONV_W)]
                db = jnp.zeros((1, LANES), F32)
                for r0 in range(0, tl, row_len):
                    rs = pl.ds(r0, row_len)
                    taps = _conv_taps(z_ref[rs, cs].astype(F32), False)
                    pre = _conv_pre(taps, w, b_ref[:, cs])
                    s = _sigmoid(pre)
                    dpre = (d0_ref[rs, cs].astype(F32) + d1_ref[rs, cs].astype(F32)) * (s + pre * s * (1.0 - s))
                    dz_ref[rs, cs] = _conv_pre(_conv_taps(dpre, True), w, 0.0).astype(BF16)
                    for k in range(CONV_W):
                        dw[k] = dw[k] + jnp.sum(dpre * taps[k], axis=0, keepdims=True)
                    db = db + jnp.sum(dpre, axis=0, keepdims=True)
                for k in range(CONV_W):
                    dw_ref[k:k + 1, cs] += dw[k]
                db_ref[0:1, cs] += db

        @pl.when(i == 0)
        def _():
            run(n_ctx)

        @pl.when(i > 0)
        def _():
            run(GRID_W)

    width = d0.shape[1]
    jo = col0 // cbw
    blk = pl.BlockSpec((tl, cbw), lambda j, i: (i, jo + j))
    dblk = pl.BlockSpec((tl, cbw), lambda j, i: (i, j))
    par = pl.BlockSpec((8, cbw), lambda j, i: (0, jo + j))
    opar = pl.BlockSpec((8, cbw), lambda j, i: (0, j))
    return _call(body, name=f"conv_bwd_{col0}",
                 out_shape=(_sds(dz_in.shape, BF16), _sds((8, width), F32), _sds((8, width), F32)),
                 grid=(width // cbw, T // tl),
                 in_specs=[blk, par, pl.BlockSpec((1, cbw), lambda j, i: (0, jo + j)), dblk, dblk, ANY],
                 out_specs=(blk, opar, opar), sem=("parallel", "arbitrary"), aliases={5: 0})(z, cw, cb, d0, d1, dz_in)


MIX_TL = 128
ROW_BLOCK = 16
COL_CHUNK = 512


def _for_row_blocks(n_rows, fn):
    def step(i, carry):
        fn(pl.ds(pl.multiple_of(i * ROW_BLOCK, ROW_BLOCK), ROW_BLOCK))
        return carry

    lax.fori_loop(0, n_rows // ROW_BLOCK, step, 0)


def _col_chunks(width):
    return [pl.ds(c0, COL_CHUNK) for c0 in range(0, width, COL_CHUNK)]


def _gate_fwd(yf, yb, z, g):
    T = yf.shape[0]
    tl = MIX_TL

    def body(yf_ref, yb_ref, zs_ref, g_ref, o_ref):
        t = (yf_ref[...].astype(F32) + yb_ref[...].astype(F32)) * _silu(zs_ref[...].astype(F32))
        o_ref[...] = (t * _rms(t) * g_ref[...]).astype(BF16)

    row = pl.BlockSpec((tl, SSD_W), lambda i: (i, 0))
    return _call(body, name="gate_fwd", out_shape=_sds((T, SSD_W + MLP_W), BF16), grid=(T // tl,),
                 in_specs=[row, row, pl.BlockSpec((tl, SSD_W), lambda i: (i, XBC_W // SSD_W)),
                           pl.BlockSpec((1, SSD_W), lambda i: (0, 0))],
                 out_specs=row, sem=("parallel",))(yf, yb, z, g)


def _gate_bwd(yf, yb, z, g, dycat, dz_in):
    T = yf.shape[0]
    tl = MIX_TL

    def body(yf_ref, yb_ref, zs_ref, g_ref, d_ref, dzin_ref, dy_ref, dz_ref, acc_ref):
        @pl.when(pl.program_id(0) == 0)
        def _():
            acc_ref[...] = jnp.zeros_like(acc_ref)

        def block(rs):
            def parts(cs):
                y = yf_ref[rs, cs].astype(F32) + yb_ref[rs, cs].astype(F32)
                zs = zs_ref[rs, cs].astype(F32)
                sg = _sigmoid(zs)
                return y, zs, sg, y * (zs * sg), d_ref[rs, cs].astype(F32)

            ss = jnp.zeros((ROW_BLOCK, 1), F32)
            su = jnp.zeros((ROW_BLOCK, 1), F32)
            for cs in _col_chunks(SSD_W):
                _, _, _, t, dd = parts(cs)
                ss = ss + jnp.sum(t * t, axis=1, keepdims=True)
                su = su + jnp.sum(t * (dd * g_ref[:, cs]), axis=1, keepdims=True)
            r = lax.rsqrt(ss * (1.0 / SSD_W) + EPS)
            c = r * r * r * (su * (1.0 / SSD_W))
            for cs in _col_chunks(SSD_W):
                y, zs, sg, t, dd = parts(cs)
                dt = r * (dd * g_ref[:, cs]) - t * c
                dy_ref[rs, cs] = (dt * (zs * sg)).astype(BF16)
                dz_ref[rs, cs] = (dt * y * (sg * (1.0 + zs * (1.0 - sg)))).astype(BF16)
                acc_ref[0:1, cs] += jnp.sum(dd * t * r, axis=0, keepdims=True)

        _for_row_blocks(tl, block)

    row = pl.BlockSpec((tl, SSD_W), lambda i: (i, 0))
    zs_spec = pl.BlockSpec((tl, SSD_W), lambda i: (i, XBC_W // SSD_W))
    return _call(body, name="gate_bwd",
                 out_shape=(_sds((T, SSD_W), BF16), _sds(dz_in.shape, BF16), _sds((8, SSD_W), F32)),
                 grid=(T // tl,),
                 in_specs=[row, row, zs_spec, pl.BlockSpec((1, SSD_W), lambda i: (0, 0)), row, ANY],
                 out_specs=(row, zs_spec, pl.BlockSpec((8, SSD_W), lambda i: (0, 0))),
                 sem=("arbitrary",), aliases={5: 1}, vmem_mb=40)(yf, yb, z, g, dycat, dz_in)


def _vnorm_f(v, gv):
    return v * _rms(v) * gv


def _mlp_out_f(u, sg, zm, gm):
    t = u * sg * _silu(zm)
    return t * _rms(t) * gm


U_BLK = (XBC_W + SSD_W) // MLP_W


def _mlp_mix(ws_ref, bst_ref, vn_s, sg_s):
    for gi in range(MLP_GROUPS):
        cols = pl.ds(gi * LANES, LANES)
        sg_s[:, cols] = _dot(ws_ref[gi], vn_s[:, cols], NN) + bst_ref[:, gi:gi + 1]


def _mlp_fwd(z, gv, ws, bst, gm, ycat_in):
    T = z.shape[0]
    tl = CHUNK

    def body(u_ref, v_ref, zm_ref, gv_ref, ws_ref, bst_ref, gm_ref, yin_ref, o_ref, vn_s, sg_s):
        vn_s[...] = _vnorm_f(v_ref[...].astype(F32), gv_ref[...]).astype(BF16)
        _mlp_mix(ws_ref, bst_ref, vn_s, sg_s)
        o_ref[...] = _mlp_out_f(u_ref[...].astype(F32), sg_s[...], zm_ref[...].astype(F32),
                                gm_ref[...]).astype(BF16)

    def zblk(b):
        return pl.BlockSpec((tl, MLP_W), lambda i: (i, b))

    vec = pl.BlockSpec((1, MLP_W), lambda i: (0, 0))
    return _call(body, name="mlp_fwd", out_shape=_sds(ycat_in.shape, BF16), grid=(T // tl,),
                 in_specs=[zblk(U_BLK), zblk(U_BLK + 1), zblk(U_BLK + 2), vec,
                           pl.BlockSpec((MLP_GROUPS, CHUNK, CHUNK), lambda i: (0, 0, 0)),
                           pl.BlockSpec((CHUNK, LANES), lambda i: (0, 0)), vec, ANY],
                 out_specs=pl.BlockSpec((tl, MLP_W), lambda i: (i, 1)),
                 scratch=[pltpu.VMEM((tl, MLP_W), BF16), pltpu.VMEM((tl, MLP_W), F32)],
                 sem=("parallel",), aliases={7: 0})(z, z, z, gv, ws, bst, gm, ycat_in)


def _mlp_bwd(z, gv, ws, bst, gm, dycat, dz_in):
    T = z.shape[0]
    tl = CHUNK

    def body(u_ref, v_ref, zm_ref, gv_ref, ws_ref, bst_ref, gm_ref, d_ref, dzin_ref,
             dz_ref, acc_ref, dws_ref, dbst_ref, vn_s, sg_s, dvn_s):
        @pl.when(pl.program_id(0) == 0)
        def _():
            acc_ref[...] = jnp.zeros_like(acc_ref)
            dws_ref[...] = jnp.zeros_like(dws_ref)
            dbst_ref[...] = jnp.zeros_like(dbst_ref)

        v = v_ref[...].astype(F32)
        vn, vjp_v = jax.vjp(_vnorm_f, v, gv_ref[...])
        vn_s[...] = vn.astype(BF16)
        _mlp_mix(ws_ref, bst_ref, vn_s, sg_s)
        _, vjp_o = jax.vjp(_mlp_out_f, u_ref[...].astype(F32), sg_s[...], zm_ref[...].astype(F32), gm_ref[...])
        du, dsg, dzm, dgm = vjp_o(d_ref[...].astype(F32))
        sg_s[...] = dsg
        for gi in range(MLP_GROUPS):
            cols = pl.ds(gi * LANES, LANES)
            dsg_g = sg_s[:, cols]
            dvn_s[:, cols] = _dot(ws_ref[gi], dsg_g, TN)
            dws_ref[gi] += _dot(dsg_g, vn_s[:, cols], NT)
            dbst_ref[:, gi:gi + 1] += jnp.sum(dsg_g, axis=1, keepdims=True)
        dv, dgv = vjp_v(dvn_s[...])
        dz_ref[:, 0:MLP_W] = du.astype(BF16)
        dz_ref[:, MLP_W:2 * MLP_W] = dv.astype(BF16)
        dz_ref[:, 2 * MLP_W:3 * MLP_W] = dzm.astype(BF16)
        acc_ref[0:1, :] += dgv
        acc_ref[1:2, :] += dgm

    def zblk(b):
        return pl.BlockSpec((tl, MLP_W), lambda i: (i, b))

    vec = pl.BlockSpec((1, MLP_W), lambda i: (0, 0))
    ws_spec = pl.BlockSpec((MLP_GROUPS, CHUNK, CHUNK), lambda i: (0, 0, 0))
    bst_spec = pl.BlockSpec((CHUNK, LANES), lambda i: (0, 0))
    return _call(body, name="mlp_bwd",
                 out_shape=(_sds(dz_in.shape, BF16), _sds((8, MLP_W), F32),
                            _sds((MLP_GROUPS, CHUNK, CHUNK), F32), _sds((CHUNK, LANES), F32)),
                 grid=(T // tl,),
                 in_specs=[zblk(U_BLK), zblk(U_BLK + 1), zblk(U_BLK + 2), vec, ws_spec, bst_spec, vec,
                           pl.BlockSpec((tl, MLP_W), lambda i: (i, 1)), ANY],
                 out_specs=(pl.BlockSpec((tl, 3 * MLP_W), lambda i: (i, 1)),
                            pl.BlockSpec((8, MLP_W), lambda i: (0, 0)), ws_spec, bst_spec),
                 scratch=[pltpu.VMEM((tl, MLP_W), BF16), pltpu.VMEM((tl, MLP_W), F32), pltpu.VMEM((tl, MLP_W), F32)],
                 sem=("arbitrary",), aliases={8: 0}, vmem_mb=48)(z, z, z, gv, ws, bst, gm, dycat, dz_in)


def _ssd_constants(direction):
    q = SSD_Q
    tri = np.tril(np.ones((q, q), np.float32))
    if direction == 1:
        tri = tri.T
    ex = np.zeros((SSD_GROUPS, LANES, GROUP_COLS), np.float32)
    for g in range(SSD_GROUPS):
        for r in range(HEADS_PER_GROUP):
            ex[g, direction * SSD_HEADS + HEADS_PER_GROUP * g + r, r * SSD_HEAD_DIM:(r + 1) * SSD_HEAD_DIM] = 1.0
    hm = np.zeros((8, GROUP_COLS), np.float32)
    for r in range(HEADS_PER_GROUP):
        hm[r, r * SSD_HEAD_DIM:(r + 1) * SSD_HEAD_DIM] = 1.0
    return (jnp.asarray(tri), jnp.asarray(np.concatenate([ex, ex], axis=1), dtype=BF16),
            jnp.asarray(np.ascontiguousarray(ex.transpose(0, 2, 1)), dtype=BF16), jnp.asarray(hm, dtype=BF16))


def _dot2(v, m, dims=NN):
    hi = v.astype(BF16)
    lo = (v - hi.astype(F32)).astype(BF16)
    return (lax.dot_general(hi, m, dims, preferred_element_type=F32)
            + lax.dot_general(lo, m, dims, preferred_element_type=F32))


def _ssd_prep(raw, prm):
    T = raw.shape[0]
    nc, Q = T // SSD_Q, SSD_Q
    tril = jnp.asarray(np.tril(np.ones((Q, Q), np.float32)))
    triu = jnp.asarray(np.triu(np.ones((Q, Q), np.float32)))

    def body(raw_ref, prm_ref, tril_ref, triu_ref, dt_ref, s_ref, pk_ref, st_ref, etot_ref, pkt_ref):
        dt = _softplus(raw_ref[...] + prm_ref[0:1, :])
        a = dt * (-jnp.exp(prm_ref[1:2, :]))
        lane = lax.broadcasted_iota(jnp.int32, (1, LANES), 1)
        s = jnp.where(lane < SSD_HEADS, _dot_hi(tril_ref[...], a), _dot_hi(triu_ref[...], a))
        tot = jnp.sum(a, axis=0, keepdims=True)
        dt_ref[...] = dt
        s_ref[...] = s
        for n, v in enumerate((dt, jnp.exp(s), jnp.exp(tot - s))):
            hi = v.astype(BF16)
            pk_ref[:, 2 * n * LANES:(2 * n + 1) * LANES] = hi
            pk_ref[:, (2 * n + 1) * LANES:(2 * n + 2) * LANES] = (v - hi.astype(F32)).astype(BF16)
            pkt_ref[0, n * LANES:(n + 1) * LANES, :] = v.T
        st_ref[0] = s.T
        etot_ref[0] = _bcast8(jnp.exp(tot))

    row = pl.BlockSpec((Q, LANES), lambda c: (c, 0))
    cst = lambda shape: pl.BlockSpec(shape, lambda c: (0, 0))
    tl = _sds((T, LANES), F32)
    return _call(body, name="ssd_prep",
                 out_shape=(tl, tl, _sds((T, 6 * LANES), BF16), _sds((nc, LANES, Q), F32), _sds((nc, 8, LANES), F32),
                            _sds((nc, 3 * LANES, Q), F32)),
                 grid=(nc,), in_specs=[row, cst((8, LANES)), cst((Q, Q)), cst((Q, Q))],
                 out_specs=(row, row, pl.BlockSpec((Q, 6 * LANES), lambda c: (c, 0)),
                            pl.BlockSpec((1, LANES, Q), lambda c: (c, 0, 0)),
                            pl.BlockSpec((1, 8, LANES), lambda c: (c, 0, 0)),
                            pl.BlockSpec((1, 3 * LANES, Q), lambda c: (c, 0, 0))),
                 sem=("parallel",))(raw, prm, tril, triu)


def _ssd_post(raw, prm, dt, ds0, ds1, ddt0, ddt1, dtot0, dtot1, daz0, daz1):
    T = raw.shape[0]
    nc, Q = T // SSD_Q, SSD_Q
    tril = jnp.asarray(np.tril(np.ones((Q, Q), np.float32)))
    triu = jnp.asarray(np.triu(np.ones((Q, Q), np.float32)))

    def body(raw_ref, prm_ref, dt_ref, ds0_ref, ds1_ref, ddt0_ref, ddt1_ref, dtot0_ref, dtot1_ref, daz0_ref, daz1_ref,
             tril_ref, triu_ref, draw_ref, dprm_ref):
        @pl.when(pl.program_id(0) == 0)
        def _():
            dprm_ref[...] = jnp.zeros_like(dprm_ref)

        pre = raw_ref[...] + prm_ref[0:1, :]
        A = -jnp.exp(prm_ref[1:2, :])
        ds = ds0_ref[...] + ds1_ref[...]
        lane = lax.broadcasted_iota(jnp.int32, (1, LANES), 1)
        da = jnp.where(lane < SSD_HEADS, _dot_hi(triu_ref[...], ds), _dot_hi(tril_ref[...], ds))
        da = da + dtot0_ref[0, 0:1, :] + dtot1_ref[0, 0:1, :]
        daz_rows = jnp.concatenate([daz0_ref[0], daz1_ref[0], jnp.zeros((LANES - 2 * SSD_HEADS, Q), F32)], axis=0)
        da = da + daz_rows.T
        draw = (da * A + ddt0_ref[...] + ddt1_ref[...]) * _sigmoid(pre)
        draw_ref[...] = draw
        dprm_ref[0:1, :] += jnp.sum(draw, axis=0, keepdims=True)
        dprm_ref[1:2, :] += jnp.sum(da * dt_ref[...], axis=0, keepdims=True) * A

    row = pl.BlockSpec((Q, LANES), lambda c: (c, 0))
    tot = pl.BlockSpec((1, 8, LANES), lambda c: (c, 0, 0))
    daz = pl.BlockSpec((1, SSD_HEADS, Q), lambda c: (c, 0, 0))
    cst = lambda shape: pl.BlockSpec(shape, lambda c: (0, 0))
    return _call(body, name="ssd_post", out_shape=(_sds((T, LANES), F32), _sds((8, LANES), F32)), grid=(nc,),
                 in_specs=[row, cst((8, LANES)), row, row, row, row, row, tot, tot, daz, daz, cst((Q, Q)), cst((Q, Q))],
                 out_specs=(row, cst((8, LANES))), sem=("arbitrary",))(raw, prm, dt, ds0, ds1, ddt0, ddt1, dtot0, dtot1,
                                                                       daz0, daz1, tril, triu)


def _chunk_order(direction, nc, nctx):
    if direction == 0:
        return lambda c: c
    return lambda c: jnp.where(c < nctx, nctx - 1 - c, nc - 1 + nctx - c)


def _bcast8(row):
    return jnp.broadcast_to(row, (8, row.shape[1]))


GPS = 8


def _cols(ref, k, width):
    return ref[:, k * width:(k + 1) * width]


def _ssd_common(k, x_ref, b_ref, c_ref, s_ref, pk_ref, st_ref, etot_ref, ex_ref, direction):
    g = pl.program_id(1) * GPS + k
    base = direction * SSD_HEADS + HEADS_PER_GROUP * g
    x = _cols(x_ref, k, GROUP_COLS).astype(F32)
    ex = ex_ref[k]

    def spread(n):
        return lax.dot_general(_cols(pk_ref, n, 2 * LANES), ex, NN, preferred_element_type=F32)

    dt_x, E_x, D_x = spread(0), spread(1), spread(2)
    lane = lax.broadcasted_iota(jnp.int32, (1, LANES), 1)
    etot = etot_ref[0, 0:1, :]
    rowi = lax.broadcasted_iota(jnp.int32, (GROUP_COLS, 1), 0)
    e = [jnp.sum(jnp.where(lane == base + r, etot, 0.0), axis=1, keepdims=True) for r in range(HEADS_PER_GROUP)]
    Etot_c = jnp.where(rowi < SSD_HEAD_DIM, e[0],
                       jnp.where(rowi < 2 * SSD_HEAD_DIM, e[1], jnp.where(rowi < 3 * SSD_HEAD_DIM, e[2], e[3])))
    rows = [st_ref[0, HEADS_PER_GROUP * k + r:HEADS_PER_GROUP * k + r + 1, :] for r in range(HEADS_PER_GROUP)]
    return dict(g=g, base=base, x=x, s=s_ref[...], rows=rows, dt_x=dt_x, lane=lane, X=x * dt_x, E_x=E_x, D_x=D_x,
                Etot_c=Etot_c, Bm=_cols(b_ref, k, SSD_STATE), Cm=_cols(c_ref, k, SSD_STATE))


def _head_decay(q, r, mask):
    sel = q["lane"] == q["base"] + r
    col = jnp.sum(jnp.where(sel, q["s"], 0.0), axis=1, keepdims=True)
    return jnp.exp(jnp.where(mask, col - q["rows"][r], NEG_BIG))


def _ssd_in_specs(order, direction):
    Q, GC, N, G = SSD_Q, GROUP_COLS, SSD_STATE, SSD_GROUPS
    row = pl.BlockSpec((Q, LANES), lambda c, g: (order(c), 0))
    hrows = HEADS_PER_GROUP * GPS
    return [
        pl.BlockSpec((Q, GPS * GC), lambda c, g: (order(c), g)),
        pl.BlockSpec((Q, GPS * N), lambda c, g: (order(c), SSD_W // (GPS * N) + g)),
        pl.BlockSpec((Q, GPS * N), lambda c, g: (order(c), (SSD_W + G * N) // (GPS * N) + g)),
        row, pl.BlockSpec((Q, 6 * LANES), lambda c, g: (order(c), 0)),
        pl.BlockSpec((1, hrows, Q), lambda c, g: (order(c), direction * (SSD_HEADS // hrows) + g, 0)),
        pl.BlockSpec((1, 8, LANES), lambda c, g: (order(c), 0, 0)),
    ]


def _ssd_fwd(xbc, prep, dskx, direction, n_ctx):
    T = xbc.shape[0]
    nc, nctx = T // SSD_Q, n_ctx // SSD_Q
    order = _chunk_order(direction, nc, nctx)
    tri_t = jnp.transpose(_ssd_constants(direction)[0])
    Q, GC, N, G, HP, HD = SSD_Q, GROUP_COLS, SSD_STATE, SSD_GROUPS, HEADS_PER_GROUP, SSD_HEAD_DIM
    _, s, _, s_t, etot, pk_t = prep

    def body(x_ref, b_ref, c_ref, s_ref, st_ref, etot_ref, dtr_ref, er_ref, dr_ref, dsk_ref, trit_ref,
             y_ref, sst_ref, S):
        c = pl.program_id(0)
        mask_t = trit_ref[...] > 0.5
        lane = lax.broadcasted_iota(jnp.int32, (1, LANES), 1)
        rowi = lax.broadcasted_iota(jnp.int32, (GC, 1), 0)
        s_all = s_ref[...]
        etot_row = etot_ref[0, 0:1, :]
        for k in range(GPS):
            g = pl.program_id(1) * GPS + k
            base = direction * SSD_HEADS + HP * g

            @pl.when(c == 0)
            def _():
                S[g] = jnp.zeros((GC, N), F32)

            S0 = S[g]
            sst_ref[0, k] = S0
            Bm, Cm = _cols(b_ref, k, N), _cols(c_ref, k, N)

            def head_rows(ref):
                return jnp.concatenate([jnp.broadcast_to(ref[0, HP * k + r:HP * k + r + 1, :], (HD, Q))
                                        for r in range(HP)], axis=0)

            x_t = jnp.transpose(_cols(x_ref, k, GC).astype(F32))
            X_t = x_t * head_rows(dtr_ref)
            X_tb = X_t.astype(BF16)
            y_t = head_rows(er_ref) * _dot(S0, Cm, NT)
            G_t = _dot(Bm, Cm, NT)
            parts = []
            for r in range(HP):
                col = jnp.sum(jnp.where(lane == base + r, s_all, 0.0), axis=1, keepdims=True)
                L_t = jnp.exp(jnp.where(mask_t, st_ref[0, HP * k + r:HP * k + r + 1, :] - col, NEG_BIG))
                parts.append(_dot(X_tb[HD * r:HD * (r + 1), :], G_t * L_t, NN))
            y_t = y_t + jnp.concatenate(parts, axis=0)
            if direction == 0:
                y_t = y_t + dsk_ref[k] * x_t
            y_ref[:, k * GC:(k + 1) * GC] = jnp.transpose(y_t).astype(BF16)
            e = [jnp.sum(jnp.where(lane == base + r, etot_row, 0.0), axis=1, keepdims=True) for r in range(HP)]
            etot_c = jnp.where(rowi < HD, e[0], jnp.where(rowi < 2 * HD, e[1], jnp.where(rowi < 3 * HD, e[2], e[3])))
            S[g] = etot_c * S0 + _dot(X_t * head_rows(dr_ref), Bm, NN)

    hrows = HP * GPS
    sec = LANES // hrows
    row_block = lambda n: pl.BlockSpec((1, hrows, Q), lambda c, g: (order(c), n * sec + direction * (SSD_HEADS // hrows) + g, 0))
    in_specs = [
        pl.BlockSpec((Q, GPS * GC), lambda c, g: (order(c), g)),
        pl.BlockSpec((Q, GPS * N), lambda c, g: (order(c), SSD_W // (GPS * N) + g)),
        pl.BlockSpec((Q, GPS * N), lambda c, g: (order(c), (SSD_W + G * N) // (GPS * N) + g)),
        pl.BlockSpec((Q, LANES), lambda c, g: (order(c), 0)),
        row_block(0),
        pl.BlockSpec((1, 8, LANES), lambda c, g: (order(c), 0, 0)),
        row_block(0), row_block(1), row_block(2),
        pl.BlockSpec((GPS, GC, Q), lambda c, g: (g, 0, 0)),
        pl.BlockSpec((Q, Q), lambda c, g: (0, 0)),
    ]
    out_specs = (pl.BlockSpec((Q, GPS * GC), lambda c, g: (order(c), g)),
                 pl.BlockSpec((1, GPS, GC, N), lambda c, g: (order(c), g, 0, 0)))
    return _call(body, name=f"ssd_fwd_{direction}",
                 out_shape=(_sds((T, SSD_W), BF16), _sds((nc, G, GC, N), F32)),
                 grid=(nc, G // GPS), in_specs=in_specs, out_specs=out_specs,
                 scratch=[pltpu.VMEM((G, GC, N), F32)], sem=("arbitrary", "arbitrary"), vmem_mb=48,
                 )(xbc, xbc, xbc, s, s_t, etot, pk_t, pk_t, pk_t, dskx, tri_t)


def _ssd_bwd(xbc, prep, dskx, sst, dy, direction, n_ctx):
    T = xbc.shape[0]
    nc, nctx = T // SSD_Q, n_ctx // SSD_Q
    fwd_order = _chunk_order(direction, nc, nctx)
    order = lambda c: fwd_order(nc - 1 - c)
    tri, ex, ext, hm = _ssd_constants(direction)
    ntri = (1.0 - tri).astype(BF16)
    Q, GC, N, G = SSD_Q, GROUP_COLS, SSD_STATE, SSD_GROUPS

    def body(x_ref, b_ref, c_ref, s_ref, pk_ref, st_ref, etot_ref, dsk_ref, sst_ref, dy_ref, tri_ref,
             ex_ref, ext_ref, hm_ref, ntri_ref, dx_ref, db_ref, dc_ref, ds_ref, ddt_ref, dtot_ref, daz_ref, dskg_ref,
             dS):
        c, gi = pl.program_id(0), pl.program_id(1)
        mask = tri_ref[...] > 0.5

        @pl.when(jnp.logical_and(c == 0, gi == 0))
        def _():
            dskg_ref[...] = jnp.zeros_like(dskg_ref)

        @pl.when(gi == 0)
        def _():
            ds_ref[...] = jnp.zeros_like(ds_ref)
            ddt_ref[...] = jnp.zeros_like(ddt_ref)
            dtot_ref[...] = jnp.zeros_like(dtot_ref)

        row8 = lax.broadcasted_iota(jnp.int32, (HEADS_PER_GROUP * GPS, 1), 0)
        daz = jnp.zeros((HEADS_PER_GROUP * GPS, Q), F32)
        for k in range(GPS):
            q = _ssd_common(k, x_ref, b_ref, c_ref, s_ref, pk_ref, st_ref, etot_ref, ex_ref, direction)
            g = q["g"]

            @pl.when(c == 0)
            def _():
                dS[g] = jnp.zeros((GC, N), F32)

            dS1 = dS[g]
            S0 = sst_ref[0, k]
            x, X, Bm, Cm, ext = q["x"], q["X"], q["Bm"], q["Cm"], ext_ref[k]
            E_x, D_x, Etot_c = q["E_x"], q["D_x"], q["Etot_c"]
            dYb = _cols(dy_ref, k, GC)
            dY = dYb.astype(F32)
            Xb = X.astype(BF16)

            CS = _dot(Cm, S0, NT)
            dCS = dY * E_x
            dC = _dot(dCS, S0, NN)
            dS0 = Etot_c * dS1 + _dot(dCS, Cm, TN)
            ds_x = dY * (E_x * CS)
            dtot_c = jnp.sum(dS1 * S0, axis=1, keepdims=True) * Etot_c
            dtot = jnp.sum(dtot_c * ext.astype(F32), axis=0, keepdims=True)
            XD = X * D_x
            dXD = _dot(Bm, dS1, NT)
            dB = _dot(XD, dS1, NN)
            dX = dXD * D_x
            t = dXD * XD
            ds_x = ds_x - t
            dtot_x = jnp.sum(t, axis=0, keepdims=True)
            Gm = _dot(Cm, Bm, NT)
            dG = jnp.zeros((Q, Q), F32)
            for r in range(HEADS_PER_GROUP):
                hmr = hm_ref[r:r + 1, :]
                Lm = _head_decay(q, r, mask)
                W = Gm * Lm
                dW = _dot(dYb * hmr, Xb, NT)
                dX = dX + _dot(W, dYb, TN) * hmr.astype(F32)
                dG = dG + dW * Lm
                P = _dot(dW * W, ntri_ref[...], NN)
                da_row = jnp.sum(jnp.where(mask, P, 0.0), axis=0, keepdims=True)
                daz = jnp.where(row8 == HEADS_PER_GROUP * k + r, da_row, daz)
            dC = dC + _dot(dG, Bm, NN)
            dB = dB + _dot(dG, Cm, TN)
            ds_ref[...] += _dot2(ds_x, ext)
            dtot_ref[0, 0:1, :] += dtot + _dot2(_bcast8(dtot_x), ext)[0:1, :]
            ddt_ref[...] += _dot(dX * x, ext, NN)
            dx = dX * q["dt_x"]
            if direction == 0:
                dx = dx + dY * _cols(dsk_ref, k, GC)[0:1, :]
                dskg_ref[0:1, :] += jnp.sum(_dot(dY * x, ext, NN), axis=0, keepdims=True)
            dx_ref[:, k * GC:(k + 1) * GC] = dx.astype(BF16)
            db_ref[:, k * N:(k + 1) * N] = dB.astype(BF16)
            dc_ref[:, k * N:(k + 1) * N] = dC.astype(BF16)
            dS[g] = dS0
        daz_ref[0] = daz

    cst2 = lambda shape: pl.BlockSpec(shape, lambda c, g: (0, 0))
    in_specs = _ssd_in_specs(order, direction) + [
        pl.BlockSpec((8, GPS * GC), lambda c, g: (0, g)),
        pl.BlockSpec((1, GPS, GC, N), lambda c, g: (order(c), g, 0, 0)),
        pl.BlockSpec((Q, GPS * GC), lambda c, g: (order(c), g)),
        cst2((Q, Q)),
        pl.BlockSpec((GPS, 2 * LANES, GC), lambda c, g: (g, 0, 0)),
        pl.BlockSpec((GPS, GC, LANES), lambda c, g: (g, 0, 0)),
        cst2((8, GC)), cst2((Q, Q)),
    ]
    row = pl.BlockSpec((Q, LANES), lambda c, g: (order(c), 0))
    hrows = HEADS_PER_GROUP * GPS
    out_specs = (pl.BlockSpec((Q, GPS * GC), lambda c, g: (order(c), g)),
                 pl.BlockSpec((Q, GPS * N), lambda c, g: (order(c), g)),
                 pl.BlockSpec((Q, GPS * N), lambda c, g: (order(c), g)),
                 row, row, pl.BlockSpec((1, 8, LANES), lambda c, g: (order(c), 0, 0)),
                 pl.BlockSpec((1, hrows, Q), lambda c, g: (order(c), g, 0)),
                 cst2((8, LANES)))
    return _call(body, name=f"ssd_bwd_{direction}",
                 out_shape=(_sds((T, SSD_W), BF16), _sds((T, G * N), BF16), _sds((T, G * N), BF16),
                            _sds((T, LANES), F32), _sds((T, LANES), F32), _sds((nc, 8, LANES), F32),
                            _sds((nc, SSD_HEADS, Q), F32), _sds((8, LANES), F32)),
                 grid=(nc, G // GPS), in_specs=in_specs, out_specs=out_specs,
                 scratch=[pltpu.VMEM((G, GC, N), F32)], sem=("arbitrary", "arbitrary"),
                 )(xbc, xbc, xbc, *prep[1:5], dskx, sst, dy, tri, ex, ext, hm, ntri)


def _ssd_bwd_t(xbc, prep, dskc, sst, dy, direction, n_ctx):
    T = xbc.shape[0]
    nc, nctx = T // SSD_Q, n_ctx // SSD_Q
    fwd_order = _chunk_order(direction, nc, nctx)
    order = lambda c: fwd_order(nc - 1 - c)
    tri = _ssd_constants(direction)[0]
    tri_t = jnp.transpose(tri)
    Q, GC, N, G, HP, HD = SSD_Q, GROUP_COLS, SSD_STATE, SSD_GROUPS, HEADS_PER_GROUP, SSD_HEAD_DIM
    hrows = HP * GPS
    _, s, _, s_t, etot, pk_t = prep

    def body(x_ref, b_ref, c_ref, s_ref, st_ref, etot_ref, dtr_ref, er_ref, dr_ref, dsk_ref, sst_ref, dy_ref,
             trit_ref, tri_ref, dx_ref, db_ref, dc_ref, rows_ref, dskg_ref, dS):
        c = pl.program_id(0)

        @pl.when(c == 0)
        def _():
            dskg_ref[...] = jnp.zeros_like(dskg_ref)

        mask_t = trit_ref[...] > 0.5
        not_tri = tri_ref[...] < 0.5
        tri_b = tri_ref[...].astype(BF16)
        lane = lax.broadcasted_iota(jnp.int32, (1, LANES), 1)
        rowi = lax.broadcasted_iota(jnp.int32, (GC, 1), 0)
        rowh = lax.broadcasted_iota(jnp.int32, (hrows, 1), 0)
        s_all = s_ref[...]
        etot_row = etot_ref[0, 0:1, :]
        out_rows = [jnp.zeros((hrows, Q), F32) for _ in range(4)]
        for k in range(GPS):
            g = pl.program_id(1) * GPS + k
            base = direction * SSD_HEADS + HP * g

            @pl.when(c == 0)
            def _():
                dS[g] = jnp.zeros((GC, N), F32)

            dS1 = dS[g]
            S0 = sst_ref[0, k]
            Bm, Cm = _cols(b_ref, k, N), _cols(c_ref, k, N)

            def head_rows(ref):
                return jnp.concatenate([jnp.broadcast_to(ref[0, HP * k + r:HP * k + r + 1, :], (HD, Q))
                                        for r in range(HP)], axis=0)

            def head_sum(v, r):
                return jnp.sum(v[HD * r:HD * (r + 1), :], axis=0, keepdims=True)

            x_t = jnp.transpose(_cols(x_ref, k, GC).astype(F32))
            dY_t = jnp.transpose(_cols(dy_ref, k, GC).astype(F32))
            dt_b, E_b, D_b = head_rows(dtr_ref), head_rows(er_ref), head_rows(dr_ref)
            X_t = x_t * dt_b
            X_tb, dY_tb = X_t.astype(BF16), dY_t.astype(BF16)
            e = [jnp.sum(jnp.where(lane == base + r, etot_row, 0.0), axis=1, keepdims=True) for r in range(HP)]
            etot_c = jnp.where(rowi < HD, e[0], jnp.where(rowi < 2 * HD, e[1], jnp.where(rowi < 3 * HD, e[2], e[3])))

            CS_t = _dot(S0, Cm, NT)
            dCS_t = dY_t * E_b
            dC = _dot(dCS_t, S0, TN)
            dS0 = etot_c * dS1 + _dot(dCS_t, Cm, NN)
            dsx_t = dY_t * (E_b * CS_t)
            dtot_c = jnp.sum(dS1 * S0, axis=1, keepdims=True) * etot_c
            XD_t = X_t * D_b
            dXD_t = _dot(dS1, Bm, NT)
            dB = _dot(XD_t, dS1, TN)
            dX_t = dXD_t * D_b
            t_t = dXD_t * XD_t
            dsx_t = dsx_t - t_t
            G_t = _dot(Bm, Cm, NT)
            dG_t = jnp.zeros((Q, Q), F32)
            dx_parts = []
            for r in range(HP):
                col = jnp.sum(jnp.where(lane == base + r, s_all, 0.0), axis=1, keepdims=True)
                L_t = jnp.exp(jnp.where(mask_t, st_ref[0, HP * k + r:HP * k + r + 1, :] - col, NEG_BIG))
                W_t = G_t * L_t
                xr, dyr = X_tb[HD * r:HD * (r + 1), :], dY_tb[HD * r:HD * (r + 1), :]
                dW_t = _dot(xr, dyr, TN)
                dx_parts.append(_dot(dyr, W_t, NT))
                dG_t = dG_t + dW_t * L_t
                P = lax.dot_general((dW_t * W_t).astype(BF16), tri_b, NN, preferred_element_type=F32)
                da_row = jnp.sum(jnp.where(not_tri, P, 0.0), axis=0, keepdims=True)
                out_rows[2] = jnp.where(rowh == HP * k + r, da_row, out_rows[2])
            dX_t = dX_t + jnp.concatenate(dx_parts, axis=0)
            dB = dB + _dot(dG_t, Cm, NN)
            dC = dC + _dot(dG_t, Bm, TN)
            dx_t = dX_t * dt_b
            if direction == 0:
                dx_t = dx_t + dY_t * dsk_ref[k]
            dxx = dX_t * x_t
            dyx = dY_t * x_t
            for r in range(HP):
                here = rowh == HP * k + r
                out_rows[0] = jnp.where(here, head_sum(dsx_t, r), out_rows[0])
                out_rows[1] = jnp.where(here, head_sum(dxx, r), out_rows[1])
                tot_r = (jnp.sum(dtot_c[HD * r:HD * (r + 1), :], axis=0, keepdims=True)
                         + jnp.sum(head_sum(t_t, r), axis=1, keepdims=True))
                out_rows[3] = jnp.where(here, tot_r, out_rows[3])
                if direction == 0:
                    dsk_r = jnp.sum(head_sum(dyx, r), axis=1, keepdims=True)
                    dskg_ref[HP * k + r:HP * k + r + 1, :] += jnp.broadcast_to(dsk_r, (1, LANES))
            dx_ref[:, k * GC:(k + 1) * GC] = jnp.transpose(dx_t).astype(BF16)
            db_ref[:, k * N:(k + 1) * N] = dB.astype(BF16)
            dc_ref[:, k * N:(k + 1) * N] = dC.astype(BF16)
            dS[g] = dS0
        for n in range(4):
            rows_ref[0, n] = out_rows[n]

    sec = LANES // hrows
    row_block = lambda n: pl.BlockSpec((1, hrows, Q), lambda c, g: (order(c), n * sec + direction * (SSD_HEADS // hrows) + g, 0))
    cst2 = lambda shape: pl.BlockSpec(shape, lambda c, g: (0, 0))
    in_specs = [
        pl.BlockSpec((Q, GPS * GC), lambda c, g: (order(c), g)),
        pl.BlockSpec((Q, GPS * N), lambda c, g: (order(c), SSD_W // (GPS * N) + g)),
        pl.BlockSpec((Q, GPS * N), lambda c, g: (order(c), (SSD_W + G * N) // (GPS * N) + g)),
        pl.BlockSpec((Q, LANES), lambda c, g: (order(c), 0)),
        row_block(0),
        pl.BlockSpec((1, 8, LANES), lambda c, g: (order(c), 0, 0)),
        row_block(0), row_block(1), row_block(2),
        pl.BlockSpec((GPS, GC, Q), lambda c, g: (g, 0, 0)),
        pl.BlockSpec((1, GPS, GC, N), lambda c, g: (order(c), g, 0, 0)),
        pl.BlockSpec((Q, GPS * GC), lambda c, g: (order(c), g)),
        cst2((Q, Q)), cst2((Q, Q)),
    ]
    out_specs = (pl.BlockSpec((Q, GPS * GC), lambda c, g: (order(c), g)),
                 pl.BlockSpec((Q, GPS * N), lambda c, g: (order(c), g)),
                 pl.BlockSpec((Q, GPS * N), lambda c, g: (order(c), g)),
                 pl.BlockSpec((1, 4, hrows, Q), lambda c, g: (order(c), 0, g, 0)),
                 pl.BlockSpec((hrows, LANES), lambda c, g: (g, 0)))
    return _call(body, name=f"ssd_bwd_{direction}",
                 out_shape=(_sds((T, SSD_W), BF16), _sds((T, G * N), BF16), _sds((T, G * N), BF16),
                            _sds((nc, 4, SSD_HEADS, Q), F32), _sds((SSD_HEADS, LANES), F32)),
                 grid=(nc, G // GPS), in_specs=in_specs, out_specs=out_specs,
                 scratch=[pltpu.VMEM((G, GC, N), F32)], sem=("arbitrary", "arbitrary"), vmem_mb=56,
                 )(xbc, xbc, xbc, s, s_t, etot, pk_t, pk_t, pk_t, dskc, sst, dy, tri_t, tri)


def _ssd_post_t(raw, prm, dt, rows0, rows1):
    T = raw.shape[0]
    nc, Q = T // SSD_Q, SSD_Q
    tril = jnp.asarray(np.tril(np.ones((Q, Q), np.float32)))
    triu = jnp.asarray(np.triu(np.ones((Q, Q), np.float32)))

    def body(raw_ref, prm_ref, dt_ref, r0_ref, r1_ref, tril_ref, triu_ref, draw_ref, dprm_ref):
        @pl.when(pl.program_id(0) == 0)
        def _():
            dprm_ref[...] = jnp.zeros_like(dprm_ref)

        def to_lanes(n):
            rows = jnp.concatenate([r0_ref[0, n], r1_ref[0, n], jnp.zeros((LANES - 2 * SSD_HEADS, Q), F32)], axis=0)
            return rows.T

        pre = raw_ref[...] + prm_ref[0:1, :]
        A = -jnp.exp(prm_ref[1:2, :])
        ds = to_lanes(0)
        lane = lax.broadcasted_iota(jnp.int32, (1, LANES), 1)
        da = jnp.where(lane < SSD_HEADS, _dot_hi(triu_ref[...], ds), _dot_hi(tril_ref[...], ds))
        da = da + to_lanes(2) + to_lanes(3)
        draw = (da * A + to_lanes(1)) * _sigmoid(pre)
        draw_ref[...] = draw
        dprm_ref[0:1, :] += jnp.sum(draw, axis=0, keepdims=True)
        dprm_ref[1:2, :] += jnp.sum(da * dt_ref[...], axis=0, keepdims=True) * A

    row = pl.BlockSpec((Q, LANES), lambda c: (c, 0))
    rows = pl.BlockSpec((1, 4, SSD_HEADS, Q), lambda c: (c, 0, 0, 0))
    cst = lambda shape: pl.BlockSpec(shape, lambda c: (0, 0))
    return _call(body, name="ssd_post", out_shape=(_sds((T, LANES), F32), _sds((8, LANES), F32)), grid=(nc,),
                 in_specs=[row, cst((8, LANES)), row, rows, rows, cst((Q, Q)), cst((Q, Q))],
                 out_specs=(row, cst((8, LANES))), sem=("arbitrary",))(raw, prm, dt, rows0, rows1, tril, triu)


ADA_ROWS = 16


def _ada_fwd(c16, w_ada, b_loc):
    depth, Dm, W = w_ada.shape

    def body(c_ref, w_ref, b_ref, o_ref):
        o_ref[0] = _dot_hi(_silu(c_ref[...]), w_ref[0]) + b_ref[0]

    return _call(body, name="ada_fwd", out_shape=_sds((depth, ADA_ROWS, W), F32), grid=(depth,),
                 in_specs=[pl.BlockSpec((ADA_ROWS, Dm), lambda l: (0, 0)),
                           pl.BlockSpec((1, Dm, W), lambda l: (l, 0, 0)),
                           pl.BlockSpec((1, 1, W), lambda l: (l, 0, 0))],
                 out_specs=pl.BlockSpec((1, ADA_ROWS, W), lambda l: (l, 0, 0)),
                 sem=("parallel",), vmem_mb=40)(c16, w_ada, b_loc)


def _ada_bwd(c16, w_ada, dmod):
    depth, Dm, W = w_ada.shape

    def body(c_ref, w_ref, d_ref, gw_ref, dc_ref):
        l = pl.program_id(0)

        @pl.when(l == 0)
        def _():
            dc_ref[...] = jnp.zeros_like(dc_ref)

        cc = c_ref[...]
        sg = _sigmoid(cc)
        gw_ref[0] = _dot_hi(cc * sg, d_ref[0], TN)
        dsc = _dot_hi(d_ref[0], w_ref[0], NT)
        dc_ref[...] += dsc[8:16, :] * (sg + cc * sg * (1.0 - sg))[8:16, :]

    return _call(body, name="ada_bwd", out_shape=(_sds((depth, Dm, W), F32), _sds((8, Dm), F32)), grid=(depth,),
                 in_specs=[pl.BlockSpec((ADA_ROWS, Dm), lambda l: (0, 0)),
                           pl.BlockSpec((1, Dm, W), lambda l: (l, 0, 0)),
                           pl.BlockSpec((1, ADA_ROWS, W), lambda l: (l, 0, 0))],
                 out_specs=(pl.BlockSpec((1, Dm, W), lambda l: (l, 0, 0)), pl.BlockSpec((8, Dm), lambda l: (0, 0))),
                 sem=("arbitrary",), vmem_mb=48)(c16, w_ada, dmod)


def _sum_slabs(g, name, tr=512):
    _, R, C = g.shape
    tr = min(tr, R)

    def body(g_ref, o_ref):
        acc = g_ref[0]
        for k in range(1, N_DEV):
            acc = acc + g_ref[k]
        o_ref[...] = acc

    return _call(body, name=name, out_shape=_sds((R, C), F32), grid=(R // tr,),
                 in_specs=[pl.BlockSpec((N_DEV, tr, C), lambda i: (0, i, 0))],
                 out_specs=pl.BlockSpec((tr, C), lambda i: (i, 0)), sem=("parallel",), vmem_mb=40)(g)


def _adamw_math(w, g, m, v):
    m = ADAM_B1 * m + (1.0 - ADAM_B1) * g
    v = ADAM_B2 * v + (1.0 - ADAM_B2) * (g * g)
    m_hat = m / (1.0 - ADAM_B1 ** ADAM_STEP)
    v_hat = v / (1.0 - ADAM_B2 ** ADAM_STEP)
    delta = -ADAM_LR * (m_hat / (jnp.sqrt(v_hat) + ADAM_EPS) + ADAM_WD * w)
    return delta, m, v


def _adamw(w, g, m, v, name, tr):
    R, C = w.shape
    tr = min(tr, R)
    assert R % tr == 0

    def body(w_ref, g_ref, m_ref, v_ref, d_ref, nm_ref, nv_ref):
        d, nm, nv = _adamw_math(w_ref[...], g_ref[...], m_ref[...], v_ref[...])
        d_ref[...] = d
        nm_ref[...] = nm
        nv_ref[...] = nv

    blk = pl.BlockSpec((tr, C), lambda i: (i, 0))
    out = _sds((R, C), F32)
    return _call(body, name=name, out_shape=(out, out, out), grid=(R // tr,), in_specs=[blk] * 4,
                 out_specs=(blk, blk, blk), sem=("parallel",), vmem_mb=40)(w, g, m, v)


def _adamw_slabs(w, slabs, m, v, name, tr, tc):
    depth, R, C = w.shape
    assert R % tr == 0 and C % tc == 0

    def body(w_ref, s_ref, m_ref, v_ref, g_ref, d_ref, nm_ref, nv_ref):
        g = s_ref[0, 0].astype(F32)
        for k in range(1, N_DEV):
            g = g + s_ref[0, k].astype(F32)
        d, nm, nv = _adamw_math(w_ref[0], g, m_ref[0], v_ref[0])
        g_ref[0] = g
        d_ref[0] = d
        nm_ref[0] = nm
        nv_ref[0] = nv

    blk = pl.BlockSpec((1, tr, tc), lambda l, i, j: (l, i, j))
    out = _sds((depth, R, C), F32)
    return _call(body, name=name, out_shape=(out, out, out, out), grid=(depth, R // tr, C // tc),
                 in_specs=[blk, pl.BlockSpec((1, N_DEV, tr, tc), lambda l, i, j: (l, 0, i, j)), blk, blk],
                 out_specs=(blk, blk, blk, blk), sem=("parallel", "parallel", "parallel"), vmem_mb=56)(w, slabs, m, v)


PACK_QUANTUM = 512 * LANES


def _pack(arrays):
    flat = jnp.concatenate([a.reshape(-1).astype(F32) for a in arrays])
    pad = (-flat.shape[0]) % PACK_QUANTUM
    return jnp.pad(flat, (0, pad)).reshape(-1, LANES)


def _unpack(bundle, shapes):
    flat = bundle.reshape(-1)
    out, off = [], 0
    for shp in shapes:
        n = int(np.prod(shp))
        out.append(flat[off:off + n].reshape(shp))
        off += n
    return out


def _row(v):
    return v.reshape(1, -1)


def _pad_rows(a, rows):
    return jnp.pad(a, ((0, rows - a.shape[0]), (0, 0)))


def kernel(x, c, ctx, c_ctx, w_ada, b_ada, g_pre, g_post, w_in, conv_w, conv_b, dt_bias, a_log, d_skip, g_ssd, g_v, w_s, b_s, g_mlp, w_out, loss_target, m_c_ctx, m_w_ada, m_b_ada, m_g_pre, m_g_post, m_w_in, m_conv_w, m_conv_b, m_dt_bias, m_a_log, m_d_skip, m_g_ssd, m_g_v, m_w_s, m_b_s, m_g_mlp, m_w_out, v_c_ctx, v_w_ada, v_b_ada, v_g_pre, v_g_post, v_w_in, v_conv_w, v_conv_b, v_dt_bias, v_a_log, v_d_skip, v_g_ssd, v_g_v, v_w_s, v_b_s, v_g_mlp, v_w_out):
    depth = w_in.shape[0]
    L = x.shape[1]
    n_ctx = ctx.shape[1]
    T = n_ctx + L
    Dm = D_MODEL
    me = _lin(_my_pos())
    ada_w = w_ada.shape[2]
    in_w = w_in.shape[2]
    out_r = w_out.shape[1]
    conv_c = conv_w.shape[2]
    TM = 768

    c_all = _all_gather_small(_pad_rows(c, 8), "gather_c")[:, 0, :]
    c16 = _pad_rows(jnp.concatenate([c_all, _row(c_ctx)], axis=0), ADA_ROWS)
    b_loc = lax.dynamic_slice_in_dim(b_ada, me * ada_w, ada_w, axis=1)[:, None, :]
    mod_loc = _ada_fwd(c16, w_ada, b_loc)
    mod_all = _all_gather_small(mod_loc.reshape(depth * ADA_ROWS, ada_w), "gather_mod")
    mod_all = mod_all.reshape(N_DEV, depth, ADA_ROWS, ada_w)
    mod_me = lax.dynamic_index_in_dim(mod_all, me, axis=2, keepdims=False)
    mod_me = jnp.transpose(mod_me, (1, 0, 2)).reshape(depth, N_DEV * ada_w)
    mod_cx = jnp.transpose(mod_all[:, :, 8, :], (1, 0, 2)).reshape(depth, N_DEV * ada_w)
    shift2 = jnp.stack([mod_cx[:, 0:Dm], mod_me[:, 0:Dm]], axis=1)
    scale2 = jnp.stack([mod_cx[:, Dm:2 * Dm], mod_me[:, Dm:2 * Dm]], axis=1)
    gate2 = jnp.stack([mod_cx[:, 2 * Dm:], mod_me[:, 2 * Dm:]], axis=1)

    w_in_t, m_w_in_t, v_w_in_t = (jnp.swapaxes(a, 1, 2) for a in (w_in, m_w_in, v_w_in))
    w_in_bf = w_in_t.astype(BF16)
    w_out_bf = w_out.astype(BF16)
    w_in_g = _all_gather_big(w_in_bf[0], "gather_w_in")
    w_out_g = _all_gather_big(w_out_bf[0], "gather_w_out")
    conv_all = _all_gather_small(_pad_rows(conv_w.reshape(depth * CONV_W, conv_c), 24).reshape(24, conv_c),
                                 "gather_conv_w")
    conv_full = jnp.transpose(conv_all[:, :depth * CONV_W, :], (1, 0, 2)).reshape(depth, CONV_W, XBC_W)

    def in_weights(gathered):
        wf = gathered.reshape(IN_W, Dm)
        return (jnp.concatenate([wf[:XBC_W], wf[XBC_W + DT_W:]], axis=0),
                jnp.pad(wf[XBC_W:XBC_W + DT_W], ((0, LANES - DT_W), (0, 0))))

    w_main, w_dt, w_o = [None] * depth, [None] * depth, [None] * depth

    def ssd_prm(l):
        rows = jnp.stack([jnp.pad(dt_bias[l].reshape(-1), (0, LANES - DT_W)),
                          jnp.pad(a_log[l].reshape(-1), (0, LANES - DT_W)),
                          jnp.pad(d_skip[l], (0, LANES - SSD_HEADS))])
        return _pad_rows(rows, 8)

    ws_bf = w_s.astype(BF16)
    bst = jnp.pad(jnp.transpose(b_s, (0, 2, 1)), ((0, 0), (0, 0), (0, LANES - MLP_GROUPS)))

    X = jnp.concatenate([ctx[0], x[0]], axis=0)
    saved = []
    for l in range(depth):
        w_main[l], w_dt[l] = in_weights(w_in_g)
        w_o[l] = w_out_g.reshape(N_DEV * out_r, Dm)
        hx = _prenorm_fwd(X, _row(g_pre[l]), scale2[l], shift2[l], n_ctx)
        if l + 1 < depth:
            z, raw, w_in_g = _mm(hx, w_main[l], "nt", BF16, "in_proj_gather", TM, 1024, 2048,
                                 exchange=("gather", w_in_bf[l + 1]), side=w_dt[l])
        else:
            z, raw = _mm(hx, w_main[l], "nt", BF16, "in_proj", TM, 1024, 2048, side=w_dt[l])
        cw = _pad_rows(conv_full[l], 8)
        cb = _row(conv_b[l])
        xbc = _conv_fwd(z, cw, cb, n_ctx)
        prm = ssd_prm(l)
        prep = _ssd_prep(raw, prm)
        dskx = _pad_rows(_row(jnp.repeat(d_skip[l], SSD_HEAD_DIM)), 8)
        dskc = jnp.broadcast_to(jnp.repeat(d_skip[l], SSD_HEAD_DIM).reshape(SSD_GROUPS, GROUP_COLS, 1),
                                (SSD_GROUPS, GROUP_COLS, SSD_Q))
        y_f, sst_f = _ssd_fwd(xbc, prep, dskc, 0, n_ctx)
        y_b, sst_b = _ssd_fwd(xbc, prep, dskc, 1, n_ctx)
        ycat = _gate_fwd(y_f, y_b, z, _row(g_ssd[l]))
        ycat = _mlp_fwd(z, _row(g_v[l]), ws_bf[l], bst[l], _row(g_mlp[l]), ycat)
        if l + 1 < depth:
            o, w_out_g = _mm(ycat, w_o[l], "nn", F32, "out_proj_gather", TM, 1024, 4096,
                             exchange=("gather", w_out_bf[l + 1]))
        else:
            o = _mm(ycat, w_o[l], "nn", F32, "out_proj", TM, 1024, 4096)
        saved.append((X, hx, z, raw, cw, cb, xbc, prm, prep, dskc, y_f, sst_f, y_b, sst_b, ycat, o))
        X = _post_fwd(X, o, _row(g_post[l]), gate2[l], n_ctx)

    dX, sq = _loss_kernel(X, loss_target[0], n_ctx)
    loss = lax.psum(0.5 * jnp.sum(sq) / Dm, ("x", "y", "c"))

    g_small = {k: [None] * depth for k in
               ("b_ada", "g_pre", "g_post", "conv_b", "dt_bias", "a_log", "d_skip", "g_ssd", "g_v", "w_s", "b_s",
                "g_mlp", "conv_w", "dmod_c", "dmod")}
    in_recv = lax.empty((depth, N_DEV, in_w, Dm), BF16)
    out_recv = lax.empty((depth, N_DEV, out_r, Dm), BF16)
    for l in reversed(range(depth)):
        Xl, hx, z, raw, cw, cb, xbc, prm, prep, dskc, y_f, sst_f, y_b, sst_b, ycat, o = saved[l]
        d_o, acc_post = _post_bwd(o, _row(g_post[l]), gate2[l], dX, n_ctx)
        dycat = _mm(d_o, w_o[l], "nt", BF16, "out_proj_dx", TM, 1024, 2048)
        out_slabs = _mm(ycat, d_o, "tn", BF16, "out_proj_dw", 1024, 1024, 2816).reshape(N_DEV, out_r, Dm)
        dz = lax.empty((T, Z_MAIN), BF16)
        dy, dz, acc_gate = _gate_bwd(y_f, y_b, z, _row(g_ssd[l]), dycat, dz)
        dz, acc_mlp, dws, dbst = _mlp_bwd(z, _row(g_v[l]), ws_bf[l], bst[l], _row(g_mlp[l]), dycat, dz)
        dx0, db0, dc0, rows0, dsk0 = _ssd_bwd_t(xbc, prep, dskc, sst_f, dy, 0, n_ctx)
        dx1, db1, dc1, rows1, _ = _ssd_bwd_t(xbc, prep, dskc, sst_b, dy, 1, n_ctx)
        draw, dprm = _ssd_post_t(raw, prm, prep[0], rows0, rows1)
        gn = SSD_GROUPS * SSD_STATE
        dz, dcw_x, dcb_x = _conv_bwd(z, cw, cb, dx0, dx1, dz, 0, n_ctx)
        dz, dcw_b, dcb_b = _conv_bwd(z, cw, cb, db0, db1, dz, SSD_W, n_ctx)
        dz, dcw_c, dcb_c = _conv_bwd(z, cw, cb, dc0, dc1, dz, SSD_W + gn, n_ctx)
        dcw = jnp.concatenate([dcw_x, dcw_b, dcw_c], axis=1)
        dcb = jnp.concatenate([dcb_x, dcb_b, dcb_c], axis=1)
        gw_main, out_recv = _mm(dz, hx, "tn", BF16, "in_proj_dw_scatter", 1024, 1024, 2816,
                                exchange=("scatter", out_slabs, (out_recv, l)))
        gw_dt = _mm(draw, hx, "tn", BF16, "dt_proj_dw", LANES, 1024, 2816)
        in_slabs = jnp.concatenate([gw_main[:XBC_W], gw_dt[:DT_W], gw_main[XBC_W:]], axis=0).reshape(N_DEV, in_w, Dm)
        dhx, in_recv = _mm(dz, w_main[l], "nn", F32, "in_proj_dx_scatter", TM, 1024, 4096,
                           exchange=("scatter", in_slabs, (in_recv, l)), add=(draw, w_dt[l]))
        dX, acc_pre = _prenorm_bwd(Xl, _row(g_pre[l]), scale2[l], shift2[l], dhx, dX, n_ctx)

        dmod_c = jnp.concatenate([acc_pre[3], acc_pre[1], acc_post[1]])
        dmod_x = jnp.concatenate([acc_pre[4], acc_pre[2], acc_post[2]])
        g_small["dmod_c"][l] = dmod_c
        g_small["dmod"][l] = dmod_x
        g_small["b_ada"][l] = dmod_c + dmod_x
        g_small["g_pre"][l] = acc_pre[0]
        g_small["g_post"][l] = acc_post[0]
        g_small["conv_b"][l] = dcb[0]
        g_small["conv_w"][l] = dcw[:CONV_W]
        g_small["dt_bias"][l] = dprm[0, :DT_W].reshape(2, SSD_HEADS)
        g_small["a_log"][l] = dprm[1, :DT_W].reshape(2, SSD_HEADS)
        g_small["d_skip"][l] = dsk0[:, 0]
        g_small["g_ssd"][l] = acc_gate[0]
        g_small["g_v"][l] = acc_mlp[0]
        g_small["g_mlp"][l] = acc_mlp[1]
        g_small["w_s"][l] = dws
        g_small["b_s"][l] = jnp.transpose(dbst[:, :MLP_GROUPS])

    grad_x = dX[n_ctx:][None]

    summed_names = ["b_ada", "g_pre", "g_post", "conv_b", "dt_bias", "a_log", "d_skip", "g_ssd", "g_v", "w_s",
                    "b_s", "g_mlp", "conv_w", "dmod_c"]
    parts = [jnp.stack(g_small[k]) for k in summed_names] + [jnp.stack(g_small["dmod"])]
    shapes = [p.shape for p in parts]
    bundle = _pack(parts)
    gathered = _all_gather_big(bundle, "gather_small_grads")
    reduced = _unpack(_sum_slabs(gathered, "sum_small_grads"), shapes)
    gs = dict(zip(summed_names, reduced[:-1]))
    n_el = int(np.prod(shapes[-1]))
    off = sum(int(np.prod(s)) for s in shapes[:-1])
    dmod_all = gathered.reshape(N_DEV, -1)[:, off:off + n_el].reshape(N_DEV, depth, 3 * Dm)

    dmod_rows = jnp.concatenate([jnp.transpose(dmod_all, (1, 0, 2)), gs["dmod_c"][:, None, :]], axis=1)
    dmod_rows = lax.dynamic_slice_in_dim(dmod_rows, me * ada_w, ada_w, axis=2)
    dmod_rows = jnp.pad(dmod_rows, ((0, 0), (0, ADA_ROWS - 9), (0, 0)))
    g_w_ada, dc_part = _ada_bwd(c16, w_ada, dmod_rows)
    dc_all = _all_gather_small(dc_part, "gather_dc")
    g_c_ctx = _sum_slabs(dc_all, "sum_dc")[0]

    grads_small = {"c_ctx": g_c_ctx, "b_ada": gs["b_ada"], "g_pre": gs["g_pre"], "g_post": gs["g_post"],
                   "conv_b": gs["conv_b"], "dt_bias": gs["dt_bias"], "a_log": gs["a_log"], "d_skip": gs["d_skip"],
                   "g_ssd": gs["g_ssd"], "g_v": gs["g_v"], "w_s": gs["w_s"], "b_s": gs["b_s"], "g_mlp": gs["g_mlp"],
                   "conv_w": lax.dynamic_slice_in_dim(gs["conv_w"], me * conv_c, conv_c, axis=2)}

    g_in, d_in, nm_in, nv_in = (jnp.swapaxes(t, 1, 2) for t in
                                _adamw_slabs(w_in_t, in_recv, m_w_in_t, v_w_in_t, "adamw_w_in", in_w, 256))
    g_out, d_out, nm_out, nv_out = _adamw_slabs(w_out, out_recv, m_w_out, v_w_out, "adamw_w_out", 128, Dm)
    d_ada, nm_ada, nv_ada = _adamw(w_ada.reshape(depth * Dm, ada_w), g_w_ada.reshape(depth * Dm, ada_w),
                                   m_w_ada.reshape(depth * Dm, ada_w), v_w_ada.reshape(depth * Dm, ada_w),
                                   "adamw_w_ada", 256)

    small_names = ["c_ctx", "b_ada", "g_pre", "g_post", "conv_w", "conv_b", "dt_bias", "a_log", "d_skip", "g_ssd",
                   "g_v", "w_s", "b_s", "g_mlp"]
    small_w = dict(c_ctx=c_ctx, b_ada=b_ada, g_pre=g_pre, g_post=g_post, conv_w=conv_w, conv_b=conv_b,
                   dt_bias=dt_bias, a_log=a_log, d_skip=d_skip, g_ssd=g_ssd, g_v=g_v, w_s=w_s, b_s=b_s, g_mlp=g_mlp)
    small_m = dict(c_ctx=m_c_ctx, b_ada=m_b_ada, g_pre=m_g_pre, g_post=m_g_post, conv_w=m_conv_w, conv_b=m_conv_b,
                   dt_bias=m_dt_bias, a_log=m_a_log, d_skip=m_d_skip, g_ssd=m_g_ssd, g_v=m_g_v, w_s=m_w_s,
                   b_s=m_b_s, g_mlp=m_g_mlp)
    small_v = dict(c_ctx=v_c_ctx, b_ada=v_b_ada, g_pre=v_g_pre, g_post=v_g_post, conv_w=v_conv_w, conv_b=v_conv_b,
                   dt_bias=v_dt_bias, a_log=v_a_log, d_skip=v_d_skip, g_ssd=v_g_ssd, g_v=v_g_v, w_s=v_w_s,
                   b_s=v_b_s, g_mlp=v_g_mlp)
    s_shapes = [small_w[k].shape for k in small_names]
    d_s, nm_s, nv_s = _adamw(_pack([small_w[k] for k in small_names]),
                             _pack([grads_small[k].reshape(small_w[k].shape) for k in small_names]),
                             _pack([small_m[k] for k in small_names]), _pack([small_v[k] for k in small_names]),
                             "adamw_small", 512)
    d_s = dict(zip(small_names, _unpack(d_s, s_shapes)))
    nm_s = dict(zip(small_names, _unpack(nm_s, s_shapes)))
    nv_s = dict(zip(small_names, _unpack(nv_s, s_shapes)))

    big = {"w_ada": (g_w_ada, d_ada.reshape(w_ada.shape), nm_ada.reshape(w_ada.shape), nv_ada.reshape(w_ada.shape)),
           "w_in": tuple(t.reshape(w_in.shape) for t in (g_in, d_in, nm_in, nv_in)),
           "w_out": tuple(t.reshape(w_out.shape) for t in (g_out, d_out, nm_out, nv_out))}
    order = ["c_ctx", "w_ada", "b_ada", "g_pre", "g_post", "w_in", "conv_w", "conv_b", "dt_bias", "a_log", "d_skip",
             "g_ssd", "g_v", "w_s", "b_s", "g_mlp", "w_out"]

    def pick(k, idx):
        if k in big:
            return big[k][idx]
        return (grads_small[k].reshape(small_w[k].shape), d_s[k], nm_s[k], nv_s[k])[idx]

    return (loss, grad_x, *[pick(k, 0) for k in order], *[pick(k, 1) for k in order],
            *[pick(k, 2) for k in order], *[pick(k, 3) for k in order])
```

```python
import numpy as np
import jax
import jax.numpy as jnp
from jax import lax
from jax.experimental import pallas as pl
from jax.experimental.pallas import tpu as pltpu

F32 = jnp.float32
BF16 = jnp.bfloat16

D_MODEL = 2048
GRID_W = 64
SSD_W = 2048
SSD_HEADS = 32
SSD_HEAD_DIM = 64
SSD_GROUPS = 8
HEADS_PER_GROUP = 4
SSD_STATE = 128
CHUNK = 128
SSD_Q = 256
CONV_W = 5
MLP_W = 2048
MLP_GROUPS = 16
XBC_W = 4096
DT_W = 64
IN_W = 12352
Z_MAIN = IN_W - DT_W
GROUP_COLS = HEADS_PER_GROUP * SSD_HEAD_DIM
EPS = 1e-6
N_DEV = 8

ADAM_LR = 0.001
ADAM_B1 = 0.9
ADAM_B2 = 0.999
ADAM_EPS = 1e-08
ADAM_WD = 0.01
ADAM_STEP = 10

LANES = 128
NEG_BIG = -1e30

NN = (((1,), (0,)), ((), ()))
NT = (((1,), (1,)), ((), ()))
TN = (((0,), (0,)), ((), ()))
HI = lax.Precision.HIGHEST


def _dot(a, b, dims):
    return lax.dot_general(a.astype(BF16), b.astype(BF16), dims, preferred_element_type=F32)


def _dot_hi(a, b, dims=NN):
    return lax.dot_general(a, b, dims, preferred_element_type=F32, precision=HI)


def _sigmoid(x):
    return 1.0 / (1.0 + jnp.exp(-x))


def _silu(x):
    return x * _sigmoid(x)


def _softplus(x):
    return jnp.maximum(x, 0.0) + jnp.log(1.0 + jnp.exp(-jnp.abs(x)))


def _call(body, *, name, out_shape, grid=None, in_specs=None, out_specs=None, scratch=(),
          sem=None, vmem_mb=None, aliases=None):
    params = {}
    if sem is not None:
        params["dimension_semantics"] = sem
    if vmem_mb is not None:
        params["vmem_limit_bytes"] = vmem_mb << 20
    kw = {}
    if grid is not None:
        kw["grid"] = grid
    if in_specs is not None:
        kw["in_specs"] = in_specs
    if out_specs is not None:
        kw["out_specs"] = out_specs
    return pl.pallas_call(body, name=name, out_shape=out_shape, scratch_shapes=list(scratch),
                          input_output_aliases=aliases or {},
                          compiler_params=pltpu.CompilerParams(**params), **kw)


def _sds(shape, dtype):
    return jax.ShapeDtypeStruct(tuple(shape), dtype)


ANY = pl.BlockSpec(memory_space=pl.ANY)


def _my_pos():
    return lax.axis_index("x"), lax.axis_index("y"), lax.axis_index("c")


def _flip(v, bit):
    return 1 - v if bit else v


def _peer(pos, k):
    mx, my, mc = pos
    return (_flip(mx, (k >> 2) & 1), _flip(my, (k >> 1) & 1), _flip(mc, k & 1))


def _lin(pos):
    return 4 * pos[0] + 2 * pos[1] + pos[2]


def _all_gather_small(x, name):
    R, C = x.shape

    def body(x_ref, o_ref, send_sems, recv_sems):
        me = _my_pos()
        o_ref[_lin(me)] = x_ref[...]
        sends = []
        for k in range(1, N_DEV):
            peer = _peer(me, k)
            cp = pltpu.make_async_remote_copy(
                src_ref=x_ref, dst_ref=o_ref.at[_lin(me)], send_sem=send_sems.at[k - 1],
                recv_sem=recv_sems.at[k - 1], device_id=peer, device_id_type=pl.DeviceIdType.MESH)
            cp.start()
            sends.append(cp)
        for k in range(1, N_DEV):
            peer = _peer(me, k)
            pltpu.make_async_remote_copy(
                src_ref=x_ref, dst_ref=o_ref.at[_lin(peer)], send_sem=send_sems.at[k - 1],
                recv_sem=recv_sems.at[k - 1], device_id=peer,
                device_id_type=pl.DeviceIdType.MESH).wait_recv()
        for cp in sends:
            cp.wait_send()

    return _call(body, name=name, out_shape=_sds((N_DEV, R, C), x.dtype),
                 in_specs=[pl.BlockSpec(memory_space=pltpu.VMEM)],
                 out_specs=pl.BlockSpec(memory_space=pltpu.VMEM),
                 scratch=[pltpu.SemaphoreType.DMA((N_DEV - 1,)), pltpu.SemaphoreType.DMA((N_DEV - 1,))],
                 vmem_mb=40)(x)


def _all_gather_big(x, name):
    def body(x_ref, o_ref, send_sems, recv_sems, local_sem):
        mx, my, mc = _my_pos()
        me, sibling = (mx, my, mc), (mx, my, 1 - mc)
        chips = [(1 - mx, my), (mx, 1 - my), (1 - mx, 1 - my)]

        def slot(pos):
            return o_ref.at[_lin(pos)]

        def copy(k, block, to, src=None):
            return pltpu.make_async_remote_copy(
                src_ref=slot(block) if src is None else src, dst_ref=slot(block),
                send_sem=send_sems.at[k], recv_sem=recv_sems.at[k], device_id=to,
                device_id_type=pl.DeviceIdType.MESH)

        mine = pltpu.make_async_copy(x_ref, slot(me), local_sem)
        mine.start()
        first = [copy(0, me, sibling, src=x_ref)]
        first += [copy(1 + j, me, (*chip, mc), src=x_ref) for j, chip in enumerate(chips)]
        for cp in first:
            cp.start()
        passed = [copy(4 + j, (*chip, mc), sibling) for j, chip in enumerate(chips)]
        for j, chip in enumerate(chips):
            copy(1 + j, (*chip, mc), me).wait_recv()
            passed[j].start()
        copy(0, sibling, me).wait_recv()
        for j, chip in enumerate(chips):
            copy(4 + j, (*chip, 1 - mc), me).wait_recv()
        for cp in first + passed:
            cp.wait_send()
        mine.wait()

    return _call(body, name=name, out_shape=_sds((N_DEV,) + x.shape, x.dtype),
                 in_specs=[ANY], out_specs=ANY,
                 scratch=[pltpu.SemaphoreType.DMA((7,)), pltpu.SemaphoreType.DMA((7,)),
                          pltpu.SemaphoreType.DMA])(x)


EXCHANGE_SEMS = [pltpu.SemaphoreType.DMA((N_DEV - 1,)), pltpu.SemaphoreType.DMA((N_DEV - 1,)),
                 pltpu.SemaphoreType.DMA]


def _exchange_copies(kind, src_ref, dst_ref, send_sems, recv_sems, local_sem, with_arrivals=True):
    me = _my_pos()
    mine_src = src_ref if kind == "gather" else src_ref.at[_lin(me)]
    local = pltpu.make_async_copy(mine_src, dst_ref.at[_lin(me)], local_sem)
    sends, recvs = [], []
    for k in range(1, N_DEV):
        peer = _peer(me, k)
        out_src = src_ref if kind == "gather" else src_ref.at[_lin(peer)]
        sends.append(pltpu.make_async_remote_copy(
            src_ref=out_src, dst_ref=dst_ref.at[_lin(me)], send_sem=send_sems.at[k - 1],
            recv_sem=recv_sems.at[k - 1], device_id=peer, device_id_type=pl.DeviceIdType.MESH))
        if with_arrivals:
            recvs.append(pltpu.make_async_remote_copy(
                src_ref=out_src, dst_ref=dst_ref.at[_lin(peer)], send_sem=send_sems.at[k - 1],
                recv_sem=recv_sems.at[k - 1], device_id=peer, device_id_type=pl.DeviceIdType.MESH))
    return sends, recvs, local


def _exchange_start(kind, src_ref, dst_ref, send_sems, recv_sems, local_sem):
    sends, _, local = _exchange_copies(kind, src_ref, dst_ref, send_sems, recv_sems, local_sem, with_arrivals=False)
    local.start()
    for cp in sends:
        cp.start()


def _exchange_wait(kind, src_ref, dst_ref, send_sems, recv_sems, local_sem):
    sends, recvs, local = _exchange_copies(kind, src_ref, dst_ref, send_sems, recv_sems, local_sem)
    for cp in recvs:
        cp.wait_recv()
    for cp in sends:
        cp.wait_send()
    local.wait()


def _exchange_shape(kind, src):
    return _sds((N_DEV,) + src.shape if kind == "gather" else src.shape, src.dtype)


def _mm(a, b, mode, out_dtype, name, tm, tn, tk, exchange=None, add=None, side=None):
    if mode == "nn":
        (M, K), (K2, N) = a.shape, b.shape
    elif mode == "nt":
        (M, K), (N, K2) = a.shape, b.shape
    else:
        (K, M), (K2, N) = a.shape, b.shape
    assert K == K2
    tm, tn, tk = min(tm, M), min(tn, N), min(tk, K)
    assert M % tm == 0 and N % tn == 0 and K % tk == 0, (name, M, N, K, tm, tn, tk)
    ni, nj, nk = M // tm, N // tn, K // tk
    dims = {"nn": NN, "nt": NT, "tn": TN}[mode]
    assert side is None or (nk == 1 and mode != "tn")

    if mode == "tn":
        a_spec = pl.BlockSpec((tk, tm), lambda i, j, k: (k, i))
    else:
        a_spec = pl.BlockSpec((tm, tk), lambda i, j, k: (i, k))
    if mode == "nt":
        b_spec = pl.BlockSpec((tn, tk), lambda i, j, k: (j, k))
    else:
        b_spec = pl.BlockSpec((tk, tn), lambda i, j, k: (k, j))
    operands, in_specs = [a, b], [a_spec, b_spec]
    out_shape, out_specs = [_sds((M, N), out_dtype)], [pl.BlockSpec((tm, tn), lambda i, j, k: (i, j))]
    if add is not None:
        k2 = add[0].shape[1]
        operands += list(add)
        in_specs += [pl.BlockSpec((tm, k2), lambda i, j, k: (i, 0)), pl.BlockSpec((k2, tn), lambda i, j, k: (0, j))]
    if side is not None:
        n3 = side.shape[0]
        operands.append(side)
        in_specs.append(pl.BlockSpec((n3, tk), lambda i, j, k: (0, 0)))
        out_shape.append(_sds((M, n3), F32))
        out_specs.append(pl.BlockSpec((tm, n3), lambda i, j, k: (i, 0)))
    scratch = [] if nk == 1 else [pltpu.VMEM((tm, tn), F32)]
    aliases = {}
    if exchange is not None:
        kind, src = exchange[:2]
        operands.append(src)
        in_specs.append(ANY)
        if len(exchange) == 3:
            into, into_idx = exchange[2]
            aliases = {len(operands): len(out_shape)}
            operands.append(into)
            in_specs.append(ANY)
            out_shape.append(_sds(into.shape, into.dtype))
        else:
            into_idx = None
            out_shape.append(_exchange_shape(kind, src))
        out_specs.append(ANY)
        scratch = EXCHANGE_SEMS + scratch
    n_in, n_out = len(operands), len(out_shape)

    def body(*refs):
        ins, outs, scr = list(refs[:n_in]), list(refs[n_in:n_in + n_out]), list(refs[n_in + n_out:])
        a_ref, b_ref, o_ref = ins[0], ins[1], outs[0]
        add_refs = ins[2:4] if add is not None else None
        side_ref = ins[2 + (2 if add is not None else 0)] if side is not None else None
        i, j, k = pl.program_id(0), pl.program_id(1), pl.program_id(2)
        if exchange is not None:
            sems, scr = scr[:3], scr[3:]
            src_ref = ins[-1] if into_idx is None else ins[-2]
            dst_ref = outs[-1] if into_idx is None else outs[-1].at[into_idx]

            @pl.when(jnp.logical_and(jnp.logical_and(i == 0, j == 0), k == 0))
            def _():
                _exchange_start(kind, src_ref, dst_ref, *sems)

        p = _dot(a_ref[...], b_ref[...], dims)
        if nk == 1:
            if add_refs is not None:
                p = p + _dot(add_refs[0][...], add_refs[1][...], NN)
            o_ref[...] = p.astype(out_dtype)
        else:
            acc_ref = scr[0]

            @pl.when(k == 0)
            def _():
                first = p if add_refs is None else p + _dot(add_refs[0][...], add_refs[1][...], NN)
                acc_ref[...] = first

            @pl.when(k > 0)
            def _():
                acc_ref[...] += p

            @pl.when(k == nk - 1)
            def _():
                o_ref[...] = acc_ref[...].astype(out_dtype)

        if side is not None:
            @pl.when(j == 0)
            def _():
                outs[1][...] = _dot(a_ref[...], side_ref[...], NT)

        if exchange is not None:
            @pl.when(jnp.logical_and(jnp.logical_and(i == ni - 1, j == nj - 1), k == nk - 1))
            def _():
                _exchange_wait(kind, src_ref, dst_ref, *sems)

    plain = exchange is None and side is None
    res = _call(body, name=name, out_shape=tuple(out_shape), grid=(ni, nj, nk), in_specs=in_specs,
                out_specs=tuple(out_specs), scratch=scratch, aliases=aliases,
                sem=("parallel", "parallel", "arbitrary") if plain else ("arbitrary",) * 3, vmem_mb=48)(*operands)
    return res[0] if n_out == 1 else res


def _rms(x):
    return lax.rsqrt(jnp.mean(x * x, axis=-1, keepdims=True) + EPS)


def _prenorm_f(x, g, sc, sh):
    return (x * _rms(x) * g) * (1.0 + sc) + sh


def _pick(is_ctx, ref):
    return jnp.where(is_ctx, ref[0:1, :], ref[1:2, :])


def _prenorm_fwd(X, g, sc2, sh2, n_ctx, tl=256):
    T, Dm = X.shape
    nct = n_ctx // tl

    def body(x_ref, g_ref, sc_ref, sh_ref, o_ref):
        is_ctx = pl.program_id(0) < nct
        o_ref[...] = _prenorm_f(x_ref[...], g_ref[...], _pick(is_ctx, sc_ref),
                                _pick(is_ctx, sh_ref)).astype(BF16)

    row = pl.BlockSpec((tl, Dm), lambda i: (i, 0))
    return _call(body, name="prenorm_fwd", out_shape=_sds((T, Dm), BF16), grid=(T // tl,),
                 in_specs=[row, pl.BlockSpec((1, Dm), lambda i: (0, 0)),
                           pl.BlockSpec((2, Dm), lambda i: (0, 0)), pl.BlockSpec((2, Dm), lambda i: (0, 0))],
                 out_specs=row, sem=("parallel",), vmem_mb=40)(X, g, sc2, sh2)


def _prenorm_bwd(X, g, sc2, sh2, d1, dres, n_ctx, tl=256):
    T, Dm = X.shape
    nct = n_ctx // tl

    def body(x_ref, g_ref, sc_ref, sh_ref, d1_ref, dres_ref, dx_ref, acc_ref):
        i = pl.program_id(0)
        is_ctx = i < nct

        @pl.when(i == 0)
        def _():
            acc_ref[...] = jnp.zeros_like(acc_ref)

        _, vjp = jax.vjp(_prenorm_f, x_ref[...], g_ref[...], _pick(is_ctx, sc_ref), _pick(is_ctx, sh_ref))
        dx, dg, dsc, dsh = vjp(d1_ref[...])
        dx_ref[...] = dres_ref[...] + dx
        zero = jnp.zeros_like(dsc)
        acc_ref[0:1, :] += dg
        acc_ref[1:2, :] += jnp.where(is_ctx, dsc, zero)
        acc_ref[2:3, :] += jnp.where(is_ctx, zero, dsc)
        acc_ref[3:4, :] += jnp.where(is_ctx, dsh, zero)
        acc_ref[4:5, :] += jnp.where(is_ctx, zero, dsh)

    row = pl.BlockSpec((tl, Dm), lambda i: (i, 0))
    return _call(body, name="prenorm_bwd", out_shape=(_sds((T, Dm), F32), _sds((8, Dm), F32)), grid=(T // tl,),
                 in_specs=[row, pl.BlockSpec((1, Dm), lambda i: (0, 0)),
                           pl.BlockSpec((2, Dm), lambda i: (0, 0)), pl.BlockSpec((2, Dm), lambda i: (0, 0)),
                           row, row],
                 out_specs=(row, pl.BlockSpec((8, Dm), lambda i: (0, 0))), sem=("arbitrary",),
                 vmem_mb=48)(X, g, sc2, sh2, d1, dres)


def _post_f(o, g, gate):
    return gate * ((o * _rms(o)) * g)


def _post_fwd(X, o, g, gate2, n_ctx, tl=256):
    T, Dm = X.shape
    nct = n_ctx // tl

    def body(x_ref, o_ref, g_ref, gate_ref, y_ref):
        is_ctx = pl.program_id(0) < nct
        y_ref[...] = x_ref[...] + _post_f(o_ref[...], g_ref[...], _pick(is_ctx, gate_ref))

    row = pl.BlockSpec((tl, Dm), lambda i: (i, 0))
    return _call(body, name="post_fwd", out_shape=_sds((T, Dm), F32), grid=(T // tl,),
                 in_specs=[row, row, pl.BlockSpec((1, Dm), lambda i: (0, 0)), pl.BlockSpec((2, Dm), lambda i: (0, 0))],
                 out_specs=row, sem=("parallel",), vmem_mb=40)(X, o, g, gate2)


def _post_bwd(o, g, gate2, dX, n_ctx, tl=256):
    T, Dm = o.shape
    nct = n_ctx // tl

    def body(o_ref, g_ref, gate_ref, dx_ref, do_ref, acc_ref):
        i = pl.program_id(0)
        is_ctx = i < nct

        @pl.when(i == 0)
        def _():
            acc_ref[...] = jnp.zeros_like(acc_ref)

        _, vjp = jax.vjp(_post_f, o_ref[...], g_ref[...], _pick(is_ctx, gate_ref))
        do, dg, dgate = vjp(dx_ref[...])
        do_ref[...] = do.astype(BF16)
        zero = jnp.zeros_like(dgate)
        acc_ref[0:1, :] += dg
        acc_ref[1:2, :] += jnp.where(is_ctx, dgate, zero)
        acc_ref[2:3, :] += jnp.where(is_ctx, zero, dgate)

    row = pl.BlockSpec((tl, Dm), lambda i: (i, 0))
    return _call(body, name="post_bwd", out_shape=(_sds((T, Dm), BF16), _sds((8, Dm), F32)), grid=(T // tl,),
                 in_specs=[row, pl.BlockSpec((1, Dm), lambda i: (0, 0)), pl.BlockSpec((2, Dm), lambda i: (0, 0)), row],
                 out_specs=(row, pl.BlockSpec((8, Dm), lambda i: (0, 0))), sem=("arbitrary",),
                 vmem_mb=48)(o, g, gate2, dX)


def _loss_kernel(X, target, n_ctx, tl=256):
    T, Dm = X.shape
    nct = n_ctx // tl

    def body(x_ref, t_ref, dx_ref, acc_ref):
        i = pl.program_id(0)

        @pl.when(i == 0)
        def _():
            acc_ref[...] = jnp.zeros_like(acc_ref)

        @pl.when(i < nct)
        def _():
            dx_ref[...] = jnp.zeros_like(dx_ref)

        @pl.when(i >= nct)
        def _():
            d = x_ref[...] - t_ref[...]
            dx_ref[...] = d * (1.0 / Dm)
            acc_ref[...] += jnp.sum(d * d, axis=0, keepdims=True)

    row = pl.BlockSpec((tl, Dm), lambda i: (i, 0))
    trow = pl.BlockSpec((tl, Dm), lambda i: (jnp.maximum(i - nct, 0), 0))
    return _call(body, name="loss", out_shape=(_sds((T, Dm), F32), _sds((1, Dm), F32)), grid=(T // tl,),
                 in_specs=[row, trow], out_specs=(row, pl.BlockSpec((1, Dm), lambda i: (0, 0))),
                 sem=("arbitrary",), vmem_mb=40)(X, target)


CONV_TL = 256
CONV_CB = 1024


def _conv_taps(x, reverse):
    rows = x.shape[0]
    pos = lax.broadcasted_iota(jnp.int32, (rows, 1), 0)
    taps = []
    for k in range(CONV_W):
        off = (2 - k) if reverse else (k - 2)
        if off == 0:
            taps.append(x)
            continue
        xs = pltpu.roll(x, (-off) % rows, 0)
        valid = (pos < rows - off) if off > 0 else (pos >= -off)
        taps.append(jnp.where(valid, xs, 0.0))
    return taps


def _conv_pre(taps, w, b):
    pre = b + taps[0] * w[0:1, :]
    for k in range(1, CONV_W):
        pre = pre + taps[k] * w[k:k + 1, :]
    return pre


def _conv_subtiles(tl, cbw, row_len):
    return [(r0, c0) for c0 in range(0, cbw, LANES) for r0 in range(0, tl, row_len)]


def _conv_fwd(z, cw, cb, n_ctx):
    T = z.shape[0]
    tl, cbw = CONV_TL, CONV_CB
    assert n_ctx == tl

    def body(z_ref, w_ref, b_ref, o_ref):
        def run(row_len):
            for r0, c0 in _conv_subtiles(tl, cbw, row_len):
                rs, cs = pl.ds(r0, row_len), pl.ds(c0, LANES)
                pre = _conv_pre(_conv_taps(z_ref[rs, cs].astype(F32), False), w_ref[:, cs], b_ref[:, cs])
                o_ref[rs, cs] = _silu(pre).astype(BF16)

        @pl.when(pl.program_id(1) == 0)
        def _():
            run(n_ctx)

        @pl.when(pl.program_id(1) > 0)
        def _():
            run(GRID_W)

    blk = pl.BlockSpec((tl, cbw), lambda j, i: (i, j))
    return _call(body, name="conv_fwd", out_shape=_sds((T, XBC_W), BF16), grid=(XBC_W // cbw, T // tl),
                 in_specs=[blk, pl.BlockSpec((8, cbw), lambda j, i: (0, j)), pl.BlockSpec((1, cbw), lambda j, i: (0, j))],
                 out_specs=blk, sem=("parallel", "parallel"))(z, cw, cb)


def _conv_bwd(z, cw, cb, d0, d1, dz_in, col0, n_ctx):
    T = z.shape[0]
    tl, cbw = CONV_TL, CONV_CB

    def body(z_ref, w_ref, b_ref, d0_ref, d1_ref, dzin_ref, dz_ref, dw_ref, db_ref):
        i = pl.program_id(1)

        @pl.when(i == 0)
        def _():
            dw_ref[...] = jnp.zeros_like(dw_ref)
            db_ref[...] = jnp.zeros_like(db_ref)

        def run(row_len):
            for c0 in range(0, cbw, LANES):
                cs = pl.ds(c0, LANES)
                w = w_ref[:, cs]
                dw = [jnp.zeros((1, LANES), F32) for _ in range(CONV_W)]
                db = jnp.zeros((1, LANES), F32)
                for r0 in range(0, tl, row_len):
                    rs = pl.ds(r0, row_len)
                    taps = _conv_taps(z_ref[rs, cs].astype(F32), False)
                    pre = _conv_pre(taps, w, b_ref[:, cs])
                    s = _sigmoid(pre)
                    dpre = (d0_ref[rs, cs].astype(F32) + d1_ref[rs, cs].astype(F32)) * (s + pre * s * (1.0 - s))
                    dz_ref[rs, cs] = _conv_pre(_conv_taps(dpre, True), w, 0.0).astype(BF16)
                    for k in range(CONV_W):
                        dw[k] = dw[k] + jnp.sum(dpre * taps[k], axis=0, keepdims=True)
                    db = db + jnp.sum(dpre, axis=0, keepdims=True)
                for k in range(CONV_W):
                    dw_ref[k:k + 1, cs] += dw[k]
                db_ref[0:1, cs] += db

        @pl.when(i == 0)
        def _():
            run(n_ctx)

        @pl.when(i > 0)
        def _():
            run(GRID_W)

    width = d0.shape[1]
    jo = col0 // cbw
    blk = pl.BlockSpec((tl, cbw), lambda j, i: (i, jo + j))
    dblk = pl.BlockSpec((tl, cbw), lambda j, i: (i, j))
    par = pl.BlockSpec((8, cbw), lambda j, i: (0, jo + j))
    opar = pl.BlockSpec((8, cbw), lambda j, i: (0, j))
    return _call(body, name=f"conv_bwd_{col0}",
                 out_shape=(_sds(dz_in.shape, BF16), _sds((8, width), F32), _sds((8, width), F32)),
                 grid=(width // cbw, T // tl),
                 in_specs=[blk, par, pl.BlockSpec((1, cbw), lambda j, i: (0, jo + j)), dblk, dblk, ANY],
                 out_specs=(blk, opar, opar), sem=("parallel", "arbitrary"), aliases={5: 0})(z, cw, cb, d0, d1, dz_in)


MIX_TL = 128
ROW_BLOCK = 16
COL_CHUNK = 512


def _for_row_blocks(n_rows, fn):
    def step(i, carry):
        fn(pl.ds(pl.multiple_of(i * ROW_BLOCK, ROW_BLOCK), ROW_BLOCK))
        return carry

    lax.fori_loop(0, n_rows // ROW_BLOCK, step, 0)


def _col_chunks(width):
    return [pl.ds(c0, COL_CHUNK) for c0 in range(0, width, COL_CHUNK)]


def _gate_fwd(yf, yb, z, g):
    T = yf.shape[0]
    tl = MIX_TL

    def body(yf_ref, yb_ref, zs_ref, g_ref, o_ref):
        t = (yf_ref[...].astype(F32) + yb_ref[...].astype(F32)) * _silu(zs_ref[...].astype(F32))
        o_ref[...] = (t * _rms(t) * g_ref[...]).astype(BF16)

    row = pl.BlockSpec((tl, SSD_W), lambda i: (i, 0))
    return _call(body, name="gate_fwd", out_shape=_sds((T, SSD_W + MLP_W), BF16), grid=(T // tl,),
                 in_specs=[row, row, pl.BlockSpec((tl, SSD_W), lambda i: (i, XBC_W // SSD_W)),
                           pl.BlockSpec((1, SSD_W), lambda i: (0, 0))],
                 out_specs=row, sem=("parallel",))(yf, yb, z, g)


def _gate_bwd(yf, yb, z, g, dycat, dz_in):
    T = yf.shape[0]
    tl = MIX_TL

    def body(yf_ref, yb_ref, zs_ref, g_ref, d_ref, dzin_ref, dy_ref, dz_ref, acc_ref):
        @pl.when(pl.program_id(0) == 0)
        def _():
            acc_ref[...] = jnp.zeros_like(acc_ref)

        def block(rs):
            def parts(cs):
                y = yf_ref[rs, cs].astype(F32) + yb_ref[rs, cs].astype(F32)
                zs = zs_ref[rs, cs].astype(F32)
                sg = _sigmoid(zs)
                return y, zs, sg, y * (zs * sg), d_ref[rs, cs].astype(F32)

            ss = jnp.zeros((ROW_BLOCK, 1), F32)
            su = jnp.zeros((ROW_BLOCK, 1), F32)
            for cs in _col_chunks(SSD_W):
                _, _, _, t, dd = parts(cs)
                ss = ss + jnp.sum(t * t, axis=1, keepdims=True)
                su = su + jnp.sum(t * (dd * g_ref[:, cs]), axis=1, keepdims=True)
            r = lax.rsqrt(ss * (1.0 / SSD_W) + EPS)
            c = r * r * r * (su * (1.0 / SSD_W))
            for cs in _col_chunks(SSD_W):
                y, zs, sg, t, dd = parts(cs)
                dt = r * (dd * g_ref[:, cs]) - t * c
                dy_ref[rs, cs] = (dt * (zs * sg)).astype(BF16)
                dz_ref[rs, cs] = (dt * y * (sg * (1.0 + zs * (1.0 - sg)))).astype(BF16)
                acc_ref[0:1, cs] += jnp.sum(dd * t * r, axis=0, keepdims=True)

        _for_row_blocks(tl, block)

    row = pl.BlockSpec((tl, SSD_W), lambda i: (i, 0))
    zs_spec = pl.BlockSpec((tl, SSD_W), lambda i: (i, XBC_W // SSD_W))
    return _call(body, name="gate_bwd",
                 out_shape=(_sds((T, SSD_W), BF16), _sds(dz_in.shape, BF16), _sds((8, SSD_W), F32)),
                 grid=(T // tl,),
                 in_specs=[row, row, zs_spec, pl.BlockSpec((1, SSD_W), lambda i: (0, 0)), row, ANY],
                 out_specs=(row, zs_spec, pl.BlockSpec((8, SSD_W), lambda i: (0, 0))),
                 sem=("arbitrary",), aliases={5: 1}, vmem_mb=40)(yf, yb, z, g, dycat, dz_in)


def _vnorm_f(v, gv):
    return v * _rms(v) * gv


def _mlp_out_f(u, sg, zm, gm):
    t = u * sg * _silu(zm)
    return t * _rms(t) * gm


U_BLK = (XBC_W + SSD_W) // MLP_W


def _mlp_mix(ws_ref, bst_ref, vn_s, sg_s):
    for gi in range(MLP_GROUPS):
        cols = pl.ds(gi * LANES, LANES)
        sg_s[:, cols] = _dot(ws_ref[gi], vn_s[:, cols], NN) + bst_ref[:, gi:gi + 1]


def _mlp_fwd(z, gv, ws, bst, gm, ycat_in):
    T = z.shape[0]
    tl = CHUNK

    def body(u_ref, v_ref, zm_ref, gv_ref, ws_ref, bst_ref, gm_ref, yin_ref, o_ref, vn_s, sg_s):
        vn_s[...] = _vnorm_f(v_ref[...].astype(F32), gv_ref[...]).astype(BF16)
        _mlp_mix(ws_ref, bst_ref, vn_s, sg_s)
        o_ref[...] = _mlp_out_f(u_ref[...].astype(F32), sg_s[...], zm_ref[...].astype(F32),
                                gm_ref[...]).astype(BF16)

    def zblk(b):
        return pl.BlockSpec((tl, MLP_W), lambda i: (i, b))

    vec = pl.BlockSpec((1, MLP_W), lambda i: (0, 0))
    return _call(body, name="mlp_fwd", out_shape=_sds(ycat_in.shape, BF16), grid=(T // tl,),
                 in_specs=[zblk(U_BLK), zblk(U_BLK + 1), zblk(U_BLK + 2), vec,
                           pl.BlockSpec((MLP_GROUPS, CHUNK, CHUNK), lambda i: (0, 0, 0)),
                           pl.BlockSpec((CHUNK, LANES), lambda i: (0, 0)), vec, ANY],
                 out_specs=pl.BlockSpec((tl, MLP_W), lambda i: (i, 1)),
                 scratch=[pltpu.VMEM((tl, MLP_W), BF16), pltpu.VMEM((tl, MLP_W), F32)],
                 sem=("parallel",), aliases={7: 0})(z, z, z, gv, ws, bst, gm, ycat_in)


def _mlp_bwd(z, gv, ws, bst, gm, dycat, dz_in):
    T = z.shape[0]
    tl = CHUNK

    def body(u_ref, v_ref, zm_ref, gv_ref, ws_ref, bst_ref, gm_ref, d_ref, dzin_ref,
             dz_ref, acc_ref, dws_ref, dbst_ref, vn_s, sg_s, dvn_s):
        @pl.when(pl.program_id(0) == 0)
        def _():
            acc_ref[...] = jnp.zeros_like(acc_ref)
            dws_ref[...] = jnp.zeros_like(dws_ref)
            dbst_ref[...] = jnp.zeros_like(dbst_ref)

        v = v_ref[...].astype(F32)
        vn, vjp_v = jax.vjp(_vnorm_f, v, gv_ref[...])
        vn_s[...] = vn.astype(BF16)
        _mlp_mix(ws_ref, bst_ref, vn_s, sg_s)
        _, vjp_o = jax.vjp(_mlp_out_f, u_ref[...].astype(F32), sg_s[...], zm_ref[...].astype(F32), gm_ref[...])
        du, dsg, dzm, dgm = vjp_o(d_ref[...].astype(F32))
        sg_s[...] = dsg
        for gi in range(MLP_GROUPS):
            cols = pl.ds(gi * LANES, LANES)
            dsg_g = sg_s[:, cols]
            dvn_s[:, cols] = _dot(ws_ref[gi], dsg_g, TN)
            dws_ref[gi] += _dot(dsg_g, vn_s[:, cols], NT)
            dbst_ref[:, gi:gi + 1] += jnp.sum(dsg_g, axis=1, keepdims=True)
        dv, dgv = vjp_v(dvn_s[...])
        dz_ref[:, 0:MLP_W] = du.astype(BF16)
        dz_ref[:, MLP_W:2 * MLP_W] = dv.astype(BF16)
        dz_ref[:, 2 * MLP_W:3 * MLP_W] = dzm.astype(BF16)
        acc_ref[0:1, :] += dgv
        acc_ref[1:2, :] += dgm

    def zblk(b):
        return pl.BlockSpec((tl, MLP_W), lambda i: (i, b))

    vec = pl.BlockSpec((1, MLP_W), lambda i: (0, 0))
    ws_spec = pl.BlockSpec((MLP_GROUPS, CHUNK, CHUNK), lambda i: (0, 0, 0))
    bst_spec = pl.BlockSpec((CHUNK, LANES), lambda i: (0, 0))
    return _call(body, name="mlp_bwd",
                 out_shape=(_sds(dz_in.shape, BF16), _sds((8, MLP_W), F32),
                            _sds((MLP_GROUPS, CHUNK, CHUNK), F32), _sds((CHUNK, LANES), F32)),
                 grid=(T // tl,),
                 in_specs=[zblk(U_BLK), zblk(U_BLK + 1), zblk(U_BLK + 2), vec, ws_spec, bst_spec, vec,
                           pl.BlockSpec((tl, MLP_W), lambda i: (i, 1)), ANY],
                 out_specs=(pl.BlockSpec((tl, 3 * MLP_W), lambda i: (i, 1)),
                            pl.BlockSpec((8, MLP_W), lambda i: (0, 0)), ws_spec, bst_spec),
                 scratch=[pltpu.VMEM((tl, MLP_W), BF16), pltpu.VMEM((tl, MLP_W), F32), pltpu.VMEM((tl, MLP_W), F32)],
                 sem=("arbitrary",), aliases={8: 0}, vmem_mb=48)(z, z, z, gv, ws, bst, gm, dycat, dz_in)


def _scan_mask(direction):
    tri = np.tril(np.ones((SSD_Q, SSD_Q), np.float32))
    return jnp.asarray(tri.T if direction == 1 else tri)


def _ssd_prep(raw, prm):
    T = raw.shape[0]
    nc, Q = T // SSD_Q, SSD_Q
    tril = jnp.asarray(np.tril(np.ones((Q, Q), np.float32)))
    triu = jnp.asarray(np.triu(np.ones((Q, Q), np.float32)))

    def body(raw_ref, prm_ref, tril_ref, triu_ref, dt_ref, s_ref, st_ref, etot_ref, pkt_ref):
        dt = _softplus(raw_ref[...] + prm_ref[0:1, :])
        a = dt * (-jnp.exp(prm_ref[1:2, :]))
        lane = lax.broadcasted_iota(jnp.int32, (1, LANES), 1)
        s = jnp.where(lane < SSD_HEADS, _dot_hi(tril_ref[...], a), _dot_hi(triu_ref[...], a))
        tot = jnp.sum(a, axis=0, keepdims=True)
        dt_ref[...] = dt
        s_ref[...] = s
        for n, v in enumerate((dt, jnp.exp(s), jnp.exp(tot - s))):
            pkt_ref[0, n * LANES:(n + 1) * LANES, :] = v.T
        st_ref[0] = s.T
        etot_ref[0] = _bcast8(jnp.exp(tot))

    row = pl.BlockSpec((Q, LANES), lambda c: (c, 0))
    cst = lambda shape: pl.BlockSpec(shape, lambda c: (0, 0))
    tl = _sds((T, LANES), F32)
    return _call(body, name="ssd_prep",
                 out_shape=(tl, tl, _sds((nc, LANES, Q), F32), _sds((nc, 8, LANES), F32), _sds((nc, 3 * LANES, Q), F32)),
                 grid=(nc,), in_specs=[row, cst((8, LANES)), cst((Q, Q)), cst((Q, Q))],
                 out_specs=(row, row, pl.BlockSpec((1, LANES, Q), lambda c: (c, 0, 0)),
                            pl.BlockSpec((1, 8, LANES), lambda c: (c, 0, 0)),
                            pl.BlockSpec((1, 3 * LANES, Q), lambda c: (c, 0, 0))),
                 sem=("parallel",))(raw, prm, tril, triu)


def _chunk_order(direction, nc, nctx):
    if direction == 0:
        return lambda c: c
    return lambda c: jnp.where(c < nctx, nctx - 1 - c, nc - 1 + nctx - c)


def _bcast8(row):
    return jnp.broadcast_to(row, (8, row.shape[1]))


GPS = 8


def _cols(ref, k, width):
    return ref[:, k * width:(k + 1) * width]


def _ssd_fwd(xbc, prep, dskc, direction, n_ctx):
    T = xbc.shape[0]
    nc, nctx = T // SSD_Q, n_ctx // SSD_Q
    order = _chunk_order(direction, nc, nctx)
    tri_t = jnp.transpose(_scan_mask(direction))
    Q, GC, N, G, HP, HD = SSD_Q, GROUP_COLS, SSD_STATE, SSD_GROUPS, HEADS_PER_GROUP, SSD_HEAD_DIM
    _, s, s_t, etot, pk_t = prep

    def body(x_ref, b_ref, c_ref, s_ref, st_ref, etot_ref, dtr_ref, er_ref, dr_ref, dsk_ref, trit_ref,
             y_ref, sst_ref, S):
        c = pl.program_id(0)
        mask_t = trit_ref[...] > 0.5
        lane = lax.broadcasted_iota(jnp.int32, (1, LANES), 1)
        rowi = lax.broadcasted_iota(jnp.int32, (GC, 1), 0)
        s_all = s_ref[...]
        etot_row = etot_ref[0, 0:1, :]
        for k in range(GPS):
            g = pl.program_id(1) * GPS + k
            base = direction * SSD_HEADS + HP * g

            @pl.when(c == 0)
            def _():
                S[g] = jnp.zeros((GC, N), F32)

            S0 = S[g]
            sst_ref[0, k] = S0
            Bm, Cm = _cols(b_ref, k, N), _cols(c_ref, k, N)

            def head_rows(ref):
                return jnp.concatenate([jnp.broadcast_to(ref[0, HP * k + r:HP * k + r + 1, :], (HD, Q))
                                        for r in range(HP)], axis=0)

            x_t = jnp.transpose(_cols(x_ref, k, GC).astype(F32))
            X_t = x_t * head_rows(dtr_ref)
            X_tb = X_t.astype(BF16)
            y_t = head_rows(er_ref) * _dot(S0, Cm, NT)
            G_t = _dot(Bm, Cm, NT)
            parts = []
            for r in range(HP):
                col = jnp.sum(jnp.where(lane == base + r, s_all, 0.0), axis=1, keepdims=True)
                L_t = jnp.exp(jnp.where(mask_t, st_ref[0, HP * k + r:HP * k + r + 1, :] - col, NEG_BIG))
                parts.append(_dot(X_tb[HD * r:HD * (r + 1), :], G_t * L_t, NN))
            y_t = y_t + jnp.concatenate(parts, axis=0)
            if direction == 0:
                y_t = y_t + dsk_ref[k] * x_t
            y_ref[:, k * GC:(k + 1) * GC] = jnp.transpose(y_t).astype(BF16)
            e = [jnp.sum(jnp.where(lane == base + r, etot_row, 0.0), axis=1, keepdims=True) for r in range(HP)]
            etot_c = jnp.where(rowi < HD, e[0], jnp.where(rowi < 2 * HD, e[1], jnp.where(rowi < 3 * HD, e[2], e[3])))
            S[g] = etot_c * S0 + _dot(X_t * head_rows(dr_ref), Bm, NN)

    hrows = HP * GPS
    sec = LANES // hrows
    row_block = lambda n: pl.BlockSpec((1, hrows, Q), lambda c, g: (order(c), n * sec + direction * (SSD_HEADS // hrows) + g, 0))
    in_specs = [
        pl.BlockSpec((Q, GPS * GC), lambda c, g: (order(c), g)),
        pl.BlockSpec((Q, GPS * N), lambda c, g: (order(c), SSD_W // (GPS * N) + g)),
        pl.BlockSpec((Q, GPS * N), lambda c, g: (order(c), (SSD_W + G * N) // (GPS * N) + g)),
        pl.BlockSpec((Q, LANES), lambda c, g: (order(c), 0)),
        row_block(0),
        pl.BlockSpec((1, 8, LANES), lambda c, g: (order(c), 0, 0)),
        row_block(0), row_block(1), row_block(2),
        pl.BlockSpec((GPS, GC, Q), lambda c, g: (g, 0, 0)),
        pl.BlockSpec((Q, Q), lambda c, g: (0, 0)),
    ]
    out_specs = (pl.BlockSpec((Q, GPS * GC), lambda c, g: (order(c), g)),
                 pl.BlockSpec((1, GPS, GC, N), lambda c, g: (order(c), g, 0, 0)))
    return _call(body, name=f"ssd_fwd_{direction}",
                 out_shape=(_sds((T, SSD_W), BF16), _sds((nc, G, GC, N), F32)),
                 grid=(nc, G // GPS), in_specs=in_specs, out_specs=out_specs,
                 scratch=[pltpu.VMEM((G, GC, N), F32)], sem=("arbitrary", "arbitrary"), vmem_mb=48,
                 )(xbc, xbc, xbc, s, s_t, etot, pk_t, pk_t, pk_t, dskc, tri_t)


def _ssd_bwd_t(xbc, prep, dskc, sst, dy, direction, n_ctx):
    T = xbc.shape[0]
    nc, nctx = T // SSD_Q, n_ctx // SSD_Q
    fwd_order = _chunk_order(direction, nc, nctx)
    order = lambda c: fwd_order(nc - 1 - c)
    tri = _scan_mask(direction)
    tri_t = jnp.transpose(tri)
    Q, GC, N, G, HP, HD = SSD_Q, GROUP_COLS, SSD_STATE, SSD_GROUPS, HEADS_PER_GROUP, SSD_HEAD_DIM
    hrows = HP * GPS
    _, s, s_t, etot, pk_t = prep

    def body(x_ref, b_ref, c_ref, s_ref, st_ref, etot_ref, dtr_ref, er_ref, dr_ref, dsk_ref, sst_ref, dy_ref,
             trit_ref, tri_ref, dx_ref, db_ref, dc_ref, rows_ref, dskg_ref, dS):
        c = pl.program_id(0)

        @pl.when(c == 0)
        def _():
            dskg_ref[...] = jnp.zeros_like(dskg_ref)

        mask_t = trit_ref[...] > 0.5
        not_tri = tri_ref[...] < 0.5
        tri_b = tri_ref[...].astype(BF16)
        lane = lax.broadcasted_iota(jnp.int32, (1, LANES), 1)
        rowi = lax.broadcasted_iota(jnp.int32, (GC, 1), 0)
        rowh = lax.broadcasted_iota(jnp.int32, (hrows, 1), 0)
        s_all = s_ref[...]
        etot_row = etot_ref[0, 0:1, :]
        out_rows = [jnp.zeros((hrows, Q), F32) for _ in range(4)]
        for k in range(GPS):
            g = pl.program_id(1) * GPS + k
            base = direction * SSD_HEADS + HP * g

            @pl.when(c == 0)
            def _():
                dS[g] = jnp.zeros((GC, N), F32)

            dS1 = dS[g]
            S0 = sst_ref[0, k]
            Bm, Cm = _cols(b_ref, k, N), _cols(c_ref, k, N)

            def head_rows(ref):
                return jnp.concatenate([jnp.broadcast_to(ref[0, HP * k + r:HP * k + r + 1, :], (HD, Q))
                                        for r in range(HP)], axis=0)

            def head_sum(v, r):
                return jnp.sum(v[HD * r:HD * (r + 1), :], axis=0, keepdims=True)

            x_t = jnp.transpose(_cols(x_ref, k, GC).astype(F32))
            dY_t = jnp.transpose(_cols(dy_ref, k, GC).astype(F32))
            dt_b, E_b, D_b = head_rows(dtr_ref), head_rows(er_ref), head_rows(dr_ref)
            X_t = x_t * dt_b
            X_tb, dY_tb = X_t.astype(BF16), dY_t.astype(BF16)
            e = [jnp.sum(jnp.where(lane == base + r, etot_row, 0.0), axis=1, keepdims=True) for r in range(HP)]
            etot_c = jnp.where(rowi < HD, e[0], jnp.where(rowi < 2 * HD, e[1], jnp.where(rowi < 3 * HD, e[2], e[3])))

            CS_t = _dot(S0, Cm, NT)
            dCS_t = dY_t * E_b
            dC = _dot(dCS_t, S0, TN)
            dS0 = etot_c * dS1 + _dot(dCS_t, Cm, NN)
            dsx_t = dY_t * (E_b * CS_t)
            dtot_c = jnp.sum(dS1 * S0, axis=1, keepdims=True) * etot_c
            XD_t = X_t * D_b
            dXD_t = _dot(dS1, Bm, NT)
            dB = _dot(XD_t, dS1, TN)
            dX_t = dXD_t * D_b
            t_t = dXD_t * XD_t
            dsx_t = dsx_t - t_t
            G_t = _dot(Bm, Cm, NT)
            dG_t = jnp.zeros((Q, Q), F32)
            dx_parts = []
            for r in range(HP):
                col = jnp.sum(jnp.where(lane == base + r, s_all, 0.0), axis=1, keepdims=True)
                L_t = jnp.exp(jnp.where(mask_t, st_ref[0, HP * k + r:HP * k + r + 1, :] - col, NEG_BIG))
                W_t = G_t * L_t
                xr, dyr = X_tb[HD * r:HD * (r + 1), :], dY_tb[HD * r:HD * (r + 1), :]
                dW_t = _dot(xr, dyr, TN)
                dx_parts.append(_dot(dyr, W_t, NT))
                dG_t = dG_t + dW_t * L_t
                P = lax.dot_general((dW_t * W_t).astype(BF16), tri_b, NN, preferred_element_type=F32)
                da_row = jnp.sum(jnp.where(not_tri, P, 0.0), axis=0, keepdims=True)
                out_rows[2] = jnp.where(rowh == HP * k + r, da_row, out_rows[2])
            dX_t = dX_t + jnp.concatenate(dx_parts, axis=0)
            dB = dB + _dot(dG_t, Cm, NN)
            dC = dC + _dot(dG_t, Bm, TN)
            dx_t = dX_t * dt_b
            if direction == 0:
                dx_t = dx_t + dY_t * dsk_ref[k]
            dxx = dX_t * x_t
            dyx = dY_t * x_t
            for r in range(HP):
                here = rowh == HP * k + r
                out_rows[0] = jnp.where(here, head_sum(dsx_t, r), out_rows[0])
                out_rows[1] = jnp.where(here, head_sum(dxx, r), out_rows[1])
                tot_r = (jnp.sum(dtot_c[HD * r:HD * (r + 1), :], axis=0, keepdims=True)
                         + jnp.sum(head_sum(t_t, r), axis=1, keepdims=True))
                out_rows[3] = jnp.where(here, tot_r, out_rows[3])
                if direction == 0:
                    dsk_r = jnp.sum(head_sum(dyx, r), axis=1, keepdims=True)
                    dskg_ref[HP * k + r:HP * k + r + 1, :] += jnp.broadcast_to(dsk_r, (1, LANES))
            dx_ref[:, k * GC:(k + 1) * GC] = jnp.transpose(dx_t).astype(BF16)
            db_ref[:, k * N:(k + 1) * N] = dB.astype(BF16)
            dc_ref[:, k * N:(k + 1) * N] = dC.astype(BF16)
            dS[g] = dS0
        for n in range(4):
            rows_ref[0, n] = out_rows[n]

    sec = LANES // hrows
    row_block = lambda n: pl.BlockSpec((1, hrows, Q), lambda c, g: (order(c), n * sec + direction * (SSD_HEADS // hrows) + g, 0))
    cst2 = lambda shape: pl.BlockSpec(shape, lambda c, g: (0, 0))
    in_specs = [
        pl.BlockSpec((Q, GPS * GC), lambda c, g: (order(c), g)),
        pl.BlockSpec((Q, GPS * N), lambda c, g: (order(c), SSD_W // (GPS * N) + g)),
        pl.BlockSpec((Q, GPS * N), lambda c, g: (order(c), (SSD_W + G * N) // (GPS * N) + g)),
        pl.BlockSpec((Q, LANES), lambda c, g: (order(c), 0)),
        row_block(0),
        pl.BlockSpec((1, 8, LANES), lambda c, g: (order(c), 0, 0)),
        row_block(0), row_block(1), row_block(2),
        pl.BlockSpec((GPS, GC, Q), lambda c, g: (g, 0, 0)),
        pl.BlockSpec((1, GPS, GC, N), lambda c, g: (order(c), g, 0, 0)),
        pl.BlockSpec((Q, GPS * GC), lambda c, g: (order(c), g)),
        cst2((Q, Q)), cst2((Q, Q)),
    ]
    out_specs = (pl.BlockSpec((Q, GPS * GC), lambda c, g: (order(c), g)),
                 pl.BlockSpec((Q, GPS * N), lambda c, g: (order(c), g)),
                 pl.BlockSpec((Q, GPS * N), lambda c, g: (order(c), g)),
                 pl.BlockSpec((1, 4, hrows, Q), lambda c, g: (order(c), 0, g, 0)),
                 pl.BlockSpec((hrows, LANES), lambda c, g: (g, 0)))
    return _call(body, name=f"ssd_bwd_{direction}",
                 out_shape=(_sds((T, SSD_W), BF16), _sds((T, G * N), BF16), _sds((T, G * N), BF16),
                            _sds((nc, 4, SSD_HEADS, Q), F32), _sds((SSD_HEADS, LANES), F32)),
                 grid=(nc, G // GPS), in_specs=in_specs, out_specs=out_specs,
                 scratch=[pltpu.VMEM((G, GC, N), F32)], sem=("arbitrary", "arbitrary"), vmem_mb=56,
                 )(xbc, xbc, xbc, s, s_t, etot, pk_t, pk_t, pk_t, dskc, sst, dy, tri_t, tri)


def _ssd_post_t(raw, prm, dt, rows0, rows1):
    T = raw.shape[0]
    nc, Q = T // SSD_Q, SSD_Q
    tril = jnp.asarray(np.tril(np.ones((Q, Q), np.float32)))
    triu = jnp.asarray(np.triu(np.ones((Q, Q), np.float32)))

    def body(raw_ref, prm_ref, dt_ref, r0_ref, r1_ref, tril_ref, triu_ref, draw_ref, dprm_ref):
        @pl.when(pl.program_id(0) == 0)
        def _():
            dprm_ref[...] = jnp.zeros_like(dprm_ref)

        def to_lanes(n):
            rows = jnp.concatenate([r0_ref[0, n], r1_ref[0, n], jnp.zeros((LANES - 2 * SSD_HEADS, Q), F32)], axis=0)
            return rows.T

        pre = raw_ref[...] + prm_ref[0:1, :]
        A = -jnp.exp(prm_ref[1:2, :])
        ds = to_lanes(0)
        lane = lax.broadcasted_iota(jnp.int32, (1, LANES), 1)
        da = jnp.where(lane < SSD_HEADS, _dot_hi(triu_ref[...], ds), _dot_hi(tril_ref[...], ds))
        da = da + to_lanes(2) + to_lanes(3)
        draw = (da * A + to_lanes(1)) * _sigmoid(pre)
        draw_ref[...] = draw
        dprm_ref[0:1, :] += jnp.sum(draw, axis=0, keepdims=True)
        dprm_ref[1:2, :] += jnp.sum(da * dt_ref[...], axis=0, keepdims=True) * A

    row = pl.BlockSpec((Q, LANES), lambda c: (c, 0))
    rows = pl.BlockSpec((1, 4, SSD_HEADS, Q), lambda c: (c, 0, 0, 0))
    cst = lambda shape: pl.BlockSpec(shape, lambda c: (0, 0))
    return _call(body, name="ssd_post", out_shape=(_sds((T, LANES), F32), _sds((8, LANES), F32)), grid=(nc,),
                 in_specs=[row, cst((8, LANES)), row, rows, rows, cst((Q, Q)), cst((Q, Q))],
                 out_specs=(row, cst((8, LANES))), sem=("arbitrary",))(raw, prm, dt, rows0, rows1, tril, triu)


ADA_ROWS = 16


def _ada_fwd(c16, w_ada, b_loc):
    depth, Dm, W = w_ada.shape

    def body(c_ref, w_ref, b_ref, o_ref):
        o_ref[0] = _dot_hi(_silu(c_ref[...]), w_ref[0]) + b_ref[0]

    return _call(body, name="ada_fwd", out_shape=_sds((depth, ADA_ROWS, W), F32), grid=(depth,),
                 in_specs=[pl.BlockSpec((ADA_ROWS, Dm), lambda l: (0, 0)),
                           pl.BlockSpec((1, Dm, W), lambda l: (l, 0, 0)),
                           pl.BlockSpec((1, 1, W), lambda l: (l, 0, 0))],
                 out_specs=pl.BlockSpec((1, ADA_ROWS, W), lambda l: (l, 0, 0)),
                 sem=("parallel",), vmem_mb=40)(c16, w_ada, b_loc)


def _ada_bwd(c16, w_ada, dmod):
    depth, Dm, W = w_ada.shape

    def body(c_ref, w_ref, d_ref, gw_ref, dc_ref):
        l = pl.program_id(0)

        @pl.when(l == 0)
        def _():
            dc_ref[...] = jnp.zeros_like(dc_ref)

        cc = c_ref[...]
        sg = _sigmoid(cc)
        gw_ref[0] = _dot_hi(cc * sg, d_ref[0], TN)
        dsc = _dot_hi(d_ref[0], w_ref[0], NT)
        dc_ref[...] += dsc[8:16, :] * (sg + cc * sg * (1.0 - sg))[8:16, :]

    return _call(body, name="ada_bwd", out_shape=(_sds((depth, Dm, W), F32), _sds((8, Dm), F32)), grid=(depth,),
                 in_specs=[pl.BlockSpec((ADA_ROWS, Dm), lambda l: (0, 0)),
                           pl.BlockSpec((1, Dm, W), lambda l: (l, 0, 0)),
                           pl.BlockSpec((1, ADA_ROWS, W), lambda l: (l, 0, 0))],
                 out_specs=(pl.BlockSpec((1, Dm, W), lambda l: (l, 0, 0)), pl.BlockSpec((8, Dm), lambda l: (0, 0))),
                 sem=("arbitrary",), vmem_mb=48)(c16, w_ada, dmod)


def _sum_slabs(g, name, tr=512):
    _, R, C = g.shape
    tr = min(tr, R)

    def body(g_ref, o_ref):
        acc = g_ref[0]
        for k in range(1, N_DEV):
            acc = acc + g_ref[k]
        o_ref[...] = acc

    return _call(body, name=name, out_shape=_sds((R, C), F32), grid=(R // tr,),
                 in_specs=[pl.BlockSpec((N_DEV, tr, C), lambda i: (0, i, 0))],
                 out_specs=pl.BlockSpec((tr, C), lambda i: (i, 0)), sem=("parallel",), vmem_mb=40)(g)


def _adamw_math(w, g, m, v):
    m = ADAM_B1 * m + (1.0 - ADAM_B1) * g
    v = ADAM_B2 * v + (1.0 - ADAM_B2) * (g * g)
    m_hat = m / (1.0 - ADAM_B1 ** ADAM_STEP)
    v_hat = v / (1.0 - ADAM_B2 ** ADAM_STEP)
    delta = -ADAM_LR * (m_hat / (jnp.sqrt(v_hat) + ADAM_EPS) + ADAM_WD * w)
    return delta, m, v


def _adamw(w, g, m, v, name, tr):
    R, C = w.shape
    tr = min(tr, R)
    assert R % tr == 0

    def body(w_ref, g_ref, m_ref, v_ref, d_ref, nm_ref, nv_ref):
        d, nm, nv = _adamw_math(w_ref[...], g_ref[...], m_ref[...], v_ref[...])
        d_ref[...] = d
        nm_ref[...] = nm
        nv_ref[...] = nv

    blk = pl.BlockSpec((tr, C), lambda i: (i, 0))
    out = _sds((R, C), F32)
    return _call(body, name=name, out_shape=(out, out, out), grid=(R // tr,), in_specs=[blk] * 4,
                 out_specs=(blk, blk, blk), sem=("parallel",), vmem_mb=40)(w, g, m, v)


def _adamw_slabs(w, slabs, m, v, name, tr, tc):
    depth, R, C = w.shape
    assert R % tr == 0 and C % tc == 0

    def body(w_ref, s_ref, m_ref, v_ref, g_ref, d_ref, nm_ref, nv_ref):
        g = s_ref[0, 0].astype(F32)
        for k in range(1, N_DEV):
            g = g + s_ref[0, k].astype(F32)
        d, nm, nv = _adamw_math(w_ref[0], g, m_ref[0], v_ref[0])
        g_ref[0] = g
        d_ref[0] = d
        nm_ref[0] = nm
        nv_ref[0] = nv

    blk = pl.BlockSpec((1, tr, tc), lambda l, i, j: (l, i, j))
    out = _sds((depth, R, C), F32)
    return _call(body, name=name, out_shape=(out, out, out, out), grid=(depth, R // tr, C // tc),
                 in_specs=[blk, pl.BlockSpec((1, N_DEV, tr, tc), lambda l, i, j: (l, 0, i, j)), blk, blk],
                 out_specs=(blk, blk, blk, blk), sem=("parallel", "parallel", "parallel"), vmem_mb=56)(w, slabs, m, v)


PACK_QUANTUM = 512 * LANES


def _pack(arrays):
    flat = jnp.concatenate([a.reshape(-1).astype(F32) for a in arrays])
    pad = (-flat.shape[0]) % PACK_QUANTUM
    return jnp.pad(flat, (0, pad)).reshape(-1, LANES)


def _unpack(bundle, shapes):
    flat = bundle.reshape(-1)
    out, off = [], 0
    for shp in shapes:
        n = int(np.prod(shp))
        out.append(flat[off:off + n].reshape(shp))
        off += n
    return out


def _row(v):
    return v.reshape(1, -1)


def _pad_rows(a, rows):
    return jnp.pad(a, ((0, rows - a.shape[0]), (0, 0)))


def kernel(x, c, ctx, c_ctx, w_ada, b_ada, g_pre, g_post, w_in, conv_w, conv_b, dt_bias, a_log, d_skip, g_ssd, g_v, w_s, b_s, g_mlp, w_out, loss_target, m_c_ctx, m_w_ada, m_b_ada, m_g_pre, m_g_post, m_w_in, m_conv_w, m_conv_b, m_dt_bias, m_a_log, m_d_skip, m_g_ssd, m_g_v, m_w_s, m_b_s, m_g_mlp, m_w_out, v_c_ctx, v_w_ada, v_b_ada, v_g_pre, v_g_post, v_w_in, v_conv_w, v_conv_b, v_dt_bias, v_a_log, v_d_skip, v_g_ssd, v_g_v, v_w_s, v_b_s, v_g_mlp, v_w_out):
    depth = w_in.shape[0]
    L = x.shape[1]
    n_ctx = ctx.shape[1]
    T = n_ctx + L
    Dm = D_MODEL
    me = _lin(_my_pos())
    ada_w = w_ada.shape[2]
    in_w = w_in.shape[2]
    out_r = w_out.shape[1]
    conv_c = conv_w.shape[2]
    TM = 768

    c_all = _all_gather_small(_pad_rows(c, 8), "gather_c")[:, 0, :]
    c16 = _pad_rows(jnp.concatenate([c_all, _row(c_ctx)], axis=0), ADA_ROWS)
    b_loc = lax.dynamic_slice_in_dim(b_ada, me * ada_w, ada_w, axis=1)[:, None, :]
    mod_loc = _ada_fwd(c16, w_ada, b_loc)
    mod_all = _all_gather_small(mod_loc.reshape(depth * ADA_ROWS, ada_w), "gather_mod")
    mod_all = mod_all.reshape(N_DEV, depth, ADA_ROWS, ada_w)
    mod_me = lax.dynamic_index_in_dim(mod_all, me, axis=2, keepdims=False)
    mod_me = jnp.transpose(mod_me, (1, 0, 2)).reshape(depth, N_DEV * ada_w)
    mod_cx = jnp.transpose(mod_all[:, :, 8, :], (1, 0, 2)).reshape(depth, N_DEV * ada_w)
    shift2 = jnp.stack([mod_cx[:, 0:Dm], mod_me[:, 0:Dm]], axis=1)
    scale2 = jnp.stack([mod_cx[:, Dm:2 * Dm], mod_me[:, Dm:2 * Dm]], axis=1)
    gate2 = jnp.stack([mod_cx[:, 2 * Dm:], mod_me[:, 2 * Dm:]], axis=1)

    w_in_t, m_w_in_t, v_w_in_t = (jnp.swapaxes(a, 1, 2) for a in (w_in, m_w_in, v_w_in))
    w_in_bf = w_in_t.astype(BF16)
    w_out_bf = w_out.astype(BF16)
    w_in_g = _all_gather_big(w_in_bf[0], "gather_w_in")
    w_out_g = _all_gather_big(w_out_bf[0], "gather_w_out")
    conv_all = _all_gather_small(_pad_rows(conv_w.reshape(depth * CONV_W, conv_c), 24).reshape(24, conv_c),
                                 "gather_conv_w")
    conv_full = jnp.transpose(conv_all[:, :depth * CONV_W, :], (1, 0, 2)).reshape(depth, CONV_W, XBC_W)

    def in_weights(gathered):
        wf = gathered.reshape(IN_W, Dm)
        return (jnp.concatenate([wf[:XBC_W], wf[XBC_W + DT_W:]], axis=0),
                jnp.pad(wf[XBC_W:XBC_W + DT_W], ((0, LANES - DT_W), (0, 0))))

    w_main, w_dt, w_o = [None] * depth, [None] * depth, [None] * depth

    def ssd_prm(l):
        rows = jnp.stack([jnp.pad(dt_bias[l].reshape(-1), (0, LANES - DT_W)),
                          jnp.pad(a_log[l].reshape(-1), (0, LANES - DT_W)),
                          jnp.pad(d_skip[l], (0, LANES - SSD_HEADS))])
        return _pad_rows(rows, 8)

    ws_bf = w_s.astype(BF16)
    bst = jnp.pad(jnp.transpose(b_s, (0, 2, 1)), ((0, 0), (0, 0), (0, LANES - MLP_GROUPS)))

    X = jnp.concatenate([ctx[0], x[0]], axis=0)
    saved = []
    for l in range(depth):
        w_main[l], w_dt[l] = in_weights(w_in_g)
        w_o[l] = w_out_g.reshape(N_DEV * out_r, Dm)
        hx = _prenorm_fwd(X, _row(g_pre[l]), scale2[l], shift2[l], n_ctx)
        if l + 1 < depth:
            z, raw, w_in_g = _mm(hx, w_main[l], "nt", BF16, "in_proj_gather", TM, 1024, 2048,
                                 exchange=("gather", w_in_bf[l + 1]), side=w_dt[l])
        else:
            z, raw = _mm(hx, w_main[l], "nt", BF16, "in_proj", TM, 1024, 2048, side=w_dt[l])
        cw = _pad_rows(conv_full[l], 8)
        cb = _row(conv_b[l])
        xbc = _conv_fwd(z, cw, cb, n_ctx)
        prm = ssd_prm(l)
        prep = _ssd_prep(raw, prm)
        dskc = jnp.broadcast_to(jnp.repeat(d_skip[l], SSD_HEAD_DIM).reshape(SSD_GROUPS, GROUP_COLS, 1),
                                (SSD_GROUPS, GROUP_COLS, SSD_Q))
        y_f, sst_f = _ssd_fwd(xbc, prep, dskc, 0, n_ctx)
        y_b, sst_b = _ssd_fwd(xbc, prep, dskc, 1, n_ctx)
        ycat = _gate_fwd(y_f, y_b, z, _row(g_ssd[l]))
        ycat = _mlp_fwd(z, _row(g_v[l]), ws_bf[l], bst[l], _row(g_mlp[l]), ycat)
        if l + 1 < depth:
            o, w_out_g = _mm(ycat, w_o[l], "nn", F32, "out_proj_gather", TM, 1024, 4096,
                             exchange=("gather", w_out_bf[l + 1]))
        else:
            o = _mm(ycat, w_o[l], "nn", F32, "out_proj", TM, 1024, 4096)
        saved.append((X, hx, z, raw, cw, cb, xbc, prm, prep, dskc, y_f, sst_f, y_b, sst_b, ycat, o))
        X = _post_fwd(X, o, _row(g_post[l]), gate2[l], n_ctx)

    dX, sq = _loss_kernel(X, loss_target[0], n_ctx)
    loss = lax.psum(0.5 * jnp.sum(sq) / Dm, ("x", "y", "c"))

    g_small = {k: [None] * depth for k in
               ("b_ada", "g_pre", "g_post", "conv_b", "dt_bias", "a_log", "d_skip", "g_ssd", "g_v", "w_s", "b_s",
                "g_mlp", "conv_w", "dmod_c", "dmod")}
    in_recv = lax.empty((depth, N_DEV, in_w, Dm), BF16)
    out_recv = lax.empty((depth, N_DEV, out_r, Dm), BF16)
    for l in reversed(range(depth)):
        Xl, hx, z, raw, cw, cb, xbc, prm, prep, dskc, y_f, sst_f, y_b, sst_b, ycat, o = saved[l]
        d_o, acc_post = _post_bwd(o, _row(g_post[l]), gate2[l], dX, n_ctx)
        dycat = _mm(d_o, w_o[l], "nt", BF16, "out_proj_dx", TM, 1024, 2048)
        out_slabs = _mm(ycat, d_o, "tn", BF16, "out_proj_dw", 1024, 1024, 2816).reshape(N_DEV, out_r, Dm)
        dz = lax.empty((T, Z_MAIN), BF16)
        dy, dz, acc_gate = _gate_bwd(y_f, y_b, z, _row(g_ssd[l]), dycat, dz)
        dz, acc_mlp, dws, dbst = _mlp_bwd(z, _row(g_v[l]), ws_bf[l], bst[l], _row(g_mlp[l]), dycat, dz)
        dx0, db0, dc0, rows0, dsk0 = _ssd_bwd_t(xbc, prep, dskc, sst_f, dy, 0, n_ctx)
        dx1, db1, dc1, rows1, _ = _ssd_bwd_t(xbc, prep, dskc, sst_b, dy, 1, n_ctx)
        draw, dprm = _ssd_post_t(raw, prm, prep[0], rows0, rows1)
        gn = SSD_GROUPS * SSD_STATE
        dz, dcw_x, dcb_x = _conv_bwd(z, cw, cb, dx0, dx1, dz, 0, n_ctx)
        dz, dcw_b, dcb_b = _conv_bwd(z, cw, cb, db0, db1, dz, SSD_W, n_ctx)
        dz, dcw_c, dcb_c = _conv_bwd(z, cw, cb, dc0, dc1, dz, SSD_W + gn, n_ctx)
        dcw = jnp.concatenate([dcw_x, dcw_b, dcw_c], axis=1)
        dcb = jnp.concatenate([dcb_x, dcb_b, dcb_c], axis=1)
        gw_main, out_recv = _mm(dz, hx, "tn", BF16, "in_proj_dw_scatter", 1024, 1024, 2816,
                                exchange=("scatter", out_slabs, (out_recv, l)))
        gw_dt = _mm(draw, hx, "tn", BF16, "dt_proj_dw", LANES, 1024, 2816)
        in_slabs = jnp.concatenate([gw_main[:XBC_W], gw_dt[:DT_W], gw_main[XBC_W:]], axis=0).reshape(N_DEV, in_w, Dm)
        dhx, in_recv = _mm(dz, w_main[l], "nn", F32, "in_proj_dx_scatter", TM, 1024, 4096,
                           exchange=("scatter", in_slabs, (in_recv, l)), add=(draw, w_dt[l]))
        dX, acc_pre = _prenorm_bwd(Xl, _row(g_pre[l]), scale2[l], shift2[l], dhx, dX, n_ctx)

        dmod_c = jnp.concatenate([acc_pre[3], acc_pre[1], acc_post[1]])
        dmod_x = jnp.concatenate([acc_pre[4], acc_pre[2], acc_post[2]])
        g_small["dmod_c"][l] = dmod_c
        g_small["dmod"][l] = dmod_x
        g_small["b_ada"][l] = dmod_c + dmod_x
        g_small["g_pre"][l] = acc_pre[0]
        g_small["g_post"][l] = acc_post[0]
        g_small["conv_b"][l] = dcb[0]
        g_small["conv_w"][l] = dcw[:CONV_W]
        g_small["dt_bias"][l] = dprm[0, :DT_W].reshape(2, SSD_HEADS)
        g_small["a_log"][l] = dprm[1, :DT_W].reshape(2, SSD_HEADS)
        g_small["d_skip"][l] = dsk0[:, 0]
        g_small["g_ssd"][l] = acc_gate[0]
        g_small["g_v"][l] = acc_mlp[0]
        g_small["g_mlp"][l] = acc_mlp[1]
        g_small["w_s"][l] = dws
        g_small["b_s"][l] = jnp.transpose(dbst[:, :MLP_GROUPS])

    grad_x = dX[n_ctx:][None]

    summed_names = ["b_ada", "g_pre", "g_post", "conv_b", "dt_bias", "a_log", "d_skip", "g_ssd", "g_v", "w_s",
                    "b_s", "g_mlp", "conv_w", "dmod_c"]
    parts = [jnp.stack(g_small[k]) for k in summed_names] + [jnp.stack(g_small["dmod"])]
    shapes = [p.shape for p in parts]
    bundle = _pack(parts)
    gathered = _all_gather_big(bundle, "gather_small_grads")
    reduced = _unpack(_sum_slabs(gathered, "sum_small_grads"), shapes)
    gs = dict(zip(summed_names, reduced[:-1]))
    n_el = int(np.prod(shapes[-1]))
    off = sum(int(np.prod(s)) for s in shapes[:-1])
    dmod_all = gathered.reshape(N_DEV, -1)[:, off:off + n_el].reshape(N_DEV, depth, 3 * Dm)

    dmod_rows = jnp.concatenate([jnp.transpose(dmod_all, (1, 0, 2)), gs["dmod_c"][:, None, :]], axis=1)
    dmod_rows = lax.dynamic_slice_in_dim(dmod_rows, me * ada_w, ada_w, axis=2)
    dmod_rows = jnp.pad(dmod_rows, ((0, 0), (0, ADA_ROWS - 9), (0, 0)))
    g_w_ada, dc_part = _ada_bwd(c16, w_ada, dmod_rows)
    dc_all = _all_gather_small(dc_part, "gather_dc")
    g_c_ctx = _sum_slabs(dc_all, "sum_dc")[0]

    grads_small = {"c_ctx": g_c_ctx, "b_ada": gs["b_ada"], "g_pre": gs["g_pre"], "g_post": gs["g_post"],
                   "conv_b": gs["conv_b"], "dt_bias": gs["dt_bias"], "a_log": gs["a_log"], "d_skip": gs["d_skip"],
                   "g_ssd": gs["g_ssd"], "g_v": gs["g_v"], "w_s": gs["w_s"], "b_s": gs["b_s"], "g_mlp": gs["g_mlp"],
                   "conv_w": lax.dynamic_slice_in_dim(gs["conv_w"], me * conv_c, conv_c, axis=2)}

    g_in, d_in, nm_in, nv_in = (jnp.swapaxes(t, 1, 2) for t in
                                _adamw_slabs(w_in_t, in_recv, m_w_in_t, v_w_in_t, "adamw_w_in", in_w, 256))
    g_out, d_out, nm_out, nv_out = _adamw_slabs(w_out, out_recv, m_w_out, v_w_out, "adamw_w_out", 128, Dm)
    d_ada, nm_ada, nv_ada = _adamw(w_ada.reshape(depth * Dm, ada_w), g_w_ada.reshape(depth * Dm, ada_w),
                                   m_w_ada.reshape(depth * Dm, ada_w), v_w_ada.reshape(depth * Dm, ada_w),
                                   "adamw_w_ada", 256)

    small_names = ["c_ctx", "b_ada", "g_pre", "g_post", "conv_w", "conv_b", "dt_bias", "a_log", "d_skip", "g_ssd",
                   "g_v", "w_s", "b_s", "g_mlp"]
    small_w = dict(c_ctx=c_ctx, b_ada=b_ada, g_pre=g_pre, g_post=g_post, conv_w=conv_w, conv_b=conv_b,
                   dt_bias=dt_bias, a_log=a_log, d_skip=d_skip, g_ssd=g_ssd, g_v=g_v, w_s=w_s, b_s=b_s, g_mlp=g_mlp)
    small_m = dict(c_ctx=m_c_ctx, b_ada=m_b_ada, g_pre=m_g_pre, g_post=m_g_post, conv_w=m_conv_w, conv_b=m_conv_b,
                   dt_bias=m_dt_bias, a_log=m_a_log, d_skip=m_d_skip, g_ssd=m_g_ssd, g_v=m_g_v, w_s=m_w_s,
                   b_s=m_b_s, g_mlp=m_g_mlp)
    small_v = dict(c_ctx=v_c_ctx, b_ada=v_b_ada, g_pre=v_g_pre, g_post=v_g_post, conv_w=v_conv_w, conv_b=v_conv_b,
                   dt_bias=v_dt_bias, a_log=v_a_log, d_skip=v_d_skip, g_ssd=v_g_ssd, g_v=v_g_v, w_s=v_w_s,
                   b_s=v_b_s, g_mlp=v_g_mlp)
    s_shapes = [small_w[k].shape for k in small_names]
    d_s, nm_s, nv_s = _adamw(_pack([small_w[k] for k in small_names]),
                             _pack([grads_small[k].reshape(small_w[k].shape) for k in small_names]),
                             _pack([small_m[k] for k in small_names]), _pack([small_v[k] for k in small_names]),
                             "adamw_small", 512)
    d_s = dict(zip(small_names, _unpack(d_s, s_shapes)))
    nm_s = dict(zip(small_names, _unpack(nm_s, s_shapes)))
    nv_s = dict(zip(small_names, _unpack(nv_s, s_shapes)))

    big = {"w_ada": (g_w_ada, d_ada.reshape(w_ada.shape), nm_ada.reshape(w_ada.shape), nv_ada.reshape(w_ada.shape)),
           "w_in": tuple(t.reshape(w_in.shape) for t in (g_in, d_in, nm_in, nv_in)),
           "w_out": tuple(t.reshape(w_out.shape) for t in (g_out, d_out, nm_out, nv_out))}
    order = ["c_ctx", "w_ada", "b_ada", "g_pre", "g_post", "w_in", "conv_w", "conv_b", "dt_bias", "a_log", "d_skip",
             "g_ssd", "g_v", "w_s", "b_s", "g_mlp", "w_out"]

    def pick(k, idx):
        if k in big:
            return big[k][idx]
        return (grads_small[k].reshape(small_w[k].shape), d_s[k], nm_s[k], nv_s[k])[idx]

    return (loss, grad_x, *[pick(k, 0) for k in order], *[pick(k, 1) for k in order],
            *[pick(k, 2) for k in order], *[pick(k, 3) for k in order])
```

```python
import numpy as np
import jax
import jax.numpy as jnp
from jax import lax
from jax.experimental import pallas as pl
from jax.experimental.pallas import tpu as pltpu

F32 = jnp.float32
BF16 = jnp.bfloat16

D_MODEL = 2048
GRID_W = 64
SSD_W = 2048
SSD_HEADS = 32
SSD_HEAD_DIM = 64
SSD_GROUPS = 8
HEADS_PER_GROUP = 4
SSD_STATE = 128
CHUNK = 128
SSD_Q = 256
CONV_W = 5
MLP_W = 2048
MLP_GROUPS = 16
XBC_W = 4096
DT_W = 64
IN_W = 12352
Z_MAIN = IN_W - DT_W
GROUP_COLS = HEADS_PER_GROUP * SSD_HEAD_DIM
EPS = 1e-6
N_DEV = 8

ADAM_LR = 0.001
ADAM_B1 = 0.9
ADAM_B2 = 0.999
ADAM_EPS = 1e-08
ADAM_WD = 0.01
ADAM_STEP = 10

LANES = 128
NEG_BIG = -1e30

NN = (((1,), (0,)), ((), ()))
NT = (((1,), (1,)), ((), ()))
TN = (((0,), (0,)), ((), ()))
HI = lax.Precision.HIGHEST


def _dot(a, b, dims):
    return lax.dot_general(a.astype(BF16), b.astype(BF16), dims, preferred_element_type=F32)


def _dot_hi(a, b, dims=NN):
    return lax.dot_general(a, b, dims, preferred_element_type=F32, precision=HI)


def _sigmoid(x):
    return 1.0 / (1.0 + jnp.exp(-x))


def _silu(x):
    return x * _sigmoid(x)


def _softplus(x):
    return jnp.maximum(x, 0.0) + jnp.log(1.0 + jnp.exp(-jnp.abs(x)))


def _call(body, *, name, out_shape, grid=None, in_specs=None, out_specs=None, scratch=(),
          sem=None, vmem_mb=None, aliases=None):
    params = {}
    if sem is not None:
        params["dimension_semantics"] = sem
    if vmem_mb is not None:
        params["vmem_limit_bytes"] = vmem_mb << 20
    kw = {}
    if grid is not None:
        kw["grid"] = grid
    if in_specs is not None:
        kw["in_specs"] = in_specs
    if out_specs is not None:
        kw["out_specs"] = out_specs
    return pl.pallas_call(body, name=name, out_shape=out_shape, scratch_shapes=list(scratch),
                          input_output_aliases=aliases or {},
                          compiler_params=pltpu.CompilerParams(**params), **kw)


def _sds(shape, dtype):
    return jax.ShapeDtypeStruct(tuple(shape), dtype)


ANY = pl.BlockSpec(memory_space=pl.ANY)


def _my_pos():
    return lax.axis_index("x"), lax.axis_index("y"), lax.axis_index("c")


def _flip(v, bit):
    return 1 - v if bit else v


def _peer(pos, k):
    mx, my, mc = pos
    return (_flip(mx, (k >> 2) & 1), _flip(my, (k >> 1) & 1), _flip(mc, k & 1))


def _lin(pos):
    return 4 * pos[0] + 2 * pos[1] + pos[2]


def _all_gather_small(x, name):
    R, C = x.shape

    def body(x_ref, o_ref, send_sems, recv_sems):
        me = _my_pos()
        o_ref[_lin(me)] = x_ref[...]
        sends = []
        for k in range(1, N_DEV):
            peer = _peer(me, k)
            cp = pltpu.make_async_remote_copy(
                src_ref=x_ref, dst_ref=o_ref.at[_lin(me)], send_sem=send_sems.at[k - 1],
                recv_sem=recv_sems.at[k - 1], device_id=peer, device_id_type=pl.DeviceIdType.MESH)
            cp.start()
            sends.append(cp)
        for k in range(1, N_DEV):
            peer = _peer(me, k)
            pltpu.make_async_remote_copy(
                src_ref=x_ref, dst_ref=o_ref.at[_lin(peer)], send_sem=send_sems.at[k - 1],
                recv_sem=recv_sems.at[k - 1], device_id=peer,
                device_id_type=pl.DeviceIdType.MESH).wait_recv()
        for cp in sends:
            cp.wait_send()

    return _call(body, name=name, out_shape=_sds((N_DEV, R, C), x.dtype),
                 in_specs=[pl.BlockSpec(memory_space=pltpu.VMEM)],
                 out_specs=pl.BlockSpec(memory_space=pltpu.VMEM),
                 scratch=[pltpu.SemaphoreType.DMA((N_DEV - 1,)), pltpu.SemaphoreType.DMA((N_DEV - 1,))],
                 vmem_mb=40)(x)


def _all_gather_big(x, name):
    def body(x_ref, o_ref, send_sems, recv_sems, local_sem):
        mx, my, mc = _my_pos()
        me, sibling = (mx, my, mc), (mx, my, 1 - mc)
        chips = [(1 - mx, my), (mx, 1 - my), (1 - mx, 1 - my)]

        def slot(pos):
            return o_ref.at[_lin(pos)]

        def copy(k, block, to, src=None):
            return pltpu.make_async_remote_copy(
                src_ref=slot(block) if src is None else src, dst_ref=slot(block),
                send_sem=send_sems.at[k], recv_sem=recv_sems.at[k], device_id=to,
                device_id_type=pl.DeviceIdType.MESH)

        mine = pltpu.make_async_copy(x_ref, slot(me), local_sem)
        mine.start()
        first = [copy(0, me, sibling, src=x_ref)]
        first += [copy(1 + j, me, (*chip, mc), src=x_ref) for j, chip in enumerate(chips)]
        for cp in first:
            cp.start()
        passed = [copy(4 + j, (*chip, mc), sibling) for j, chip in enumerate(chips)]
        for j, chip in enumerate(chips):
            copy(1 + j, (*chip, mc), me).wait_recv()
            passed[j].start()
        copy(0, sibling, me).wait_recv()
        for j, chip in enumerate(chips):
            copy(4 + j, (*chip, 1 - mc), me).wait_recv()
        for cp in first + passed:
            cp.wait_send()
        mine.wait()

    return _call(body, name=name, out_shape=_sds((N_DEV,) + x.shape, x.dtype),
                 in_specs=[ANY], out_specs=ANY,
                 scratch=[pltpu.SemaphoreType.DMA((7,)), pltpu.SemaphoreType.DMA((7,)),
                          pltpu.SemaphoreType.DMA])(x)


EXCHANGE_SEMS = [pltpu.SemaphoreType.DMA((N_DEV - 1,)), pltpu.SemaphoreType.DMA((N_DEV - 1,)),
                 pltpu.SemaphoreType.DMA]


def _exchange_copies(kind, src_ref, dst_ref, send_sems, recv_sems, local_sem, with_arrivals=True):
    me = _my_pos()
    mine_src = src_ref if kind == "gather" else src_ref.at[_lin(me)]
    local = pltpu.make_async_copy(mine_src, dst_ref.at[_lin(me)], local_sem)
    sends, recvs = [], []
    for k in range(1, N_DEV):
        peer = _peer(me, k)
        out_src = src_ref if kind == "gather" else src_ref.at[_lin(peer)]
        sends.append(pltpu.make_async_remote_copy(
            src_ref=out_src, dst_ref=dst_ref.at[_lin(me)], send_sem=send_sems.at[k - 1],
            recv_sem=recv_sems.at[k - 1], device_id=peer, device_id_type=pl.DeviceIdType.MESH))
        if with_arrivals:
            recvs.append(pltpu.make_async_remote_copy(
                src_ref=out_src, dst_ref=dst_ref.at[_lin(peer)], send_sem=send_sems.at[k - 1],
                recv_sem=recv_sems.at[k - 1], device_id=peer, device_id_type=pl.DeviceIdType.MESH))
    return sends, recvs, local


def _exchange_start(kind, src_ref, dst_ref, send_sems, recv_sems, local_sem):
    sends, _, local = _exchange_copies(kind, src_ref, dst_ref, send_sems, recv_sems, local_sem, with_arrivals=False)
    local.start()
    for cp in sends:
        cp.start()


def _exchange_wait(kind, src_ref, dst_ref, send_sems, recv_sems, local_sem):
    sends, recvs, local = _exchange_copies(kind, src_ref, dst_ref, send_sems, recv_sems, local_sem)
    for cp in recvs:
        cp.wait_recv()
    for cp in sends:
        cp.wait_send()
    local.wait()


def _exchange_shape(kind, src):
    return _sds((N_DEV,) + src.shape if kind == "gather" else src.shape, src.dtype)


def _mm(a, b, mode, out_dtype, name, tm, tn, tk, exchange=None, add=None, side=None):
    if mode == "nn":
        (M, K), (K2, N) = a.shape, b.shape
    elif mode == "nt":
        (M, K), (N, K2) = a.shape, b.shape
    else:
        (K, M), (K2, N) = a.shape, b.shape
    assert K == K2
    tm, tn, tk = min(tm, M), min(tn, N), min(tk, K)
    assert M % tm == 0 and N % tn == 0 and K % tk == 0, (name, M, N, K, tm, tn, tk)
    ni, nj, nk = M // tm, N // tn, K // tk
    dims = {"nn": NN, "nt": NT, "tn": TN}[mode]
    assert side is None or (nk == 1 and mode != "tn")

    if mode == "tn":
        a_spec = pl.BlockSpec((tk, tm), lambda i, j, k: (k, i))
    else:
        a_spec = pl.BlockSpec((tm, tk), lambda i, j, k: (i, k))
    if mode == "nt":
        b_spec = pl.BlockSpec((tn, tk), lambda i, j, k: (j, k))
    else:
        b_spec = pl.BlockSpec((tk, tn), lambda i, j, k: (k, j))
    operands, in_specs = [a, b], [a_spec, b_spec]
    out_shape, out_specs = [_sds((M, N), out_dtype)], [pl.BlockSpec((tm, tn), lambda i, j, k: (i, j))]
    if add is not None:
        k2 = add[0].shape[1]
        operands += list(add)
        in_specs += [pl.BlockSpec((tm, k2), lambda i, j, k: (i, 0)), pl.BlockSpec((k2, tn), lambda i, j, k: (0, j))]
    if side is not None:
        n3 = side.shape[0]
        operands.append(side)
        in_specs.append(pl.BlockSpec((n3, tk), lambda i, j, k: (0, 0)))
        out_shape.append(_sds((M, n3), F32))
        out_specs.append(pl.BlockSpec((tm, n3), lambda i, j, k: (i, 0)))
    scratch = [] if nk == 1 else [pltpu.VMEM((tm, tn), F32)]
    aliases = {}
    if exchange is not None:
        kind, src = exchange[:2]
        operands.append(src)
        in_specs.append(ANY)
        if len(exchange) == 3:
            into, into_idx = exchange[2]
            aliases = {len(operands): len(out_shape)}
            operands.append(into)
            in_specs.append(ANY)
            out_shape.append(_sds(into.shape, into.dtype))
        else:
            into_idx = None
            out_shape.append(_exchange_shape(kind, src))
        out_specs.append(ANY)
        scratch = EXCHANGE_SEMS + scratch
    n_in, n_out = len(operands), len(out_shape)

    def body(*refs):
        ins, outs, scr = list(refs[:n_in]), list(refs[n_in:n_in + n_out]), list(refs[n_in + n_out:])
        a_ref, b_ref, o_ref = ins[0], ins[1], outs[0]
        add_refs = ins[2:4] if add is not None else None
        side_ref = ins[2 + (2 if add is not None else 0)] if side is not None else None
        i, j, k = pl.program_id(0), pl.program_id(1), pl.program_id(2)
        if exchange is not None:
            sems, scr = scr[:3], scr[3:]
            src_ref = ins[-1] if into_idx is None else ins[-2]
            dst_ref = outs[-1] if into_idx is None else outs[-1].at[into_idx]

            @pl.when(jnp.logical_and(jnp.logical_and(i == 0, j == 0), k == 0))
            def _():
                _exchange_start(kind, src_ref, dst_ref, *sems)

        p = _dot(a_ref[...], b_ref[...], dims)
        if nk == 1:
            if add_refs is not None:
                p = p + _dot(add_refs[0][...], add_refs[1][...], NN)
            o_ref[...] = p.astype(out_dtype)
        else:
            acc_ref = scr[0]

            @pl.when(k == 0)
            def _():
                first = p if add_refs is None else p + _dot(add_refs[0][...], add_refs[1][...], NN)
                acc_ref[...] = first

            @pl.when(k > 0)
            def _():
                acc_ref[...] += p

            @pl.when(k == nk - 1)
            def _():
                o_ref[...] = acc_ref[...].astype(out_dtype)

        if side is not None:
            @pl.when(j == 0)
            def _():
                outs[1][...] = _dot(a_ref[...], side_ref[...], NT)

        if exchange is not None:
            @pl.when(jnp.logical_and(jnp.logical_and(i == ni - 1, j == nj - 1), k == nk - 1))
            def _():
                _exchange_wait(kind, src_ref, dst_ref, *sems)

    plain = exchange is None and side is None
    res = _call(body, name=name, out_shape=tuple(out_shape), grid=(ni, nj, nk), in_specs=in_specs,
                out_specs=tuple(out_specs), scratch=scratch, aliases=aliases,
                sem=("parallel", "parallel", "arbitrary") if plain else ("arbitrary",) * 3, vmem_mb=48)(*operands)
    return res[0] if n_out == 1 else res


def _rms(x):
    return lax.rsqrt(jnp.mean(x * x, axis=-1, keepdims=True) + EPS)


def _prenorm_f(x, g, sc, sh):
    return (x * _rms(x) * g) * (1.0 + sc) + sh


def _pick(is_ctx, ref):
    return jnp.where(is_ctx, ref[0:1, :], ref[1:2, :])


def _prenorm_fwd(X, g, sc2, sh2, n_ctx, tl=256):
    T, Dm = X.shape
    nct = n_ctx // tl

    def body(x_ref, g_ref, sc_ref, sh_ref, o_ref):
        is_ctx = pl.program_id(0) < nct
        o_ref[...] = _prenorm_f(x_ref[...], g_ref[...], _pick(is_ctx, sc_ref),
                                _pick(is_ctx, sh_ref)).astype(BF16)

    row = pl.BlockSpec((tl, Dm), lambda i: (i, 0))
    return _call(body, name="prenorm_fwd", out_shape=_sds((T, Dm), BF16), grid=(T // tl,),
                 in_specs=[row, pl.BlockSpec((1, Dm), lambda i: (0, 0)),
                           pl.BlockSpec((2, Dm), lambda i: (0, 0)), pl.BlockSpec((2, Dm), lambda i: (0, 0))],
                 out_specs=row, sem=("parallel",), vmem_mb=40)(X, g, sc2, sh2)


def _prenorm_bwd(X, g, sc2, sh2, d1, dres, n_ctx, tl=256):
    T, Dm = X.shape
    nct = n_ctx // tl

    def body(x_ref, g_ref, sc_ref, sh_ref, d1_ref, dres_ref, dx_ref, acc_ref):
        i = pl.program_id(0)
        is_ctx = i < nct

        @pl.when(i == 0)
        def _():
            acc_ref[...] = jnp.zeros_like(acc_ref)

        _, vjp = jax.vjp(_prenorm_f, x_ref[...], g_ref[...], _pick(is_ctx, sc_ref), _pick(is_ctx, sh_ref))
        dx, dg, dsc, dsh = vjp(d1_ref[...])
        dx_ref[...] = dres_ref[...] + dx
        zero = jnp.zeros_like(dsc)
        acc_ref[0:1, :] += dg
        acc_ref[1:2, :] += jnp.where(is_ctx, dsc, zero)
        acc_ref[2:3, :] += jnp.where(is_ctx, zero, dsc)
        acc_ref[3:4, :] += jnp.where(is_ctx, dsh, zero)
        acc_ref[4:5, :] += jnp.where(is_ctx, zero, dsh)

    row = pl.BlockSpec((tl, Dm), lambda i: (i, 0))
    return _call(body, name="prenorm_bwd", out_shape=(_sds((T, Dm), F32), _sds((8, Dm), F32)), grid=(T // tl,),
                 in_specs=[row, pl.BlockSpec((1, Dm), lambda i: (0, 0)),
                           pl.BlockSpec((2, Dm), lambda i: (0, 0)), pl.BlockSpec((2, Dm), lambda i: (0, 0)),
                           row, row],
                 out_specs=(row, pl.BlockSpec((8, Dm), lambda i: (0, 0))), sem=("arbitrary",),
                 vmem_mb=48)(X, g, sc2, sh2, d1, dres)


def _post_f(o, g, gate):
    return gate * ((o * _rms(o)) * g)


def _post_fwd(X, o, g, gate2, n_ctx, tl=256):
    T, Dm = X.shape
    nct = n_ctx // tl

    def body(x_ref, o_ref, g_ref, gate_ref, y_ref):
        is_ctx = pl.program_id(0) < nct
        y_ref[...] = x_ref[...] + _post_f(o_ref[...], g_ref[...], _pick(is_ctx, gate_ref))

    row = pl.BlockSpec((tl, Dm), lambda i: (i, 0))
    return _call(body, name="post_fwd", out_shape=_sds((T, Dm), F32), grid=(T // tl,),
                 in_specs=[row, row, pl.BlockSpec((1, Dm), lambda i: (0, 0)), pl.BlockSpec((2, Dm), lambda i: (0, 0))],
                 out_specs=row, sem=("parallel",), vmem_mb=40)(X, o, g, gate2)


def _post_bwd(o, g, gate2, dX, n_ctx, tl=256):
    T, Dm = o.shape
    nct = n_ctx // tl

    def body(o_ref, g_ref, gate_ref, dx_ref, do_ref, acc_ref):
        i = pl.program_id(0)
        is_ctx = i < nct

        @pl.when(i == 0)
        def _():
            acc_ref[...] = jnp.zeros_like(acc_ref)

        _, vjp = jax.vjp(_post_f, o_ref[...], g_ref[...], _pick(is_ctx, gate_ref))
        do, dg, dgate = vjp(dx_ref[...])
        do_ref[...] = do.astype(BF16)
        zero = jnp.zeros_like(dgate)
        acc_ref[0:1, :] += dg
        acc_ref[1:2, :] += jnp.where(is_ctx, dgate, zero)
        acc_ref[2:3, :] += jnp.where(is_ctx, zero, dgate)

    row = pl.BlockSpec((tl, Dm), lambda i: (i, 0))
    return _call(body, name="post_bwd", out_shape=(_sds((T, Dm), BF16), _sds((8, Dm), F32)), grid=(T // tl,),
                 in_specs=[row, pl.BlockSpec((1, Dm), lambda i: (0, 0)), pl.BlockSpec((2, Dm), lambda i: (0, 0)), row],
                 out_specs=(row, pl.BlockSpec((8, Dm), lambda i: (0, 0))), sem=("arbitrary",),
                 vmem_mb=48)(o, g, gate2, dX)


def _loss_kernel(X, target, n_ctx, tl=256):
    T, Dm = X.shape
    nct = n_ctx // tl

    def body(x_ref, t_ref, dx_ref, acc_ref):
        i = pl.program_id(0)

        @pl.when(i == 0)
        def _():
            acc_ref[...] = jnp.zeros_like(acc_ref)

        @pl.when(i < nct)
        def _():
            dx_ref[...] = jnp.zeros_like(dx_ref)

        @pl.when(i >= nct)
        def _():
            d = x_ref[...] - t_ref[...]
            dx_ref[...] = d * (1.0 / Dm)
            acc_ref[...] += jnp.sum(d * d, axis=0, keepdims=True)

    row = pl.BlockSpec((tl, Dm), lambda i: (i, 0))
    trow = pl.BlockSpec((tl, Dm), lambda i: (jnp.maximum(i - nct, 0), 0))
    return _call(body, name="loss", out_shape=(_sds((T, Dm), F32), _sds((1, Dm), F32)), grid=(T // tl,),
                 in_specs=[row, trow], out_specs=(row, pl.BlockSpec((1, Dm), lambda i: (0, 0))),
                 sem=("arbitrary",), vmem_mb=40)(X, target)


CONV_TL = 256
CONV_CB = 1024


def _conv_taps(x, reverse):
    rows = x.shape[0]
    pos = lax.broadcasted_iota(jnp.int32, (rows, 1), 0)
    taps = []
    for k in range(CONV_W):
        off = (2 - k) if reverse else (k - 2)
        if off == 0:
            taps.append(x)
            continue
        xs = pltpu.roll(x, (-off) % rows, 0)
        valid = (pos < rows - off) if off > 0 else (pos >= -off)
        taps.append(jnp.where(valid, xs, 0.0))
    return taps


def _conv_pre(taps, w, b):
    pre = b + taps[0] * w[0:1, :]
    for k in range(1, CONV_W):
        pre = pre + taps[k] * w[k:k + 1, :]
    return pre


def _conv_subtiles(tl, cbw, row_len):
    return [(r0, c0) for c0 in range(0, cbw, LANES) for r0 in range(0, tl, row_len)]


def _conv_fwd(z, cw, cb, n_ctx):
    T = z.shape[0]
    tl, cbw = CONV_TL, CONV_CB
    assert n_ctx == tl

    def body(z_ref, w_ref, b_ref, o_ref):
        def run(row_len):
            for r0, c0 in _conv_subtiles(tl, cbw, row_len):
                rs, cs = pl.ds(r0, row_len), pl.ds(c0, LANES)
                pre = _conv_pre(_conv_taps(z_ref[rs, cs].astype(F32), False), w_ref[:, cs], b_ref[:, cs])
                o_ref[rs, cs] = _silu(pre).astype(BF16)

        @pl.when(pl.program_id(1) == 0)
        def _():
            run(n_ctx)

        @pl.when(pl.program_id(1) > 0)
        def _():
            run(GRID_W)

    blk = pl.BlockSpec((tl, cbw), lambda j, i: (i, j))
    return _call(body, name="conv_fwd", out_shape=_sds((T, XBC_W), BF16), grid=(XBC_W // cbw, T // tl),
                 in_specs=[blk, pl.BlockSpec((8, cbw), lambda j, i: (0, j)), pl.BlockSpec((1, cbw), lambda j, i: (0, j))],
                 out_specs=blk, sem=("parallel", "parallel"))(z, cw, cb)


def _conv_bwd(z, cw, cb, d0, d1, dz_in, col0, n_ctx):
    T = z.shape[0]
    tl, cbw = CONV_TL, CONV_CB

    def body(z_ref, w_ref, b_ref, d0_ref, d1_ref, dzin_ref, dz_ref, dw_ref, db_ref):
        i = pl.program_id(1)

        @pl.when(i == 0)
        def _():
            dw_ref[...] = jnp.zeros_like(dw_ref)
            db_ref[...] = jnp.zeros_like(db_ref)

        def run(row_len):
            for c0 in range(0, cbw, LANES):
                cs = pl.ds(c0, LANES)
                w = w_ref[:, cs]
                dw = [jnp.zeros((1, LANES), F32) for _ in range(CONV_W)]
                db = jnp.zeros((1, LANES), F32)
                for r0 in range(0, tl, row_len):
                    rs = pl.ds(r0, row_len)
                    taps = _conv_taps(z_ref[rs, cs].astype(F32), False)
                    pre = _conv_pre(taps, w, b_ref[:, cs])
                    s = _sigmoid(pre)
                    dpre = (d0_ref[rs, cs].astype(F32) + d1_ref[rs, cs].astype(F32)) * (s + pre * s * (1.0 - s))
                    dz_ref[rs, cs] = _conv_pre(_conv_taps(dpre, True), w, 0.0).astype(BF16)
                    for k in range(CONV_W):
                        dw[k] = dw[k] + jnp.sum(dpre * taps[k], axis=0, keepdims=True)
                    db = db + jnp.sum(dpre, axis=0, keepdims=True)
                for k in range(CONV_W):
                    dw_ref[k:k + 1, cs] += dw[k]
                db_ref[0:1, cs] += db

        @pl.when(i == 0)
        def _():
            run(n_ctx)

        @pl.when(i > 0)
        def _():
            run(GRID_W)

    width = d0.shape[1]
    jo = col0 // cbw
    blk = pl.BlockSpec((tl, cbw), lambda j, i: (i, jo + j))
    dblk = pl.BlockSpec((tl, cbw), lambda j, i: (i, j))
    par = pl.BlockSpec((8, cbw), lambda j, i: (0, jo + j))
    opar = pl.BlockSpec((8, cbw), lambda j, i: (0, j))
    return _call(body, name=f"conv_bwd_{col0}",
                 out_shape=(_sds(dz_in.shape, BF16), _sds((8, width), F32), _sds((8, width), F32)),
                 grid=(width // cbw, T // tl),
                 in_specs=[blk, par, pl.BlockSpec((1, cbw), lambda j, i: (0, jo + j)), dblk, dblk, ANY],
                 out_specs=(blk, opar, opar), sem=("parallel", "arbitrary"), aliases={5: 0})(z, cw, cb, d0, d1, dz_in)


MIX_TL = 128
ROW_BLOCK = 16
COL_CHUNK = 512


def _for_row_blocks(n_rows, fn):
    def step(i, carry):
        fn(pl.ds(pl.multiple_of(i * ROW_BLOCK, ROW_BLOCK), ROW_BLOCK))
        return carry

    lax.fori_loop(0, n_rows // ROW_BLOCK, step, 0)


def _col_chunks(width):
    return [pl.ds(c0, COL_CHUNK) for c0 in range(0, width, COL_CHUNK)]


def _gate_fwd(yf, yb, z, g):
    T = yf.shape[0]
    tl = MIX_TL

    def body(yf_ref, yb_ref, zs_ref, g_ref, o_ref):
        t = (yf_ref[...].astype(F32) + yb_ref[...].astype(F32)) * _silu(zs_ref[...].astype(F32))
        o_ref[...] = (t * _rms(t) * g_ref[...]).astype(BF16)

    row = pl.BlockSpec((tl, SSD_W), lambda i: (i, 0))
    return _call(body, name="gate_fwd", out_shape=_sds((T, SSD_W + MLP_W), BF16), grid=(T // tl,),
                 in_specs=[row, row, pl.BlockSpec((tl, SSD_W), lambda i: (i, XBC_W // SSD_W)),
                           pl.BlockSpec((1, SSD_W), lambda i: (0, 0))],
                 out_specs=row, sem=("parallel",))(yf, yb, z, g)


def _gate_bwd(yf, yb, z, g, dycat, dz_in):
    T = yf.shape[0]
    tl = MIX_TL

    def body(yf_ref, yb_ref, zs_ref, g_ref, d_ref, dzin_ref, dy_ref, dz_ref, acc_ref):
        @pl.when(pl.program_id(0) == 0)
        def _():
            acc_ref[...] = jnp.zeros_like(acc_ref)

        def block(rs):
            def parts(cs):
                y = yf_ref[rs, cs].astype(F32) + yb_ref[rs, cs].astype(F32)
                zs = zs_ref[rs, cs].astype(F32)
                sg = _sigmoid(zs)
                return y, zs, sg, y * (zs * sg), d_ref[rs, cs].astype(F32)

            ss = jnp.zeros((ROW_BLOCK, 1), F32)
            su = jnp.zeros((ROW_BLOCK, 1), F32)
            for cs in _col_chunks(SSD_W):
                _, _, _, t, dd = parts(cs)
                ss = ss + jnp.sum(t * t, axis=1, keepdims=True)
                su = su + jnp.sum(t * (dd * g_ref[:, cs]), axis=1, keepdims=True)
            r = lax.rsqrt(ss * (1.0 / SSD_W) + EPS)
            c = r * r * r * (su * (1.0 / SSD_W))
            for cs in _col_chunks(SSD_W):
                y, zs, sg, t, dd = parts(cs)
                dt = r * (dd * g_ref[:, cs]) - t * c
                dy_ref[rs, cs] = (dt * (zs * sg)).astype(BF16)
                dz_ref[rs, cs] = (dt * y * (sg * (1.0 + zs * (1.0 - sg)))).astype(BF16)
                acc_ref[0:1, cs] += jnp.sum(dd * t * r, axis=0, keepdims=True)

        _for_row_blocks(tl, block)

    row = pl.BlockSpec((tl, SSD_W), lambda i: (i, 0))
    zs_spec = pl.BlockSpec((tl, SSD_W), lambda i: (i, XBC_W // SSD_W))
    return _call(body, name="gate_bwd",
                 out_shape=(_sds((T, SSD_W), BF16), _sds(dz_in.shape, BF16), _sds((8, SSD_W), F32)),
                 grid=(T // tl,),
                 in_specs=[row, row, zs_spec, pl.BlockSpec((1, SSD_W), lambda i: (0, 0)), row, ANY],
                 out_specs=(row, zs_spec, pl.BlockSpec((8, SSD_W), lambda i: (0, 0))),
                 sem=("arbitrary",), aliases={5: 1}, vmem_mb=40)(yf, yb, z, g, dycat, dz_in)


def _vnorm_f(v, gv):
    return v * _rms(v) * gv


def _mlp_out_f(u, sg, zm, gm):
    t = u * sg * _silu(zm)
    return t * _rms(t) * gm


U_BLK = (XBC_W + SSD_W) // MLP_W


def _mlp_mix(ws_ref, bst_ref, vn_s, sg_s):
    for gi in range(MLP_GROUPS):
        cols = pl.ds(gi * LANES, LANES)
        sg_s[:, cols] = _dot(ws_ref[gi], vn_s[:, cols], NN) + bst_ref[:, gi:gi + 1]


def _mlp_fwd(z, gv, ws, bst, gm, ycat_in):
    T = z.shape[0]
    tl = CHUNK

    def body(u_ref, v_ref, zm_ref, gv_ref, ws_ref, bst_ref, gm_ref, yin_ref, o_ref, vn_s, sg_s):
        vn_s[...] = _vnorm_f(v_ref[...].astype(F32), gv_ref[...]).astype(BF16)
        _mlp_mix(ws_ref, bst_ref, vn_s, sg_s)
        o_ref[...] = _mlp_out_f(u_ref[...].astype(F32), sg_s[...], zm_ref[...].astype(F32),
                                gm_ref[...]).astype(BF16)

    def zblk(b):
        return pl.BlockSpec((tl, MLP_W), lambda i: (i, b))

    vec = pl.BlockSpec((1, MLP_W), lambda i: (0, 0))
    return _call(body, name="mlp_fwd", out_shape=_sds(ycat_in.shape, BF16), grid=(T // tl,),
                 in_specs=[zblk(U_BLK), zblk(U_BLK + 1), zblk(U_BLK + 2), vec,
                           pl.BlockSpec((MLP_GROUPS, CHUNK, CHUNK), lambda i: (0, 0, 0)),
                           pl.BlockSpec((CHUNK, LANES), lambda i: (0, 0)), vec, ANY],
                 out_specs=pl.BlockSpec((tl, MLP_W), lambda i: (i, 1)),
                 scratch=[pltpu.VMEM((tl, MLP_W), BF16), pltpu.VMEM((tl, MLP_W), F32)],
                 sem=("parallel",), aliases={7: 0})(z, z, z, gv, ws, bst, gm, ycat_in)


def _mlp_bwd(z, gv, ws, bst, gm, dycat, dz_in):
    T = z.shape[0]
    tl = CHUNK

    def body(u_ref, v_ref, zm_ref, gv_ref, ws_ref, bst_ref, gm_ref, d_ref, dzin_ref,
             dz_ref, acc_ref, dws_ref, dbst_ref, vn_s, sg_s, dvn_s):
        @pl.when(pl.program_id(0) == 0)
        def _():
            acc_ref[...] = jnp.zeros_like(acc_ref)
            dws_ref[...] = jnp.zeros_like(dws_ref)
            dbst_ref[...] = jnp.zeros_like(dbst_ref)

        v = v_ref[...].astype(F32)
        vn, vjp_v = jax.vjp(_vnorm_f, v, gv_ref[...])
        vn_s[...] = vn.astype(BF16)
        _mlp_mix(ws_ref, bst_ref, vn_s, sg_s)
        _, vjp_o = jax.vjp(_mlp_out_f, u_ref[...].astype(F32), sg_s[...], zm_ref[...].astype(F32), gm_ref[...])
        du, dsg, dzm, dgm = vjp_o(d_ref[...].astype(F32))
        sg_s[...] = dsg
        for gi in range(MLP_GROUPS):
            cols = pl.ds(gi * LANES, LANES)
            dsg_g = sg_s[:, cols]
            dvn_s[:, cols] = _dot(ws_ref[gi], dsg_g, TN)
            dws_ref[gi] += _dot(dsg_g, vn_s[:, cols], NT)
            dbst_ref[:, gi:gi + 1] += jnp.sum(dsg_g, axis=1, keepdims=True)
        dv, dgv = vjp_v(dvn_s[...])
        dz_ref[:, 0:MLP_W] = du.astype(BF16)
        dz_ref[:, MLP_W:2 * MLP_W] = dv.astype(BF16)
        dz_ref[:, 2 * MLP_W:3 * MLP_W] = dzm.astype(BF16)
        acc_ref[0:1, :] += dgv
        acc_ref[1:2, :] += dgm

    def zblk(b):
        return pl.BlockSpec((tl, MLP_W), lambda i: (i, b))

    vec = pl.BlockSpec((1, MLP_W), lambda i: (0, 0))
    ws_spec = pl.BlockSpec((MLP_GROUPS, CHUNK, CHUNK), lambda i: (0, 0, 0))
    bst_spec = pl.BlockSpec((CHUNK, LANES), lambda i: (0, 0))
    return _call(body, name="mlp_bwd",
                 out_shape=(_sds(dz_in.shape, BF16), _sds((8, MLP_W), F32),
                            _sds((MLP_GROUPS, CHUNK, CHUNK), F32), _sds((CHUNK, LANES), F32)),
                 grid=(T // tl,),
                 in_specs=[zblk(U_BLK), zblk(U_BLK + 1), zblk(U_BLK + 2), vec, ws_spec, bst_spec, vec,
                           pl.BlockSpec((tl, MLP_W), lambda i: (i, 1)), ANY],
                 out_specs=(pl.BlockSpec((tl, 3 * MLP_W), lambda i: (i, 1)),
                            pl.BlockSpec((8, MLP_W), lambda i: (0, 0)), ws_spec, bst_spec),
                 scratch=[pltpu.VMEM((tl, MLP_W), BF16), pltpu.VMEM((tl, MLP_W), F32), pltpu.VMEM((tl, MLP_W), F32)],
                 sem=("arbitrary",), aliases={8: 0}, vmem_mb=48)(z, z, z, gv, ws, bst, gm, dycat, dz_in)


def _scan_mask(direction):
    tri = np.tril(np.ones((SSD_Q, SSD_Q), np.float32))
    return jnp.asarray(tri.T if direction == 1 else tri)


def _ssd_prep(raw, prm):
    T = raw.shape[0]
    nc, Q = T // SSD_Q, SSD_Q
    tril = jnp.asarray(np.tril(np.ones((Q, Q), np.float32)))
    triu = jnp.asarray(np.triu(np.ones((Q, Q), np.float32)))

    def body(raw_ref, prm_ref, tril_ref, triu_ref, dt_ref, s_ref, st_ref, etot_ref, pkt_ref):
        dt = _softplus(raw_ref[...] + prm_ref[0:1, :])
        a = dt * (-jnp.exp(prm_ref[1:2, :]))
        lane = lax.broadcasted_iota(jnp.int32, (1, LANES), 1)
        s = jnp.where(lane < SSD_HEADS, _dot_hi(tril_ref[...], a), _dot_hi(triu_ref[...], a))
        tot = jnp.sum(a, axis=0, keepdims=True)
        dt_ref[...] = dt
        s_ref[...] = s
        for n, v in enumerate((dt, jnp.exp(s), jnp.exp(tot - s))):
            pkt_ref[0, n * LANES:(n + 1) * LANES, :] = v.T
        st_ref[0] = s.T
        etot_ref[0] = _bcast8(jnp.exp(tot))

    row = pl.BlockSpec((Q, LANES), lambda c: (c, 0))
    cst = lambda shape: pl.BlockSpec(shape, lambda c: (0, 0))
    tl = _sds((T, LANES), F32)
    return _call(body, name="ssd_prep",
                 out_shape=(tl, tl, _sds((nc, LANES, Q), F32), _sds((nc, 8, LANES), F32), _sds((nc, 3 * LANES, Q), F32)),
                 grid=(nc,), in_specs=[row, cst((8, LANES)), cst((Q, Q)), cst((Q, Q))],
                 out_specs=(row, row, pl.BlockSpec((1, LANES, Q), lambda c: (c, 0, 0)),
                            pl.BlockSpec((1, 8, LANES), lambda c: (c, 0, 0)),
                            pl.BlockSpec((1, 3 * LANES, Q), lambda c: (c, 0, 0))),
                 sem=("parallel",))(raw, prm, tril, triu)


def _chunk_order(direction, nc, nctx):
    if direction == 0:
        return lambda c: c
    return lambda c: jnp.where(c < nctx, nctx - 1 - c, nc - 1 + nctx - c)


def _bcast8(row):
    return jnp.broadcast_to(row, (8, row.shape[1]))


GPS = 8


def _cols(ref, k, width):
    return ref[:, k * width:(k + 1) * width]


def _ssd_fwd(xbc, prep, dskc, direction, n_ctx):
    T = xbc.shape[0]
    nc, nctx = T // SSD_Q, n_ctx // SSD_Q
    order = _chunk_order(direction, nc, nctx)
    tri_t = jnp.transpose(_scan_mask(direction))
    Q, GC, N, G, HP, HD = SSD_Q, GROUP_COLS, SSD_STATE, SSD_GROUPS, HEADS_PER_GROUP, SSD_HEAD_DIM
    _, s, s_t, etot, pk_t = prep

    def body(x_ref, b_ref, c_ref, s_ref, st_ref, etot_ref, dtr_ref, er_ref, dr_ref, dsk_ref, trit_ref,
             y_ref, sst_ref, S):
        c = pl.program_id(0)
        mask_t = trit_ref[...] > 0.5
        lane = lax.broadcasted_iota(jnp.int32, (1, LANES), 1)
        rowi = lax.broadcasted_iota(jnp.int32, (GC, 1), 0)
        s_all = s_ref[...]
        etot_row = etot_ref[0, 0:1, :]
        for k in range(GPS):
            g = pl.program_id(1) * GPS + k
            base = direction * SSD_HEADS + HP * g

            @pl.when(c == 0)
            def _():
                S[g] = jnp.zeros((GC, N), F32)

            S0 = S[g]
            sst_ref[0, k] = S0
            Bm, Cm = _cols(b_ref, k, N), _cols(c_ref, k, N)

            def head_rows(ref):
                return jnp.concatenate([jnp.broadcast_to(ref[0, HP * k + r:HP * k + r + 1, :], (HD, Q))
                                        for r in range(HP)], axis=0)

            x_t = jnp.transpose(_cols(x_ref, k, GC).astype(F32))
            X_t = x_t * head_rows(dtr_ref)
            X_tb = X_t.astype(BF16)
            y_t = head_rows(er_ref) * _dot(S0, Cm, NT)
            G_t = _dot(Bm, Cm, NT)
            parts = []
            for r in range(HP):
                col = jnp.sum(jnp.where(lane == base + r, s_all, 0.0), axis=1, keepdims=True)
                L_t = jnp.exp(jnp.where(mask_t, st_ref[0, HP * k + r:HP * k + r + 1, :] - col, NEG_BIG))
                parts.append(_dot(X_tb[HD * r:HD * (r + 1), :], G_t * L_t, NN))
            y_t = y_t + jnp.concatenate(parts, axis=0)
            if direction == 0:
                y_t = y_t + dsk_ref[k] * x_t
            y_ref[:, k * GC:(k + 1) * GC] = jnp.transpose(y_t).astype(BF16)
            e = [jnp.sum(jnp.where(lane == base + r, etot_row, 0.0), axis=1, keepdims=True) for r in range(HP)]
            etot_c = jnp.where(rowi < HD, e[0], jnp.where(rowi < 2 * HD, e[1], jnp.where(rowi < 3 * HD, e[2], e[3])))
            S[g] = etot_c * S0 + _dot(X_t * head_rows(dr_ref), Bm, NN)

    hrows = HP * GPS
    sec = LANES // hrows
    row_block = lambda n: pl.BlockSpec((1, hrows, Q), lambda c, g: (order(c), n * sec + direction * (SSD_HEADS // hrows) + g, 0))
    in_specs = [
        pl.BlockSpec((Q, GPS * GC), lambda c, g: (order(c), g)),
        pl.BlockSpec((Q, GPS * N), lambda c, g: (order(c), SSD_W // (GPS * N) + g)),
        pl.BlockSpec((Q, GPS * N), lambda c, g: (order(c), (SSD_W + G * N) // (GPS * N) + g)),
        pl.BlockSpec((Q, LANES), lambda c, g: (order(c), 0)),
        row_block(0),
        pl.BlockSpec((1, 8, LANES), lambda c, g: (order(c), 0, 0)),
        row_block(0), row_block(1), row_block(2),
        pl.BlockSpec((GPS, GC, Q), lambda c, g: (g, 0, 0)),
        pl.BlockSpec((Q, Q), lambda c, g: (0, 0)),
    ]
    out_specs = (pl.BlockSpec((Q, GPS * GC), lambda c, g: (order(c), g)),
                 pl.BlockSpec((1, GPS, GC, N), lambda c, g: (order(c), g, 0, 0)))
    return _call(body, name=f"ssd_fwd_{direction}",
                 out_shape=(_sds((T, SSD_W), BF16), _sds((nc, G, GC, N), F32)),
                 grid=(nc, G // GPS), in_specs=in_specs, out_specs=out_specs,
                 scratch=[pltpu.VMEM((G, GC, N), F32)], sem=("arbitrary", "arbitrary"), vmem_mb=48,
                 )(xbc, xbc, xbc, s, s_t, etot, pk_t, pk_t, pk_t, dskc, tri_t)


def _ssd_bwd_t(xbc, prep, dskc, sst, dy, direction, n_ctx):
    T = xbc.shape[0]
    nc, nctx = T // SSD_Q, n_ctx // SSD_Q
    fwd_order = _chunk_order(direction, nc, nctx)
    order = lambda c: fwd_order(nc - 1 - c)
    tri = _scan_mask(direction)
    tri_t = jnp.transpose(tri)
    Q, GC, N, G, HP, HD = SSD_Q, GROUP_COLS, SSD_STATE, SSD_GROUPS, HEADS_PER_GROUP, SSD_HEAD_DIM
    hrows = HP * GPS
    _, s, s_t, etot, pk_t = prep

    def body(x_ref, b_ref, c_ref, s_ref, st_ref, etot_ref, dtr_ref, er_ref, dr_ref, dsk_ref, sst_ref, dy_ref,
             trit_ref, tri_ref, dx_ref, db_ref, dc_ref, rows_ref, dskg_ref, dS):
        c = pl.program_id(0)

        @pl.when(c == 0)
        def _():
            dskg_ref[...] = jnp.zeros_like(dskg_ref)

        mask_t = trit_ref[...] > 0.5
        not_tri = tri_ref[...] < 0.5
        tri_b = tri_ref[...].astype(BF16)
        lane = lax.broadcasted_iota(jnp.int32, (1, LANES), 1)
        rowi = lax.broadcasted_iota(jnp.int32, (GC, 1), 0)
        rowh = lax.broadcasted_iota(jnp.int32, (hrows, 1), 0)
        s_all = s_ref[...]
        etot_row = etot_ref[0, 0:1, :]
        out_rows = [jnp.zeros((hrows, Q), F32) for _ in range(4)]
        for k in range(GPS):
            g = pl.program_id(1) * GPS + k
            base = direction * SSD_HEADS + HP * g

            @pl.when(c == 0)
            def _():
                dS[g] = jnp.zeros((GC, N), F32)

            dS1 = dS[g]
            S0 = sst_ref[0, k]
            Bm, Cm = _cols(b_ref, k, N), _cols(c_ref, k, N)

            def head_rows(ref):
                return jnp.concatenate([jnp.broadcast_to(ref[0, HP * k + r:HP * k + r + 1, :], (HD, Q))
                                        for r in range(HP)], axis=0)

            def head_sum(v, r):
                return jnp.sum(v[HD * r:HD * (r + 1), :], axis=0, keepdims=True)

            x_t = jnp.transpose(_cols(x_ref, k, GC).astype(F32))
            dY_t = jnp.transpose(_cols(dy_ref, k, GC).astype(F32))
            dt_b, E_b, D_b = head_rows(dtr_ref), head_rows(er_ref), head_rows(dr_ref)
            X_t = x_t * dt_b
            X_tb, dY_tb = X_t.astype(BF16), dY_t.astype(BF16)
            e = [jnp.sum(jnp.where(lane == base + r, etot_row, 0.0), axis=1, keepdims=True) for r in range(HP)]
            etot_c = jnp.where(rowi < HD, e[0], jnp.where(rowi < 2 * HD, e[1], jnp.where(rowi < 3 * HD, e[2], e[3])))

            CS_t = _dot(S0, Cm, NT)
            dCS_t = dY_t * E_b
            dC = _dot(dCS_t, S0, TN)
            dS0 = etot_c * dS1 + _dot(dCS_t, Cm, NN)
            dsx_t = dY_t * (E_b * CS_t)
            dtot_c = jnp.sum(dS1 * S0, axis=1, keepdims=True) * etot_c
            XD_t = X_t * D_b
            dXD_t = _dot(dS1, Bm, NT)
            dB = _dot(XD_t, dS1, TN)
            dX_t = dXD_t * D_b
            t_t = dXD_t * XD_t
            dsx_t = dsx_t - t_t
            G_t = _dot(Bm, Cm, NT)
            dG_t = jnp.zeros((Q, Q), F32)
            dx_parts = []
            for r in range(HP):
                col = jnp.sum(jnp.where(lane == base + r, s_all, 0.0), axis=1, keepdims=True)
                L_t = jnp.exp(jnp.where(mask_t, st_ref[0, HP * k + r:HP * k + r + 1, :] - col, NEG_BIG))
                W_t = G_t * L_t
                xr, dyr = X_tb[HD * r:HD * (r + 1), :], dY_tb[HD * r:HD * (r + 1), :]
                dW_t = _dot(xr, dyr, TN)
                dx_parts.append(_dot(dyr, W_t, NT))
                dG_t = dG_t + dW_t * L_t
                P = lax.dot_general((dW_t * W_t).astype(BF16), tri_b, NN, preferred_element_type=F32)
                da_row = jnp.sum(jnp.where(not_tri, P, 0.0), axis=0, keepdims=True)
                out_rows[2] = jnp.where(rowh == HP * k + r, da_row, out_rows[2])
            dX_t = dX_t + jnp.concatenate(dx_parts, axis=0)
            dB = dB + _dot(dG_t, Cm, NN)
            dC = dC + _dot(dG_t, Bm, TN)
            dx_t = dX_t * dt_b
            if direction == 0:
                dx_t = dx_t + dY_t * dsk_ref[k]
            dxx = dX_t * x_t
            dyx = dY_t * x_t
            for r in range(HP):
                here = rowh == HP * k + r
                out_rows[0] = jnp.where(here, head_sum(dsx_t, r), out_rows[0])
                out_rows[1] = jnp.where(here, head_sum(dxx, r), out_rows[1])
                tot_r = (jnp.sum(dtot_c[HD * r:HD * (r + 1), :], axis=0, keepdims=True)
                         + jnp.sum(head_sum(t_t, r), axis=1, keepdims=True))
                out_rows[3] = jnp.where(here, tot_r, out_rows[3])
                if direction == 0:
                    dsk_r = jnp.sum(head_sum(dyx, r), axis=1, keepdims=True)
                    dskg_ref[HP * k + r:HP * k + r + 1, :] += jnp.broadcast_to(dsk_r, (1, LANES))
            dx_ref[:, k * GC:(k + 1) * GC] = jnp.transpose(dx_t).astype(BF16)
            db_ref[:, k * N:(k + 1) * N] = dB.astype(BF16)
            dc_ref[:, k * N:(k + 1) * N] = dC.astype(BF16)
            dS[g] = dS0
        for n in range(4):
            rows_ref[0, n] = out_rows[n]

    sec = LANES // hrows
    row_block = lambda n: pl.BlockSpec((1, hrows, Q), lambda c, g: (order(c), n * sec + direction * (SSD_HEADS // hrows) + g, 0))
    cst2 = lambda shape: pl.BlockSpec(shape, lambda c, g: (0, 0))
    in_specs = [
        pl.BlockSpec((Q, GPS * GC), lambda c, g: (order(c), g)),
        pl.BlockSpec((Q, GPS * N), lambda c, g: (order(c), SSD_W // (GPS * N) + g)),
        pl.BlockSpec((Q, GPS * N), lambda c, g: (order(c), (SSD_W + G * N) // (GPS * N) + g)),
        pl.BlockSpec((Q, LANES), lambda c, g: (order(c), 0)),
        row_block(0),
        pl.BlockSpec((1, 8, LANES), lambda c, g: (order(c), 0, 0)),
        row_block(0), row_block(1), row_block(2),
        pl.BlockSpec((GPS, GC, Q), lambda c, g: (g, 0, 0)),
        pl.BlockSpec((1, GPS, GC, N), lambda c, g: (order(c), g, 0, 0)),
        pl.BlockSpec((Q, GPS * GC), lambda c, g: (order(c), g)),
        cst2((Q, Q)), cst2((Q, Q)),
    ]
    out_specs = (pl.BlockSpec((Q, GPS * GC), lambda c, g: (order(c), g)),
                 pl.BlockSpec((Q, GPS * N), lambda c, g: (order(c), g)),
                 pl.BlockSpec((Q, GPS * N), lambda c, g: (order(c), g)),
                 pl.BlockSpec((1, 4, hrows, Q), lambda c, g: (order(c), 0, g, 0)),
                 pl.BlockSpec((hrows, LANES), lambda c, g: (g, 0)))
    return _call(body, name=f"ssd_bwd_{direction}",
                 out_shape=(_sds((T, SSD_W), BF16), _sds((T, G * N), BF16), _sds((T, G * N), BF16),
                            _sds((nc, 4, SSD_HEADS, Q), F32), _sds((SSD_HEADS, LANES), F32)),
                 grid=(nc, G // GPS), in_specs=in_specs, out_specs=out_specs,
                 scratch=[pltpu.VMEM((G, GC, N), F32)], sem=("arbitrary", "arbitrary"), vmem_mb=56,
                 )(xbc, xbc, xbc, s, s_t, etot, pk_t, pk_t, pk_t, dskc, sst, dy, tri_t, tri)


def _ssd_post_t(raw, prm, dt, rows0, rows1):
    T = raw.shape[0]
    nc, Q = T // SSD_Q, SSD_Q
    tril = jnp.asarray(np.tril(np.ones((Q, Q), np.float32)))
    triu = jnp.asarray(np.triu(np.ones((Q, Q), np.float32)))

    def body(raw_ref, prm_ref, dt_ref, r0_ref, r1_ref, tril_ref, triu_ref, draw_ref, dprm_ref):
        @pl.when(pl.program_id(0) == 0)
        def _():
            dprm_ref[...] = jnp.zeros_like(dprm_ref)

        def to_lanes(n):
            rows = jnp.concatenate([r0_ref[0, n], r1_ref[0, n], jnp.zeros((LANES - 2 * SSD_HEADS, Q), F32)], axis=0)
            return rows.T

        pre = raw_ref[...] + prm_ref[0:1, :]
        A = -jnp.exp(prm_ref[1:2, :])
        ds = to_lanes(0)
        lane = lax.broadcasted_iota(jnp.int32, (1, LANES), 1)
        da = jnp.where(lane < SSD_HEADS, _dot_hi(triu_ref[...], ds), _dot_hi(tril_ref[...], ds))
        da = da + to_lanes(2) + to_lanes(3)
        draw = (da * A + to_lanes(1)) * _sigmoid(pre)
        draw_ref[...] = draw
        dprm_ref[0:1, :] += jnp.sum(draw, axis=0, keepdims=True)
        dprm_ref[1:2, :] += jnp.sum(da * dt_ref[...], axis=0, keepdims=True) * A

    row = pl.BlockSpec((Q, LANES), lambda c: (c, 0))
    rows = pl.BlockSpec((1, 4, SSD_HEADS, Q), lambda c: (c, 0, 0, 0))
    cst = lambda shape: pl.BlockSpec(shape, lambda c: (0, 0))
    return _call(body, name="ssd_post", out_shape=(_sds((T, LANES), F32), _sds((8, LANES), F32)), grid=(nc,),
                 in_specs=[row, cst((8, LANES)), row, rows, rows, cst((Q, Q)), cst((Q, Q))],
                 out_specs=(row, cst((8, LANES))), sem=("arbitrary",))(raw, prm, dt, rows0, rows1, tril, triu)


ADA_ROWS = 16


def _ada_fwd(c16, w_ada, b_loc):
    depth, Dm, W = w_ada.shape

    def body(c_ref, w_ref, b_ref, o_ref):
        o_ref[0] = _dot_hi(_silu(c_ref[...]), w_ref[0]) + b_ref[0]

    return _call(body, name="ada_fwd", out_shape=_sds((depth, ADA_ROWS, W), F32), grid=(depth,),
                 in_specs=[pl.BlockSpec((ADA_ROWS, Dm), lambda l: (0, 0)),
                           pl.BlockSpec((1, Dm, W), lambda l: (l, 0, 0)),
                           pl.BlockSpec((1, 1, W), lambda l: (l, 0, 0))],
                 out_specs=pl.BlockSpec((1, ADA_ROWS, W), lambda l: (l, 0, 0)),
                 sem=("parallel",), vmem_mb=40)(c16, w_ada, b_loc)


def _ada_bwd(c16, w_ada, dmod):
    depth, Dm, W = w_ada.shape

    def body(c_ref, w_ref, d_ref, gw_ref, dc_ref):
        l = pl.program_id(0)

        @pl.when(l == 0)
        def _():
            dc_ref[...] = jnp.zeros_like(dc_ref)

        cc = c_ref[...]
        sg = _sigmoid(cc)
        gw_ref[0] = _dot_hi(cc * sg, d_ref[0], TN)
        dsc = _dot_hi(d_ref[0], w_ref[0], NT)
        dc_ref[...] += dsc[8:16, :] * (sg + cc * sg * (1.0 - sg))[8:16, :]

    return _call(body, name="ada_bwd", out_shape=(_sds((depth, Dm, W), F32), _sds((8, Dm), F32)), grid=(depth,),
                 in_specs=[pl.BlockSpec((ADA_ROWS, Dm), lambda l: (0, 0)),
                           pl.BlockSpec((1, Dm, W), lambda l: (l, 0, 0)),
                           pl.BlockSpec((1, ADA_ROWS, W), lambda l: (l, 0, 0))],
                 out_specs=(pl.BlockSpec((1, Dm, W), lambda l: (l, 0, 0)), pl.BlockSpec((8, Dm), lambda l: (0, 0))),
                 sem=("arbitrary",), vmem_mb=48)(c16, w_ada, dmod)


def _sum_slabs(g, name, tr=512):
    _, R, C = g.shape
    tr = min(tr, R)

    def body(g_ref, o_ref):
        acc = g_ref[0].astype(F32)
        for k in range(1, N_DEV):
            acc = acc + g_ref[k].astype(F32)
        o_ref[...] = acc

    return _call(body, name=name, out_shape=_sds((R, C), F32), grid=(R // tr,),
                 in_specs=[pl.BlockSpec((N_DEV, tr, C), lambda i: (0, i, 0))],
                 out_specs=pl.BlockSpec((tr, C), lambda i: (i, 0)), sem=("parallel",), vmem_mb=40)(g)


def _adamw_math(w, g, m, v):
    m = ADAM_B1 * m + (1.0 - ADAM_B1) * g
    v = ADAM_B2 * v + (1.0 - ADAM_B2) * (g * g)
    m_hat = m / (1.0 - ADAM_B1 ** ADAM_STEP)
    v_hat = v / (1.0 - ADAM_B2 ** ADAM_STEP)
    delta = -ADAM_LR * (m_hat / (jnp.sqrt(v_hat) + ADAM_EPS) + ADAM_WD * w)
    return delta, m, v


def _adamw(w, g, m, v, name, tr):
    R, C = w.shape
    tr = min(tr, R)
    assert R % tr == 0

    def body(w_ref, g_ref, m_ref, v_ref, d_ref, nm_ref, nv_ref):
        d, nm, nv = _adamw_math(w_ref[...], g_ref[...], m_ref[...], v_ref[...])
        d_ref[...] = d
        nm_ref[...] = nm
        nv_ref[...] = nv

    blk = pl.BlockSpec((tr, C), lambda i: (i, 0))
    out = _sds((R, C), F32)
    return _call(body, name=name, out_shape=(out, out, out), grid=(R // tr,), in_specs=[blk] * 4,
                 out_specs=(blk, blk, blk), sem=("parallel",), vmem_mb=40)(w, g, m, v)


def _adamw_slabs(w, slabs, m, v, name, tr, tc):
    depth, R, C = w.shape
    assert R % tr == 0 and C % tc == 0

    def body(w_ref, s_ref, m_ref, v_ref, g_ref, d_ref, nm_ref, nv_ref):
        g = s_ref[0, 0].astype(F32)
        for k in range(1, N_DEV):
            g = g + s_ref[0, k].astype(F32)
        d, nm, nv = _adamw_math(w_ref[0], g, m_ref[0], v_ref[0])
        g_ref[0] = g
        d_ref[0] = d
        nm_ref[0] = nm
        nv_ref[0] = nv

    blk = pl.BlockSpec((1, tr, tc), lambda l, i, j: (l, i, j))
    out = _sds((depth, R, C), F32)
    return _call(body, name=name, out_shape=(out, out, out, out), grid=(depth, R // tr, C // tc),
                 in_specs=[blk, pl.BlockSpec((1, N_DEV, tr, tc), lambda l, i, j: (l, 0, i, j)), blk, blk],
                 out_specs=(blk, blk, blk, blk), sem=("parallel", "parallel", "parallel"), vmem_mb=56)(w, slabs, m, v)


PACK_QUANTUM = 512 * LANES


def _pack(arrays):
    flat = jnp.concatenate([a.reshape(-1).astype(F32) for a in arrays])
    pad = (-flat.shape[0]) % PACK_QUANTUM
    return jnp.pad(flat, (0, pad)).reshape(-1, LANES)


def _unpack(bundle, shapes):
    flat = bundle.reshape(-1)
    out, off = [], 0
    for shp in shapes:
        n = int(np.prod(shp))
        out.append(flat[off:off + n].reshape(shp))
        off += n
    return out


def _row(v):
    return v.reshape(1, -1)


def _pad_rows(a, rows):
    return jnp.pad(a, ((0, rows - a.shape[0]), (0, 0)))


def kernel(x, c, ctx, c_ctx, w_ada, b_ada, g_pre, g_post, w_in, conv_w, conv_b, dt_bias, a_log, d_skip, g_ssd, g_v, w_s, b_s, g_mlp, w_out, loss_target, m_c_ctx, m_w_ada, m_b_ada, m_g_pre, m_g_post, m_w_in, m_conv_w, m_conv_b, m_dt_bias, m_a_log, m_d_skip, m_g_ssd, m_g_v, m_w_s, m_b_s, m_g_mlp, m_w_out, v_c_ctx, v_w_ada, v_b_ada, v_g_pre, v_g_post, v_w_in, v_conv_w, v_conv_b, v_dt_bias, v_a_log, v_d_skip, v_g_ssd, v_g_v, v_w_s, v_b_s, v_g_mlp, v_w_out):
    depth = w_in.shape[0]
    L = x.shape[1]
    n_ctx = ctx.shape[1]
    T = n_ctx + L
    Dm = D_MODEL
    me = _lin(_my_pos())
    ada_w = w_ada.shape[2]
    in_w = w_in.shape[2]
    out_r = w_out.shape[1]
    conv_c = conv_w.shape[2]
    TM = 768

    c_all = _all_gather_small(_pad_rows(c, 8), "gather_c")[:, 0, :]
    c16 = _pad_rows(jnp.concatenate([c_all, _row(c_ctx)], axis=0), ADA_ROWS)
    b_loc = lax.dynamic_slice_in_dim(b_ada, me * ada_w, ada_w, axis=1)[:, None, :]
    mod_loc = _ada_fwd(c16, w_ada, b_loc)
    mod_all = _all_gather_small(mod_loc.reshape(depth * ADA_ROWS, ada_w), "gather_mod")
    mod_all = mod_all.reshape(N_DEV, depth, ADA_ROWS, ada_w)
    mod_me = lax.dynamic_index_in_dim(mod_all, me, axis=2, keepdims=False)
    mod_me = jnp.transpose(mod_me, (1, 0, 2)).reshape(depth, N_DEV * ada_w)
    mod_cx = jnp.transpose(mod_all[:, :, 8, :], (1, 0, 2)).reshape(depth, N_DEV * ada_w)
    shift2 = jnp.stack([mod_cx[:, 0:Dm], mod_me[:, 0:Dm]], axis=1)
    scale2 = jnp.stack([mod_cx[:, Dm:2 * Dm], mod_me[:, Dm:2 * Dm]], axis=1)
    gate2 = jnp.stack([mod_cx[:, 2 * Dm:], mod_me[:, 2 * Dm:]], axis=1)

    w_in_t, m_w_in_t, v_w_in_t = (jnp.swapaxes(a, 1, 2) for a in (w_in, m_w_in, v_w_in))
    w_in_bf = w_in_t.astype(BF16)
    w_out_bf = w_out.astype(BF16)
    w_in_g = _all_gather_big(w_in_bf[0], "gather_w_in")
    w_out_g = _all_gather_big(w_out_bf[0], "gather_w_out")
    conv_all = _all_gather_small(_pad_rows(conv_w.reshape(depth * CONV_W, conv_c), 24).reshape(24, conv_c),
                                 "gather_conv_w")
    conv_full = jnp.transpose(conv_all[:, :depth * CONV_W, :], (1, 0, 2)).reshape(depth, CONV_W, XBC_W)

    def in_weights(gathered):
        wf = gathered.reshape(IN_W, Dm)
        return (jnp.concatenate([wf[:XBC_W], wf[XBC_W + DT_W:]], axis=0),
                jnp.pad(wf[XBC_W:XBC_W + DT_W], ((0, LANES - DT_W), (0, 0))))

    w_main, w_dt, w_o = [None] * depth, [None] * depth, [None] * depth

    def ssd_prm(l):
        rows = jnp.stack([jnp.pad(dt_bias[l].reshape(-1), (0, LANES - DT_W)),
                          jnp.pad(a_log[l].reshape(-1), (0, LANES - DT_W)),
                          jnp.pad(d_skip[l], (0, LANES - SSD_HEADS))])
        return _pad_rows(rows, 8)

    ws_bf = w_s.astype(BF16)
    bst = jnp.pad(jnp.transpose(b_s, (0, 2, 1)), ((0, 0), (0, 0), (0, LANES - MLP_GROUPS)))

    X = jnp.concatenate([ctx[0], x[0]], axis=0)
    saved = []
    for l in range(depth):
        w_main[l], w_dt[l] = in_weights(w_in_g)
        w_o[l] = w_out_g.reshape(N_DEV * out_r, Dm)
        hx = _prenorm_fwd(X, _row(g_pre[l]), scale2[l], shift2[l], n_ctx)
        if l + 1 < depth:
            z, raw, w_in_g = _mm(hx, w_main[l], "nt", BF16, "in_proj_gather", TM, 1024, 2048,
                                 exchange=("gather", w_in_bf[l + 1]), side=w_dt[l])
        else:
            z, raw = _mm(hx, w_main[l], "nt", BF16, "in_proj", TM, 1024, 2048, side=w_dt[l])
        cw = _pad_rows(conv_full[l], 8)
        cb = _row(conv_b[l])
        xbc = _conv_fwd(z, cw, cb, n_ctx)
        prm = ssd_prm(l)
        prep = _ssd_prep(raw, prm)
        dskc = jnp.broadcast_to(jnp.repeat(d_skip[l], SSD_HEAD_DIM).reshape(SSD_GROUPS, GROUP_COLS, 1),
                                (SSD_GROUPS, GROUP_COLS, SSD_Q))
        y_f, sst_f = _ssd_fwd(xbc, prep, dskc, 0, n_ctx)
        y_b, sst_b = _ssd_fwd(xbc, prep, dskc, 1, n_ctx)
        ycat = _gate_fwd(y_f, y_b, z, _row(g_ssd[l]))
        ycat = _mlp_fwd(z, _row(g_v[l]), ws_bf[l], bst[l], _row(g_mlp[l]), ycat)
        if l + 1 < depth:
            o, w_out_g = _mm(ycat, w_o[l], "nn", F32, "out_proj_gather", TM, 1024, 4096,
                             exchange=("gather", w_out_bf[l + 1]))
        else:
            o = _mm(ycat, w_o[l], "nn", F32, "out_proj", TM, 1024, 4096)
        saved.append((X, hx, z, raw, cw, cb, xbc, prm, prep, dskc, y_f, sst_f, y_b, sst_b, ycat, o))
        X = _post_fwd(X, o, _row(g_post[l]), gate2[l], n_ctx)

    dX, sq = _loss_kernel(X, loss_target[0], n_ctx)
    loss = lax.psum(0.5 * jnp.sum(sq) / Dm, ("x", "y", "c"))

    g_small = {k: [None] * depth for k in
               ("b_ada", "g_pre", "g_post", "conv_b", "dt_bias", "a_log", "d_skip", "g_ssd", "g_v", "w_s", "b_s",
                "g_mlp", "conv_w", "dmod_c", "dmod")}
    in_recv = lax.empty((depth, N_DEV, in_w, Dm), BF16)
    out_recv = lax.empty((depth, N_DEV, out_r, Dm), BF16)
    for l in reversed(range(depth)):
        Xl, hx, z, raw, cw, cb, xbc, prm, prep, dskc, y_f, sst_f, y_b, sst_b, ycat, o = saved[l]
        d_o, acc_post = _post_bwd(o, _row(g_post[l]), gate2[l], dX, n_ctx)
        dycat = _mm(d_o, w_o[l], "nt", BF16, "out_proj_dx", TM, 1024, 2048)
        out_slabs = _mm(ycat, d_o, "tn", BF16, "out_proj_dw", 1024, 1024, 2816).reshape(N_DEV, out_r, Dm)
        dz = lax.empty((T, Z_MAIN), BF16)
        dy, dz, acc_gate = _gate_bwd(y_f, y_b, z, _row(g_ssd[l]), dycat, dz)
        dz, acc_mlp, dws, dbst = _mlp_bwd(z, _row(g_v[l]), ws_bf[l], bst[l], _row(g_mlp[l]), dycat, dz)
        dx0, db0, dc0, rows0, dsk0 = _ssd_bwd_t(xbc, prep, dskc, sst_f, dy, 0, n_ctx)
        dx1, db1, dc1, rows1, _ = _ssd_bwd_t(xbc, prep, dskc, sst_b, dy, 1, n_ctx)
        draw, dprm = _ssd_post_t(raw, prm, prep[0], rows0, rows1)
        gn = SSD_GROUPS * SSD_STATE
        dz, dcw_x, dcb_x = _conv_bwd(z, cw, cb, dx0, dx1, dz, 0, n_ctx)
        dz, dcw_b, dcb_b = _conv_bwd(z, cw, cb, db0, db1, dz, SSD_W, n_ctx)
        dz, dcw_c, dcb_c = _conv_bwd(z, cw, cb, dc0, dc1, dz, SSD_W + gn, n_ctx)
        dcw = jnp.concatenate([dcw_x, dcw_b, dcw_c], axis=1)
        dcb = jnp.concatenate([dcb_x, dcb_b, dcb_c], axis=1)
        gw_main, out_recv = _mm(dz, hx, "tn", BF16, "in_proj_dw_scatter", 1024, 1024, 2816,
                                exchange=("scatter", out_slabs, (out_recv, l)))
        gw_dt = _mm(draw, hx, "tn", BF16, "dt_proj_dw", LANES, 1024, 2816)
        in_slabs = jnp.concatenate([gw_main[:XBC_W], gw_dt[:DT_W], gw_main[XBC_W:]], axis=0).reshape(N_DEV, in_w, Dm)
        dhx, in_recv = _mm(dz, w_main[l], "nn", F32, "in_proj_dx_scatter", TM, 1024, 4096,
                           exchange=("scatter", in_slabs, (in_recv, l)), add=(draw, w_dt[l]))
        dX, acc_pre = _prenorm_bwd(Xl, _row(g_pre[l]), scale2[l], shift2[l], dhx, dX, n_ctx)

        dmod_c = jnp.concatenate([acc_pre[3], acc_pre[1], acc_post[1]])
        dmod_x = jnp.concatenate([acc_pre[4], acc_pre[2], acc_post[2]])
        g_small["dmod_c"][l] = dmod_c
        g_small["dmod"][l] = dmod_x
        g_small["b_ada"][l] = dmod_c + dmod_x
        g_small["g_pre"][l] = acc_pre[0]
        g_small["g_post"][l] = acc_post[0]
        g_small["conv_b"][l] = dcb[0]
        g_small["conv_w"][l] = dcw[:CONV_W]
        g_small["dt_bias"][l] = dprm[0, :DT_W].reshape(2, SSD_HEADS)
        g_small["a_log"][l] = dprm[1, :DT_W].reshape(2, SSD_HEADS)
        g_small["d_skip"][l] = dsk0[:, 0]
        g_small["g_ssd"][l] = acc_gate[0]
        g_small["g_v"][l] = acc_mlp[0]
        g_small["g_mlp"][l] = acc_mlp[1]
        g_small["w_s"][l] = dws
        g_small["b_s"][l] = jnp.transpose(dbst[:, :MLP_GROUPS])

    grad_x = dX[n_ctx:][None]

    summed_names = ["b_ada", "g_pre", "g_post", "conv_b", "dt_bias", "a_log", "d_skip", "g_ssd", "g_v", "w_s",
                    "b_s", "g_mlp", "conv_w", "dmod_c"]
    parts = [jnp.stack(g_small[k]) for k in summed_names] + [jnp.stack(g_small["dmod"])]
    shapes = [p.shape for p in parts]
    bundle = _pack(parts)
    gathered = _all_gather_big(bundle.astype(BF16), "gather_small_grads")
    reduced = _unpack(_sum_slabs(gathered, "sum_small_grads"), shapes)
    gs = dict(zip(summed_names, reduced[:-1]))
    n_el = int(np.prod(shapes[-1]))
    off = sum(int(np.prod(s)) for s in shapes[:-1])
    dmod_all = gathered.reshape(N_DEV, -1)[:, off:off + n_el].reshape(N_DEV, depth, 3 * Dm).astype(F32)

    dmod_rows = jnp.concatenate([jnp.transpose(dmod_all, (1, 0, 2)), gs["dmod_c"][:, None, :]], axis=1)
    dmod_rows = lax.dynamic_slice_in_dim(dmod_rows, me * ada_w, ada_w, axis=2)
    dmod_rows = jnp.pad(dmod_rows, ((0, 0), (0, ADA_ROWS - 9), (0, 0)))
    g_w_ada, dc_part = _ada_bwd(c16, w_ada, dmod_rows)
    dc_all = _all_gather_small(dc_part, "gather_dc")
    g_c_ctx = _sum_slabs(dc_all, "sum_dc")[0]

    grads_small = {"c_ctx": g_c_ctx, "b_ada": gs["b_ada"], "g_pre": gs["g_pre"], "g_post": gs["g_post"],
                   "conv_b": gs["conv_b"], "dt_bias": gs["dt_bias"], "a_log": gs["a_log"], "d_skip": gs["d_skip"],
                   "g_ssd": gs["g_ssd"], "g_v": gs["g_v"], "w_s": gs["w_s"], "b_s": gs["b_s"], "g_mlp": gs["g_mlp"],
                   "conv_w": lax.dynamic_slice_in_dim(gs["conv_w"], me * conv_c, conv_c, axis=2)}

    g_in, d_in, nm_in, nv_in = (jnp.swapaxes(t, 1, 2) for t in
                                _adamw_slabs(w_in_t, in_recv, m_w_in_t, v_w_in_t, "adamw_w_in", in_w, 256))
    g_out, d_out, nm_out, nv_out = _adamw_slabs(w_out, out_recv, m_w_out, v_w_out, "adamw_w_out", 128, Dm)
    d_ada, nm_ada, nv_ada = _adamw(w_ada.reshape(depth * Dm, ada_w), g_w_ada.reshape(depth * Dm, ada_w),
                                   m_w_ada.reshape(depth * Dm, ada_w), v_w_ada.reshape(depth * Dm, ada_w),
                                   "adamw_w_ada", 256)

    small_names = ["c_ctx", "b_ada", "g_pre", "g_post", "conv_w", "conv_b", "dt_bias", "a_log", "d_skip", "g_ssd",
                   "g_v", "w_s", "b_s", "g_mlp"]
    small_w = dict(c_ctx=c_ctx, b_ada=b_ada, g_pre=g_pre, g_post=g_post, conv_w=conv_w, conv_b=conv_b,
                   dt_bias=dt_bias, a_log=a_log, d_skip=d_skip, g_ssd=g_ssd, g_v=g_v, w_s=w_s, b_s=b_s, g_mlp=g_mlp)
    small_m = dict(c_ctx=m_c_ctx, b_ada=m_b_ada, g_pre=m_g_pre, g_post=m_g_post, conv_w=m_conv_w, conv_b=m_conv_b,
                   dt_bias=m_dt_bias, a_log=m_a_log, d_skip=m_d_skip, g_ssd=m_g_ssd, g_v=m_g_v, w_s=m_w_s,
                   b_s=m_b_s, g_mlp=m_g_mlp)
    small_v = dict(c_ctx=v_c_ctx, b_ada=v_b_ada, g_pre=v_g_pre, g_post=v_g_post, conv_w=v_conv_w, conv_b=v_conv_b,
                   dt_bias=v_dt_bias, a_log=v_a_log, d_skip=v_d_skip, g_ssd=v_g_ssd, g_v=v_g_v, w_s=v_w_s,
                   b_s=v_b_s, g_mlp=v_g_mlp)
    s_shapes = [small_w[k].shape for k in small_names]
    d_s, nm_s, nv_s = _adamw(_pack([small_w[k] for k in small_names]),
                             _pack([grads_small[k].reshape(small_w[k].shape) for k in small_names]),
                             _pack([small_m[k] for k in small_names]), _pack([small_v[k] for k in small_names]),
                             "adamw_small", 512)
    d_s = dict(zip(small_names, _unpack(d_s, s_shapes)))
    nm_s = dict(zip(small_names, _unpack(nm_s, s_shapes)))
    nv_s = dict(zip(small_names, _unpack(nv_s, s_shapes)))

    big = {"w_ada": (g_w_ada, d_ada.reshape(w_ada.shape), nm_ada.reshape(w_ada.shape), nv_ada.reshape(w_ada.shape)),
           "w_in": tuple(t.reshape(w_in.shape) for t in (g_in, d_in, nm_in, nv_in)),
           "w_out": tuple(t.reshape(w_out.shape) for t in (g_out, d_out, nm_out, nv_out))}
    order = ["c_ctx", "w_ada", "b_ada", "g_pre", "g_post", "w_in", "conv_w", "conv_b", "dt_bias", "a_log", "d_skip",
             "g_ssd", "g_v", "w_s", "b_s", "g_mlp", "w_out"]

    def pick(k, idx):
        if k in big:
            return big[k][idx]
        return (grads_small[k].reshape(small_w[k].shape), d_s[k], nm_s[k], nv_s[k])[idx]

    return (loss, grad_x, *[pick(k, 0) for k in order], *[pick(k, 1) for k in order],
            *[pick(k, 2) for k in order], *[pick(k, 3) for k in order])
```
